```python
import math
import jax, jax.numpy as jnp
from jax import lax
import numpy as np

D_MODEL = 1024
BATCH = 8
SEQ = 8192
DEPTH = 1

N_META = 16
D_FF = 2816
D_SSM = D_MODEL // 2
GROUP_CH = 16
N_GROUPS = D_SSM // GROUP_CH
STATE = 64
D_CONV = D_MODEL // 2
CONV_W = 3
N_BRANCH = 2
IN_COLS = D_SSM + 3 * D_CONV + N_BRANCH * D_MODEL
EPS = 1e-6

kernel_name = "hybrid_s5_shortconv_gated_macaron"


def rmsnorm(x, g):
    xf = x.astype(jnp.float32)
    y = xf * lax.rsqrt(jnp.mean(xf * xf, axis=-1, keepdims=True) + EPS)
    return (y * g.astype(jnp.float32)).astype(x.dtype)


def swiglu(x, w_gate, w_up, w_down):
    return (jax.nn.silu(x @ w_gate) * (x @ w_up)) @ w_down


def s5_mixer(u, a_re, a_im, log_dt, b_re, b_im, c_re, c_im, d_skip):
    bsz, length, _ = u.shape
    f32 = jnp.float32
    ug = u.astype(f32).reshape(bsz, length, N_GROUPS, GROUP_CH)
    a_re = a_re.astype(f32); a_im = a_im.astype(f32)
    dt = jnp.exp(log_dt.astype(f32))[:, None]
    mag = jnp.exp(a_re * dt)
    lam_re = mag * jnp.cos(a_im * dt)
    lam_im = mag * jnp.sin(a_im * dt)
    den = a_re * a_re + a_im * a_im
    q_re = ((lam_re - 1.0) * a_re + lam_im * a_im) / den
    q_im = (lam_im * a_re - (lam_re - 1.0) * a_im) / den
    b_re = b_re.astype(f32); b_im = b_im.astype(f32)
    bb_re = q_re[..., None] * b_re - q_im[..., None] * b_im
    bb_im = q_re[..., None] * b_im + q_im[..., None] * b_re
    bu_re = jnp.einsum('blgc,gpc->blgp', ug, bb_re)
    bu_im = jnp.einsum('blgc,gpc->blgp', ug, bb_im)
    la_re = jnp.broadcast_to(lam_re, bu_re.shape)
    la_im = jnp.broadcast_to(lam_im, bu_re.shape)

    def combine(e1, e2):
        a1r, a1i, b1r, b1i = e1
        a2r, a2i, b2r, b2i = e2
        return (a1r * a2r - a1i * a2i,
                a1r * a2i + a1i * a2r,
                a2r * b1r - a2i * b1i + b2r,
                a2r * b1i + a2i * b1r + b2i)

    _, _, h_re, h_im = lax.associative_scan(combine, (la_re, la_im, bu_re, bu_im), axis=1)
    y = (jnp.einsum('blgp,gcp->blgc', h_re, c_re.astype(f32))
         - jnp.einsum('blgp,gcp->blgc', h_im, c_im.astype(f32))
         + d_skip.astype(f32).reshape(N_GROUPS, GROUP_CH) * ug)
    return y.reshape(bsz, length, D_SSM).astype(u.dtype)


def short_conv(v, w_conv):
    return lax.conv_general_dilated(
        v, w_conv.astype(v.dtype), window_strides=(1,), padding=[(CONV_W - 1, 0)],
        dimension_numbers=('NWC', 'WIO', 'NWC'), feature_group_count=v.shape[-1])


def _fwd_setup_inputs(seed: int = 0) -> dict:
    key = jax.random.key(seed)
    ks = jax.random.split(key, 32)
    D = D_MODEL
    nrm = lambda k, shape, s: jax.random.normal(k, shape, jnp.float32) * s
    gain = lambda k, shape: 1.0 + 0.02 * jax.random.normal(k, shape, jnp.float32)
    a_im_base = jnp.pi * jnp.arange(STATE, dtype=jnp.float32)
    return {
        "x": nrm(ks[0], (BATCH, SEQ, D), 1.0),
        "meta_tokens": nrm(ks[1], (N_META, D), 1.0),
        "g_ffn1": gain(ks[2], (DEPTH, D)),
        "ffn1_w_gate": nrm(ks[3], (DEPTH, D, D_FF), D ** -0.5),
        "ffn1_w_up": nrm(ks[4], (DEPTH, D, D_FF), D ** -0.5),
        "ffn1_w_down": nrm(ks[5], (DEPTH, D_FF, D), D_FF ** -0.5),
        "g_mix": gain(ks[6], (DEPTH, D)),
        "w_in": nrm(ks[7], (DEPTH, D, IN_COLS), D ** -0.5),
        "b_gate": nrm(ks[8], (DEPTH, N_BRANCH * D), 0.01),
        "ssm_a_re": -0.5 + nrm(ks[9], (DEPTH, N_GROUPS, STATE), 0.01),
        "ssm_a_im": a_im_base + nrm(ks[10], (DEPTH, N_GROUPS, STATE), 0.01),
        "ssm_log_dt": jax.random.uniform(ks[11], (DEPTH, N_GROUPS), jnp.float32,
                                         math.log(1e-3), math.log(1e-1)),
        "ssm_b_re": nrm(ks[12], (DEPTH, N_GROUPS, STATE, GROUP_CH), GROUP_CH ** -0.5),
        "ssm_b_im": nrm(ks[13], (DEPTH, N_GROUPS, STATE, GROUP_CH), GROUP_CH ** -0.5),
        "ssm_c_re": nrm(ks[14], (DEPTH, N_GROUPS, GROUP_CH, STATE), (2 * STATE) ** -0.5),
        "ssm_c_im": nrm(ks[15], (DEPTH, N_GROUPS, GROUP_CH, STATE), (2 * STATE) ** -0.5),
        "ssm_d": nrm(ks[16], (DEPTH, D_SSM), 1.0),
        "ssm_w_glu": nrm(ks[17], (DEPTH, D_SSM, 2 * D), D_SSM ** -0.5),
        "conv_w": nrm(ks[18], (DEPTH, CONV_W, 1, D_CONV), CONV_W ** -0.5),
        "conv_w_out": nrm(ks[19], (DEPTH, D_CONV, D), D_CONV ** -0.5),
        "w_o": nrm(ks[20], (DEPTH, D, D), D ** -0.5),
        "g_ffn2": gain(ks[21], (DEPTH, D)),
        "ffn2_w_gate": nrm(ks[22], (DEPTH, D, D_FF), D ** -0.5),
        "ffn2_w_up": nrm(ks[23], (DEPTH, D, D_FF), D ** -0.5),
        "ffn2_w_down": nrm(ks[24], (DEPTH, D_FF, D), D_FF ** -0.5),
        "g_final": gain(ks[25], (D,)),
    }


def _fwd_reference(x, meta_tokens, g_ffn1, ffn1_w_gate, ffn1_w_up, ffn1_w_down, g_mix, w_in,
              b_gate, ssm_a_re, ssm_a_im, ssm_log_dt, ssm_b_re, ssm_b_im, ssm_c_re,
              ssm_c_im, ssm_d, ssm_w_glu, conv_w, conv_w_out, w_o, g_ffn2,
              ffn2_w_gate, ffn2_w_up, ffn2_w_down, g_final):
    bsz = x.shape[0]
    meta = jnp.broadcast_to(meta_tokens.astype(x.dtype)[None], (bsz, N_META, D_MODEL))
    h = jnp.concatenate([meta, x], axis=1)
    s0 = D_SSM
    s1 = s0 + D_CONV
    s2 = s1 + D_CONV
    s3 = s2 + D_CONV
    s4 = s3 + D_MODEL
    for l in range(DEPTH):
        h = h + 0.5 * swiglu(rmsnorm(h, g_ffn1[l]), ffn1_w_gate[l], ffn1_w_up[l], ffn1_w_down[l])
        u = rmsnorm(h, g_mix[l])
        p = u @ w_in[l]
        gates = jax.nn.sigmoid(p[..., s3:] + b_gate[l])
        gate_ssm, gate_conv = gates[..., :D_MODEL], gates[..., D_MODEL:]
        y_ssm = s5_mixer(p[..., :s0], ssm_a_re[l], ssm_a_im[l], ssm_log_dt[l], ssm_b_re[l],
                         ssm_b_im[l], ssm_c_re[l], ssm_c_im[l], ssm_d[l])
        z = jax.nn.gelu(y_ssm) @ ssm_w_glu[l]
        y_ssm = z[..., :D_MODEL] * jax.nn.sigmoid(z[..., D_MODEL:])
        v, gb, gc = p[..., s0:s1], p[..., s1:s2], p[..., s2:s3]
        y_conv = (gb * short_conv(gc * v, conv_w[l])) @ conv_w_out[l]
        mixed = gate_ssm * y_ssm + gate_conv * y_conv
        h = h + mixed @ w_o[l]
        h = h + 0.5 * swiglu(rmsnorm(h, g_ffn2[l]), ffn2_w_gate[l], ffn2_w_up[l], ffn2_w_down[l])
    out = rmsnorm(h, g_final)
    return out[:, N_META:]


import jax as _jax
import jax.numpy as _jnp

TWIN_FORMAT = 'train_step'
FWD_PARAMS = ['x', 'meta_tokens', 'g_ffn1', 'ffn1_w_gate', 'ffn1_w_up', 'ffn1_w_down', 'g_mix', 'w_in', 'b_gate', 'ssm_a_re', 'ssm_a_im', 'ssm_log_dt', 'ssm_b_re', 'ssm_b_im', 'ssm_c_re', 'ssm_c_im', 'ssm_d', 'ssm_w_glu', 'conv_w', 'conv_w_out', 'w_o', 'g_ffn2', 'ffn2_w_gate', 'ffn2_w_up', 'ffn2_w_down', 'g_final']
TWIN_WEIGHTS = ['meta_tokens', 'g_ffn1', 'ffn1_w_gate', 'ffn1_w_up', 'ffn1_w_down', 'g_mix', 'w_in', 'b_gate', 'ssm_a_re', 'ssm_a_im', 'ssm_log_dt', 'ssm_b_re', 'ssm_b_im', 'ssm_c_re', 'ssm_c_im', 'ssm_d', 'ssm_w_glu', 'conv_w', 'conv_w_out', 'w_o', 'g_ffn2', 'ffn2_w_gate', 'ffn2_w_up', 'ffn2_w_down', 'g_final']
TWIN_DIFF_INPUT = 'x'
TWIN_INPUTS = ['x', 'meta_tokens', 'g_ffn1', 'ffn1_w_gate', 'ffn1_w_up', 'ffn1_w_down', 'g_mix', 'w_in', 'b_gate', 'ssm_a_re', 'ssm_a_im', 'ssm_log_dt', 'ssm_b_re', 'ssm_b_im', 'ssm_c_re', 'ssm_c_im', 'ssm_d', 'ssm_w_glu', 'conv_w', 'conv_w_out', 'w_o', 'g_ffn2', 'ffn2_w_gate', 'ffn2_w_up', 'ffn2_w_down', 'g_final', 'loss_target', 'm_meta_tokens', 'm_g_ffn1', 'm_ffn1_w_gate', 'm_ffn1_w_up', 'm_ffn1_w_down', 'm_g_mix', 'm_w_in', 'm_b_gate', 'm_ssm_a_re', 'm_ssm_a_im', 'm_ssm_log_dt', 'm_ssm_b_re', 'm_ssm_b_im', 'm_ssm_c_re', 'm_ssm_c_im', 'm_ssm_d', 'm_ssm_w_glu', 'm_conv_w', 'm_conv_w_out', 'm_w_o', 'm_g_ffn2', 'm_ffn2_w_gate', 'm_ffn2_w_up', 'm_ffn2_w_down', 'm_g_final', 'v_meta_tokens', 'v_g_ffn1', 'v_ffn1_w_gate', 'v_ffn1_w_up', 'v_ffn1_w_down', 'v_g_mix', 'v_w_in', 'v_b_gate', 'v_ssm_a_re', 'v_ssm_a_im', 'v_ssm_log_dt', 'v_ssm_b_re', 'v_ssm_b_im', 'v_ssm_c_re', 'v_ssm_c_im', 'v_ssm_d', 'v_ssm_w_glu', 'v_conv_w', 'v_conv_w_out', 'v_w_o', 'v_g_ffn2', 'v_ffn2_w_gate', 'v_ffn2_w_up', 'v_ffn2_w_down', 'v_g_final']
TWIN_OUTPUTS = ['loss', 'grad_x', 'grad_meta_tokens', 'grad_g_ffn1', 'grad_ffn1_w_gate', 'grad_ffn1_w_up', 'grad_ffn1_w_down', 'grad_g_mix', 'grad_w_in', 'grad_b_gate', 'grad_ssm_a_re', 'grad_ssm_a_im', 'grad_ssm_log_dt', 'grad_ssm_b_re', 'grad_ssm_b_im', 'grad_ssm_c_re', 'grad_ssm_c_im', 'grad_ssm_d', 'grad_ssm_w_glu', 'grad_conv_w', 'grad_conv_w_out', 'grad_w_o', 'grad_g_ffn2', 'grad_ffn2_w_gate', 'grad_ffn2_w_up', 'grad_ffn2_w_down', 'grad_g_final', 'delta_meta_tokens', 'delta_g_ffn1', 'delta_ffn1_w_gate', 'delta_ffn1_w_up', 'delta_ffn1_w_down', 'delta_g_mix', 'delta_w_in', 'delta_b_gate', 'delta_ssm_a_re', 'delta_ssm_a_im', 'delta_ssm_log_dt', 'delta_ssm_b_re', 'delta_ssm_b_im', 'delta_ssm_c_re', 'delta_ssm_c_im', 'delta_ssm_d', 'delta_ssm_w_glu', 'delta_conv_w', 'delta_conv_w_out', 'delta_w_o', 'delta_g_ffn2', 'delta_ffn2_w_gate', 'delta_ffn2_w_up', 'delta_ffn2_w_down', 'delta_g_final', 'new_m_meta_tokens', 'new_m_g_ffn1', 'new_m_ffn1_w_gate', 'new_m_ffn1_w_up', 'new_m_ffn1_w_down', 'new_m_g_mix', 'new_m_w_in', 'new_m_b_gate', 'new_m_ssm_a_re', 'new_m_ssm_a_im', 'new_m_ssm_log_dt', 'new_m_ssm_b_re', 'new_m_ssm_b_im', 'new_m_ssm_c_re', 'new_m_ssm_c_im', 'new_m_ssm_d', 'new_m_ssm_w_glu', 'new_m_conv_w', 'new_m_conv_w_out', 'new_m_w_o', 'new_m_g_ffn2', 'new_m_ffn2_w_gate', 'new_m_ffn2_w_up', 'new_m_ffn2_w_down', 'new_m_g_final', 'new_v_meta_tokens', 'new_v_g_ffn1', 'new_v_ffn1_w_gate', 'new_v_ffn1_w_up', 'new_v_ffn1_w_down', 'new_v_g_mix', 'new_v_w_in', 'new_v_b_gate', 'new_v_ssm_a_re', 'new_v_ssm_a_im', 'new_v_ssm_log_dt', 'new_v_ssm_b_re', 'new_v_ssm_b_im', 'new_v_ssm_c_re', 'new_v_ssm_c_im', 'new_v_ssm_d', 'new_v_ssm_w_glu', 'new_v_conv_w', 'new_v_conv_w_out', 'new_v_w_o', 'new_v_g_ffn2', 'new_v_ffn2_w_gate', 'new_v_ffn2_w_up', 'new_v_ffn2_w_down', 'new_v_g_final']
TWIN_LEAF_KINDS = {'loss': 'loss', 'grad_x': 'grad_x', 'grad_meta_tokens': 'grad_w', 'grad_g_ffn1': 'grad_w', 'grad_ffn1_w_gate': 'grad_w', 'grad_ffn1_w_up': 'grad_w', 'grad_ffn1_w_down': 'grad_w', 'grad_g_mix': 'grad_w', 'grad_w_in': 'grad_w', 'grad_b_gate': 'grad_w', 'grad_ssm_a_re': 'grad_w', 'grad_ssm_a_im': 'grad_w', 'grad_ssm_log_dt': 'grad_w', 'grad_ssm_b_re': 'grad_w', 'grad_ssm_b_im': 'grad_w', 'grad_ssm_c_re': 'grad_w', 'grad_ssm_c_im': 'grad_w', 'grad_ssm_d': 'grad_w', 'grad_ssm_w_glu': 'grad_w', 'grad_conv_w': 'grad_w', 'grad_conv_w_out': 'grad_w', 'grad_w_o': 'grad_w', 'grad_g_ffn2': 'grad_w', 'grad_ffn2_w_gate': 'grad_w', 'grad_ffn2_w_up': 'grad_w', 'grad_ffn2_w_down': 'grad_w', 'grad_g_final': 'grad_w', 'delta_meta_tokens': 'delta_w', 'delta_g_ffn1': 'delta_w', 'delta_ffn1_w_gate': 'delta_w', 'delta_ffn1_w_up': 'delta_w', 'delta_ffn1_w_down': 'delta_w', 'delta_g_mix': 'delta_w', 'delta_w_in': 'delta_w', 'delta_b_gate': 'delta_w', 'delta_ssm_a_re': 'delta_w', 'delta_ssm_a_im': 'delta_w', 'delta_ssm_log_dt': 'delta_w', 'delta_ssm_b_re': 'delta_w', 'delta_ssm_b_im': 'delta_w', 'delta_ssm_c_re': 'delta_w', 'delta_ssm_c_im': 'delta_w', 'delta_ssm_d': 'delta_w', 'delta_ssm_w_glu': 'delta_w', 'delta_conv_w': 'delta_w', 'delta_conv_w_out': 'delta_w', 'delta_w_o': 'delta_w', 'delta_g_ffn2': 'delta_w', 'delta_ffn2_w_gate': 'delta_w', 'delta_ffn2_w_up': 'delta_w', 'delta_ffn2_w_down': 'delta_w', 'delta_g_final': 'delta_w', 'new_m_meta_tokens': 'new_m', 'new_m_g_ffn1': 'new_m', 'new_m_ffn1_w_gate': 'new_m', 'new_m_ffn1_w_up': 'new_m', 'new_m_ffn1_w_down': 'new_m', 'new_m_g_mix': 'new_m', 'new_m_w_in': 'new_m', 'new_m_b_gate': 'new_m', 'new_m_ssm_a_re': 'new_m', 'new_m_ssm_a_im': 'new_m', 'new_m_ssm_log_dt': 'new_m', 'new_m_ssm_b_re': 'new_m', 'new_m_ssm_b_im': 'new_m', 'new_m_ssm_c_re': 'new_m', 'new_m_ssm_c_im': 'new_m', 'new_m_ssm_d': 'new_m', 'new_m_ssm_w_glu': 'new_m', 'new_m_conv_w': 'new_m', 'new_m_conv_w_out': 'new_m', 'new_m_w_o': 'new_m', 'new_m_g_ffn2': 'new_m', 'new_m_ffn2_w_gate': 'new_m', 'new_m_ffn2_w_up': 'new_m', 'new_m_ffn2_w_down': 'new_m', 'new_m_g_final': 'new_m', 'new_v_meta_tokens': 'new_v', 'new_v_g_ffn1': 'new_v', 'new_v_ffn1_w_gate': 'new_v', 'new_v_ffn1_w_up': 'new_v', 'new_v_ffn1_w_down': 'new_v', 'new_v_g_mix': 'new_v', 'new_v_w_in': 'new_v', 'new_v_b_gate': 'new_v', 'new_v_ssm_a_re': 'new_v', 'new_v_ssm_a_im': 'new_v', 'new_v_ssm_log_dt': 'new_v', 'new_v_ssm_b_re': 'new_v', 'new_v_ssm_b_im': 'new_v', 'new_v_ssm_c_re': 'new_v', 'new_v_ssm_c_im': 'new_v', 'new_v_ssm_d': 'new_v', 'new_v_ssm_w_glu': 'new_v', 'new_v_conv_w': 'new_v', 'new_v_conv_w_out': 'new_v', 'new_v_w_o': 'new_v', 'new_v_g_ffn2': 'new_v', 'new_v_ffn2_w_gate': 'new_v', 'new_v_ffn2_w_up': 'new_v', 'new_v_ffn2_w_down': 'new_v', 'new_v_g_final': 'new_v'}


def _forward(args):
    return _fwd_reference(*[args[k] for k in FWD_PARAMS])


def _output_shape():
    def fwd():
        inp = _fwd_setup_inputs(0)
        return _fwd_reference(*[inp[k] for k in FWD_PARAMS])
    out = _jax.eval_shape(fwd)
    return out.shape, out.dtype

N_MICROBATCH = 1
ADAM_LR = 0.001
ADAM_B1 = 0.9
ADAM_B2 = 0.999
ADAM_EPS = 1e-08
ADAM_WD = 0.01
ADAM_STEP = 10
PER_EXAMPLE_BATCH_AXIS = {'x': 0, 'loss_target': 0}
SHARED_INPUTS = []
_WEIGHT_DTYPES = {'meta_tokens': _jnp.float32, 'g_ffn1': _jnp.float32, 'ffn1_w_gate': _jnp.float32, 'ffn1_w_up': _jnp.float32, 'ffn1_w_down': _jnp.float32, 'g_mix': _jnp.float32, 'w_in': _jnp.float32, 'b_gate': _jnp.float32, 'ssm_a_re': _jnp.float32, 'ssm_a_im': _jnp.float32, 'ssm_log_dt': _jnp.float32, 'ssm_b_re': _jnp.float32, 'ssm_b_im': _jnp.float32, 'ssm_c_re': _jnp.float32, 'ssm_c_im': _jnp.float32, 'ssm_d': _jnp.float32, 'ssm_w_glu': _jnp.float32, 'conv_w': _jnp.float32, 'conv_w_out': _jnp.float32, 'w_o': _jnp.float32, 'g_ffn2': _jnp.float32, 'ffn2_w_gate': _jnp.float32, 'ffn2_w_up': _jnp.float32, 'ffn2_w_down': _jnp.float32, 'g_final': _jnp.float32}
MOMENT_SCALE = {'meta_tokens': 2.774438e-03, 'g_ffn1': 1.291347e-01, 'ffn1_w_gate': 5.530536e-02, 'ffn1_w_up': 5.359230e-02, 'ffn1_w_down': 8.893641e-02, 'g_mix': 2.223860e-01, 'w_in': 1.063078e-01, 'b_gate': 3.540824e-02, 'ssm_a_re': 4.736061e-03, 'ssm_a_im': 4.546629e-03, 'ssm_log_dt': 6.390286e+00, 'ssm_b_re': 2.015922e-03, 'ssm_b_im': 2.014335e-03, 'ssm_c_re': 5.846038e-03, 'ssm_c_im': 5.800146e-03, 'ssm_d': 6.613973e-02, 'ssm_w_glu': 3.006572e-02, 'conv_w': 1.664282e-01, 'conv_w_out': 1.177176e-01, 'w_o': 1.251807e-01, 'g_ffn2': 8.702763e-02, 'ffn2_w_gate': 3.811791e-02, 'ffn2_w_up': 3.689292e-02, 'ffn2_w_down': 6.122928e-02, 'g_final': 6.393000e+01}


def _to_microbatches(a, axis):
    t = _jnp.moveaxis(a, axis, 0)
    t = t.reshape((N_MICROBATCH, t.shape[0] // N_MICROBATCH) + t.shape[1:])
    return _jnp.moveaxis(t, 1, axis + 1)


def setup_inputs(seed: int = 0) -> dict:
    inp = _fwd_setup_inputs(seed)
    key = _jax.random.fold_in(_jax.random.key(seed), 7919)
    shape, _ = _output_shape()
    out = dict(inp)
    out["loss_target"] = _jax.random.normal(_jax.random.fold_in(key, 0), shape, _jnp.float32)
    for i, name in enumerate(TWIN_WEIGHTS):
        w = inp[name].astype(_jnp.float32)
        if MOMENT_SCALE is None:
            s = _jnp.sqrt(_jnp.mean(_jnp.square(w)) + 1e-30)
        else:
            s = MOMENT_SCALE[name]
        km, kv = _jax.random.split(_jax.random.fold_in(key, i + 1))
        out[name] = w
        out["m_" + name] = s * _jax.random.normal(km, w.shape, _jnp.float32)
        out["v_" + name] = (s * s) * _jax.random.uniform(kv, w.shape, _jnp.float32, 0.5, 1.5)
    if N_MICROBATCH > 1:
        for name, axis in PER_EXAMPLE_BATCH_AXIS.items():
            out[name] = _to_microbatches(out[name], axis)
    return {'x': out['x'], 'meta_tokens': out['meta_tokens'], 'g_ffn1': out['g_ffn1'], 'ffn1_w_gate': out['ffn1_w_gate'], 'ffn1_w_up': out['ffn1_w_up'], 'ffn1_w_down': out['ffn1_w_down'], 'g_mix': out['g_mix'], 'w_in': out['w_in'], 'b_gate': out['b_gate'], 'ssm_a_re': out['ssm_a_re'], 'ssm_a_im': out['ssm_a_im'], 'ssm_log_dt': out['ssm_log_dt'], 'ssm_b_re': out['ssm_b_re'], 'ssm_b_im': out['ssm_b_im'], 'ssm_c_re': out['ssm_c_re'], 'ssm_c_im': out['ssm_c_im'], 'ssm_d': out['ssm_d'], 'ssm_w_glu': out['ssm_w_glu'], 'conv_w': out['conv_w'], 'conv_w_out': out['conv_w_out'], 'w_o': out['w_o'], 'g_ffn2': out['g_ffn2'], 'ffn2_w_gate': out['ffn2_w_gate'], 'ffn2_w_up': out['ffn2_w_up'], 'ffn2_w_down': out['ffn2_w_down'], 'g_final': out['g_final'], 'loss_target': out['loss_target'], 'm_meta_tokens': out['m_meta_tokens'], 'm_g_ffn1': out['m_g_ffn1'], 'm_ffn1_w_gate': out['m_ffn1_w_gate'], 'm_ffn1_w_up': out['m_ffn1_w_up'], 'm_ffn1_w_down': out['m_ffn1_w_down'], 'm_g_mix': out['m_g_mix'], 'm_w_in': out['m_w_in'], 'm_b_gate': out['m_b_gate'], 'm_ssm_a_re': out['m_ssm_a_re'], 'm_ssm_a_im': out['m_ssm_a_im'], 'm_ssm_log_dt': out['m_ssm_log_dt'], 'm_ssm_b_re': out['m_ssm_b_re'], 'm_ssm_b_im': out['m_ssm_b_im'], 'm_ssm_c_re': out['m_ssm_c_re'], 'm_ssm_c_im': out['m_ssm_c_im'], 'm_ssm_d': out['m_ssm_d'], 'm_ssm_w_glu': out['m_ssm_w_glu'], 'm_conv_w': out['m_conv_w'], 'm_conv_w_out': out['m_conv_w_out'], 'm_w_o': out['m_w_o'], 'm_g_ffn2': out['m_g_ffn2'], 'm_ffn2_w_gate': out['m_ffn2_w_gate'], 'm_ffn2_w_up': out['m_ffn2_w_up'], 'm_ffn2_w_down': out['m_ffn2_w_down'], 'm_g_final': out['m_g_final'], 'v_meta_tokens': out['v_meta_tokens'], 'v_g_ffn1': out['v_g_ffn1'], 'v_ffn1_w_gate': out['v_ffn1_w_gate'], 'v_ffn1_w_up': out['v_ffn1_w_up'], 'v_ffn1_w_down': out['v_ffn1_w_down'], 'v_g_mix': out['v_g_mix'], 'v_w_in': out['v_w_in'], 'v_b_gate': out['v_b_gate'], 'v_ssm_a_re': out['v_ssm_a_re'], 'v_ssm_a_im': out['v_ssm_a_im'], 'v_ssm_log_dt': out['v_ssm_log_dt'], 'v_ssm_b_re': out['v_ssm_b_re'], 'v_ssm_b_im': out['v_ssm_b_im'], 'v_ssm_c_re': out['v_ssm_c_re'], 'v_ssm_c_im': out['v_ssm_c_im'], 'v_ssm_d': out['v_ssm_d'], 'v_ssm_w_glu': out['v_ssm_w_glu'], 'v_conv_w': out['v_conv_w'], 'v_conv_w_out': out['v_conv_w_out'], 'v_w_o': out['v_w_o'], 'v_g_ffn2': out['v_g_ffn2'], 'v_ffn2_w_gate': out['v_ffn2_w_gate'], 'v_ffn2_w_up': out['v_ffn2_w_up'], 'v_ffn2_w_down': out['v_ffn2_w_down'], 'v_g_final': out['v_g_final']}


def _loss(weights, diff, rest, loss_target):
    with _jax.named_scope("forward"):
        args = {**rest, TWIN_DIFF_INPUT: diff, **{k: w.astype(_WEIGHT_DTYPES[k]) for k, w in weights.items()}}
        y = _forward(args)
    with _jax.named_scope("loss_head"):
        err = _jnp.square(y.astype(_jnp.float32) - loss_target)
        return 0.5 * _jnp.sum(_jnp.mean(err, axis=-1)) if err.ndim else 0.5 * err


def _adamw(w, g, m, v):
    m = ADAM_B1 * m + (1.0 - ADAM_B1) * g
    v = ADAM_B2 * v + (1.0 - ADAM_B2) * _jnp.square(g)
    m_hat = m / (1.0 - ADAM_B1 ** ADAM_STEP)
    v_hat = v / (1.0 - ADAM_B2 ** ADAM_STEP)
    delta = -ADAM_LR * (m_hat / (_jnp.sqrt(v_hat) + ADAM_EPS) + ADAM_WD * w)
    return delta, m, v


def reference(x, meta_tokens, g_ffn1, ffn1_w_gate, ffn1_w_up, ffn1_w_down, g_mix, w_in, b_gate, ssm_a_re, ssm_a_im, ssm_log_dt, ssm_b_re, ssm_b_im, ssm_c_re, ssm_c_im, ssm_d, ssm_w_glu, conv_w, conv_w_out, w_o, g_ffn2, ffn2_w_gate, ffn2_w_up, ffn2_w_down, g_final, loss_target, m_meta_tokens, m_g_ffn1, m_ffn1_w_gate, m_ffn1_w_up, m_ffn1_w_down, m_g_mix, m_w_in, m_b_gate, m_ssm_a_re, m_ssm_a_im, m_ssm_log_dt, m_ssm_b_re, m_ssm_b_im, m_ssm_c_re, m_ssm_c_im, m_ssm_d, m_ssm_w_glu, m_conv_w, m_conv_w_out, m_w_o, m_g_ffn2, m_ffn2_w_gate, m_ffn2_w_up, m_ffn2_w_down, m_g_final, v_meta_tokens, v_g_ffn1, v_ffn1_w_gate, v_ffn1_w_up, v_ffn1_w_down, v_g_mix, v_w_in, v_b_gate, v_ssm_a_re, v_ssm_a_im, v_ssm_log_dt, v_ssm_b_re, v_ssm_b_im, v_ssm_c_re, v_ssm_c_im, v_ssm_d, v_ssm_w_glu, v_conv_w, v_conv_w_out, v_w_o, v_g_ffn2, v_ffn2_w_gate, v_ffn2_w_up, v_ffn2_w_down, v_g_final):
    given = dict(x=x, meta_tokens=meta_tokens, g_ffn1=g_ffn1, ffn1_w_gate=ffn1_w_gate, ffn1_w_up=ffn1_w_up, ffn1_w_down=ffn1_w_down, g_mix=g_mix, w_in=w_in, b_gate=b_gate, ssm_a_re=ssm_a_re, ssm_a_im=ssm_a_im, ssm_log_dt=ssm_log_dt, ssm_b_re=ssm_b_re, ssm_b_im=ssm_b_im, ssm_c_re=ssm_c_re, ssm_c_im=ssm_c_im, ssm_d=ssm_d, ssm_w_glu=ssm_w_glu, conv_w=conv_w, conv_w_out=conv_w_out, w_o=w_o, g_ffn2=g_ffn2, ffn2_w_gate=ffn2_w_gate, ffn2_w_up=ffn2_w_up, ffn2_w_down=ffn2_w_down, g_final=g_final, loss_target=loss_target, m_meta_tokens=m_meta_tokens, m_g_ffn1=m_g_ffn1, m_ffn1_w_gate=m_ffn1_w_gate, m_ffn1_w_up=m_ffn1_w_up, m_ffn1_w_down=m_ffn1_w_down, m_g_mix=m_g_mix, m_w_in=m_w_in, m_b_gate=m_b_gate, m_ssm_a_re=m_ssm_a_re, m_ssm_a_im=m_ssm_a_im, m_ssm_log_dt=m_ssm_log_dt, m_ssm_b_re=m_ssm_b_re, m_ssm_b_im=m_ssm_b_im, m_ssm_c_re=m_ssm_c_re, m_ssm_c_im=m_ssm_c_im, m_ssm_d=m_ssm_d, m_ssm_w_glu=m_ssm_w_glu, m_conv_w=m_conv_w, m_conv_w_out=m_conv_w_out, m_w_o=m_w_o, m_g_ffn2=m_g_ffn2, m_ffn2_w_gate=m_ffn2_w_gate, m_ffn2_w_up=m_ffn2_w_up, m_ffn2_w_down=m_ffn2_w_down, m_g_final=m_g_final, v_meta_tokens=v_meta_tokens, v_g_ffn1=v_g_ffn1, v_ffn1_w_gate=v_ffn1_w_gate, v_ffn1_w_up=v_ffn1_w_up, v_ffn1_w_down=v_ffn1_w_down, v_g_mix=v_g_mix, v_w_in=v_w_in, v_b_gate=v_b_gate, v_ssm_a_re=v_ssm_a_re, v_ssm_a_im=v_ssm_a_im, v_ssm_log_dt=v_ssm_log_dt, v_ssm_b_re=v_ssm_b_re, v_ssm_b_im=v_ssm_b_im, v_ssm_c_re=v_ssm_c_re, v_ssm_c_im=v_ssm_c_im, v_ssm_d=v_ssm_d, v_ssm_w_glu=v_ssm_w_glu, v_conv_w=v_conv_w, v_conv_w_out=v_conv_w_out, v_w_o=v_w_o, v_g_ffn2=v_g_ffn2, v_ffn2_w_gate=v_ffn2_w_gate, v_ffn2_w_up=v_ffn2_w_up, v_ffn2_w_down=v_ffn2_w_down, v_g_final=v_g_final)
    weights = {n: given[n] for n in TWIN_WEIGHTS}
    shared = {n: given[n] for n in SHARED_INPUTS}
    per_example = {n: given[n] for n in ['x']}
    grad_fn = _jax.value_and_grad(_loss, argnums=(0, 1))

    def one_microbatch(ex, loss_target):
        ex = dict(ex)
        diff = ex.pop(TWIN_DIFF_INPUT)
        return grad_fn(weights, diff, {**shared, **ex}, loss_target)

    if N_MICROBATCH == 1:
        loss, (grad_w, grad_x) = one_microbatch(per_example, given["loss_target"])
    else:
        def body(carry, xs):
            loss_sum, grad_sum = carry
            l_k, (gw_k, gx_k) = one_microbatch(xs[0], xs[1])
            with _jax.named_scope("update"):
                return (loss_sum + l_k, _jax.tree.map(_jnp.add, grad_sum, gw_k)), gx_k

        init = (_jnp.zeros((), _jnp.float32), _jax.tree.map(_jnp.zeros_like, weights))
        (loss, grad_w), grad_x = _jax.lax.scan(body, init, (per_example, given["loss_target"]))
    with _jax.named_scope("update"):
        delta_w, new_m, new_v = {}, {}, {}
        for n in TWIN_WEIGHTS:
            delta_w[n], new_m[n], new_v[n] = _adamw(weights[n], grad_w[n], given["m_" + n], given["v_" + n])
    return (loss, grad_x, *[grad_w[n] for n in TWIN_WEIGHTS], *[delta_w[n] for n in TWIN_WEIGHTS],
            *[new_m[n] for n in TWIN_WEIGHTS], *[new_v[n] for n in TWIN_WEIGHTS])
```

```python
import functools

import jax
import jax.numpy as jnp
from jax import lax
from jax.experimental import pallas as pl
from jax.experimental.pallas import tpu as pltpu

F32 = jnp.float32
BF16 = jnp.bfloat16
MESH = pl.DeviceIdType.MESH
AXES = ("x", "y", "c")
NDEV = 8
SLAB_COLS = 1024
RMS_EPS = 1e-6
TOKEN_TILE = 256
MIX_TILE = 128
SUBLANES = 8
SCAN_LANES = 512
FFN_CHUNK = 1024
VMEM_LIMIT_BYTES = 56 * 1024 * 1024

ADAM_LR = 0.001
ADAM_B1 = 0.9
ADAM_B2 = 0.999
ADAM_EPS = 1e-08
ADAM_WD = 0.01
ADAM_STEP = 10

_VM = pl.BlockSpec(memory_space=pltpu.VMEM)
_ANY = pl.BlockSpec(memory_space=pl.ANY)


def _params(sem=("arbitrary",)):
    return pltpu.CompilerParams(dimension_semantics=sem, vmem_limit_bytes=VMEM_LIMIT_BYTES)


def _dot(a, b):
    return jnp.dot(a, b, preferred_element_type=F32)


def _dot_nt(a, b):
    return lax.dot_general(a, b, (((1,), (1,)), ((), ())), preferred_element_type=F32)


def _dot_tn(a, b):
    return lax.dot_general(a, b, (((0,), (0,)), ((), ())), preferred_element_type=F32)


def _chunks(n, step):
    return [(s, min(s + step, n)) for s in range(0, n, step)]


def _all_gather(x_shard, name):
    m_per, n = x_shard.shape

    def body(x_ref, out_ref, send_sems, recv_sems, local_sem):
        x, y, c = lax.axis_index("x"), lax.axis_index("y"), lax.axis_index("c")
        me, sibling = (x, y, c), (x, y, 1 - c)
        chips = [(1 - x, y), (x, 1 - y), (1 - x, 1 - y)]

        def rows(px, py, pc):
            return out_ref.at[pl.ds((4 * px + 2 * py + pc) * m_per, m_per), :]

        def copy(k, block, to, src=None):
            return pltpu.make_async_remote_copy(
                src_ref=rows(*block) if src is None else src, dst_ref=rows(*block),
                send_sem=send_sems.at[k], recv_sem=recv_sems.at[k], device_id=to, device_id_type=MESH)

        mine = pltpu.make_async_copy(x_ref, rows(*me), local_sem)
        mine.start()
        first = [copy(0, me, sibling, src=x_ref)]
        first += [copy(1 + j, me, (*chip, c), src=x_ref) for j, chip in enumerate(chips)]
        for cp in first:
            cp.start()
        passed = [copy(4 + j, (*chip, c), sibling) for j, chip in enumerate(chips)]
        for j, chip in enumerate(chips):
            copy(1 + j, (*chip, c), me).wait_recv()
            passed[j].start()
        copy(0, sibling, me).wait_recv()
        for j, chip in enumerate(chips):
            copy(4 + j, (*chip, 1 - c), me).wait_recv()
        for cp in first + passed:
            cp.wait_send()
        mine.wait()

    out = pl.pallas_call(
        body, name=name,
        out_shape=jax.ShapeDtypeStruct((NDEV * m_per, n), x_shard.dtype),
        in_specs=[_ANY], out_specs=_ANY,
        scratch_shapes=[pltpu.SemaphoreType.DMA((7,)), pltpu.SemaphoreType.DMA((7,)), pltpu.SemaphoreType.DMA],
    )(x_shard)
    return out.reshape(NDEV, m_per, n)


def _rs_pair(gs, name):
    _, r, n = gs.shape

    def body(gs_ref, out_ref, send_sems, recv_sems):
        x, y, c = lax.axis_index("x"), lax.axis_index("y"), lax.axis_index("c")
        copies = []
        for j in range(4):
            cp = pltpu.make_async_remote_copy(
                src_ref=gs_ref.at[2 * j + (1 - c)], dst_ref=out_ref.at[j],
                send_sem=send_sems.at[j], recv_sem=recv_sems.at[j], device_id=(x, y, 1 - c), device_id_type=MESH)
            cp.start()
            copies.append(cp)
        for cp in copies:
            cp.wait()

    return pl.pallas_call(
        body, name=name, out_shape=jax.ShapeDtypeStruct((4, r, n), gs.dtype),
        in_specs=[_ANY], out_specs=_ANY,
        scratch_shapes=[pltpu.SemaphoreType.DMA((4,)), pltpu.SemaphoreType.DMA((4,))],
    )(gs)


def _rs_chips(p, name):
    _, r, n = p.shape

    def body(p_ref, out_ref, send_sems, recv_sems):
        x, y, c = lax.axis_index("x"), lax.axis_index("y"), lax.axis_index("c")
        copies = []
        for k, (px, py) in enumerate([(1 - x, y), (x, 1 - y), (1 - x, 1 - y)]):
            cp = pltpu.make_async_remote_copy(
                src_ref=p_ref.at[2 * px + py], dst_ref=out_ref.at[k],
                send_sem=send_sems.at[k], recv_sem=recv_sems.at[k], device_id=(px, py, c), device_id_type=MESH)
            cp.start()
            copies.append(cp)
        for cp in copies:
            cp.wait()

    return pl.pallas_call(
        body, name=name, out_shape=jax.ShapeDtypeStruct((3, r, n), p.dtype),
        in_specs=[_ANY], out_specs=_ANY,
        scratch_shapes=[pltpu.SemaphoreType.DMA((3,)), pltpu.SemaphoreType.DMA((3,))],
    )(p)


def _slab_tile(rows):
    return TOKEN_TILE if rows % TOKEN_TILE == 0 else rows


def _add_pairs(a, b, name):
    k, r, n = a.shape
    tr = _slab_tile(r)

    def body(a_ref, b_ref, o_ref):
        o_ref[...] = (a_ref[...].astype(F32) + b_ref[...].astype(F32)).astype(o_ref.dtype)

    spec = pl.BlockSpec((1, tr, n), lambda j, i: (j, i, 0))
    return pl.pallas_call(
        body, name=name, grid=(k, r // tr), in_specs=[spec, spec], out_specs=spec,
        out_shape=jax.ShapeDtypeStruct(a.shape, a.dtype), compiler_params=_params(("arbitrary", "arbitrary")),
    )(a, b)


def _adamw(w, m, v, parts, name):
    r, n = w.shape
    tr = _slab_tile(r)
    nparts = len(parts)
    bc1 = 1.0 - ADAM_B1 ** ADAM_STEP
    bc2 = 1.0 - ADAM_B2 ** ADAM_STEP

    def body(*refs):
        w_ref, m_ref, v_ref = refs[:3]
        p_refs = refs[3:3 + nparts]
        g_ref, d_ref, nm_ref, nv_ref = refs[3 + nparts:]
        g = p_refs[0][...].astype(F32)
        for p in p_refs[1:]:
            g = g + p[...].astype(F32)
        nm = ADAM_B1 * m_ref[...] + (1.0 - ADAM_B1) * g
        nv = ADAM_B2 * v_ref[...] + (1.0 - ADAM_B2) * (g * g)
        m_hat = nm / bc1
        v_hat = nv / bc2
        g_ref[...] = g
        d_ref[...] = -ADAM_LR * (m_hat / (jnp.sqrt(v_hat) + ADAM_EPS) + ADAM_WD * w_ref[...])
        nm_ref[...] = nm
        nv_ref[...] = nv

    spec = pl.BlockSpec((tr, n), lambda i: (i, 0))
    out = jax.ShapeDtypeStruct((r, n), F32)
    return pl.pallas_call(
        body, name=name, grid=(r // tr,), in_specs=[spec] * (3 + nparts), out_specs=[spec] * 4,
        out_shape=[out] * 4, compiler_params=_params(),
    )(w, m, v, *parts)


def _rms_parts(h, g):
    r = lax.rsqrt(jnp.mean(h * h, axis=-1, keepdims=True) + RMS_EPS)
    xhat = h * r
    return r, xhat, xhat * g


def _rms_bwd(dn, g, r, xhat):
    dxh = dn * g
    return r * (dxh - xhat * jnp.mean(dxh * xhat, axis=-1, keepdims=True))


def _ffn_fwd(h, g, wg, wu, wd, name):
    t, d = h.shape
    f = wg.shape[1]
    tm = TOKEN_TILE
    chunks = _chunks(f, FFN_CHUNK)

    def body(h_ref, g_ref, wg_ref, wu_ref, wd_ref, o_ref):
        hv = h_ref[...]
        n = _rms_parts(hv, g_ref[...])[2].astype(BF16)
        acc = jnp.zeros((tm, d), F32)
        for s, e in chunks:
            a = _dot(n, wg_ref[:, s:e])
            b = _dot(n, wu_ref[:, s:e])
            acc = acc + _dot((a * jax.nn.sigmoid(a) * b).astype(BF16), wd_ref[s:e, :])
        o_ref[...] = hv + 0.5 * acc

    tile = pl.BlockSpec((tm, d), lambda i: (i, 0))
    return pl.pallas_call(
        body, name=name, grid=(t // tm,), in_specs=[tile, _VM, _VM, _VM, _VM], out_specs=tile,
        out_shape=jax.ShapeDtypeStruct((t, d), F32), compiler_params=_params(),
    )(h, g, wg, wu, wd)


def _ffn_bwd(h, dh_out, g, wg, wu, wd, name):
    t, d = h.shape
    f = wg.shape[1]
    tm = TOKEN_TILE
    chunks = _chunks(f, FFN_CHUNK)

    def body(h_ref, dho_ref, g_ref, wg_ref, wu_ref, wd_ref, dh_ref, dg_ref, n_ref, da_ref, db_ref, s_ref, do_ref):
        @pl.when(pl.program_id(0) == 0)
        def _():
            dg_ref[...] = jnp.zeros_like(dg_ref)

        hv = h_ref[...]
        gv = g_ref[...]
        r, xhat, n32 = _rms_parts(hv, gv)
        n = n32.astype(BF16)
        dho = dho_ref[...]
        do = (0.5 * dho).astype(BF16)
        dn = jnp.zeros((tm, d), F32)
        for s, e in chunks:
            a = _dot(n, wg_ref[:, s:e])
            b = _dot(n, wu_ref[:, s:e])
            sig = jax.nn.sigmoid(a)
            sa = a * sig
            ds = _dot_nt(do, wd_ref[s:e, :])
            da = (ds * b * (sig * (1.0 + a * (1.0 - sig)))).astype(BF16)
            db = (ds * sa).astype(BF16)
            s_ref[:, s:e] = (sa * b).astype(BF16)
            da_ref[:, s:e] = da
            db_ref[:, s:e] = db
            dn = dn + _dot_nt(da, wg_ref[:, s:e]) + _dot_nt(db, wu_ref[:, s:e])
        dh_ref[...] = dho + _rms_bwd(dn, gv, r, xhat)
        dg_ref[...] += jnp.sum(dn * xhat, axis=0, keepdims=True)
        n_ref[...] = n
        do_ref[...] = do

    tile = pl.BlockSpec((tm, d), lambda i: (i, 0))
    wide = pl.BlockSpec((tm, f), lambda i: (i, 0))
    one = pl.BlockSpec((1, d), lambda i: (0, 0))
    return pl.pallas_call(
        body, name=name, grid=(t // tm,),
        in_specs=[tile, tile, _VM, _VM, _VM, _VM],
        out_specs=[tile, one, tile, wide, wide, wide, tile],
        out_shape=[jax.ShapeDtypeStruct((t, d), F32), jax.ShapeDtypeStruct((1, d), F32),
                   jax.ShapeDtypeStruct((t, d), BF16), jax.ShapeDtypeStruct((t, f), BF16),
                   jax.ShapeDtypeStruct((t, f), BF16), jax.ShapeDtypeStruct((t, f), BF16),
                   jax.ShapeDtypeStruct((t, d), BF16)],
        compiler_params=_params(),
    )(h, dh_out, g, wg, wu, wd)


def _head(h, tgt, g, lo, hi, name):
    t, d = h.shape
    tm = TOKEN_TILE

    def body(h_ref, t_ref, g_ref, dh_ref, loss_ref, dg_ref):
        i = pl.program_id(0)

        @pl.when(i == 0)
        def _():
            loss_ref[...] = jnp.zeros_like(loss_ref)
            dg_ref[...] = jnp.zeros_like(dg_ref)

        gv = g_ref[...]
        r, xhat, y = _rms_parts(h_ref[...], gv)
        row = i * tm + lax.broadcasted_iota(jnp.int32, (tm, 1), 0)
        err = jnp.where((row >= lo) & (row < hi), y - t_ref[...], 0.0)
        loss_ref[...] += jnp.full(loss_ref.shape, 0.5 * jnp.sum(jnp.mean(err * err, axis=-1, keepdims=True)), F32)
        dy = err * (1.0 / d)
        dg_ref[...] += jnp.sum(dy * xhat, axis=0, keepdims=True)
        dh_ref[...] = _rms_bwd(dy, gv, r, xhat)

    tile = pl.BlockSpec((tm, d), lambda i: (i, 0))
    return pl.pallas_call(
        body, name=name, grid=(t // tm,), in_specs=[tile, tile, _VM],
        out_specs=[tile, pl.BlockSpec((SUBLANES, 128), lambda i: (0, 0)), pl.BlockSpec((1, d), lambda i: (0, 0))],
        out_shape=[jax.ShapeDtypeStruct((t, d), F32), jax.ShapeDtypeStruct((SUBLANES, 128), F32),
                   jax.ShapeDtypeStruct((1, d), F32)],
        compiler_params=_params(),
    )(h, tgt, g)


def _dw(a, b, name, bn):
    t, m = a.shape
    n = b.shape[1]
    tk = next(k for k in (768, 512, 256, t) if t % k == 0)
    bn = min(bn, n)

    def body(a_ref, b_ref, o_ref):
        @pl.when(pl.program_id(1) == 0)
        def _():
            o_ref[...] = jnp.zeros_like(o_ref)

        o_ref[...] += _dot_tn(a_ref[...], b_ref[...])

    return pl.pallas_call(
        body, name=name, grid=(n // bn, t // tk),
        in_specs=[pl.BlockSpec((tk, m), lambda j, k: (k, 0)), pl.BlockSpec((tk, bn), lambda j, k: (k, j))],
        out_specs=pl.BlockSpec((m, bn), lambda j, k: (0, j)),
        out_shape=jax.ShapeDtypeStruct((m, n), F32), compiler_params=_params(("arbitrary", "arbitrary")),
    )(a, b)


def _s5_discretise(a_re, a_im, log_dt, b_re, b_im):
    dt = jnp.exp(log_dt)
    mag = jnp.exp(a_re * dt)
    lam_re = mag * jnp.cos(a_im * dt)
    lam_im = mag * jnp.sin(a_im * dt)
    den = a_re * a_re + a_im * a_im
    q_re = ((lam_re - 1.0) * a_re + lam_im * a_im) / den
    q_im = (lam_im * a_re - (lam_re - 1.0) * a_im) / den
    bb_re = q_re[:, None, :] * b_re - q_im[:, None, :] * b_im
    bb_im = q_re[:, None, :] * b_im + q_im[:, None, :] * b_re
    return lam_re, lam_im, bb_re, bb_im


def _s5_params_fwd(a_re, a_im, log_dt, b_re, b_im):
    g, p = a_re.shape
    c = b_re.shape[1]

    def body(are_ref, aim_ref, ldt_ref, bre_ref, bim_ref, pwr_ref, pwi_ref, bbr_ref, bbi_ref):
        lr, li, bbr, bbi = _s5_discretise(are_ref[...], aim_ref[...], ldt_ref[...], bre_ref[...], bim_ref[...])
        bbr_ref[...] = bbr
        bbi_ref[...] = bbi
        pr, pi = lr, li
        pwr_ref[0] = pr
        pwi_ref[0] = pi
        for k in range(1, SUBLANES):
            pr, pi = pr * lr - pi * li, pr * li + pi * lr
            pwr_ref[k] = pr
            pwi_ref[k] = pi

    return pl.pallas_call(
        body, name="s5_params_fwd",
        out_shape=[jax.ShapeDtypeStruct((SUBLANES, g, p), F32), jax.ShapeDtypeStruct((SUBLANES, g, p), F32),
                   jax.ShapeDtypeStruct((g, c, p), F32), jax.ShapeDtypeStruct((g, c, p), F32)],
    )(a_re, a_im, log_dt, b_re, b_im)


def _s5_params_bwd(a_re, a_im, log_dt, b_re, b_im, dlam, dbb_re, dbb_im):
    g, p = a_re.shape
    c = b_re.shape[1]

    def body(are_ref, aim_ref, ldt_ref, bre_ref, bim_ref, dlam_ref, dbr_ref, dbi_ref,
             dare_ref, daim_ref, dldt_ref, dbre_ref, dbim_ref):
        dlr = jnp.sum(dlam_ref[0], axis=0)
        dli = jnp.sum(dlam_ref[1], axis=0)
        _, vjp = jax.vjp(_s5_discretise, are_ref[...], aim_ref[...], ldt_ref[...], bre_ref[...], bim_ref[...])
        dare, daim, dldt, dbre, dbim = vjp((dlr, dli, dbr_ref[...], dbi_ref[...]))
        dare_ref[...] = dare
        daim_ref[...] = daim
        dldt_ref[...] = dldt
        dbre_ref[...] = dbre
        dbim_ref[...] = dbim

    return pl.pallas_call(
        body, name="s5_params_bwd",
        out_shape=[jax.ShapeDtypeStruct((g, p), F32), jax.ShapeDtypeStruct((g, p), F32),
                   jax.ShapeDtypeStruct((g, 1), F32), jax.ShapeDtypeStruct((g, c, p), F32),
                   jax.ShapeDtypeStruct((g, c, p), F32)],
    )(a_re, a_im, log_dt, b_re, b_im, dlam, dbb_re, dbb_im)


def _scan_chunks(gp):
    hg = gp // 2
    w = min(SCAN_LANES, hg)
    return w, [(half * hg + k * w, half * gp + k * w, half * gp + hg + k * w) for half in range(2) for k in range(hg // w)]


def _cmul_acc(xr, xi, tr, ti, sr, si):
    return xr + tr * sr - ti * si, xi + tr * si + ti * sr


def _scan_fwd(buf_ref, row0, tm, ltab_ref, cin_ref, cout_ref, gp):
    w, chunks = _scan_chunks(gp)
    for lo_t, lo_r, lo_i in chunks:
        def body(r, carry, lo_t=lo_t, lo_r=lo_r, lo_i=lo_i):
            cr, ci = carry
            row = pl.multiple_of(row0 + r * SUBLANES, SUBLANES)
            xr = buf_ref[pl.ds(row, SUBLANES), lo_r:lo_r + w]
            xi = buf_ref[pl.ds(row, SUBLANES), lo_i:lo_i + w]
            for tab, shift in ((0, 1), (2, 2), (4, 4)):
                xr, xi = _cmul_acc(xr, xi, ltab_ref[tab, :, lo_t:lo_t + w], ltab_ref[tab + 1, :, lo_t:lo_t + w],
                                   pltpu.roll(xr, shift, 0), pltpu.roll(xi, shift, 0))
            xr, xi = _cmul_acc(xr, xi, ltab_ref[6, :, lo_t:lo_t + w], ltab_ref[7, :, lo_t:lo_t + w], cr, ci)
            buf_ref[pl.ds(row, SUBLANES), lo_r:lo_r + w] = xr
            buf_ref[pl.ds(row, SUBLANES), lo_i:lo_i + w] = xi
            last = SUBLANES - 1
            return (jnp.broadcast_to(xr[last:last + 1], (SUBLANES, w)), jnp.broadcast_to(xi[last:last + 1], (SUBLANES, w)))

        cr, ci = lax.fori_loop(0, tm // SUBLANES, body,
                               (cin_ref[0:SUBLANES, lo_r:lo_r + w], cin_ref[0:SUBLANES, lo_i:lo_i + w]))
        if cout_ref is not None:
            cout_ref[0:SUBLANES, lo_r:lo_r + w] = cr
            cout_ref[0:SUBLANES, lo_i:lo_i + w] = ci


def _scan_rev(g_ref, hext_ref, tm, ltab_ref, gc_ref, dlam_ref, gp):
    w, chunks = _scan_chunks(gp)
    nb = tm // SUBLANES
    for lo_t, lo_r, lo_i in chunks:
        def body(k, carry, lo_t=lo_t, lo_r=lo_r, lo_i=lo_i):
            cr, ci, ar, ai = carry
            row = pl.multiple_of((nb - 1 - k) * SUBLANES, SUBLANES)
            xr = g_ref[pl.ds(row, SUBLANES), lo_r:lo_r + w]
            xi = g_ref[pl.ds(row, SUBLANES), lo_i:lo_i + w]
            for tab, shift in ((8, 7), (10, 6), (12, 4)):
                xr, xi = _cmul_acc(xr, xi, ltab_ref[tab, :, lo_t:lo_t + w], ltab_ref[tab + 1, :, lo_t:lo_t + w],
                                   pltpu.roll(xr, shift, 0), pltpu.roll(xi, shift, 0))
            xr, xi = _cmul_acc(xr, xi, ltab_ref[14, :, lo_t:lo_t + w], ltab_ref[15, :, lo_t:lo_t + w], cr, ci)
            g_ref[pl.ds(row, SUBLANES), lo_r:lo_r + w] = xr
            g_ref[pl.ds(row, SUBLANES), lo_i:lo_i + w] = xi
            first = lax.broadcasted_iota(jnp.int32, (SUBLANES, w), 0) == 0
            prev = pl.ds(row, SUBLANES)
            here = pl.ds(row + SUBLANES, SUBLANES)
            hpr = jnp.where(first, pltpu.roll(hext_ref[prev, lo_r:lo_r + w], 1, 0), pltpu.roll(hext_ref[here, lo_r:lo_r + w], 1, 0))
            hpi = jnp.where(first, pltpu.roll(hext_ref[prev, lo_i:lo_i + w], 1, 0), pltpu.roll(hext_ref[here, lo_i:lo_i + w], 1, 0))
            ar = ar + xr * hpr + xi * hpi
            ai = ai - xr * hpi + xi * hpr
            return (jnp.broadcast_to(xr[0:1], (SUBLANES, w)), jnp.broadcast_to(xi[0:1], (SUBLANES, w)), ar, ai)

        cr, ci, ar, ai = lax.fori_loop(
            0, nb, body, (gc_ref[:, lo_r:lo_r + w], gc_ref[:, lo_i:lo_i + w], dlam_ref[:, lo_r:lo_r + w], dlam_ref[:, lo_i:lo_i + w]))
        gc_ref[:, lo_r:lo_r + w] = cr
        gc_ref[:, lo_i:lo_i + w] = ci
        dlam_ref[:, lo_r:lo_r + w] = ar
        dlam_ref[:, lo_i:lo_i + w] = ai


def _mix_math(h, gm, win_ref, bg, bc_ref, cc_ref, dsk, wglu_ref, cw, wco_ref, ltab_ref,
              hbuf_ref, hrow0, st_in_ref, st_out_ref, cext_ref, dims):
    d, ds, dc, gp = dims
    tm = h.shape[0]
    dsh = ds // 2
    r, xhat, n32 = _rms_parts(h, gm)
    u = n32.astype(BF16)
    o1, o2, o3 = ds + dc, ds + 2 * dc, ds + 3 * dc
    us = _dot(u, win_ref[:, 0:ds])
    v = _dot(u, win_ref[:, ds:o1])
    gb = _dot(u, win_ref[:, o1:o2])
    gcv = _dot(u, win_ref[:, o2:o3])
    gs = jax.nn.sigmoid(_dot(u, win_ref[:, o3:o3 + d]) + bg[:, 0:d])
    gcg = jax.nn.sigmoid(_dot(u, win_ref[:, o3 + d:o3 + 2 * d]) + bg[:, d:2 * d])
    us16 = us.astype(BF16)
    rows = slice(hrow0, hrow0 + tm)
    for half in range(2):
        hbuf_ref[rows, half * gp:(half + 1) * gp] = _dot(us16[:, half * dsh:(half + 1) * dsh], bc_ref[half])
    _scan_fwd(hbuf_ref, hrow0, tm, ltab_ref, st_in_ref, st_out_ref, gp)
    y5 = jnp.concatenate(
        [_dot(hbuf_ref[rows, half * gp:(half + 1) * gp].astype(BF16), cc_ref[half]) for half in range(2)], axis=1) + dsk * us
    ge16 = jax.nn.gelu(y5).astype(BF16)
    z = _dot(ge16, wglu_ref[...])
    z1, sz = z[:, 0:d], jax.nn.sigmoid(z[:, d:2 * d])
    ys = z1 * sz
    cin = gcv * v
    cext_ref[SUBLANES:SUBLANES + tm, :] = cin
    cv = cw[0:1] * cext_ref[SUBLANES - 2:SUBLANES - 2 + tm, :] + cw[1:2] * cext_ref[SUBLANES - 1:SUBLANES - 1 + tm, :] + cw[2:3] * cin
    cg16 = (gb * cv).astype(BF16)
    yc = _dot(cg16, wco_ref[...])
    mixed = gs * ys + gcg * yc
    return dict(r=r, xhat=xhat, u=u, us=us, us16=us16, v=v, gb=gb, gcv=gcv, gs=gs, gcg=gcg, y5=y5, ge16=ge16,
                z1=z1, sz=sz, ys=ys, cin=cin, cv=cv, cg16=cg16, yc=yc, mixed=mixed)


def _mix_fwd(h, gm, win, bg, bc, cc, dsk, wglu, cw, wco, wo, ltab, dims):
    d, ds, dc, gp = dims
    t = h.shape[0]
    tm = MIX_TILE
    nt = t // tm

    def body(h_ref, gm_ref, win_ref, bg_ref, bc_ref, cc_ref, dsk_ref, wglu_ref, cw_ref, wco_ref, wo_ref, ltab_ref,
             h2_ref, st_ref, cvs_ref, hbuf_ref, carry_ref, cext_ref):
        @pl.when(pl.program_id(0) == 0)
        def _():
            carry_ref[...] = jnp.zeros_like(carry_ref)
            cext_ref[0:SUBLANES, :] = jnp.zeros((SUBLANES, dc), F32)

        st_ref[0] = carry_ref[...]
        cvs_ref[0] = cext_ref[0:SUBLANES, :]
        hv = h_ref[...]
        m = _mix_math(hv, gm_ref[...], win_ref, bg_ref[...], bc_ref, cc_ref, dsk_ref[...], wglu_ref, cw_ref[...], wco_ref,
                      ltab_ref, hbuf_ref, 0, carry_ref, carry_ref, cext_ref, dims)
        h2_ref[...] = hv + _dot(m["mixed"].astype(BF16), wo_ref[...])
        cext_ref[0:SUBLANES, :] = cext_ref[tm:tm + SUBLANES, :]

    tile = pl.BlockSpec((tm, d), lambda i: (i, 0))
    return pl.pallas_call(
        body, name="mix_fwd", grid=(nt,),
        in_specs=[tile] + [_VM] * 11,
        out_specs=[tile, pl.BlockSpec((1, SUBLANES, 2 * gp), lambda i: (i, 0, 0)), pl.BlockSpec((1, SUBLANES, dc), lambda i: (i, 0, 0))],
        out_shape=[jax.ShapeDtypeStruct((t, d), F32), jax.ShapeDtypeStruct((nt, SUBLANES, 2 * gp), F32),
                   jax.ShapeDtypeStruct((nt, SUBLANES, dc), F32)],
        scratch_shapes=[pltpu.VMEM((tm, 2 * gp), F32), pltpu.VMEM((SUBLANES, 2 * gp), F32), pltpu.VMEM((SUBLANES + tm, dc), F32)],
        compiler_params=_params(),
    )(h, gm, win, bg, bc, cc, dsk, wglu, cw, wco, wo, ltab)


def _mix_bwd(h, dh2, st, cvs, gm, win, bg, bc, cc, dsk, wglu, cw, wco, wo, ltab, dims):
    d, ds, dc, gp = dims
    t = h.shape[0]
    tm = MIX_TILE
    nt = t // tm
    dsh = ds // 2
    ncols = ds + 3 * dc + 2 * d

    def body(h_ref, dh2_ref, st_ref, cvs_ref, gm_ref, win_ref, bg_ref, bc_ref, cc_ref, dsk_ref, wglu_ref, cw_ref, wco_ref,
             wo_ref, ltab_ref,
             dh1_ref, u_ref, dp_ref, ge_ref, dz_ref, cg_ref, dyc_ref, mx_ref, dh216_ref,
             dgm_ref, dbg_ref, ddsk_ref, dcw_ref, dlam_ref, dbc_ref, dcc_ref,
             hext_ref, gbuf_ref, gcarry_ref, cext_ref, dcvext_ref):
        @pl.when(pl.program_id(0) == 0)
        def _():
            for ref in (dgm_ref, dbg_ref, ddsk_ref, dcw_ref, dlam_ref, dbc_ref, dcc_ref, gcarry_ref):
                ref[...] = jnp.zeros_like(ref)
            dcvext_ref[tm:tm + SUBLANES, :] = jnp.zeros((SUBLANES, dc), F32)

        hext_ref[0:SUBLANES, :] = st_ref[0]
        cext_ref[0:SUBLANES, :] = cvs_ref[0]
        hv = h_ref[...]
        gmv = gm_ref[...]
        cw_v = cw_ref[...]
        dskv = dsk_ref[...]
        m = _mix_math(hv, gmv, win_ref, bg_ref[...], bc_ref, cc_ref, dskv, wglu_ref, cw_v, wco_ref,
                      ltab_ref, hext_ref, SUBLANES, hext_ref, None, cext_ref, dims)
        dh2v = dh2_ref[...]
        dh216 = dh2v.astype(BF16)
        dmixed = _dot_nt(dh216, wo_ref[...])
        gs, gcg, ys, yc, sz = m["gs"], m["gcg"], m["ys"], m["yc"], m["sz"]
        dys = dmixed * gs
        dyc16 = (dmixed * gcg).astype(BF16)
        dpgs = dmixed * ys * gs * (1.0 - gs)
        dpgc = dmixed * yc * gcg * (1.0 - gcg)
        dz16 = jnp.concatenate([dys * sz, dys * m["z1"] * sz * (1.0 - sz)], axis=1).astype(BF16)
        dge = _dot_nt(dz16, wglu_ref[...])
        dy5 = jax.vjp(jax.nn.gelu, m["y5"])[1](dge)[0]
        dy516 = dy5.astype(BF16)
        for half in range(2):
            gbuf_ref[:, half * gp:(half + 1) * gp] = _dot_nt(dy516[:, half * dsh:(half + 1) * dsh], cc_ref[half])
        _scan_rev(gbuf_ref, hext_ref, tm, ltab_ref, gcarry_ref, dlam_ref, gp)
        dus = []
        for half in range(2):
            g16 = gbuf_ref[:, half * gp:(half + 1) * gp].astype(BF16)
            dus.append(_dot_nt(g16, bc_ref[half]))
            dbc_ref[half] += _dot_tn(m["us16"][:, half * dsh:(half + 1) * dsh], g16)
            h16 = hext_ref[SUBLANES:SUBLANES + tm, half * gp:(half + 1) * gp].astype(BF16)
            dcc_ref[half] += _dot_tn(h16, dy516[:, half * dsh:(half + 1) * dsh])
        dus = jnp.concatenate(dus, axis=1) + dskv * dy5
        ddsk_ref[...] += jnp.sum(dy5 * m["us"], axis=0, keepdims=True)
        dcg = _dot_nt(dyc16, wco_ref[...])
        dgb = dcg * m["cv"]
        dcv = dcg * m["gb"]
        dcvext_ref[0:tm, :] = dcv
        dcin = cw_v[2:3] * dcv + cw_v[1:2] * dcvext_ref[1:1 + tm, :] + cw_v[0:1] * dcvext_ref[2:2 + tm, :]
        dcw_ref[0:1, :] += jnp.sum(dcv * cext_ref[SUBLANES - 2:SUBLANES - 2 + tm, :], axis=0, keepdims=True)
        dcw_ref[1:2, :] += jnp.sum(dcv * cext_ref[SUBLANES - 1:SUBLANES - 1 + tm, :], axis=0, keepdims=True)
        dcw_ref[2:3, :] += jnp.sum(dcv * m["cin"], axis=0, keepdims=True)
        dcvext_ref[tm:tm + SUBLANES, :] = dcvext_ref[0:SUBLANES, :]
        dp16 = jnp.concatenate([dus, dcin * m["gcv"], dgb, dcin * m["v"], dpgs, dpgc], axis=1).astype(BF16)
        du = _dot_nt(dp16, win_ref[...])
        dh1_ref[...] = dh2v + _rms_bwd(du, gmv, m["r"], m["xhat"])
        dgm_ref[...] += jnp.sum(du * m["xhat"], axis=0, keepdims=True)
        dbg_ref[...] += jnp.concatenate([jnp.sum(dpgs, axis=0, keepdims=True), jnp.sum(dpgc, axis=0, keepdims=True)], axis=1)
        u_ref[...] = m["u"]
        dp_ref[...] = dp16
        ge_ref[...] = m["ge16"]
        dz_ref[...] = dz16
        cg_ref[...] = m["cg16"]
        dyc_ref[...] = dyc16
        mx_ref[...] = m["mixed"].astype(BF16)
        dh216_ref[...] = dh216

    def rev(cols):
        return pl.BlockSpec((tm, cols), lambda j: (nt - 1 - j, 0))

    def rev3(cols):
        return pl.BlockSpec((1, SUBLANES, cols), lambda j: (nt - 1 - j, 0, 0))

    def bf(cols):
        return jax.ShapeDtypeStruct((t, cols), BF16)

    return pl.pallas_call(
        body, name="mix_bwd", grid=(nt,),
        in_specs=[rev(d), rev(d), rev3(2 * gp), rev3(dc)] + [_VM] * 11,
        out_specs=[rev(d), rev(d), rev(ncols), rev(ds), rev(2 * d), rev(dc), rev(d), rev(d), rev(d)] + [_VM] * 7,
        out_shape=[jax.ShapeDtypeStruct((t, d), F32), bf(d), bf(ncols), bf(ds), bf(2 * d), bf(dc), bf(d), bf(d), bf(d),
                   jax.ShapeDtypeStruct((1, d), F32), jax.ShapeDtypeStruct((1, 2 * d), F32), jax.ShapeDtypeStruct((1, ds), F32),
                   jax.ShapeDtypeStruct((SUBLANES, dc), F32), jax.ShapeDtypeStruct((SUBLANES, 2 * gp), F32),
                   jax.ShapeDtypeStruct((2, dsh, gp), F32), jax.ShapeDtypeStruct((2, gp, dsh), F32)],
        scratch_shapes=[pltpu.VMEM((SUBLANES + tm, 2 * gp), F32), pltpu.VMEM((tm, 2 * gp), F32),
                        pltpu.VMEM((SUBLANES, 2 * gp), F32), pltpu.VMEM((SUBLANES + tm, dc), F32),
                        pltpu.VMEM((tm + SUBLANES, dc), F32)],
        compiler_params=_params(),
    )(h, dh2, st, cvs, gm, win, bg, bc, cc, dsk, wglu, cw, wco, wo, ltab)


def _pad_rows(a, rows):
    return jnp.pad(a, ((0, rows - a.shape[0]), (0, 0)))


def _as_rows(a):
    flat = a.reshape(-1)
    n = -(-flat.shape[0] // SLAB_COLS) * SLAB_COLS
    return jnp.pad(flat, (0, n - flat.shape[0])).reshape(-1, SLAB_COLS)


def _slab_rows(n):
    return -(-n // TOKEN_TILE) * TOKEN_TILE if n > TOKEN_TILE else -(-n // 16) * 16


def _pack(arrs):
    rows = jnp.concatenate([_as_rows(a) for a in arrs], axis=0)
    return _pad_rows(rows, _slab_rows(rows.shape[0]))


def _unpack(slab, shapes):
    out, r = [], 0
    for shp in shapes:
        size = 1
        for s in shp:
            size *= s
        n = -(-size // SLAB_COLS)
        out.append(slab[r:r + n].reshape(-1)[:size].reshape(shp))
        r += n
    return out


def _block_diag(blocks):
    n, a, b = blocks.shape
    eye = jnp.eye(n, dtype=blocks.dtype)
    return (blocks[:, :, None, :] * eye[:, None, :, None]).reshape(n * a, n * b)


def _diag_blocks(mat, n):
    a, b = mat.shape[0] // n, mat.shape[1] // n
    m4 = mat.reshape(n, a, n, b)
    return jnp.stack([m4[i, :, i, :] for i in range(n)], axis=0)


BIG = (("ffn1_w_gate", "col"), ("ffn1_w_up", "col"), ("ffn1_w_down", "row"), ("w_in", "col"), ("ssm_w_glu", "col"),
       ("conv_w_out", "col"), ("w_o", "row"), ("ffn2_w_gate", "col"), ("ffn2_w_up", "col"), ("ffn2_w_down", "row"))
REPLICATED = ("g_ffn1", "g_mix", "b_gate", "ssm_a_re", "ssm_a_im", "ssm_log_dt", "ssm_b_re", "ssm_b_im", "ssm_c_re",
              "ssm_c_im", "ssm_d", "g_ffn2", "g_final")
WEIGHTS = ("meta_tokens", "g_ffn1", "ffn1_w_gate", "ffn1_w_up", "ffn1_w_down", "g_mix", "w_in", "b_gate", "ssm_a_re",
           "ssm_a_im", "ssm_log_dt", "ssm_b_re", "ssm_b_im", "ssm_c_re", "ssm_c_im", "ssm_d", "ssm_w_glu", "conv_w",
           "conv_w_out", "w_o", "g_ffn2", "ffn2_w_gate", "ffn2_w_up", "ffn2_w_down", "g_final")


def _full_from_blocks(blocks, kind):
    n, r, c = blocks.shape
    if kind == "col":
        return jnp.transpose(blocks, (1, 0, 2)).reshape(r, n * c)
    return blocks.reshape(n * r, c)


def _blocks_from_full(full, kind):
    if kind == "col":
        r, nc = full.shape
        return jnp.transpose(full.reshape(r, NDEV, nc // NDEV), (1, 0, 2))
    nr, c = full.shape
    return full.reshape(NDEV, nr // NDEV, c)


def kernel(x, meta_tokens, g_ffn1, ffn1_w_gate, ffn1_w_up, ffn1_w_down, g_mix, w_in, b_gate, ssm_a_re, ssm_a_im, ssm_log_dt, ssm_b_re, ssm_b_im, ssm_c_re, ssm_c_im, ssm_d, ssm_w_glu, conv_w, conv_w_out, w_o, g_ffn2, ffn2_w_gate, ffn2_w_up, ffn2_w_down, g_final, loss_target, m_meta_tokens, m_g_ffn1, m_ffn1_w_gate, m_ffn1_w_up, m_ffn1_w_down, m_g_mix, m_w_in, m_b_gate, m_ssm_a_re, m_ssm_a_im, m_ssm_log_dt, m_ssm_b_re, m_ssm_b_im, m_ssm_c_re, m_ssm_c_im, m_ssm_d, m_ssm_w_glu, m_conv_w, m_conv_w_out, m_w_o, m_g_ffn2, m_ffn2_w_gate, m_ffn2_w_up, m_ffn2_w_down, m_g_final, v_meta_tokens, v_g_ffn1, v_ffn1_w_gate, v_ffn1_w_up, v_ffn1_w_down, v_g_mix, v_w_in, v_b_gate, v_ssm_a_re, v_ssm_a_im, v_ssm_log_dt, v_ssm_b_re, v_ssm_b_im, v_ssm_c_re, v_ssm_c_im, v_ssm_d, v_ssm_w_glu, v_conv_w, v_conv_w_out, v_w_o, v_g_ffn2, v_ffn2_w_gate, v_ffn2_w_up, v_ffn2_w_down, v_g_final):
    args = dict(locals())
    w = {n: args[n] for n in WEIGHTS}
    mom_m = {n: args["m_" + n] for n in WEIGHTS}
    mom_v = {n: args["v_" + n] for n in WEIGHTS}

    seq, d = x.shape[1], x.shape[2]
    n_meta = meta_tokens.shape[0]
    ds = ssm_d.shape[1]
    n_grp, n_state = ssm_a_re.shape[1], ssm_a_re.shape[2]
    grp_ch = ds // n_grp
    gp = n_grp * n_state
    dc = conv_w.shape[3] * NDEV
    dims = (d, ds, dc, gp)
    t_real = n_meta + seq
    t_pad = -(-t_real // TOKEN_TILE) * TOKEN_TILE
    if t_pad % 768 and t_pad > 768:
        t_pad = -(-t_real // 768) * 768
    me_chip = 2 * lax.axis_index("x") + lax.axis_index("y")
    me_core = lax.axis_index("c")
    me = 2 * me_chip + me_core

    big_shapes = [w[n].shape[1:] for n, _ in BIG]
    wslab = _pack([w[n][0].astype(BF16) for n, _ in BIG])
    wall = _all_gather(wslab, "gather_weights")
    per_dev = [_unpack(wall[b], big_shapes) for b in range(NDEV)]
    full = {n: _full_from_blocks(jnp.stack([per_dev[b][i] for b in range(NDEV)]), kind) for i, (n, kind) in enumerate(BIG)}

    small_shapes = [meta_tokens.shape, conv_w.shape[1:]]
    sall = _all_gather(_pack([meta_tokens, conv_w[0]]), "gather_small")
    sper = [_unpack(sall[b], small_shapes) for b in range(NDEV)]
    meta_full = _full_from_blocks(jnp.stack([sper[b][0] for b in range(NDEV)]), "col")
    cw_full = _full_from_blocks(jnp.stack([sper[b][1].reshape(3, -1) for b in range(NDEV)]), "col")
    cw_rows = _pad_rows(cw_full, SUBLANES)

    a_re, a_im, ldt = ssm_a_re[0], ssm_a_im[0], ssm_log_dt[0].reshape(n_grp, 1)
    b_re_t = jnp.transpose(ssm_b_re[0], (0, 2, 1))
    b_im_t = jnp.transpose(ssm_b_im[0], (0, 2, 1))
    pw_r, pw_i, bb_r, bb_i = _s5_params_fwd(a_re, a_im, ldt, b_re_t, b_im_t)
    pw_r = pw_r.reshape(SUBLANES, gp)
    pw_i = pw_i.reshape(SUBLANES, gp)
    sub = jnp.arange(SUBLANES)[:, None]

    def fwd_tab(p, k):
        return jnp.where(sub >= k, p[k - 1][None, :], 0.0)

    def rev_tab(p, k):
        return jnp.where(sub <= SUBLANES - 1 - k, p[k - 1][None, :], 0.0)

    ltab = jnp.stack(
        [fwd_tab(pw_r, 1), fwd_tab(pw_i, 1), fwd_tab(pw_r, 2), fwd_tab(pw_i, 2), fwd_tab(pw_r, 4), fwd_tab(pw_i, 4), pw_r, pw_i,
         rev_tab(pw_r, 1), -rev_tab(pw_i, 1), rev_tab(pw_r, 2), -rev_tab(pw_i, 2), rev_tab(pw_r, 4), -rev_tab(pw_i, 4),
         pw_r[::-1], -pw_i[::-1]], axis=0)
    gh = n_grp // 2
    bc = jnp.stack([jnp.concatenate([_block_diag(bb_r[h * gh:(h + 1) * gh]), _block_diag(bb_i[h * gh:(h + 1) * gh])], axis=1)
                    for h in range(2)]).astype(BF16)
    c_re_t = jnp.transpose(ssm_c_re[0], (0, 2, 1))
    c_im_t = jnp.transpose(ssm_c_im[0], (0, 2, 1))
    cc = jnp.stack([jnp.concatenate([_block_diag(c_re_t[h * gh:(h + 1) * gh]), -_block_diag(c_im_t[h * gh:(h + 1) * gh])], axis=0)
                    for h in range(2)]).astype(BF16)

    zpad = jnp.zeros((t_pad - t_real, d), F32)
    h0 = jnp.concatenate([meta_full, x[0], zpad], axis=0)
    tgt = jnp.concatenate([jnp.zeros((n_meta, d), F32), loss_target[0], zpad], axis=0)
    h1 = _ffn_fwd(h0, g_ffn1, full["ffn1_w_gate"], full["ffn1_w_up"], full["ffn1_w_down"], "ffn1_fwd")
    mix_w = (g_mix, full["w_in"], b_gate, bc, cc, ssm_d, full["ssm_w_glu"], cw_rows, full["conv_w_out"], full["w_o"], ltab)
    h2, st, cvs = _mix_fwd(h1, *mix_w, dims)
    h3 = _ffn_fwd(h2, g_ffn2, full["ffn2_w_gate"], full["ffn2_w_up"], full["ffn2_w_down"], "ffn2_fwd")
    dh3, loss_blk, dg_final = _head(h3, tgt, g_final.reshape(1, d), n_meta, t_real, "loss_head")
    loss = lax.psum(loss_blk[0, 0], AXES)

    dh2, dg_ffn2, n2, da2, db2, s2, do2 = _ffn_bwd(h2, dh3, g_ffn2, full["ffn2_w_gate"], full["ffn2_w_up"], full["ffn2_w_down"], "ffn2_bwd")
    (dh1, u16, dp16, ge16, dz16, cg16, dyc16, mx16, dh216, dg_mix, dbg, ddsk, dcw, dlam, dbc, dcc) = _mix_bwd(h1, dh2, st, cvs, *mix_w, dims)
    dh0, dg_ffn1, n1, da1, db1, s1, do1 = _ffn_bwd(h0, dh1, g_ffn1, full["ffn1_w_gate"], full["ffn1_w_up"], full["ffn1_w_down"], "ffn1_bwd")

    dfull = {
        "ffn1_w_gate": _dw(n1, da1, "dw_ffn1_gate", 1408), "ffn1_w_up": _dw(n1, db1, "dw_ffn1_up", 1408),
        "ffn1_w_down": _dw(s1, do1, "dw_ffn1_down", 512),
        "w_in": _dw(u16, dp16, "dw_in", 1024), "ssm_w_glu": _dw(ge16, dz16, "dw_glu", 1024),
        "conv_w_out": _dw(cg16, dyc16, "dw_conv_out", 1024), "w_o": _dw(mx16, dh216, "dw_o", 1024),
        "ffn2_w_gate": _dw(n2, da2, "dw_ffn2_gate", 1408), "ffn2_w_up": _dw(n2, db2, "dw_ffn2_up", 1408),
        "ffn2_w_down": _dw(s2, do2, "dw_ffn2_down", 512),
    }

    dlam4 = dlam.reshape(SUBLANES, 2, 2, gh, n_state)
    dlam_in = jnp.transpose(dlam4, (2, 0, 1, 3, 4)).reshape(2, SUBLANES, n_grp, n_state)
    hg = gp // 2
    dbb_r = jnp.concatenate([_diag_blocks(dbc[h][:, :hg], gh) for h in range(2)], axis=0)
    dbb_i = jnp.concatenate([_diag_blocks(dbc[h][:, hg:], gh) for h in range(2)], axis=0)
    da_re, da_im, dldt, dbre_t, dbim_t = _s5_params_bwd(a_re, a_im, ldt, b_re_t, b_im_t, dlam_in, dbb_r, dbb_i)
    dc_re = jnp.concatenate([_diag_blocks(dcc[h][:hg], gh) for h in range(2)], axis=0)
    dc_im = -jnp.concatenate([_diag_blocks(dcc[h][hg:], gh) for h in range(2)], axis=0)

    grads_rep = {
        "g_ffn1": dg_ffn1, "g_mix": dg_mix, "b_gate": dbg, "ssm_a_re": da_re[None], "ssm_a_im": da_im[None],
        "ssm_log_dt": dldt.reshape(1, n_grp), "ssm_b_re": jnp.transpose(dbre_t, (0, 2, 1))[None],
        "ssm_b_im": jnp.transpose(dbim_t, (0, 2, 1))[None], "ssm_c_re": jnp.transpose(dc_re, (0, 2, 1))[None],
        "ssm_c_im": jnp.transpose(dc_im, (0, 2, 1))[None], "ssm_d": ddsk, "g_ffn2": dg_ffn2, "g_final": dg_final.reshape(d),
    }

    gs = jnp.stack([_pack([b.astype(BF16) for b in blocks]) for blocks in
                    zip(*[list(_blocks_from_full(dfull[n], kind)) for n, kind in BIG])])
    rows = gs.shape[1]
    from_sibling = _rs_pair(gs, "reduce_pair")
    own = lax.dynamic_index_in_dim(gs.reshape(4, 2, rows, SLAB_COLS), me_core, axis=1, keepdims=False)
    pair = _add_pairs(own, from_sibling, "reduce_pair_add")
    from_chips = _rs_chips(pair, "reduce_chips")
    mine = lax.dynamic_index_in_dim(pair, me_chip, axis=0, keepdims=False)
    big_names = [n for n, _ in BIG]
    gb, db_, mb, vb = _adamw(_pack([w[n][0] for n in big_names]), _pack([mom_m[n][0] for n in big_names]),
                             _pack([mom_v[n][0] for n in big_names]), [mine, from_chips[0], from_chips[1], from_chips[2]], "adamw_big")
    shard_shapes = [w[n].shape for n in big_names]
    out_g = dict(zip(big_names, _unpack(gb, shard_shapes)))
    out_d = dict(zip(big_names, _unpack(db_, shard_shapes)))
    out_m = dict(zip(big_names, _unpack(mb, shard_shapes)))
    out_v = dict(zip(big_names, _unpack(vb, shard_shapes)))

    rep_shapes = [w[n].shape for n in REPLICATED]
    small_g_shapes = rep_shapes + [(n_meta, d), (3, dc)]
    gsmall = _pack([grads_rep[n] for n in REPLICATED] + [dh0[0:n_meta], dcw[0:3]])
    gall = _all_gather(gsmall, "gather_small_grads")
    zer = [jnp.zeros((n_meta, d), F32), jnp.zeros((3, dc), F32)]
    gr, dr, mr, vr = _adamw(_pack([w[n] for n in REPLICATED] + zer), _pack([mom_m[n] for n in REPLICATED] + zer),
                            _pack([mom_v[n] for n in REPLICATED] + zer), [gall[b] for b in range(NDEV)], "adamw_replicated")
    g_list = _unpack(gr, small_g_shapes)
    out_g.update(zip(REPLICATED, g_list[:len(REPLICATED)]))
    out_d.update(zip(REPLICATED, _unpack(dr, rep_shapes)))
    out_m.update(zip(REPLICATED, _unpack(mr, rep_shapes)))
    out_v.update(zip(REPLICATED, _unpack(vr, rep_shapes)))

    mcols = d // NDEV
    ccols = dc // NDEV
    g_meta = lax.dynamic_slice_in_dim(g_list[-2], me * mcols, mcols, axis=1)
    g_cw = lax.dynamic_slice_in_dim(g_list[-1], me * ccols, ccols, axis=1).reshape(conv_w.shape)
    tiny = ("meta_tokens", "conv_w")
    tiny_shapes = [meta_tokens.shape, conv_w.shape]
    gt, dt_, mt, vt = _adamw(_pack([w[n] for n in tiny]), _pack([mom_m[n] for n in tiny]), _pack([mom_v[n] for n in tiny]),
                             [_pack([g_meta, g_cw])], "adamw_tiny")
    out_g.update(zip(tiny, _unpack(gt, tiny_shapes)))
    out_d.update(zip(tiny, _unpack(dt_, tiny_shapes)))
    out_m.update(zip(tiny, _unpack(mt, tiny_shapes)))
    out_v.update(zip(tiny, _unpack(vt, tiny_shapes)))

    grad_x = dh0[n_meta:t_real][None]
    return (loss, grad_x, *[out_g[n] for n in WEIGHTS], *[out_d[n] for n in WEIGHTS],
            *[out_m[n] for n in WEIGHTS], *[out_v[n] for n in WEIGHTS])
```

```python
import functools

import jax
import jax.numpy as jnp
from jax import lax
from jax.experimental import pallas as pl
from jax.experimental.pallas import tpu as pltpu

F32 = jnp.float32
BF16 = jnp.bfloat16
MESH = pl.DeviceIdType.MESH
AXES = ("x", "y", "c")
NDEV = 8
SLAB_COLS = 1024
RMS_EPS = 1e-6
TOKEN_TILE = 256
MIX_TILE = 128
SUBLANES = 8
SCAN_LANES = 512
FFN_CHUNK = 1024
VMEM_LIMIT_BYTES = 56 * 1024 * 1024

ADAM_LR = 0.001
ADAM_B1 = 0.9
ADAM_B2 = 0.999
ADAM_EPS = 1e-08
ADAM_WD = 0.01
ADAM_STEP = 10

_VM = pl.BlockSpec(memory_space=pltpu.VMEM)
_ANY = pl.BlockSpec(memory_space=pl.ANY)


def _params(sem=("arbitrary",)):
    return pltpu.CompilerParams(dimension_semantics=sem, vmem_limit_bytes=VMEM_LIMIT_BYTES)


def _dot(a, b):
    return jnp.dot(a, b, preferred_element_type=F32)


def _dot_nt(a, b):
    return lax.dot_general(a, b, (((1,), (1,)), ((), ())), preferred_element_type=F32)


def _dot_tn(a, b):
    return lax.dot_general(a, b, (((0,), (0,)), ((), ())), preferred_element_type=F32)


def _chunks(n, step):
    return [(s, min(s + step, n)) for s in range(0, n, step)]


def _all_gather(x_shard, name):
    m_per, n = x_shard.shape

    def body(x_ref, out_ref, send_sems, recv_sems, local_sem):
        x, y, c = lax.axis_index("x"), lax.axis_index("y"), lax.axis_index("c")
        me, sibling = (x, y, c), (x, y, 1 - c)
        chips = [(1 - x, y), (x, 1 - y), (1 - x, 1 - y)]

        def rows(px, py, pc):
            return out_ref.at[pl.ds((4 * px + 2 * py + pc) * m_per, m_per), :]

        def copy(k, block, to, src=None):
            return pltpu.make_async_remote_copy(
                src_ref=rows(*block) if src is None else src, dst_ref=rows(*block),
                send_sem=send_sems.at[k], recv_sem=recv_sems.at[k], device_id=to, device_id_type=MESH)

        mine = pltpu.make_async_copy(x_ref, rows(*me), local_sem)
        mine.start()
        first = [copy(0, me, sibling, src=x_ref)]
        first += [copy(1 + j, me, (*chip, c), src=x_ref) for j, chip in enumerate(chips)]
        for cp in first:
            cp.start()
        passed = [copy(4 + j, (*chip, c), sibling) for j, chip in enumerate(chips)]
        for j, chip in enumerate(chips):
            copy(1 + j, (*chip, c), me).wait_recv()
            passed[j].start()
        copy(0, sibling, me).wait_recv()
        for j, chip in enumerate(chips):
            copy(4 + j, (*chip, 1 - c), me).wait_recv()
        for cp in first + passed:
            cp.wait_send()
        mine.wait()

    out = pl.pallas_call(
        body, name=name,
        out_shape=jax.ShapeDtypeStruct((NDEV * m_per, n), x_shard.dtype),
        in_specs=[_ANY], out_specs=_ANY,
        scratch_shapes=[pltpu.SemaphoreType.DMA((7,)), pltpu.SemaphoreType.DMA((7,)), pltpu.SemaphoreType.DMA],
    )(x_shard)
    return out.reshape(NDEV, m_per, n)


def _rs_pair(gs, name):
    _, r, n = gs.shape

    def body(gs_ref, out_ref, send_sems, recv_sems):
        x, y, c = lax.axis_index("x"), lax.axis_index("y"), lax.axis_index("c")
        copies = []
        for j in range(4):
            cp = pltpu.make_async_remote_copy(
                src_ref=gs_ref.at[2 * j + (1 - c)], dst_ref=out_ref.at[j],
                send_sem=send_sems.at[j], recv_sem=recv_sems.at[j], device_id=(x, y, 1 - c), device_id_type=MESH)
            cp.start()
            copies.append(cp)
        for cp in copies:
            cp.wait()

    return pl.pallas_call(
        body, name=name, out_shape=jax.ShapeDtypeStruct((4, r, n), gs.dtype),
        in_specs=[_ANY], out_specs=_ANY,
        scratch_shapes=[pltpu.SemaphoreType.DMA((4,)), pltpu.SemaphoreType.DMA((4,))],
    )(gs)


def _rs_chips(p, name):
    _, r, n = p.shape

    def body(p_ref, out_ref, send_sems, recv_sems):
        x, y, c = lax.axis_index("x"), lax.axis_index("y"), lax.axis_index("c")
        copies = []
        for k, (px, py) in enumerate([(1 - x, y), (x, 1 - y), (1 - x, 1 - y)]):
            cp = pltpu.make_async_remote_copy(
                src_ref=p_ref.at[2 * px + py], dst_ref=out_ref.at[k],
                send_sem=send_sems.at[k], recv_sem=recv_sems.at[k], device_id=(px, py, c), device_id_type=MESH)
            cp.start()
            copies.append(cp)
        for cp in copies:
            cp.wait()

    return pl.pallas_call(
        body, name=name, out_shape=jax.ShapeDtypeStruct((3, r, n), p.dtype),
        in_specs=[_ANY], out_specs=_ANY,
        scratch_shapes=[pltpu.SemaphoreType.DMA((3,)), pltpu.SemaphoreType.DMA((3,))],
    )(p)


def _slab_tile(rows):
    return TOKEN_TILE if rows % TOKEN_TILE == 0 else rows


def _add_pairs(gs, core, b, name):
    k, r, n = b.shape
    tr = _slab_tile(r)

    def body(core_ref, a_ref, b_ref, o_ref):
        o_ref[0] = (a_ref[0, 0].astype(F32) + b_ref[0].astype(F32)).astype(o_ref.dtype)

    spec = pl.BlockSpec((1, tr, n), lambda j, i, c: (j, i, 0))
    return pl.pallas_call(
        body, name=name,
        grid_spec=pltpu.PrefetchScalarGridSpec(
            num_scalar_prefetch=1, grid=(k, r // tr),
            in_specs=[pl.BlockSpec((1, 1, tr, n), lambda j, i, c: (j, c[0], i, 0)), spec], out_specs=spec),
        out_shape=jax.ShapeDtypeStruct(b.shape, b.dtype), compiler_params=_params(("arbitrary", "arbitrary")),
    )(core.reshape(1), gs.reshape(k, 2, r, n), b)


def _adamw(w, m, v, parts, sel, name):
    r, n = w.shape
    tr = _slab_tile(r)
    nparts = len(parts)
    bc1 = 1.0 - ADAM_B1 ** ADAM_STEP
    bc2 = 1.0 - ADAM_B2 ** ADAM_STEP

    def body(sel_ref, *refs):
        w_ref, m_ref, v_ref = refs[:3]
        p_refs = refs[3:3 + nparts]
        g_ref, d_ref, nm_ref, nv_ref = refs[3 + nparts:]
        g = p_refs[0][0].astype(F32)
        for p in p_refs[1:]:
            g = g + p[0].astype(F32)
        nm = ADAM_B1 * m_ref[...] + (1.0 - ADAM_B1) * g
        nv = ADAM_B2 * v_ref[...] + (1.0 - ADAM_B2) * (g * g)
        m_hat = nm / bc1
        v_hat = nv / bc2
        g_ref[...] = g
        d_ref[...] = -ADAM_LR * (m_hat / (jnp.sqrt(v_hat) + ADAM_EPS) + ADAM_WD * w_ref[...])
        nm_ref[...] = nm
        nv_ref[...] = nv

    def part_spec(idx):
        if idx is None:
            return pl.BlockSpec((1, tr, n), lambda i, s: (s[0], i, 0))
        return pl.BlockSpec((1, tr, n), lambda i, s, idx=idx: (idx, i, 0))

    spec = pl.BlockSpec((tr, n), lambda i, s: (i, 0))
    out = jax.ShapeDtypeStruct((r, n), F32)
    return pl.pallas_call(
        body, name=name,
        grid_spec=pltpu.PrefetchScalarGridSpec(
            num_scalar_prefetch=1, grid=(r // tr,),
            in_specs=[spec] * 3 + [part_spec(idx) for _, idx in parts], out_specs=[spec] * 4),
        out_shape=[out] * 4, compiler_params=_params(),
    )(jnp.zeros((1,), jnp.int32) if sel is None else sel.reshape(1), w, m, v, *[p for p, _ in parts])


def _rms_parts(h, g):
    r = lax.rsqrt(jnp.mean(h * h, axis=-1, keepdims=True) + RMS_EPS)
    xhat = h * r
    return r, xhat, xhat * g


def _rms_bwd(dn, g, r, xhat):
    dxh = dn * g
    return r * (dxh - xhat * jnp.mean(dxh * xhat, axis=-1, keepdims=True))


def _ffn_fwd(h, g, wg, wu, wd, name):
    t, d = h.shape
    f = wg.shape[1]
    tm = TOKEN_TILE
    chunks = _chunks(f, FFN_CHUNK)

    def body(h_ref, g_ref, wg_ref, wu_ref, wd_ref, o_ref):
        hv = h_ref[...]
        n = _rms_parts(hv, g_ref[...])[2].astype(BF16)
        acc = jnp.zeros((tm, d), F32)
        for s, e in chunks:
            a = _dot(n, wg_ref[:, s:e])
            b = _dot(n, wu_ref[:, s:e])
            acc = acc + _dot((a * jax.nn.sigmoid(a) * b).astype(BF16), wd_ref[s:e, :])
        o_ref[...] = hv + 0.5 * acc

    tile = pl.BlockSpec((tm, d), lambda i: (i, 0))
    return pl.pallas_call(
        body, name=name, grid=(t // tm,), in_specs=[tile, _VM, _VM, _VM, _VM], out_specs=tile,
        out_shape=jax.ShapeDtypeStruct((t, d), F32), compiler_params=_params(),
    )(h, g, wg, wu, wd)


def _ffn_bwd(h, dh_out, g, wg, wu, wd, name):
    t, d = h.shape
    f = wg.shape[1]
    tm = TOKEN_TILE
    chunks = _chunks(f, FFN_CHUNK)

    def body(h_ref, dho_ref, g_ref, wg_ref, wu_ref, wd_ref, dh_ref, dg_ref, n_ref, da_ref, db_ref, s_ref, do_ref):
        @pl.when(pl.program_id(0) == 0)
        def _():
            dg_ref[...] = jnp.zeros_like(dg_ref)

        hv = h_ref[...]
        gv = g_ref[...]
        r, xhat, n32 = _rms_parts(hv, gv)
        n = n32.astype(BF16)
        dho = dho_ref[...]
        do = (0.5 * dho).astype(BF16)
        dn = jnp.zeros((tm, d), F32)
        for s, e in chunks:
            a = _dot(n, wg_ref[:, s:e])
            b = _dot(n, wu_ref[:, s:e])
            sig = jax.nn.sigmoid(a)
            sa = a * sig
            ds = _dot_nt(do, wd_ref[s:e, :])
            da = (ds * b * (sig * (1.0 + a * (1.0 - sig)))).astype(BF16)
            db = (ds * sa).astype(BF16)
            s_ref[:, s:e] = (sa * b).astype(BF16)
            da_ref[:, s:e] = da
            db_ref[:, s:e] = db
            dn = dn + _dot_nt(da, wg_ref[:, s:e]) + _dot_nt(db, wu_ref[:, s:e])
        dh_ref[...] = dho + _rms_bwd(dn, gv, r, xhat)
        dg_ref[...] += jnp.sum(dn * xhat, axis=0, keepdims=True)
        n_ref[...] = n
        do_ref[...] = do

    tile = pl.BlockSpec((tm, d), lambda i: (i, 0))
    wide = pl.BlockSpec((tm, f), lambda i: (i, 0))
    one = pl.BlockSpec((1, d), lambda i: (0, 0))
    return pl.pallas_call(
        body, name=name, grid=(t // tm,),
        in_specs=[tile, tile, _VM, _VM, _VM, _VM],
        out_specs=[tile, one, tile, wide, wide, wide, tile],
        out_shape=[jax.ShapeDtypeStruct((t, d), F32), jax.ShapeDtypeStruct((1, d), F32),
                   jax.ShapeDtypeStruct((t, d), BF16), jax.ShapeDtypeStruct((t, f), BF16),
                   jax.ShapeDtypeStruct((t, f), BF16), jax.ShapeDtypeStruct((t, f), BF16),
                   jax.ShapeDtypeStruct((t, d), BF16)],
        compiler_params=_params(),
    )(h, dh_out, g, wg, wu, wd)


def _head(h, tgt, g, lo, hi, name):
    t, d = h.shape
    tm = TOKEN_TILE

    def body(h_ref, t_ref, g_ref, dh_ref, loss_ref, dg_ref):
        i = pl.program_id(0)

        @pl.when(i == 0)
        def _():
            loss_ref[...] = jnp.zeros_like(loss_ref)
            dg_ref[...] = jnp.zeros_like(dg_ref)

        gv = g_ref[...]
        r, xhat, y = _rms_parts(h_ref[...], gv)
        row = i * tm + lax.broadcasted_iota(jnp.int32, (tm, 1), 0)
        err = jnp.where((row >= lo) & (row < hi), y - t_ref[...], 0.0)
        loss_ref[...] += jnp.full(loss_ref.shape, 0.5 * jnp.sum(jnp.mean(err * err, axis=-1, keepdims=True)), F32)
        dy = err * (1.0 / d)
        dg_ref[...] += jnp.sum(dy * xhat, axis=0, keepdims=True)
        dh_ref[...] = _rms_bwd(dy, gv, r, xhat)

    tile = pl.BlockSpec((tm, d), lambda i: (i, 0))
    return pl.pallas_call(
        body, name=name, grid=(t // tm,), in_specs=[tile, tile, _VM],
        out_specs=[tile, pl.BlockSpec((SUBLANES, 128), lambda i: (0, 0)), pl.BlockSpec((1, d), lambda i: (0, 0))],
        out_shape=[jax.ShapeDtypeStruct((t, d), F32), jax.ShapeDtypeStruct((SUBLANES, 128), F32),
                   jax.ShapeDtypeStruct((1, d), F32)],
        compiler_params=_params(),
    )(h, tgt, g)


def _dw(a, b, name):
    t, m = a.shape
    n = b.shape[1]
    bn = next(k for k in (512, 256, n) if n % k == 0)

    def body(a_ref, b_ref, o_ref):
        o_ref[...] = _dot_tn(a_ref[...], b_ref[...])

    return pl.pallas_call(
        body, name=name, grid=(n // bn,),
        in_specs=[_VM, pl.BlockSpec((t, bn), lambda j: (0, j))], out_specs=pl.BlockSpec((m, bn), lambda j: (0, j)),
        out_shape=jax.ShapeDtypeStruct((m, n), F32), compiler_params=_params(),
    )(a, b)


def _s5_discretise(a_re, a_im, log_dt, b_re, b_im):
    dt = jnp.exp(log_dt)
    mag = jnp.exp(a_re * dt)
    lam_re = mag * jnp.cos(a_im * dt)
    lam_im = mag * jnp.sin(a_im * dt)
    den = a_re * a_re + a_im * a_im
    q_re = ((lam_re - 1.0) * a_re + lam_im * a_im) / den
    q_im = (lam_im * a_re - (lam_re - 1.0) * a_im) / den
    bb_re = q_re[:, None, :] * b_re - q_im[:, None, :] * b_im
    bb_im = q_re[:, None, :] * b_im + q_im[:, None, :] * b_re
    return lam_re, lam_im, bb_re, bb_im


def _s5_params_fwd(a_re, a_im, log_dt, b_re, b_im):
    g, p = a_re.shape
    c = b_re.shape[1]

    def body(are_ref, aim_ref, ldt_ref, bre_ref, bim_ref, pwr_ref, pwi_ref, bbr_ref, bbi_ref):
        lr, li, bbr, bbi = _s5_discretise(are_ref[...], aim_ref[...], ldt_ref[...], bre_ref[...], bim_ref[...])
        bbr_ref[...] = bbr
        bbi_ref[...] = bbi
        pr, pi = lr, li
        pwr_ref[0] = pr
        pwi_ref[0] = pi
        for k in range(1, SUBLANES):
            pr, pi = pr * lr - pi * li, pr * li + pi * lr
            pwr_ref[k] = pr
            pwi_ref[k] = pi

    return pl.pallas_call(
        body, name="s5_params_fwd",
        out_shape=[jax.ShapeDtypeStruct((SUBLANES, g, p), F32), jax.ShapeDtypeStruct((SUBLANES, g, p), F32),
                   jax.ShapeDtypeStruct((g, c, p), F32), jax.ShapeDtypeStruct((g, c, p), F32)],
    )(a_re, a_im, log_dt, b_re, b_im)


def _s5_params_bwd(a_re, a_im, log_dt, b_re, b_im, dlam, dbb_re, dbb_im):
    g, p = a_re.shape
    c = b_re.shape[1]

    def body(are_ref, aim_ref, ldt_ref, bre_ref, bim_ref, dlam_ref, dbr_ref, dbi_ref,
             dare_ref, daim_ref, dldt_ref, dbre_ref, dbim_ref):
        dlr = jnp.sum(dlam_ref[0], axis=0)
        dli = jnp.sum(dlam_ref[1], axis=0)
        _, vjp = jax.vjp(_s5_discretise, are_ref[...], aim_ref[...], ldt_ref[...], bre_ref[...], bim_ref[...])
        dare, daim, dldt, dbre, dbim = vjp((dlr, dli, dbr_ref[...], dbi_ref[...]))
        dare_ref[...] = dare
        daim_ref[...] = daim
        dldt_ref[...] = dldt
        dbre_ref[...] = dbre
        dbim_ref[...] = dbim

    return pl.pallas_call(
        body, name="s5_params_bwd",
        out_shape=[jax.ShapeDtypeStruct((g, p), F32), jax.ShapeDtypeStruct((g, p), F32),
                   jax.ShapeDtypeStruct((g, 1), F32), jax.ShapeDtypeStruct((g, c, p), F32),
                   jax.ShapeDtypeStruct((g, c, p), F32)],
    )(a_re, a_im, log_dt, b_re, b_im, dlam, dbb_re, dbb_im)


def _scan_chunks(gp):
    hg = gp // 2
    w = min(SCAN_LANES, hg)
    return w, [(half * hg + k * w, half * gp + k * w, half * gp + hg + k * w) for half in range(2) for k in range(hg // w)]


def _cmul_acc(xr, xi, tr, ti, sr, si):
    return xr + tr * sr - ti * si, xi + tr * si + ti * sr


def _scan_fwd(buf_ref, row0, tm, ltab_ref, cin_ref, cout_ref, gp):
    w, chunks = _scan_chunks(gp)
    for lo_t, lo_r, lo_i in chunks:
        def body(r, carry, lo_t=lo_t, lo_r=lo_r, lo_i=lo_i):
            cr, ci = carry
            row = pl.multiple_of(row0 + r * SUBLANES, SUBLANES)
            xr = buf_ref[pl.ds(row, SUBLANES), lo_r:lo_r + w]
            xi = buf_ref[pl.ds(row, SUBLANES), lo_i:lo_i + w]
            for tab, shift in ((0, 1), (2, 2), (4, 4)):
                xr, xi = _cmul_acc(xr, xi, ltab_ref[tab, :, lo_t:lo_t + w], ltab_ref[tab + 1, :, lo_t:lo_t + w],
                                   pltpu.roll(xr, shift, 0), pltpu.roll(xi, shift, 0))
            xr, xi = _cmul_acc(xr, xi, ltab_ref[6, :, lo_t:lo_t + w], ltab_ref[7, :, lo_t:lo_t + w], cr, ci)
            buf_ref[pl.ds(row, SUBLANES), lo_r:lo_r + w] = xr
            buf_ref[pl.ds(row, SUBLANES), lo_i:lo_i + w] = xi
            last = SUBLANES - 1
            return (jnp.broadcast_to(xr[last:last + 1], (SUBLANES, w)), jnp.broadcast_to(xi[last:last + 1], (SUBLANES, w)))

        cr, ci = lax.fori_loop(0, tm // SUBLANES, body,
                               (cin_ref[0:SUBLANES, lo_r:lo_r + w], cin_ref[0:SUBLANES, lo_i:lo_i + w]))
        if cout_ref is not None:
            cout_ref[0:SUBLANES, lo_r:lo_r + w] = cr
            cout_ref[0:SUBLANES, lo_i:lo_i + w] = ci


def _scan_rev(g_ref, hext_ref, tm, ltab_ref, gc_ref, dlam_ref, gp):
    w, chunks = _scan_chunks(gp)
    nb = tm // SUBLANES
    for lo_t, lo_r, lo_i in chunks:
        def body(k, carry, lo_t=lo_t, lo_r=lo_r, lo_i=lo_i):
            cr, ci, ar, ai = carry
            row = pl.multiple_of((nb - 1 - k) * SUBLANES, SUBLANES)
            xr = g_ref[pl.ds(row, SUBLANES), lo_r:lo_r + w]
            xi = g_ref[pl.ds(row, SUBLANES), lo_i:lo_i + w]
            for tab, shift in ((8, 7), (10, 6), (12, 4)):
                xr, xi = _cmul_acc(xr, xi, ltab_ref[tab, :, lo_t:lo_t + w], ltab_ref[tab + 1, :, lo_t:lo_t + w],
                                   pltpu.roll(xr, shift, 0), pltpu.roll(xi, shift, 0))
            xr, xi = _cmul_acc(xr, xi, ltab_ref[14, :, lo_t:lo_t + w], ltab_ref[15, :, lo_t:lo_t + w], cr, ci)
            g_ref[pl.ds(row, SUBLANES), lo_r:lo_r + w] = xr
            g_ref[pl.ds(row, SUBLANES), lo_i:lo_i + w] = xi
            first = lax.broadcasted_iota(jnp.int32, (SUBLANES, w), 0) == 0
            prev = pl.ds(row, SUBLANES)
            here = pl.ds(row + SUBLANES, SUBLANES)
            hpr = jnp.where(first, pltpu.roll(hext_ref[prev, lo_r:lo_r + w], 1, 0), pltpu.roll(hext_ref[here, lo_r:lo_r + w], 1, 0))
            hpi = jnp.where(first, pltpu.roll(hext_ref[prev, lo_i:lo_i + w], 1, 0), pltpu.roll(hext_ref[here, lo_i:lo_i + w], 1, 0))
            ar = ar + xr * hpr + xi * hpi
            ai = ai - xr * hpi + xi * hpr
            return (jnp.broadcast_to(xr[0:1], (SUBLANES, w)), jnp.broadcast_to(xi[0:1], (SUBLANES, w)), ar, ai)

        cr, ci, ar, ai = lax.fori_loop(
            0, nb, body, (gc_ref[:, lo_r:lo_r + w], gc_ref[:, lo_i:lo_i + w], dlam_ref[:, lo_r:lo_r + w], dlam_ref[:, lo_i:lo_i + w]))
        gc_ref[:, lo_r:lo_r + w] = cr
        gc_ref[:, lo_i:lo_i + w] = ci
        dlam_ref[:, lo_r:lo_r + w] = ar
        dlam_ref[:, lo_i:lo_i + w] = ai


def _mix_math(h, gm, win_ref, bg, bc_ref, cc_ref, dsk, wglu_ref, cw, wco_ref, ltab_ref,
              hbuf_ref, hrow0, st_in_ref, st_out_ref, cext_ref, dims):
    d, ds, dc, gp = dims
    tm = h.shape[0]
    dsh = ds // 2
    r, xhat, n32 = _rms_parts(h, gm)
    u = n32.astype(BF16)
    o1, o2, o3 = ds + dc, ds + 2 * dc, ds + 3 * dc
    us = _dot(u, win_ref[:, 0:ds])
    v = _dot(u, win_ref[:, ds:o1])
    gb = _dot(u, win_ref[:, o1:o2])
    gcv = _dot(u, win_ref[:, o2:o3])
    gs = jax.nn.sigmoid(_dot(u, win_ref[:, o3:o3 + d]) + bg[:, 0:d])
    gcg = jax.nn.sigmoid(_dot(u, win_ref[:, o3 + d:o3 + 2 * d]) + bg[:, d:2 * d])
    us16 = us.astype(BF16)
    rows = slice(hrow0, hrow0 + tm)
    for half in range(2):
        hbuf_ref[rows, half * gp:(half + 1) * gp] = _dot(us16[:, half * dsh:(half + 1) * dsh], bc_ref[half])
    _scan_fwd(hbuf_ref, hrow0, tm, ltab_ref, st_in_ref, st_out_ref, gp)
    y5 = jnp.concatenate(
        [_dot(hbuf_ref[rows, half * gp:(half + 1) * gp].astype(BF16), cc_ref[half]) for half in range(2)], axis=1) + dsk * us
    ge16 = jax.nn.gelu(y5).astype(BF16)
    z = _dot(ge16, wglu_ref[...])
    z1, sz = z[:, 0:d], jax.nn.sigmoid(z[:, d:2 * d])
    ys = z1 * sz
    cin = gcv * v
    cext_ref[SUBLANES:SUBLANES + tm, :] = cin
    cv = cw[0:1] * cext_ref[SUBLANES - 2:SUBLANES - 2 + tm, :] + cw[1:2] * cext_ref[SUBLANES - 1:SUBLANES - 1 + tm, :] + cw[2:3] * cin
    cg16 = (gb * cv).astype(BF16)
    yc = _dot(cg16, wco_ref[...])
    mixed = gs * ys + gcg * yc
    return dict(r=r, xhat=xhat, u=u, us=us, us16=us16, v=v, gb=gb, gcv=gcv, gs=gs, gcg=gcg, y5=y5, ge16=ge16,
                z1=z1, sz=sz, ys=ys, cin=cin, cv=cv, cg16=cg16, yc=yc, mixed=mixed)


def _mix_fwd(h, gm, win, bg, bc, cc, dsk, wglu, cw, wco, wo, ltab, dims):
    d, ds, dc, gp = dims
    t = h.shape[0]
    tm = MIX_TILE
    nt = t // tm

    def body(h_ref, gm_ref, win_ref, bg_ref, bc_ref, cc_ref, dsk_ref, wglu_ref, cw_ref, wco_ref, wo_ref, ltab_ref,
             h2_ref, st_ref, cvs_ref, hbuf_ref, carry_ref, cext_ref):
        @pl.when(pl.program_id(0) == 0)
        def _():
            carry_ref[...] = jnp.zeros_like(carry_ref)
            cext_ref[0:SUBLANES, :] = jnp.zeros((SUBLANES, dc), F32)

        st_ref[0] = carry_ref[...]
        cvs_ref[0] = cext_ref[0:SUBLANES, :]
        hv = h_ref[...]
        m = _mix_math(hv, gm_ref[...], win_ref, bg_ref[...], bc_ref, cc_ref, dsk_ref[...], wglu_ref, cw_ref[...], wco_ref,
                      ltab_ref, hbuf_ref, 0, carry_ref, carry_ref, cext_ref, dims)
        h2_ref[...] = hv + _dot(m["mixed"].astype(BF16), wo_ref[...])
        cext_ref[0:SUBLANES, :] = cext_ref[tm:tm + SUBLANES, :]

    tile = pl.BlockSpec((tm, d), lambda i: (i, 0))
    return pl.pallas_call(
        body, name="mix_fwd", grid=(nt,),
        in_specs=[tile] + [_VM] * 11,
        out_specs=[tile, pl.BlockSpec((1, SUBLANES, 2 * gp), lambda i: (i, 0, 0)), pl.BlockSpec((1, SUBLANES, dc), lambda i: (i, 0, 0))],
        out_shape=[jax.ShapeDtypeStruct((t, d), F32), jax.ShapeDtypeStruct((nt, SUBLANES, 2 * gp), F32),
                   jax.ShapeDtypeStruct((nt, SUBLANES, dc), F32)],
        scratch_shapes=[pltpu.VMEM((tm, 2 * gp), F32), pltpu.VMEM((SUBLANES, 2 * gp), F32), pltpu.VMEM((SUBLANES + tm, dc), F32)],
        compiler_params=_params(),
    )(h, gm, win, bg, bc, cc, dsk, wglu, cw, wco, wo, ltab)


def _mix_bwd(h, dh2, st, cvs, gm, win, bg, bc, cc, dsk, wglu, cw, wco, wo, ltab, dims):
    d, ds, dc, gp = dims
    t = h.shape[0]
    tm = MIX_TILE
    nt = t // tm
    dsh = ds // 2
    ncols = ds + 3 * dc + 2 * d

    def body(h_ref, dh2_ref, st_ref, cvs_ref, gm_ref, win_ref, bg_ref, bc_ref, cc_ref, dsk_ref, wglu_ref, cw_ref, wco_ref,
             wo_ref, ltab_ref,
             dh1_ref, u_ref, dp_ref, ge_ref, dz_ref, cg_ref, dyc_ref, mx_ref, dh216_ref,
             dgm_ref, dbg_ref, ddsk_ref, dcw_ref, dlam_ref, dbc_ref, dcc_ref,
             hext_ref, gbuf_ref, gcarry_ref, cext_ref, dcvext_ref):
        @pl.when(pl.program_id(0) == 0)
        def _():
            for ref in (dgm_ref, dbg_ref, ddsk_ref, dcw_ref, dlam_ref, dbc_ref, dcc_ref, gcarry_ref):
                ref[...] = jnp.zeros_like(ref)
            dcvext_ref[tm:tm + SUBLANES, :] = jnp.zeros((SUBLANES, dc), F32)

        hext_ref[0:SUBLANES, :] = st_ref[0]
        cext_ref[0:SUBLANES, :] = cvs_ref[0]
        hv = h_ref[...]
        gmv = gm_ref[...]
        cw_v = cw_ref[...]
        dskv = dsk_ref[...]
        m = _mix_math(hv, gmv, win_ref, bg_ref[...], bc_ref, cc_ref, dskv, wglu_ref, cw_v, wco_ref,
                      ltab_ref, hext_ref, SUBLANES, hext_ref, None, cext_ref, dims)
        dh2v = dh2_ref[...]
        dh216 = dh2v.astype(BF16)
        dmixed = _dot_nt(dh216, wo_ref[...])
        gs, gcg, ys, yc, sz = m["gs"], m["gcg"], m["ys"], m["yc"], m["sz"]
        dys = dmixed * gs
        dyc16 = (dmixed * gcg).astype(BF16)
        dpgs = dmixed * ys * gs * (1.0 - gs)
        dpgc = dmixed * yc * gcg * (1.0 - gcg)
        dz16 = jnp.concatenate([dys * sz, dys * m["z1"] * sz * (1.0 - sz)], axis=1).astype(BF16)
        dge = _dot_nt(dz16, wglu_ref[...])
        dy5 = jax.vjp(jax.nn.gelu, m["y5"])[1](dge)[0]
        dy516 = dy5.astype(BF16)
        for half in range(2):
            gbuf_ref[:, half * gp:(half + 1) * gp] = _dot_nt(dy516[:, half * dsh:(half + 1) * dsh], cc_ref[half])
        _scan_rev(gbuf_ref, hext_ref, tm, ltab_ref, gcarry_ref, dlam_ref, gp)
        dus = []
        for half in range(2):
            g16 = gbuf_ref[:, half * gp:(half + 1) * gp].astype(BF16)
            dus.append(_dot_nt(g16, bc_ref[half]))
            dbc_ref[half] += _dot_tn(m["us16"][:, half * dsh:(half + 1) * dsh], g16)
            h16 = hext_ref[SUBLANES:SUBLANES + tm, half * gp:(half + 1) * gp].astype(BF16)
            dcc_ref[half] += _dot_tn(h16, dy516[:, half * dsh:(half + 1) * dsh])
        dus = jnp.concatenate(dus, axis=1) + dskv * dy5
        ddsk_ref[...] += jnp.sum(dy5 * m["us"], axis=0, keepdims=True)
        dcg = _dot_nt(dyc16, wco_ref[...])
        dgb = dcg * m["cv"]
        dcv = dcg * m["gb"]
        dcvext_ref[0:tm, :] = dcv
        dcin = cw_v[2:3] * dcv + cw_v[1:2] * dcvext_ref[1:1 + tm, :] + cw_v[0:1] * dcvext_ref[2:2 + tm, :]
        dcw_ref[0:1, :] += jnp.sum(dcv * cext_ref[SUBLANES - 2:SUBLANES - 2 + tm, :], axis=0, keepdims=True)
        dcw_ref[1:2, :] += jnp.sum(dcv * cext_ref[SUBLANES - 1:SUBLANES - 1 + tm, :], axis=0, keepdims=True)
        dcw_ref[2:3, :] += jnp.sum(dcv * m["cin"], axis=0, keepdims=True)
        dcvext_ref[tm:tm + SUBLANES, :] = dcvext_ref[0:SUBLANES, :]
        dp16 = jnp.concatenate([dus, dcin * m["gcv"], dgb, dcin * m["v"], dpgs, dpgc], axis=1).astype(BF16)
        du = _dot_nt(dp16, win_ref[...])
        dh1_ref[...] = dh2v + _rms_bwd(du, gmv, m["r"], m["xhat"])
        dgm_ref[...] += jnp.sum(du * m["xhat"], axis=0, keepdims=True)
        dbg_ref[...] += jnp.concatenate([jnp.sum(dpgs, axis=0, keepdims=True), jnp.sum(dpgc, axis=0, keepdims=True)], axis=1)
        u_ref[...] = m["u"]
        dp_ref[...] = dp16
        ge_ref[...] = m["ge16"]
        dz_ref[...] = dz16
        cg_ref[...] = m["cg16"]
        dyc_ref[...] = dyc16
        mx_ref[...] = m["mixed"].astype(BF16)
        dh216_ref[...] = dh216

    def rev(cols):
        return pl.BlockSpec((tm, cols), lambda j: (nt - 1 - j, 0))

    def rev3(cols):
        return pl.BlockSpec((1, SUBLANES, cols), lambda j: (nt - 1 - j, 0, 0))

    def bf(cols):
        return jax.ShapeDtypeStruct((t, cols), BF16)

    return pl.pallas_call(
        body, name="mix_bwd", grid=(nt,),
        in_specs=[rev(d), rev(d), rev3(2 * gp), rev3(dc)] + [_VM] * 11,
        out_specs=[rev(d), rev(d), rev(ncols), rev(ds), rev(2 * d), rev(dc), rev(d), rev(d), rev(d)] + [_VM] * 7,
        out_shape=[jax.ShapeDtypeStruct((t, d), F32), bf(d), bf(ncols), bf(ds), bf(2 * d), bf(dc), bf(d), bf(d), bf(d),
                   jax.ShapeDtypeStruct((1, d), F32), jax.ShapeDtypeStruct((1, 2 * d), F32), jax.ShapeDtypeStruct((1, ds), F32),
                   jax.ShapeDtypeStruct((SUBLANES, dc), F32), jax.ShapeDtypeStruct((SUBLANES, 2 * gp), F32),
                   jax.ShapeDtypeStruct((2, dsh, gp), F32), jax.ShapeDtypeStruct((2, gp, dsh), F32)],
        scratch_shapes=[pltpu.VMEM((SUBLANES + tm, 2 * gp), F32), pltpu.VMEM((tm, 2 * gp), F32),
                        pltpu.VMEM((SUBLANES, 2 * gp), F32), pltpu.VMEM((SUBLANES + tm, dc), F32),
                        pltpu.VMEM((tm + SUBLANES, dc), F32)],
        compiler_params=_params(),
    )(h, dh2, st, cvs, gm, win, bg, bc, cc, dsk, wglu, cw, wco, wo, ltab)


def _pad_rows(a, rows, axis=0):
    pad = [(0, 0)] * a.ndim
    pad[axis] = (0, rows - a.shape[axis])
    return jnp.pad(a, pad)


def _as_rows(a):
    flat = a.reshape(-1)
    n = -(-flat.shape[0] // SLAB_COLS) * SLAB_COLS
    return jnp.pad(flat, (0, n - flat.shape[0])).reshape(-1, SLAB_COLS)


def _slab_rows(n):
    return -(-n // TOKEN_TILE) * TOKEN_TILE if n > TOKEN_TILE else -(-n // 16) * 16


def _pack(arrs):
    rows = jnp.concatenate([_as_rows(a) for a in arrs], axis=0)
    return _pad_rows(rows, _slab_rows(rows.shape[0]))


def _unpack(slab, shapes):
    out, r = [], 0
    for shp in shapes:
        size = 1
        for s in shp:
            size *= s
        n = -(-size // SLAB_COLS)
        out.append(slab[r:r + n].reshape(-1)[:size].reshape(shp))
        r += n
    return out


def _block_diag(blocks):
    n, a, b = blocks.shape
    eye = jnp.eye(n, dtype=blocks.dtype)
    return (blocks[:, :, None, :] * eye[:, None, :, None]).reshape(n * a, n * b)


def _diag_blocks(mat, n):
    a, b = mat.shape[0] // n, mat.shape[1] // n
    m4 = mat.reshape(n, a, n, b)
    return jnp.stack([m4[i, :, i, :] for i in range(n)], axis=0)


BIG = (("ffn1_w_gate", "col"), ("ffn1_w_up", "col"), ("ffn1_w_down", "row"), ("w_in", "col"), ("ssm_w_glu", "col"),
       ("conv_w_out", "col"), ("w_o", "row"), ("ffn2_w_gate", "col"), ("ffn2_w_up", "col"), ("ffn2_w_down", "row"))
REPLICATED = ("g_ffn1", "g_mix", "b_gate", "ssm_a_re", "ssm_a_im", "ssm_log_dt", "ssm_b_re", "ssm_b_im", "ssm_c_re",
              "ssm_c_im", "ssm_d", "g_ffn2", "g_final")
WEIGHTS = ("meta_tokens", "g_ffn1", "ffn1_w_gate", "ffn1_w_up", "ffn1_w_down", "g_mix", "w_in", "b_gate", "ssm_a_re",
           "ssm_a_im", "ssm_log_dt", "ssm_b_re", "ssm_b_im", "ssm_c_re", "ssm_c_im", "ssm_d", "ssm_w_glu", "conv_w",
           "conv_w_out", "w_o", "g_ffn2", "ffn2_w_gate", "ffn2_w_up", "ffn2_w_down", "g_final")
RIDER_ROWS = 16


def _full_from_blocks(blocks, kind):
    n, r, c = blocks.shape
    if kind == "col":
        return jnp.transpose(blocks, (1, 0, 2)).reshape(r, n * c)
    return blocks.reshape(n * r, c)


def _blocks_from_full(full, kind):
    if kind == "col":
        r, nc = full.shape
        return jnp.transpose(full.reshape(r, NDEV, nc // NDEV), (1, 0, 2))
    nr, c = full.shape
    return full.reshape(NDEV, nr // NDEV, c)


def kernel(x, meta_tokens, g_ffn1, ffn1_w_gate, ffn1_w_up, ffn1_w_down, g_mix, w_in, b_gate, ssm_a_re, ssm_a_im, ssm_log_dt, ssm_b_re, ssm_b_im, ssm_c_re, ssm_c_im, ssm_d, ssm_w_glu, conv_w, conv_w_out, w_o, g_ffn2, ffn2_w_gate, ffn2_w_up, ffn2_w_down, g_final, loss_target, m_meta_tokens, m_g_ffn1, m_ffn1_w_gate, m_ffn1_w_up, m_ffn1_w_down, m_g_mix, m_w_in, m_b_gate, m_ssm_a_re, m_ssm_a_im, m_ssm_log_dt, m_ssm_b_re, m_ssm_b_im, m_ssm_c_re, m_ssm_c_im, m_ssm_d, m_ssm_w_glu, m_conv_w, m_conv_w_out, m_w_o, m_g_ffn2, m_ffn2_w_gate, m_ffn2_w_up, m_ffn2_w_down, m_g_final, v_meta_tokens, v_g_ffn1, v_ffn1_w_gate, v_ffn1_w_up, v_ffn1_w_down, v_g_mix, v_w_in, v_b_gate, v_ssm_a_re, v_ssm_a_im, v_ssm_log_dt, v_ssm_b_re, v_ssm_b_im, v_ssm_c_re, v_ssm_c_im, v_ssm_d, v_ssm_w_glu, v_conv_w, v_conv_w_out, v_w_o, v_g_ffn2, v_ffn2_w_gate, v_ffn2_w_up, v_ffn2_w_down, v_g_final):
    args = dict(locals())
    w = {n: args[n] for n in WEIGHTS}
    mom_m = {n: args["m_" + n] for n in WEIGHTS}
    mom_v = {n: args["v_" + n] for n in WEIGHTS}

    seq, d = x.shape[1], x.shape[2]
    n_meta = meta_tokens.shape[0]
    ds = ssm_d.shape[1]
    n_grp, n_state = ssm_a_re.shape[1], ssm_a_re.shape[2]
    gp = n_grp * n_state
    dc = conv_w.shape[3] * NDEV
    dims = (d, ds, dc, gp)
    t_real = n_meta + seq
    t_pad = -(-t_real // TOKEN_TILE) * TOKEN_TILE
    me_chip = 2 * lax.axis_index("x") + lax.axis_index("y")
    me_core = lax.axis_index("c")
    me = 2 * me_chip + me_core
    mcols, ccols = d // NDEV, dc // NDEV

    layout, r0 = [], 0
    for name, kind in BIG:
        r, c = w[name].shape[1:]
        rows = (r * c) // SLAB_COLS
        assert rows * SLAB_COLS == r * c, name
        layout.append((name, kind, r, c, r0, rows))
        r0 += rows
    rider = _pad_rows(jnp.concatenate([_as_rows(meta_tokens), _as_rows(conv_w)], axis=0), RIDER_ROWS // 2)
    rider = lax.bitcast_convert_type(rider, BF16).reshape(RIDER_ROWS, SLAB_COLS)
    wslab = jnp.concatenate([w[name][0].astype(BF16).reshape(rows, SLAB_COLS) for name, _, _, _, _, rows in layout] + [rider], axis=0)
    wall = _all_gather(_pad_rows(wslab, _slab_rows(r0 + RIDER_ROWS)), "gather_weights")
    full = {name: _full_from_blocks(wall[:, o:o + rows].reshape(NDEV, r, c), kind) for name, kind, r, c, o, rows in layout}
    small = lax.bitcast_convert_type(wall[:, r0:r0 + RIDER_ROWS].reshape(NDEV, RIDER_ROWS // 2, SLAB_COLS, 2), F32)
    small = small.reshape(NDEV, -1)
    meta_rows = -(-(n_meta * mcols) // SLAB_COLS)
    meta_full = _full_from_blocks(small[:, :n_meta * mcols].reshape(NDEV, n_meta, mcols), "col")
    cw_full = _full_from_blocks(small[:, meta_rows * SLAB_COLS:meta_rows * SLAB_COLS + 3 * ccols].reshape(NDEV, 3, ccols), "col")
    cw_rows = _pad_rows(cw_full, SUBLANES)

    a_re, a_im, ldt = ssm_a_re[0], ssm_a_im[0], ssm_log_dt[0].reshape(n_grp, 1)
    b_re_t = jnp.transpose(ssm_b_re[0], (0, 2, 1))
    b_im_t = jnp.transpose(ssm_b_im[0], (0, 2, 1))
    pw_r, pw_i, bb_r, bb_i = _s5_params_fwd(a_re, a_im, ldt, b_re_t, b_im_t)
    pw_r = pw_r.reshape(SUBLANES, gp)
    pw_i = pw_i.reshape(SUBLANES, gp)
    sub = jnp.arange(SUBLANES)[:, None]

    def fwd_tab(p, k):
        return jnp.where(sub >= k, p[k - 1][None, :], 0.0)

    def rev_tab(p, k):
        return jnp.where(sub <= SUBLANES - 1 - k, p[k - 1][None, :], 0.0)

    ltab = jnp.stack(
        [fwd_tab(pw_r, 1), fwd_tab(pw_i, 1), fwd_tab(pw_r, 2), fwd_tab(pw_i, 2), fwd_tab(pw_r, 4), fwd_tab(pw_i, 4), pw_r, pw_i,
         rev_tab(pw_r, 1), -rev_tab(pw_i, 1), rev_tab(pw_r, 2), -rev_tab(pw_i, 2), rev_tab(pw_r, 4), -rev_tab(pw_i, 4),
         pw_r[::-1], -pw_i[::-1]], axis=0)
    gh = n_grp // 2
    bc = jnp.stack([jnp.concatenate([_block_diag(bb_r[h * gh:(h + 1) * gh]), _block_diag(bb_i[h * gh:(h + 1) * gh])], axis=1)
                    for h in range(2)]).astype(BF16)
    c_re_t = jnp.transpose(ssm_c_re[0], (0, 2, 1))
    c_im_t = jnp.transpose(ssm_c_im[0], (0, 2, 1))
    cc = jnp.stack([jnp.concatenate([_block_diag(c_re_t[h * gh:(h + 1) * gh]), -_block_diag(c_im_t[h * gh:(h + 1) * gh])], axis=0)
                    for h in range(2)]).astype(BF16)

    zpad = jnp.zeros((t_pad - t_real, d), F32)
    h0 = jnp.concatenate([meta_full, x[0], zpad], axis=0)
    tgt = jnp.concatenate([jnp.zeros((n_meta, d), F32), loss_target[0], zpad], axis=0)
    h1 = _ffn_fwd(h0, g_ffn1, full["ffn1_w_gate"], full["ffn1_w_up"], full["ffn1_w_down"], "ffn1_fwd")
    mix_w = (g_mix, full["w_in"], b_gate, bc, cc, ssm_d, full["ssm_w_glu"], cw_rows, full["conv_w_out"], full["w_o"], ltab)
    h2, st, cvs = _mix_fwd(h1, *mix_w, dims)
    h3 = _ffn_fwd(h2, g_ffn2, full["ffn2_w_gate"], full["ffn2_w_up"], full["ffn2_w_down"], "ffn2_fwd")
    dh3, loss_blk, dg_final = _head(h3, tgt, g_final.reshape(1, d), n_meta, t_real, "loss_head")
    loss = lax.psum(loss_blk[0, 0], AXES)

    dh2, dg_ffn2, n2, da2, db2, s2, do2 = _ffn_bwd(h2, dh3, g_ffn2, full["ffn2_w_gate"], full["ffn2_w_up"], full["ffn2_w_down"], "ffn2_bwd")
    (dh1, u16, dp16, ge16, dz16, cg16, dyc16, mx16, dh216, dg_mix, dbg, ddsk, dcw, dlam, dbc, dcc) = _mix_bwd(h1, dh2, st, cvs, *mix_w, dims)
    dh0, dg_ffn1, n1, da1, db1, s1, do1 = _ffn_bwd(h0, dh1, g_ffn1, full["ffn1_w_gate"], full["ffn1_w_up"], full["ffn1_w_down"], "ffn1_bwd")

    dblocks = {
        "ffn1_w_gate": _blocks_from_full(_dw(n1, da1, "dw_ffn1_gate"), "col"),
        "ffn1_w_up": _blocks_from_full(_dw(n1, db1, "dw_ffn1_up"), "col"),
        "ffn1_w_down": jnp.transpose(_blocks_from_full(_dw(do1, s1, "dw_ffn1_down"), "col"), (0, 2, 1)),
        "w_in": _blocks_from_full(_dw(u16, dp16, "dw_in"), "col"),
        "ssm_w_glu": _blocks_from_full(_dw(ge16, dz16, "dw_glu"), "col"),
        "conv_w_out": _blocks_from_full(_dw(cg16, dyc16, "dw_conv_out"), "col"),
        "w_o": _blocks_from_full(_dw(mx16, dh216, "dw_o"), "row"),
        "ffn2_w_gate": _blocks_from_full(_dw(n2, da2, "dw_ffn2_gate"), "col"),
        "ffn2_w_up": _blocks_from_full(_dw(n2, db2, "dw_ffn2_up"), "col"),
        "ffn2_w_down": jnp.transpose(_blocks_from_full(_dw(do2, s2, "dw_ffn2_down"), "col"), (0, 2, 1)),
    }

    dlam4 = dlam.reshape(SUBLANES, 2, 2, gh, n_state)
    dlam_in = jnp.transpose(dlam4, (2, 0, 1, 3, 4)).reshape(2, SUBLANES, n_grp, n_state)
    hg = gp // 2
    dbb_r = jnp.concatenate([_diag_blocks(dbc[h][:, :hg], gh) for h in range(2)], axis=0)
    dbb_i = jnp.concatenate([_diag_blocks(dbc[h][:, hg:], gh) for h in range(2)], axis=0)
    da_re, da_im, dldt, dbre_t, dbim_t = _s5_params_bwd(a_re, a_im, ldt, b_re_t, b_im_t, dlam_in, dbb_r, dbb_i)
    dc_re = jnp.concatenate([_diag_blocks(dcc[h][:hg], gh) for h in range(2)], axis=0)
    dc_im = -jnp.concatenate([_diag_blocks(dcc[h][hg:], gh) for h in range(2)], axis=0)

    grads_rep = {
        "g_ffn1": dg_ffn1, "g_mix": dg_mix, "b_gate": dbg, "ssm_a_re": da_re[None], "ssm_a_im": da_im[None],
        "ssm_log_dt": dldt.reshape(1, n_grp), "ssm_b_re": jnp.transpose(dbre_t, (0, 2, 1))[None],
        "ssm_b_im": jnp.transpose(dbim_t, (0, 2, 1))[None], "ssm_c_re": jnp.transpose(dc_re, (0, 2, 1))[None],
        "ssm_c_im": jnp.transpose(dc_im, (0, 2, 1))[None], "ssm_d": ddsk, "g_ffn2": dg_ffn2, "g_final": dg_final.reshape(d),
    }

    gs = jnp.concatenate([dblocks[name].astype(BF16).reshape(NDEV, rows, SLAB_COLS) for name, _, _, _, _, rows in layout], axis=1)
    gs = _pad_rows(gs, _slab_rows(r0), axis=1)
    from_sibling = _rs_pair(gs, "reduce_pair")
    pair = _add_pairs(gs, me_core, from_sibling, "reduce_pair_add")
    from_chips = _rs_chips(pair, "reduce_chips")
    big_names = [name for name, _ in BIG]
    gb, db_, mb, vb = _adamw(_pack([w[n][0] for n in big_names]), _pack([mom_m[n][0] for n in big_names]),
                             _pack([mom_v[n][0] for n in big_names]),
                             [(pair, None), (from_chips, 0), (from_chips, 1), (from_chips, 2)], me_chip, "adamw_big")
    shard_shapes = [w[n].shape for n in big_names]
    out_g = dict(zip(big_names, _unpack(gb, shard_shapes)))
    out_d = dict(zip(big_names, _unpack(db_, shard_shapes)))
    out_m = dict(zip(big_names, _unpack(mb, shard_shapes)))
    out_v = dict(zip(big_names, _unpack(vb, shard_shapes)))

    rep_shapes = [w[n].shape for n in REPLICATED]
    small_g_shapes = rep_shapes + [(n_meta, d), (3, dc)]
    gsmall = _pack([grads_rep[n] for n in REPLICATED] + [dh0[0:n_meta], dcw[0:3]])
    gall = _all_gather(gsmall, "gather_small_grads")
    zer = [jnp.zeros((n_meta, d), F32), jnp.zeros((3, dc), F32)]
    gr, dr, mr, vr = _adamw(_pack([w[n] for n in REPLICATED] + zer), _pack([mom_m[n] for n in REPLICATED] + zer),
                            _pack([mom_v[n] for n in REPLICATED] + zer), [(gall, b) for b in range(NDEV)], None, "adamw_replicated")
    g_list = _unpack(gr, small_g_shapes)
    out_g.update(zip(REPLICATED, g_list[:len(REPLICATED)]))
    out_d.update(zip(REPLICATED, _unpack(dr, rep_shapes)))
    out_m.update(zip(REPLICATED, _unpack(mr, rep_shapes)))
    out_v.update(zip(REPLICATED, _unpack(vr, rep_shapes)))

    g_meta = lax.dynamic_slice_in_dim(g_list[-2], me * mcols, mcols, axis=1)
    g_cw = lax.dynamic_slice_in_dim(g_list[-1], me * ccols, ccols, axis=1).reshape(conv_w.shape)
    tiny = ("meta_tokens", "conv_w")
    tiny_shapes = [meta_tokens.shape, conv_w.shape]
    gt, dt_, mt, vt = _adamw(_pack([w[n] for n in tiny]), _pack([mom_m[n] for n in tiny]), _pack([mom_v[n] for n in tiny]),
                             [(_pack([g_meta, g_cw])[None], 0)], None, "adamw_tiny")
    out_g.update(zip(tiny, _unpack(gt, tiny_shapes)))
    out_d.update(zip(tiny, _unpack(dt_, tiny_shapes)))
    out_m.update(zip(tiny, _unpack(mt, tiny_shapes)))
    out_v.update(zip(tiny, _unpack(vt, tiny_shapes)))

    grad_x = dh0[n_meta:t_real][None]
    return (loss, grad_x, *[out_g[n] for n in WEIGHTS], *[out_d[n] for n in WEIGHTS],
            *[out_m[n] for n in WEIGHTS], *[out_v[n] for n in WEIGHTS])
```

```python
import functools

import jax
import jax.numpy as jnp
from jax import lax
from jax.experimental import pallas as pl
from jax.experimental.pallas import tpu as pltpu

F32 = jnp.float32
BF16 = jnp.bfloat16
MESH = pl.DeviceIdType.MESH
AXES = ("x", "y", "c")
NDEV = 8
SLAB_COLS = 1024
RMS_EPS = 1e-6
TOKEN_TILE = 256
MIX_TILE = 128
SUBLANES = 8
SCAN_LANES = 512
FFN_CHUNK = 1024
VMEM_LIMIT_BYTES = 56 * 1024 * 1024

ADAM_LR = 0.001
ADAM_B1 = 0.9
ADAM_B2 = 0.999
ADAM_EPS = 1e-08
ADAM_WD = 0.01
ADAM_STEP = 10

_VM = pl.BlockSpec(memory_space=pltpu.VMEM)
_ANY = pl.BlockSpec(memory_space=pl.ANY)


def _params(sem=("arbitrary",)):
    return pltpu.CompilerParams(dimension_semantics=sem, vmem_limit_bytes=VMEM_LIMIT_BYTES)


def _dot(a, b):
    return jnp.dot(a, b, preferred_element_type=F32)


def _dot_nt(a, b):
    return lax.dot_general(a, b, (((1,), (1,)), ((), ())), preferred_element_type=F32)


def _dot_tn(a, b):
    return lax.dot_general(a, b, (((0,), (0,)), ((), ())), preferred_element_type=F32)


def _chunks(n, step):
    return [(s, min(s + step, n)) for s in range(0, n, step)]


def _all_gather(shards, name):
    n = len(shards)

    def body(*refs):
        x_refs, out_refs = refs[:n], refs[n:2 * n]
        send_sems, recv_sems, local_sems = refs[2 * n:]
        x, y, c = lax.axis_index("x"), lax.axis_index("y"), lax.axis_index("c")
        me, sibling = (x, y, c), (x, y, 1 - c)
        chips = [(1 - x, y), (x, 1 - y), (1 - x, 1 - y)]

        def copy(i, k, block, to, src=None):
            slot = out_refs[i].at[4 * block[0] + 2 * block[1] + block[2]]
            return pltpu.make_async_remote_copy(
                src_ref=slot if src is None else src, dst_ref=slot,
                send_sem=send_sems.at[7 * i + k], recv_sem=recv_sems.at[7 * i + k], device_id=to, device_id_type=MESH)

        mine = [pltpu.make_async_copy(x_refs[i], out_refs[i].at[4 * x + 2 * y + c], local_sems.at[i]) for i in range(n)]
        for cp in mine:
            cp.start()
        first = []
        for i in range(n):
            first.append(copy(i, 0, me, sibling, src=x_refs[i]))
            first += [copy(i, 1 + j, me, (*chip, c), src=x_refs[i]) for j, chip in enumerate(chips)]
        for cp in first:
            cp.start()
        passed = []
        for j, chip in enumerate(chips):
            for i in range(n):
                copy(i, 1 + j, (*chip, c), me).wait_recv()
                cp = copy(i, 4 + j, (*chip, c), sibling)
                cp.start()
                passed.append(cp)
        for i in range(n):
            copy(i, 0, sibling, me).wait_recv()
            for j, chip in enumerate(chips):
                copy(i, 4 + j, (*chip, 1 - c), me).wait_recv()
        for cp in first + passed:
            cp.wait_send()
        for cp in mine:
            cp.wait()

    return pl.pallas_call(
        body, name=name,
        out_shape=[jax.ShapeDtypeStruct((NDEV, *s.shape), s.dtype) for s in shards],
        in_specs=[_ANY] * n, out_specs=[_ANY] * n,
        scratch_shapes=[pltpu.SemaphoreType.DMA((7 * n,)), pltpu.SemaphoreType.DMA((7 * n,)), pltpu.SemaphoreType.DMA((n,))],
    )(*shards)


def _rs_pair(blocks, name):
    n = len(blocks)

    def body(*refs):
        g_refs, out_refs = refs[:n], refs[n:2 * n]
        send_sems, recv_sems = refs[2 * n:]
        x, y, c = lax.axis_index("x"), lax.axis_index("y"), lax.axis_index("c")
        copies = []
        for i in range(n):
            for j in range(4):
                cp = pltpu.make_async_remote_copy(
                    src_ref=g_refs[i].at[2 * j + (1 - c)], dst_ref=out_refs[i].at[j],
                    send_sem=send_sems.at[4 * i + j], recv_sem=recv_sems.at[4 * i + j],
                    device_id=(x, y, 1 - c), device_id_type=MESH)
                cp.start()
                copies.append(cp)
        for cp in copies:
            cp.wait()

    return pl.pallas_call(
        body, name=name, out_shape=[jax.ShapeDtypeStruct((4, *b.shape[1:]), b.dtype) for b in blocks],
        in_specs=[_ANY] * n, out_specs=[_ANY] * n,
        scratch_shapes=[pltpu.SemaphoreType.DMA((4 * n,)), pltpu.SemaphoreType.DMA((4 * n,))],
    )(*blocks)


def _rs_chips(partials, name):
    n = len(partials)

    def body(*refs):
        p_refs, out_refs = refs[:n], refs[n:2 * n]
        send_sems, recv_sems = refs[2 * n:]
        x, y, c = lax.axis_index("x"), lax.axis_index("y"), lax.axis_index("c")
        copies = []
        for i in range(n):
            for k, (px, py) in enumerate([(1 - x, y), (x, 1 - y), (1 - x, 1 - y)]):
                cp = pltpu.make_async_remote_copy(
                    src_ref=p_refs[i].at[2 * px + py], dst_ref=out_refs[i].at[k],
                    send_sem=send_sems.at[3 * i + k], recv_sem=recv_sems.at[3 * i + k],
                    device_id=(px, py, c), device_id_type=MESH)
                cp.start()
                copies.append(cp)
        for cp in copies:
            cp.wait()

    return pl.pallas_call(
        body, name=name, out_shape=[jax.ShapeDtypeStruct((3, *p.shape[1:]), p.dtype) for p in partials],
        in_specs=[_ANY] * n, out_specs=[_ANY] * n,
        scratch_shapes=[pltpu.SemaphoreType.DMA((3 * n,)), pltpu.SemaphoreType.DMA((3 * n,))],
    )(*partials)


def _row_block(rows):
    return rows if rows <= 512 else next(k for k in (512, 256, 128, rows) if rows % k == 0)


def _add_pairs(gs, core, b, name):
    k, r, n = b.shape
    tr = _row_block(r)

    def body(core_ref, a_ref, b_ref, o_ref):
        o_ref[0] = (a_ref[0, 0].astype(F32) + b_ref[0].astype(F32)).astype(o_ref.dtype)

    spec = pl.BlockSpec((1, tr, n), lambda j, i, c: (j, i, 0))
    return pl.pallas_call(
        body, name=name,
        grid_spec=pltpu.PrefetchScalarGridSpec(
            num_scalar_prefetch=1, grid=(k, r // tr),
            in_specs=[pl.BlockSpec((1, 1, tr, n), lambda j, i, c: (j, c[0], i, 0)), spec], out_specs=spec),
        out_shape=jax.ShapeDtypeStruct(b.shape, b.dtype), compiler_params=_params(("arbitrary", "arbitrary")),
    )(core.reshape(1), gs.reshape(k, 2, r, n), b)


def _adamw(w, m, v, parts, sel, name):
    _, r, n = w.shape
    tr = _row_block(r)
    nparts = len(parts)
    bc1 = 1.0 - ADAM_B1 ** ADAM_STEP
    bc2 = 1.0 - ADAM_B2 ** ADAM_STEP

    def body(sel_ref, *refs):
        w_ref, m_ref, v_ref = refs[:3]
        p_refs = refs[3:3 + nparts]
        g_ref, d_ref, nm_ref, nv_ref = refs[3 + nparts:]
        g = p_refs[0][...].astype(F32)
        for p in p_refs[1:]:
            g = g + p[...].astype(F32)
        nm = ADAM_B1 * m_ref[...] + (1.0 - ADAM_B1) * g
        nv = ADAM_B2 * v_ref[...] + (1.0 - ADAM_B2) * (g * g)
        m_hat = nm / bc1
        v_hat = nv / bc2
        g_ref[...] = g
        d_ref[...] = -ADAM_LR * (m_hat / (jnp.sqrt(v_hat) + ADAM_EPS) + ADAM_WD * w_ref[...])
        nm_ref[...] = nm
        nv_ref[...] = nv

    def part_spec(idx):
        if idx is None:
            return pl.BlockSpec((1, tr, n), lambda i, s: (s[0], i, 0))
        return pl.BlockSpec((1, tr, n), lambda i, s, idx=idx: (idx, i, 0))

    spec = pl.BlockSpec((1, tr, n), lambda i, s: (0, i, 0))
    out = jax.ShapeDtypeStruct((1, r, n), F32)
    return pl.pallas_call(
        body, name=name,
        grid_spec=pltpu.PrefetchScalarGridSpec(
            num_scalar_prefetch=1, grid=(r // tr,),
            in_specs=[spec] * 3 + [part_spec(idx) for _, idx in parts], out_specs=[spec] * 4),
        out_shape=[out] * 4, compiler_params=_params(),
    )(jnp.zeros((1,), jnp.int32) if sel is None else sel.reshape(1), w, m, v, *[p for p, _ in parts])


def _rms_parts(h, g):
    r = lax.rsqrt(jnp.mean(h * h, axis=-1, keepdims=True) + RMS_EPS)
    xhat = h * r
    return r, xhat, xhat * g


def _rms_bwd(dn, g, r, xhat):
    dxh = dn * g
    return r * (dxh - xhat * jnp.mean(dxh * xhat, axis=-1, keepdims=True))


def _ffn_fwd(h, g, wg, wu, wd, name):
    t, d = h.shape
    f = wg.shape[1]
    tm = TOKEN_TILE
    chunks = _chunks(f, FFN_CHUNK)

    def body(h_ref, g_ref, wg_ref, wu_ref, wd_ref, o_ref):
        hv = h_ref[...]
        n = _rms_parts(hv, g_ref[...])[2].astype(BF16)
        acc = jnp.zeros((tm, d), F32)
        for s, e in chunks:
            a = _dot(n, wg_ref[:, s:e])
            b = _dot(n, wu_ref[:, s:e])
            acc = acc + _dot((a * jax.nn.sigmoid(a) * b).astype(BF16), wd_ref[s:e, :])
        o_ref[...] = hv + 0.5 * acc

    tile = pl.BlockSpec((tm, d), lambda i: (i, 0))
    return pl.pallas_call(
        body, name=name, grid=(t // tm,), in_specs=[tile, _VM, _VM, _VM, _VM], out_specs=tile,
        out_shape=jax.ShapeDtypeStruct((t, d), F32), compiler_params=_params(),
    )(h, g, wg, wu, wd)


def _ffn_bwd(h, dh_out, g, wg, wu, wd, name):
    t, d = h.shape
    f = wg.shape[1]
    tm = TOKEN_TILE
    chunks = _chunks(f, FFN_CHUNK)

    def body(h_ref, dho_ref, g_ref, wg_ref, wu_ref, wd_ref, dh_ref, dg_ref, n_ref, da_ref, db_ref, s_ref, do_ref):
        @pl.when(pl.program_id(0) == 0)
        def _():
            dg_ref[...] = jnp.zeros_like(dg_ref)

        hv = h_ref[...]
        gv = g_ref[...]
        r, xhat, n32 = _rms_parts(hv, gv)
        n = n32.astype(BF16)
        dho = dho_ref[...]
        do = (0.5 * dho).astype(BF16)
        dn = jnp.zeros((tm, d), F32)
        for s, e in chunks:
            a = _dot(n, wg_ref[:, s:e])
            b = _dot(n, wu_ref[:, s:e])
            sig = jax.nn.sigmoid(a)
            sa = a * sig
            ds = _dot_nt(do, wd_ref[s:e, :])
            da = (ds * b * (sig * (1.0 + a * (1.0 - sig)))).astype(BF16)
            db = (ds * sa).astype(BF16)
            s_ref[:, s:e] = (sa * b).astype(BF16)
            da_ref[:, s:e] = da
            db_ref[:, s:e] = db
            dn = dn + _dot_nt(da, wg_ref[:, s:e]) + _dot_nt(db, wu_ref[:, s:e])
        dh_ref[...] = dho + _rms_bwd(dn, gv, r, xhat)
        dg_ref[...] += jnp.sum(dn * xhat, axis=0, keepdims=True)
        n_ref[...] = n
        do_ref[...] = do

    tile = pl.BlockSpec((tm, d), lambda i: (i, 0))
    wide = pl.BlockSpec((tm, f), lambda i: (i, 0))
    one = pl.BlockSpec((1, d), lambda i: (0, 0))
    return pl.pallas_call(
        body, name=name, grid=(t // tm,),
        in_specs=[tile, tile, _VM, _VM, _VM, _VM],
        out_specs=[tile, one, tile, wide, wide, wide, tile],
        out_shape=[jax.ShapeDtypeStruct((t, d), F32), jax.ShapeDtypeStruct((1, d), F32),
                   jax.ShapeDtypeStruct((t, d), BF16), jax.ShapeDtypeStruct((t, f), BF16),
                   jax.ShapeDtypeStruct((t, f), BF16), jax.ShapeDtypeStruct((t, f), BF16),
                   jax.ShapeDtypeStruct((t, d), BF16)],
        compiler_params=_params(),
    )(h, dh_out, g, wg, wu, wd)


def _head(h, tgt, g, lo, hi, name):
    t, d = h.shape
    tm = TOKEN_TILE

    def body(h_ref, t_ref, g_ref, dh_ref, loss_ref, dg_ref):
        i = pl.program_id(0)

        @pl.when(i == 0)
        def _():
            loss_ref[...] = jnp.zeros_like(loss_ref)
            dg_ref[...] = jnp.zeros_like(dg_ref)

        gv = g_ref[...]
        r, xhat, y = _rms_parts(h_ref[...], gv)
        row = i * tm + lax.broadcasted_iota(jnp.int32, (tm, 1), 0)
        err = jnp.where((row >= lo) & (row < hi), y - t_ref[...], 0.0)
        loss_ref[...] += jnp.full(loss_ref.shape, 0.5 * jnp.sum(jnp.mean(err * err, axis=-1, keepdims=True)), F32)
        dy = err * (1.0 / d)
        dg_ref[...] += jnp.sum(dy * xhat, axis=0, keepdims=True)
        dh_ref[...] = _rms_bwd(dy, gv, r, xhat)

    tile = pl.BlockSpec((tm, d), lambda i: (i, 0))
    return pl.pallas_call(
        body, name=name, grid=(t // tm,), in_specs=[tile, tile, _VM],
        out_specs=[tile, pl.BlockSpec((SUBLANES, 128), lambda i: (0, 0)), pl.BlockSpec((1, d), lambda i: (0, 0))],
        out_shape=[jax.ShapeDtypeStruct((t, d), F32), jax.ShapeDtypeStruct((SUBLANES, 128), F32),
                   jax.ShapeDtypeStruct((1, d), F32)],
        compiler_params=_params(),
    )(h, tgt, g)


def _dw(a, b, name):
    t, m = a.shape
    n = b.shape[1]
    bn = next(k for k in (512, 256, n) if n % k == 0)

    def body(a_ref, b_ref, o_ref):
        o_ref[...] = _dot_tn(a_ref[...], b_ref[...])

    return pl.pallas_call(
        body, name=name, grid=(n // bn,),
        in_specs=[_VM, pl.BlockSpec((t, bn), lambda j: (0, j))], out_specs=pl.BlockSpec((m, bn), lambda j: (0, j)),
        out_shape=jax.ShapeDtypeStruct((m, n), F32), compiler_params=_params(),
    )(a, b)


def _s5_discretise(a_re, a_im, log_dt, b_re, b_im):
    dt = jnp.exp(log_dt)
    mag = jnp.exp(a_re * dt)
    lam_re = mag * jnp.cos(a_im * dt)
    lam_im = mag * jnp.sin(a_im * dt)
    den = a_re * a_re + a_im * a_im
    q_re = ((lam_re - 1.0) * a_re + lam_im * a_im) / den
    q_im = (lam_im * a_re - (lam_re - 1.0) * a_im) / den
    bb_re = q_re[:, None, :] * b_re - q_im[:, None, :] * b_im
    bb_im = q_re[:, None, :] * b_im + q_im[:, None, :] * b_re
    return lam_re, lam_im, bb_re, bb_im


def _s5_params_fwd(a_re, a_im, log_dt, b_re, b_im):
    g, p = a_re.shape
    c = b_re.shape[1]

    def body(are_ref, aim_ref, ldt_ref, bre_ref, bim_ref, pwr_ref, pwi_ref, bbr_ref, bbi_ref):
        lr, li, bbr, bbi = _s5_discretise(are_ref[...], aim_ref[...], ldt_ref[...], bre_ref[...], bim_ref[...])
        bbr_ref[...] = bbr
        bbi_ref[...] = bbi
        pr, pi = lr, li
        pwr_ref[0] = pr
        pwi_ref[0] = pi
        for k in range(1, SUBLANES):
            pr, pi = pr * lr - pi * li, pr * li + pi * lr
            pwr_ref[k] = pr
            pwi_ref[k] = pi

    return pl.pallas_call(
        body, name="s5_params_fwd",
        out_shape=[jax.ShapeDtypeStruct((SUBLANES, g, p), F32), jax.ShapeDtypeStruct((SUBLANES, g, p), F32),
                   jax.ShapeDtypeStruct((g, c, p), F32), jax.ShapeDtypeStruct((g, c, p), F32)],
    )(a_re, a_im, log_dt, b_re, b_im)


def _s5_params_bwd(a_re, a_im, log_dt, b_re, b_im, dlam, dbb_re, dbb_im):
    g, p = a_re.shape
    c = b_re.shape[1]

    def body(are_ref, aim_ref, ldt_ref, bre_ref, bim_ref, dlam_ref, dbr_ref, dbi_ref,
             dare_ref, daim_ref, dldt_ref, dbre_ref, dbim_ref):
        dlr = jnp.sum(dlam_ref[0], axis=0)
        dli = jnp.sum(dlam_ref[1], axis=0)
        _, vjp = jax.vjp(_s5_discretise, are_ref[...], aim_ref[...], ldt_ref[...], bre_ref[...], bim_ref[...])
        dare, daim, dldt, dbre, dbim = vjp((dlr, dli, dbr_ref[...], dbi_ref[...]))
        dare_ref[...] = dare
        daim_ref[...] = daim
        dldt_ref[...] = dldt
        dbre_ref[...] = dbre
        dbim_ref[...] = dbim

    return pl.pallas_call(
        body, name="s5_params_bwd",
        out_shape=[jax.ShapeDtypeStruct((g, p), F32), jax.ShapeDtypeStruct((g, p), F32),
                   jax.ShapeDtypeStruct((g, 1), F32), jax.ShapeDtypeStruct((g, c, p), F32),
                   jax.ShapeDtypeStruct((g, c, p), F32)],
    )(a_re, a_im, log_dt, b_re, b_im, dlam, dbb_re, dbb_im)


def _scan_chunks(gp):
    hg = gp // 2
    w = min(SCAN_LANES, hg)
    return w, [(half * hg + k * w, half * gp + k * w, half * gp + hg + k * w) for half in range(2) for k in range(hg // w)]


def _cmul_acc(xr, xi, tr, ti, sr, si):
    return xr + tr * sr - ti * si, xi + tr * si + ti * sr


def _scan_fwd(buf_ref, row0, tm, ltab_ref, cin_ref, cout_ref, gp):
    w, chunks = _scan_chunks(gp)
    for lo_t, lo_r, lo_i in chunks:
        def body(r, carry, lo_t=lo_t, lo_r=lo_r, lo_i=lo_i):
            cr, ci = carry
            row = pl.multiple_of(row0 + r * SUBLANES, SUBLANES)
            xr = buf_ref[pl.ds(row, SUBLANES), lo_r:lo_r + w]
            xi = buf_ref[pl.ds(row, SUBLANES), lo_i:lo_i + w]
            for tab, shift in ((0, 1), (2, 2), (4, 4)):
                xr, xi = _cmul_acc(xr, xi, ltab_ref[tab, :, lo_t:lo_t + w], ltab_ref[tab + 1, :, lo_t:lo_t + w],
                                   pltpu.roll(xr, shift, 0), pltpu.roll(xi, shift, 0))
            xr, xi = _cmul_acc(xr, xi, ltab_ref[6, :, lo_t:lo_t + w], ltab_ref[7, :, lo_t:lo_t + w], cr, ci)
            buf_ref[pl.ds(row, SUBLANES), lo_r:lo_r + w] = xr
            buf_ref[pl.ds(row, SUBLANES), lo_i:lo_i + w] = xi
            last = SUBLANES - 1
            return (jnp.broadcast_to(xr[last:last + 1], (SUBLANES, w)), jnp.broadcast_to(xi[last:last + 1], (SUBLANES, w)))

        cr, ci = lax.fori_loop(0, tm // SUBLANES, body,
                               (cin_ref[0:SUBLANES, lo_r:lo_r + w], cin_ref[0:SUBLANES, lo_i:lo_i + w]))
        if cout_ref is not None:
            cout_ref[0:SUBLANES, lo_r:lo_r + w] = cr
            cout_ref[0:SUBLANES, lo_i:lo_i + w] = ci


def _scan_rev(g_ref, hext_ref, tm, ltab_ref, gc_ref, dlam_ref, gp):
    w, chunks = _scan_chunks(gp)
    nb = tm // SUBLANES
    for lo_t, lo_r, lo_i in chunks:
        def body(k, carry, lo_t=lo_t, lo_r=lo_r, lo_i=lo_i):
            cr, ci, ar, ai = carry
            row = pl.multiple_of((nb - 1 - k) * SUBLANES, SUBLANES)
            xr = g_ref[pl.ds(row, SUBLANES), lo_r:lo_r + w]
            xi = g_ref[pl.ds(row, SUBLANES), lo_i:lo_i + w]
            for tab, shift in ((8, 7), (10, 6), (12, 4)):
                xr, xi = _cmul_acc(xr, xi, ltab_ref[tab, :, lo_t:lo_t + w], ltab_ref[tab + 1, :, lo_t:lo_t + w],
                                   pltpu.roll(xr, shift, 0), pltpu.roll(xi, shift, 0))
            xr, xi = _cmul_acc(xr, xi, ltab_ref[14, :, lo_t:lo_t + w], ltab_ref[15, :, lo_t:lo_t + w], cr, ci)
            g_ref[pl.ds(row, SUBLANES), lo_r:lo_r + w] = xr
            g_ref[pl.ds(row, SUBLANES), lo_i:lo_i + w] = xi
            first = lax.broadcasted_iota(jnp.int32, (SUBLANES, w), 0) == 0
            prev = pl.ds(row, SUBLANES)
            here = pl.ds(row + SUBLANES, SUBLANES)
            hpr = jnp.where(first, pltpu.roll(hext_ref[prev, lo_r:lo_r + w], 1, 0), pltpu.roll(hext_ref[here, lo_r:lo_r + w], 1, 0))
            hpi = jnp.where(first, pltpu.roll(hext_ref[prev, lo_i:lo_i + w], 1, 0), pltpu.roll(hext_ref[here, lo_i:lo_i + w], 1, 0))
            ar = ar + xr * hpr + xi * hpi
            ai = ai - xr * hpi + xi * hpr
            return (jnp.broadcast_to(xr[0:1], (SUBLANES, w)), jnp.broadcast_to(xi[0:1], (SUBLANES, w)), ar, ai)

        cr, ci, ar, ai = lax.fori_loop(
            0, nb, body, (gc_ref[:, lo_r:lo_r + w], gc_ref[:, lo_i:lo_i + w], dlam_ref[:, lo_r:lo_r + w], dlam_ref[:, lo_i:lo_i + w]))
        gc_ref[:, lo_r:lo_r + w] = cr
        gc_ref[:, lo_i:lo_i + w] = ci
        dlam_ref[:, lo_r:lo_r + w] = ar
        dlam_ref[:, lo_i:lo_i + w] = ai


def _mix_math(h, gm, win_ref, bg, bc_ref, cc_ref, dsk, wglu_ref, cw, wco_ref, ltab_ref,
              hbuf_ref, hrow0, st_in_ref, st_out_ref, cext_ref, dims):
    d, ds, dc, gp = dims
    tm = h.shape[0]
    dsh = ds // 2
    r, xhat, n32 = _rms_parts(h, gm)
    u = n32.astype(BF16)
    o1, o2, o3 = ds + dc, ds + 2 * dc, ds + 3 * dc
    us = _dot(u, win_ref[:, 0:ds])
    v = _dot(u, win_ref[:, ds:o1])
    gb = _dot(u, win_ref[:, o1:o2])
    gcv = _dot(u, win_ref[:, o2:o3])
    gs = jax.nn.sigmoid(_dot(u, win_ref[:, o3:o3 + d]) + bg[:, 0:d])
    gcg = jax.nn.sigmoid(_dot(u, win_ref[:, o3 + d:o3 + 2 * d]) + bg[:, d:2 * d])
    us16 = us.astype(BF16)
    rows = slice(hrow0, hrow0 + tm)
    for half in range(2):
        hbuf_ref[rows, half * gp:(half + 1) * gp] = _dot(us16[:, half * dsh:(half + 1) * dsh], bc_ref[half])
    _scan_fwd(hbuf_ref, hrow0, tm, ltab_ref, st_in_ref, st_out_ref, gp)
    y5 = jnp.concatenate(
        [_dot(hbuf_ref[rows, half * gp:(half + 1) * gp].astype(BF16), cc_ref[half]) for half in range(2)], axis=1) + dsk * us
    ge16 = jax.nn.gelu(y5).astype(BF16)
    z = _dot(ge16, wglu_ref[...])
    z1, sz = z[:, 0:d], jax.nn.sigmoid(z[:, d:2 * d])
    ys = z1 * sz
    cin = gcv * v
    cext_ref[SUBLANES:SUBLANES + tm, :] = cin
    cv = cw[0:1] * cext_ref[SUBLANES - 2:SUBLANES - 2 + tm, :] + cw[1:2] * cext_ref[SUBLANES - 1:SUBLANES - 1 + tm, :] + cw[2:3] * cin
    cg16 = (gb * cv).astype(BF16)
    yc = _dot(cg16, wco_ref[...])
    mixed = gs * ys + gcg * yc
    return dict(r=r, xhat=xhat, u=u, us=us, us16=us16, v=v, gb=gb, gcv=gcv, gs=gs, gcg=gcg, y5=y5, ge16=ge16,
                z1=z1, sz=sz, ys=ys, cin=cin, cv=cv, cg16=cg16, yc=yc, mixed=mixed)


def _mix_fwd(h, gm, win, bg, bc, cc, dsk, wglu, cw, wco, wo, ltab, dims):
    d, ds, dc, gp = dims
    t = h.shape[0]
    tm = MIX_TILE
    nt = t // tm

    def body(h_ref, gm_ref, win_ref, bg_ref, bc_ref, cc_ref, dsk_ref, wglu_ref, cw_ref, wco_ref, wo_ref, ltab_ref,
             h2_ref, st_ref, cvs_ref, hbuf_ref, carry_ref, cext_ref):
        @pl.when(pl.program_id(0) == 0)
        def _():
            carry_ref[...] = jnp.zeros_like(carry_ref)
            cext_ref[0:SUBLANES, :] = jnp.zeros((SUBLANES, dc), F32)

        st_ref[0] = carry_ref[...]
        cvs_ref[0] = cext_ref[0:SUBLANES, :]
        hv = h_ref[...]
        m = _mix_math(hv, gm_ref[...], win_ref, bg_ref[...], bc_ref, cc_ref, dsk_ref[...], wglu_ref, cw_ref[...], wco_ref,
                      ltab_ref, hbuf_ref, 0, carry_ref, carry_ref, cext_ref, dims)
        h2_ref[...] = hv + _dot(m["mixed"].astype(BF16), wo_ref[...])
        cext_ref[0:SUBLANES, :] = cext_ref[tm:tm + SUBLANES, :]

    tile = pl.BlockSpec((tm, d), lambda i: (i, 0))
    return pl.pallas_call(
        body, name="mix_fwd", grid=(nt,),
        in_specs=[tile] + [_VM] * 11,
        out_specs=[tile, pl.BlockSpec((1, SUBLANES, 2 * gp), lambda i: (i, 0, 0)), pl.BlockSpec((1, SUBLANES, dc), lambda i: (i, 0, 0))],
        out_shape=[jax.ShapeDtypeStruct((t, d), F32), jax.ShapeDtypeStruct((nt, SUBLANES, 2 * gp), F32),
                   jax.ShapeDtypeStruct((nt, SUBLANES, dc), F32)],
        scratch_shapes=[pltpu.VMEM((tm, 2 * gp), F32), pltpu.VMEM((SUBLANES, 2 * gp), F32), pltpu.VMEM((SUBLANES + tm, dc), F32)],
        compiler_params=_params(),
    )(h, gm, win, bg, bc, cc, dsk, wglu, cw, wco, wo, ltab)


def _mix_bwd(h, dh2, st, cvs, gm, win, bg, bc, cc, dsk, wglu, cw, wco, wo, ltab, dims):
    d, ds, dc, gp = dims
    t = h.shape[0]
    tm = MIX_TILE
    nt = t // tm
    dsh = ds // 2
    ncols = ds + 3 * dc + 2 * d

    def body(h_ref, dh2_ref, st_ref, cvs_ref, gm_ref, win_ref, bg_ref, bc_ref, cc_ref, dsk_ref, wglu_ref, cw_ref, wco_ref,
             wo_ref, ltab_ref,
             dh1_ref, u_ref, dp_ref, ge_ref, dz_ref, cg_ref, dyc_ref, mx_ref, dh216_ref,
             dgm_ref, dbg_ref, ddsk_ref, dcw_ref, dlam_ref, dbc_ref, dcc_ref,
             hext_ref, gbuf_ref, gcarry_ref, cext_ref, dcvext_ref):
        @pl.when(pl.program_id(0) == 0)
        def _():
            for ref in (dgm_ref, dbg_ref, ddsk_ref, dcw_ref, dlam_ref, dbc_ref, dcc_ref, gcarry_ref):
                ref[...] = jnp.zeros_like(ref)
            dcvext_ref[tm:tm + SUBLANES, :] = jnp.zeros((SUBLANES, dc), F32)

        hext_ref[0:SUBLANES, :] = st_ref[0]
        cext_ref[0:SUBLANES, :] = cvs_ref[0]
        hv = h_ref[...]
        gmv = gm_ref[...]
        cw_v = cw_ref[...]
        dskv = dsk_ref[...]
        m = _mix_math(hv, gmv, win_ref, bg_ref[...], bc_ref, cc_ref, dskv, wglu_ref, cw_v, wco_ref,
                      ltab_ref, hext_ref, SUBLANES, hext_ref, None, cext_ref, dims)
        dh2v = dh2_ref[...]
        dh216 = dh2v.astype(BF16)
        dmixed = _dot_nt(dh216, wo_ref[...])
        gs, gcg, ys, yc, sz = m["gs"], m["gcg"], m["ys"], m["yc"], m["sz"]
        dys = dmixed * gs
        dyc16 = (dmixed * gcg).astype(BF16)
        dpgs = dmixed * ys * gs * (1.0 - gs)
        dpgc = dmixed * yc * gcg * (1.0 - gcg)
        dz16 = jnp.concatenate([dys * sz, dys * m["z1"] * sz * (1.0 - sz)], axis=1).astype(BF16)
        dge = _dot_nt(dz16, wglu_ref[...])
        dy5 = jax.vjp(jax.nn.gelu, m["y5"])[1](dge)[0]
        dy516 = dy5.astype(BF16)
        for half in range(2):
            gbuf_ref[:, half * gp:(half + 1) * gp] = _dot_nt(dy516[:, half * dsh:(half + 1) * dsh], cc_ref[half])
        _scan_rev(gbuf_ref, hext_ref, tm, ltab_ref, gcarry_ref, dlam_ref, gp)
        dus = []
        for half in range(2):
            g16 = gbuf_ref[:, half * gp:(half + 1) * gp].astype(BF16)
            dus.append(_dot_nt(g16, bc_ref[half]))
            dbc_ref[half] += _dot_tn(m["us16"][:, half * dsh:(half + 1) * dsh], g16)
            h16 = hext_ref[SUBLANES:SUBLANES + tm, half * gp:(half + 1) * gp].astype(BF16)
            dcc_ref[half] += _dot_tn(h16, dy516[:, half * dsh:(half + 1) * dsh])
        dus = jnp.concatenate(dus, axis=1) + dskv * dy5
        ddsk_ref[...] += jnp.sum(dy5 * m["us"], axis=0, keepdims=True)
        dcg = _dot_nt(dyc16, wco_ref[...])
        dgb = dcg * m["cv"]
        dcv = dcg * m["gb"]
        dcvext_ref[0:tm, :] = dcv
        dcin = cw_v[2:3] * dcv + cw_v[1:2] * dcvext_ref[1:1 + tm, :] + cw_v[0:1] * dcvext_ref[2:2 + tm, :]
        dcw_ref[0:1, :] += jnp.sum(dcv * cext_ref[SUBLANES - 2:SUBLANES - 2 + tm, :], axis=0, keepdims=True)
        dcw_ref[1:2, :] += jnp.sum(dcv * cext_ref[SUBLANES - 1:SUBLANES - 1 + tm, :], axis=0, keepdims=True)
        dcw_ref[2:3, :] += jnp.sum(dcv * m["cin"], axis=0, keepdims=True)
        dcvext_ref[tm:tm + SUBLANES, :] = dcvext_ref[0:SUBLANES, :]
        dp16 = jnp.concatenate([dus, dcin * m["gcv"], dgb, dcin * m["v"], dpgs, dpgc], axis=1).astype(BF16)
        du = _dot_nt(dp16, win_ref[...])
        dh1_ref[...] = dh2v + _rms_bwd(du, gmv, m["r"], m["xhat"])
        dgm_ref[...] += jnp.sum(du * m["xhat"], axis=0, keepdims=True)
        dbg_ref[...] += jnp.concatenate([jnp.sum(dpgs, axis=0, keepdims=True), jnp.sum(dpgc, axis=0, keepdims=True)], axis=1)
        u_ref[...] = m["u"]
        dp_ref[...] = dp16
        ge_ref[...] = m["ge16"]
        dz_ref[...] = dz16
        cg_ref[...] = m["cg16"]
        dyc_ref[...] = dyc16
        mx_ref[...] = m["mixed"].astype(BF16)
        dh216_ref[...] = dh216

    def rev(cols):
        return pl.BlockSpec((tm, cols), lambda j: (nt - 1 - j, 0))

    def rev3(cols):
        return pl.BlockSpec((1, SUBLANES, cols), lambda j: (nt - 1 - j, 0, 0))

    def bf(cols):
        return jax.ShapeDtypeStruct((t, cols), BF16)

    return pl.pallas_call(
        body, name="mix_bwd", grid=(nt,),
        in_specs=[rev(d), rev(d), rev3(2 * gp), rev3(dc)] + [_VM] * 11,
        out_specs=[rev(d), rev(d), rev(ncols), rev(ds), rev(2 * d), rev(dc), rev(d), rev(d), rev(d)] + [_VM] * 7,
        out_shape=[jax.ShapeDtypeStruct((t, d), F32), bf(d), bf(ncols), bf(ds), bf(2 * d), bf(dc), bf(d), bf(d), bf(d),
                   jax.ShapeDtypeStruct((1, d), F32), jax.ShapeDtypeStruct((1, 2 * d), F32), jax.ShapeDtypeStruct((1, ds), F32),
                   jax.ShapeDtypeStruct((SUBLANES, dc), F32), jax.ShapeDtypeStruct((SUBLANES, 2 * gp), F32),
                   jax.ShapeDtypeStruct((2, dsh, gp), F32), jax.ShapeDtypeStruct((2, gp, dsh), F32)],
        scratch_shapes=[pltpu.VMEM((SUBLANES + tm, 2 * gp), F32), pltpu.VMEM((tm, 2 * gp), F32),
                        pltpu.VMEM((SUBLANES, 2 * gp), F32), pltpu.VMEM((SUBLANES + tm, dc), F32),
                        pltpu.VMEM((tm + SUBLANES, dc), F32)],
        compiler_params=_params(),
    )(h, dh2, st, cvs, gm, win, bg, bc, cc, dsk, wglu, cw, wco, wo, ltab)


def _pad_rows(a, rows, axis=0):
    pad = [(0, 0)] * a.ndim
    pad[axis] = (0, rows - a.shape[axis])
    return jnp.pad(a, pad)


def _as_rows(a):
    flat = a.reshape(-1)
    n = -(-flat.shape[0] // SLAB_COLS) * SLAB_COLS
    return jnp.pad(flat, (0, n - flat.shape[0])).reshape(-1, SLAB_COLS)


def _pack(arrs):
    rows = jnp.concatenate([_as_rows(a) for a in arrs], axis=0)
    return _pad_rows(rows, -(-rows.shape[0] // 16) * 16)


def _unpack(slab, shapes):
    out, r = [], 0
    for shp in shapes:
        size = 1
        for s in shp:
            size *= s
        n = -(-size // SLAB_COLS)
        out.append(slab[r:r + n].reshape(-1)[:size].reshape(shp))
        r += n
    return out


def _block_diag(blocks):
    n, a, b = blocks.shape
    eye = jnp.eye(n, dtype=blocks.dtype)
    return (blocks[:, :, None, :] * eye[:, None, :, None]).reshape(n * a, n * b)


def _diag_blocks(mat, n):
    a, b = mat.shape[0] // n, mat.shape[1] // n
    eye = jnp.eye(n, dtype=mat.dtype)
    return jnp.sum(mat.reshape(n, a, n, b) * eye[:, None, :, None], axis=2)


BIG = (("ffn1_w_gate", "col"), ("ffn1_w_up", "col"), ("ffn1_w_down", "row"), ("w_in", "col"), ("ssm_w_glu", "col"),
       ("conv_w_out", "col"), ("w_o", "row"), ("ffn2_w_gate", "col"), ("ffn2_w_up", "col"), ("ffn2_w_down", "row"))
REPLICATED = ("g_ffn1", "g_mix", "b_gate", "ssm_a_re", "ssm_a_im", "ssm_log_dt", "ssm_b_re", "ssm_b_im", "ssm_c_re",
              "ssm_c_im", "ssm_d", "g_ffn2", "g_final")
WEIGHTS = ("meta_tokens", "g_ffn1", "ffn1_w_gate", "ffn1_w_up", "ffn1_w_down", "g_mix", "w_in", "b_gate", "ssm_a_re",
           "ssm_a_im", "ssm_log_dt", "ssm_b_re", "ssm_b_im", "ssm_c_re", "ssm_c_im", "ssm_d", "ssm_w_glu", "conv_w",
           "conv_w_out", "w_o", "g_ffn2", "ffn2_w_gate", "ffn2_w_up", "ffn2_w_down", "g_final")


def _full_from_blocks(blocks, kind):
    n, r, c = blocks.shape
    if kind == "col":
        return jnp.transpose(blocks, (1, 0, 2)).reshape(r, n * c)
    return blocks.reshape(n * r, c)


def _blocks_from_full(full, kind):
    if kind == "col":
        r, nc = full.shape
        return jnp.transpose(full.reshape(r, NDEV, nc // NDEV), (1, 0, 2))
    nr, c = full.shape
    return full.reshape(NDEV, nr // NDEV, c)


def kernel(x, meta_tokens, g_ffn1, ffn1_w_gate, ffn1_w_up, ffn1_w_down, g_mix, w_in, b_gate, ssm_a_re, ssm_a_im, ssm_log_dt, ssm_b_re, ssm_b_im, ssm_c_re, ssm_c_im, ssm_d, ssm_w_glu, conv_w, conv_w_out, w_o, g_ffn2, ffn2_w_gate, ffn2_w_up, ffn2_w_down, g_final, loss_target, m_meta_tokens, m_g_ffn1, m_ffn1_w_gate, m_ffn1_w_up, m_ffn1_w_down, m_g_mix, m_w_in, m_b_gate, m_ssm_a_re, m_ssm_a_im, m_ssm_log_dt, m_ssm_b_re, m_ssm_b_im, m_ssm_c_re, m_ssm_c_im, m_ssm_d, m_ssm_w_glu, m_conv_w, m_conv_w_out, m_w_o, m_g_ffn2, m_ffn2_w_gate, m_ffn2_w_up, m_ffn2_w_down, m_g_final, v_meta_tokens, v_g_ffn1, v_ffn1_w_gate, v_ffn1_w_up, v_ffn1_w_down, v_g_mix, v_w_in, v_b_gate, v_ssm_a_re, v_ssm_a_im, v_ssm_log_dt, v_ssm_b_re, v_ssm_b_im, v_ssm_c_re, v_ssm_c_im, v_ssm_d, v_ssm_w_glu, v_conv_w, v_conv_w_out, v_w_o, v_g_ffn2, v_ffn2_w_gate, v_ffn2_w_up, v_ffn2_w_down, v_g_final):
    args = dict(locals())
    w = {n: args[n] for n in WEIGHTS}
    mom_m = {n: args["m_" + n] for n in WEIGHTS}
    mom_v = {n: args["v_" + n] for n in WEIGHTS}

    seq, d = x.shape[1], x.shape[2]
    n_meta = meta_tokens.shape[0]
    ds = ssm_d.shape[1]
    n_grp, n_state = ssm_a_re.shape[1], ssm_a_re.shape[2]
    gp = n_grp * n_state
    dc = conv_w.shape[3] * NDEV
    dims = (d, ds, dc, gp)
    t_real = n_meta + seq
    t_pad = -(-t_real // TOKEN_TILE) * TOKEN_TILE
    me_chip = 2 * lax.axis_index("x") + lax.axis_index("y")
    me_core = lax.axis_index("c")
    me = 2 * me_chip + me_core
    mcols, ccols = d // NDEV, dc // NDEV

    cw_shard = _pad_rows(_pad_rows(conv_w.reshape(3, ccols), SUBLANES), 128, axis=1)
    gathered = _all_gather([w[name][0].astype(BF16) for name, _ in BIG] + [meta_tokens, cw_shard], "gather_weights")
    full = {name: _full_from_blocks(gathered[i], kind) for i, (name, kind) in enumerate(BIG)}
    meta_full = _full_from_blocks(gathered[-2], "col")
    cw_rows = _pad_rows(_full_from_blocks(gathered[-1][:, 0:3, 0:ccols], "col"), SUBLANES)

    a_re, a_im, ldt = ssm_a_re[0], ssm_a_im[0], ssm_log_dt[0].reshape(n_grp, 1)
    b_re_t = jnp.transpose(ssm_b_re[0], (0, 2, 1))
    b_im_t = jnp.transpose(ssm_b_im[0], (0, 2, 1))
    pw_r, pw_i, bb_r, bb_i = _s5_params_fwd(a_re, a_im, ldt, b_re_t, b_im_t)
    pw_r = pw_r.reshape(SUBLANES, gp)
    pw_i = pw_i.reshape(SUBLANES, gp)
    sub = jnp.arange(SUBLANES)[:, None]

    def fwd_tab(p, k):
        return jnp.where(sub >= k, p[k - 1][None, :], 0.0)

    def rev_tab(p, k):
        return jnp.where(sub <= SUBLANES - 1 - k, p[k - 1][None, :], 0.0)

    ltab = jnp.stack(
        [fwd_tab(pw_r, 1), fwd_tab(pw_i, 1), fwd_tab(pw_r, 2), fwd_tab(pw_i, 2), fwd_tab(pw_r, 4), fwd_tab(pw_i, 4), pw_r, pw_i,
         rev_tab(pw_r, 1), -rev_tab(pw_i, 1), rev_tab(pw_r, 2), -rev_tab(pw_i, 2), rev_tab(pw_r, 4), -rev_tab(pw_i, 4),
         pw_r[::-1], -pw_i[::-1]], axis=0)
    gh = n_grp // 2
    bc = jnp.stack([jnp.concatenate([_block_diag(bb_r[h * gh:(h + 1) * gh]), _block_diag(bb_i[h * gh:(h + 1) * gh])], axis=1)
                    for h in range(2)]).astype(BF16)
    c_re_t = jnp.transpose(ssm_c_re[0], (0, 2, 1))
    c_im_t = jnp.transpose(ssm_c_im[0], (0, 2, 1))
    cc = jnp.stack([jnp.concatenate([_block_diag(c_re_t[h * gh:(h + 1) * gh]), -_block_diag(c_im_t[h * gh:(h + 1) * gh])], axis=0)
                    for h in range(2)]).astype(BF16)

    zpad = jnp.zeros((t_pad - t_real, d), F32)
    h0 = jnp.concatenate([meta_full, x[0], zpad], axis=0)
    tgt = jnp.concatenate([jnp.zeros((n_meta, d), F32), loss_target[0], zpad], axis=0)
    h1 = _ffn_fwd(h0, g_ffn1, full["ffn1_w_gate"], full["ffn1_w_up"], full["ffn1_w_down"], "ffn1_fwd")
    mix_w = (g_mix, full["w_in"], b_gate, bc, cc, ssm_d, full["ssm_w_glu"], cw_rows, full["conv_w_out"], full["w_o"], ltab)
    h2, st, cvs = _mix_fwd(h1, *mix_w, dims)
    h3 = _ffn_fwd(h2, g_ffn2, full["ffn2_w_gate"], full["ffn2_w_up"], full["ffn2_w_down"], "ffn2_fwd")
    dh3, loss_blk, dg_final = _head(h3, tgt, g_final.reshape(1, d), n_meta, t_real, "loss_head")
    loss = lax.psum(loss_blk[0, 0], AXES)

    dh2, dg_ffn2, n2, da2, db2, s2, do2 = _ffn_bwd(h2, dh3, g_ffn2, full["ffn2_w_gate"], full["ffn2_w_up"], full["ffn2_w_down"], "ffn2_bwd")
    (dh1, u16, dp16, ge16, dz16, cg16, dyc16, mx16, dh216, dg_mix, dbg, ddsk, dcw, dlam, dbc, dcc) = _mix_bwd(h1, dh2, st, cvs, *mix_w, dims)
    dh0, dg_ffn1, n1, da1, db1, s1, do1 = _ffn_bwd(h0, dh1, g_ffn1, full["ffn1_w_gate"], full["ffn1_w_up"], full["ffn1_w_down"], "ffn1_bwd")

    dblocks = {
        "ffn1_w_gate": _blocks_from_full(_dw(n1, da1, "dw_ffn1_gate"), "col"),
        "ffn1_w_up": _blocks_from_full(_dw(n1, db1, "dw_ffn1_up"), "col"),
        "ffn1_w_down": jnp.transpose(_blocks_from_full(_dw(do1, s1, "dw_ffn1_down"), "col"), (0, 2, 1)),
        "w_in": _blocks_from_full(_dw(u16, dp16, "dw_in"), "col"),
        "ssm_w_glu": _blocks_from_full(_dw(ge16, dz16, "dw_glu"), "col"),
        "conv_w_out": _blocks_from_full(_dw(cg16, dyc16, "dw_conv_out"), "col"),
        "w_o": _blocks_from_full(_dw(mx16, dh216, "dw_o"), "row"),
        "ffn2_w_gate": _blocks_from_full(_dw(n2, da2, "dw_ffn2_gate"), "col"),
        "ffn2_w_up": _blocks_from_full(_dw(n2, db2, "dw_ffn2_up"), "col"),
        "ffn2_w_down": jnp.transpose(_blocks_from_full(_dw(do2, s2, "dw_ffn2_down"), "col"), (0, 2, 1)),
    }

    dlam4 = dlam.reshape(SUBLANES, 2, 2, gh, n_state)
    dlam_in = jnp.transpose(dlam4, (2, 0, 1, 3, 4)).reshape(2, SUBLANES, n_grp, n_state)
    hg = gp // 2
    dbb_r = jnp.concatenate([_diag_blocks(dbc[h][:, :hg], gh) for h in range(2)], axis=0)
    dbb_i = jnp.concatenate([_diag_blocks(dbc[h][:, hg:], gh) for h in range(2)], axis=0)
    da_re, da_im, dldt, dbre_t, dbim_t = _s5_params_bwd(a_re, a_im, ldt, b_re_t, b_im_t, dlam_in, dbb_r, dbb_i)
    dc_re = jnp.concatenate([_diag_blocks(dcc[h][:hg], gh) for h in range(2)], axis=0)
    dc_im = -jnp.concatenate([_diag_blocks(dcc[h][hg:], gh) for h in range(2)], axis=0)

    grads_rep = {
        "g_ffn1": dg_ffn1, "g_mix": dg_mix, "b_gate": dbg, "ssm_a_re": da_re[None], "ssm_a_im": da_im[None],
        "ssm_log_dt": dldt.reshape(1, n_grp), "ssm_b_re": jnp.transpose(dbre_t, (0, 2, 1))[None],
        "ssm_b_im": jnp.transpose(dbim_t, (0, 2, 1))[None], "ssm_c_re": jnp.transpose(dc_re, (0, 2, 1))[None],
        "ssm_c_im": jnp.transpose(dc_im, (0, 2, 1))[None], "ssm_d": ddsk, "g_ffn2": dg_ffn2, "g_final": dg_final.reshape(d),
    }

    big_names = [name for name, _ in BIG]
    gs = [dblocks[name].astype(BF16) for name in big_names]
    from_sibling = _rs_pair(gs, "reduce_pair")
    pairs = [_add_pairs(g, me_core, b, "reduce_pair_add_" + name) for g, b, name in zip(gs, from_sibling, big_names)]
    from_chips = _rs_chips(pairs, "reduce_chips")
    out_g, out_d, out_m, out_v = {}, {}, {}, {}
    for name, pair, fc in zip(big_names, pairs, from_chips):
        out_g[name], out_d[name], out_m[name], out_v[name] = _adamw(
            w[name], mom_m[name], mom_v[name], [(pair, None), (fc, 0), (fc, 1), (fc, 2)], me_chip, "adamw_" + name)

    rep_shapes = [w[n].shape for n in REPLICATED]
    small_g_shapes = rep_shapes + [(n_meta, d), (3, dc)]
    gsmall = _pack([grads_rep[n] for n in REPLICATED] + [dh0[0:n_meta], dcw[0:3]])
    gall = _all_gather([gsmall], "gather_small_grads")[0]
    zer = [jnp.zeros((n_meta, d), F32), jnp.zeros((3, dc), F32)]
    gr, dr, mr, vr = [o[0] for o in _adamw(
        _pack([w[n] for n in REPLICATED] + zer)[None], _pack([mom_m[n] for n in REPLICATED] + zer)[None],
        _pack([mom_v[n] for n in REPLICATED] + zer)[None], [(gall, b) for b in range(NDEV)], None, "adamw_replicated")]
    g_list = _unpack(gr, small_g_shapes)
    out_g.update(zip(REPLICATED, g_list[:len(REPLICATED)]))
    out_d.update(zip(REPLICATED, _unpack(dr, rep_shapes)))
    out_m.update(zip(REPLICATED, _unpack(mr, rep_shapes)))
    out_v.update(zip(REPLICATED, _unpack(vr, rep_shapes)))

    g_meta = lax.dynamic_slice_in_dim(g_list[-2], me * mcols, mcols, axis=1)
    g_cw = lax.dynamic_slice_in_dim(g_list[-1], me * ccols, ccols, axis=1).reshape(conv_w.shape)
    tiny = ("meta_tokens", "conv_w")
    tiny_shapes = [meta_tokens.shape, conv_w.shape]
    gt, dt_, mt, vt = [o[0] for o in _adamw(
        _pack([w[n] for n in tiny])[None], _pack([mom_m[n] for n in tiny])[None], _pack([mom_v[n] for n in tiny])[None],
        [(_pack([g_meta, g_cw])[None], 0)], None, "adamw_tiny")]
    out_g.update(zip(tiny, _unpack(gt, tiny_shapes)))
    out_d.update(zip(tiny, _unpack(dt_, tiny_shapes)))
    out_m.update(zip(tiny, _unpack(mt, tiny_shapes)))
    out_v.update(zip(tiny, _unpack(vt, tiny_shapes)))

    grad_x = dh0[n_meta:t_real][None]
    return (loss, grad_x, *[out_g[n] for n in WEIGHTS], *[out_d[n] for n in WEIGHTS],
            *[out_m[n] for n in WEIGHTS], *[out_v[n] for n in WEIGHTS])
```

```python
import functools

import jax
import jax.numpy as jnp
from jax import lax
from jax.experimental import pallas as pl
from jax.experimental.pallas import tpu as pltpu

F32 = jnp.float32
BF16 = jnp.bfloat16
MESH = pl.DeviceIdType.MESH
AXES = ("x", "y", "c")
NDEV = 8
SLAB_COLS = 1024
RMS_EPS = 1e-6
TOKEN_TILE = 256
MIX_TILE = 128
SUBLANES = 8
SCAN_LANES = 512
FFN_CHUNK = 1024
VMEM_LIMIT_BYTES = 56 * 1024 * 1024

ADAM_LR = 0.001
ADAM_B1 = 0.9
ADAM_B2 = 0.999
ADAM_EPS = 1e-08
ADAM_WD = 0.01
ADAM_STEP = 10

_VM = pl.BlockSpec(memory_space=pltpu.VMEM)
_ANY = pl.BlockSpec(memory_space=pl.ANY)


def _params(sem=("arbitrary",)):
    return pltpu.CompilerParams(dimension_semantics=sem, vmem_limit_bytes=VMEM_LIMIT_BYTES)


def _dot(a, b):
    return jnp.dot(a, b, preferred_element_type=F32)


def _dot_nt(a, b):
    return lax.dot_general(a, b, (((1,), (1,)), ((), ())), preferred_element_type=F32)


def _dot_tn(a, b):
    return lax.dot_general(a, b, (((0,), (0,)), ((), ())), preferred_element_type=F32)


def _chunks(n, step):
    return [(s, min(s + step, n)) for s in range(0, n, step)]


def _gather_plan(x_refs, out_refs, send_sems, recv_sems, local_sems):
    n = len(x_refs)
    x, y, c = lax.axis_index("x"), lax.axis_index("y"), lax.axis_index("c")
    me, sibling = (x, y, c), (x, y, 1 - c)
    chips = [(1 - x, y), (x, 1 - y), (1 - x, 1 - y)]

    def copy(i, k, block, to, src=None):
        slot = out_refs[i].at[4 * block[0] + 2 * block[1] + block[2]]
        return pltpu.make_async_remote_copy(
            src_ref=slot if src is None else src, dst_ref=slot,
            send_sem=send_sems.at[7 * i + k], recv_sem=recv_sems.at[7 * i + k], device_id=to, device_id_type=MESH)

    def mine():
        return [pltpu.make_async_copy(x_refs[i], out_refs[i].at[4 * x + 2 * y + c], local_sems.at[i]) for i in range(n)]

    def first():
        out = []
        for i in range(n):
            out.append(copy(i, 0, me, sibling, src=x_refs[i]))
            out += [copy(i, 1 + j, me, (*chip, c), src=x_refs[i]) for j, chip in enumerate(chips)]
        return out

    def start():
        for cp in mine() + first():
            cp.start()

    def finish():
        passed = []
        for j, chip in enumerate(chips):
            for i in range(n):
                copy(i, 1 + j, (*chip, c), me).wait_recv()
                cp = copy(i, 4 + j, (*chip, c), sibling)
                cp.start()
                passed.append(cp)
        for i in range(n):
            copy(i, 0, sibling, me).wait_recv()
            for j, chip in enumerate(chips):
                copy(i, 4 + j, (*chip, 1 - c), me).wait_recv()
        for cp in first() + passed:
            cp.wait_send()
        for cp in mine():
            cp.wait()

    return start, finish


def _pair_plan(g_refs, out_refs, send_sems, recv_sems):
    x, y, c = lax.axis_index("x"), lax.axis_index("y"), lax.axis_index("c")

    def copies():
        return [pltpu.make_async_remote_copy(
            src_ref=g_refs[i].at[2 * j + (1 - c)], dst_ref=out_refs[i].at[j],
            send_sem=send_sems.at[4 * i + j], recv_sem=recv_sems.at[4 * i + j],
            device_id=(x, y, 1 - c), device_id_type=MESH) for i in range(len(g_refs)) for j in range(4)]

    def start():
        for cp in copies():
            cp.start()

    def finish():
        for cp in copies():
            cp.wait()

    return start, finish


def _chips_plan(p_refs, out_refs, send_sems, recv_sems):
    x, y, c = lax.axis_index("x"), lax.axis_index("y"), lax.axis_index("c")

    def copies():
        return [pltpu.make_async_remote_copy(
            src_ref=p_refs[i].at[2 * px + py], dst_ref=out_refs[i].at[k],
            send_sem=send_sems.at[3 * i + k], recv_sem=recv_sems.at[3 * i + k],
            device_id=(px, py, c), device_id_type=MESH)
            for i in range(len(p_refs)) for k, (px, py) in enumerate([(1 - x, y), (x, 1 - y), (1 - x, 1 - y)])]

    def start():
        for cp in copies():
            cp.start()

    def finish():
        for cp in copies():
            cp.wait()

    return start, finish


def _gather_ride(shards):
    n = len(shards)
    return dict(plan=_gather_plan, arrays=list(shards),
                out_shape=[jax.ShapeDtypeStruct((NDEV, *s.shape), s.dtype) for s in shards],
                sems=[pltpu.SemaphoreType.DMA((7 * n,)), pltpu.SemaphoreType.DMA((7 * n,)), pltpu.SemaphoreType.DMA((n,))])


def _pair_ride(blocks):
    n = len(blocks)
    return dict(plan=_pair_plan, arrays=list(blocks),
                out_shape=[jax.ShapeDtypeStruct((4, *b.shape[1:]), b.dtype) for b in blocks],
                sems=[pltpu.SemaphoreType.DMA((4 * n,)), pltpu.SemaphoreType.DMA((4 * n,))])


def _chips_ride(partials):
    n = len(partials)
    return dict(plan=_chips_plan, arrays=list(partials),
                out_shape=[jax.ShapeDtypeStruct((3, *p.shape[1:]), p.dtype) for p in partials],
                sems=[pltpu.SemaphoreType.DMA((3 * n,)), pltpu.SemaphoreType.DMA((3 * n,))])


def _exchange(ride, name):
    n = len(ride["arrays"])

    def body(*refs):
        start, finish = ride["plan"](refs[:n], refs[n:2 * n], *refs[2 * n:])
        start()
        finish()

    return pl.pallas_call(
        body, name=name, out_shape=ride["out_shape"], in_specs=[_ANY] * n, out_specs=[_ANY] * n, scratch_shapes=ride["sems"],
    )(*ride["arrays"])


def _grid_call(body, name, steps, in_specs, out_specs, out_shape, scratch_shapes, args, ride=None):
    if ride is None:
        outs = pl.pallas_call(body, name=name, grid=(steps,), in_specs=in_specs, out_specs=out_specs, out_shape=out_shape,
                              scratch_shapes=scratch_shapes, compiler_params=_params())(*args)
        return list(outs), []
    n_in, n_out, n_scr, n_ride, n_sems = len(in_specs), len(out_specs), len(scratch_shapes), len(ride["arrays"]), len(ride["sems"])

    def carrying(*refs):
        ins, r_in = refs[:n_in], refs[n_in:n_in + n_ride]
        o0 = n_in + n_ride
        outs, r_out = refs[o0:o0 + n_out], refs[o0 + n_out:o0 + n_out + n_ride]
        s0 = o0 + n_out + n_ride
        scratch, sems = refs[s0:s0 + n_scr], refs[s0 + n_scr:s0 + n_scr + n_sems]
        start, finish = ride["plan"](r_in, r_out, *sems)
        pl.when(pl.program_id(0) == 0)(start)
        body(*ins, *outs, *scratch)
        pl.when(pl.program_id(0) == steps - 1)(finish)

    outs = pl.pallas_call(
        carrying, name=name, grid=(steps,), in_specs=list(in_specs) + [_ANY] * n_ride, out_specs=list(out_specs) + [_ANY] * n_ride,
        out_shape=list(out_shape) + ride["out_shape"], scratch_shapes=list(scratch_shapes) + ride["sems"],
        compiler_params=_params())(*args, *ride["arrays"])
    return list(outs[:n_out]), list(outs[n_out:])


def _row_block(rows):
    return rows if rows <= 512 else next(k for k in (512, 256, 128, rows) if rows % k == 0)


def _add_pairs(gs, core, b, name):
    k, r, n = b.shape
    tr = _row_block(r)

    def body(core_ref, a_ref, b_ref, o_ref):
        o_ref[0] = (a_ref[0, 0].astype(F32) + b_ref[0].astype(F32)).astype(o_ref.dtype)

    spec = pl.BlockSpec((1, tr, n), lambda j, i, c: (j, i, 0))
    return pl.pallas_call(
        body, name=name,
        grid_spec=pltpu.PrefetchScalarGridSpec(
            num_scalar_prefetch=1, grid=(k, r // tr),
            in_specs=[pl.BlockSpec((1, 1, tr, n), lambda j, i, c: (j, c[0], i, 0)), spec], out_specs=spec),
        out_shape=jax.ShapeDtypeStruct(b.shape, b.dtype), compiler_params=_params(("arbitrary", "arbitrary")),
    )(core.reshape(1), gs.reshape(k, 2, r, n), b)


def _adamw(w, m, v, parts, sel, name):
    _, r, n = w.shape
    tr = _row_block(r)
    nparts = len(parts)
    bc1 = 1.0 - ADAM_B1 ** ADAM_STEP
    bc2 = 1.0 - ADAM_B2 ** ADAM_STEP

    def body(sel_ref, *refs):
        w_ref, m_ref, v_ref = refs[:3]
        p_refs = refs[3:3 + nparts]
        g_ref, d_ref, nm_ref, nv_ref = refs[3 + nparts:]
        g = p_refs[0][...].astype(F32)
        for p in p_refs[1:]:
            g = g + p[...].astype(F32)
        nm = ADAM_B1 * m_ref[...] + (1.0 - ADAM_B1) * g
        nv = ADAM_B2 * v_ref[...] + (1.0 - ADAM_B2) * (g * g)
        m_hat = nm / bc1
        v_hat = nv / bc2
        g_ref[...] = g
        d_ref[...] = -ADAM_LR * (m_hat / (jnp.sqrt(v_hat) + ADAM_EPS) + ADAM_WD * w_ref[...])
        nm_ref[...] = nm
        nv_ref[...] = nv

    def part_spec(idx):
        if idx is None:
            return pl.BlockSpec((1, tr, n), lambda i, s: (s[0], i, 0))
        return pl.BlockSpec((1, tr, n), lambda i, s, idx=idx: (idx, i, 0))

    spec = pl.BlockSpec((1, tr, n), lambda i, s: (0, i, 0))
    out = jax.ShapeDtypeStruct((1, r, n), F32)
    return pl.pallas_call(
        body, name=name,
        grid_spec=pltpu.PrefetchScalarGridSpec(
            num_scalar_prefetch=1, grid=(r // tr,),
            in_specs=[spec] * 3 + [part_spec(idx) for _, idx in parts], out_specs=[spec] * 4),
        out_shape=[out] * 4, compiler_params=_params(),
    )(jnp.zeros((1,), jnp.int32) if sel is None else sel.reshape(1), w, m, v, *[p for p, _ in parts])


def _rms_parts(h, g):
    r = lax.rsqrt(jnp.mean(h * h, axis=-1, keepdims=True) + RMS_EPS)
    xhat = h * r
    return r, xhat, xhat * g


def _rms_bwd(dn, g, r, xhat):
    dxh = dn * g
    return r * (dxh - xhat * jnp.mean(dxh * xhat, axis=-1, keepdims=True))


def _ffn_fwd(h, g, wg, wu, wd, name, ride=None):
    t, d = h.shape
    f = wg.shape[1]
    tm = TOKEN_TILE
    chunks = _chunks(f, FFN_CHUNK)

    def body(h_ref, g_ref, wg_ref, wu_ref, wd_ref, o_ref, a_ref, b_ref):
        hv = h_ref[...]
        n = _rms_parts(hv, g_ref[...])[2].astype(BF16)
        acc = jnp.zeros((tm, d), F32)
        for s, e in chunks:
            a = _dot(n, wg_ref[:, s:e])
            b = _dot(n, wu_ref[:, s:e])
            a_ref[:, s:e] = a.astype(BF16)
            b_ref[:, s:e] = b.astype(BF16)
            acc = acc + _dot((a * jax.nn.sigmoid(a) * b).astype(BF16), wd_ref[s:e, :])
        o_ref[...] = hv + 0.5 * acc

    tile = pl.BlockSpec((tm, d), lambda i: (i, 0))
    wide = pl.BlockSpec((tm, f), lambda i: (i, 0))
    return _grid_call(
        body, name, t // tm, [tile, _VM, _VM, _VM, _VM], [tile, wide, wide],
        [jax.ShapeDtypeStruct((t, d), F32), jax.ShapeDtypeStruct((t, f), BF16), jax.ShapeDtypeStruct((t, f), BF16)],
        [], (h, g, wg, wu, wd), ride)


def _ffn_bwd(h, dh_out, a16, b16, g, wg, wu, wd, name, ride=None):
    t, d = h.shape
    f = wg.shape[1]
    tm = TOKEN_TILE
    chunks = _chunks(f, FFN_CHUNK)

    def body(h_ref, dho_ref, a_ref, b_ref, g_ref, wg_ref, wu_ref, wd_ref, dh_ref, dg_ref, n_ref, da_ref, db_ref, s_ref, do_ref):
        @pl.when(pl.program_id(0) == 0)
        def _():
            dg_ref[...] = jnp.zeros_like(dg_ref)

        hv = h_ref[...]
        gv = g_ref[...]
        r, xhat, n32 = _rms_parts(hv, gv)
        dho = dho_ref[...]
        do = (0.5 * dho).astype(BF16)
        dn = jnp.zeros((tm, d), F32)
        for s, e in chunks:
            a = a_ref[:, s:e].astype(F32)
            b = b_ref[:, s:e].astype(F32)
            sig = jax.nn.sigmoid(a)
            sa = a * sig
            ds = _dot_nt(do, wd_ref[s:e, :])
            da = (ds * b * (sig * (1.0 + a * (1.0 - sig)))).astype(BF16)
            db = (ds * sa).astype(BF16)
            s_ref[:, s:e] = (sa * b).astype(BF16)
            da_ref[:, s:e] = da
            db_ref[:, s:e] = db
            dn = dn + _dot_nt(da, wg_ref[:, s:e]) + _dot_nt(db, wu_ref[:, s:e])
        dh_ref[...] = dho + _rms_bwd(dn, gv, r, xhat)
        dg_ref[...] += jnp.sum(dn * xhat, axis=0, keepdims=True)
        n_ref[...] = n32.astype(BF16)
        do_ref[...] = do

    tile = pl.BlockSpec((tm, d), lambda i: (i, 0))
    wide = pl.BlockSpec((tm, f), lambda i: (i, 0))
    one = pl.BlockSpec((1, d), lambda i: (0, 0))
    return _grid_call(
        body, name, t // tm, [tile, tile, wide, wide, _VM, _VM, _VM, _VM], [tile, one, tile, wide, wide, wide, tile],
        [jax.ShapeDtypeStruct((t, d), F32), jax.ShapeDtypeStruct((1, d), F32),
         jax.ShapeDtypeStruct((t, d), BF16), jax.ShapeDtypeStruct((t, f), BF16),
         jax.ShapeDtypeStruct((t, f), BF16), jax.ShapeDtypeStruct((t, f), BF16),
         jax.ShapeDtypeStruct((t, d), BF16)],
        [], (h, dh_out, a16, b16, g, wg, wu, wd), ride)


def _head(h, tgt, g, lo, hi, name):
    t, d = h.shape
    tm = TOKEN_TILE

    def body(h_ref, t_ref, g_ref, dh_ref, loss_ref, dg_ref):
        i = pl.program_id(0)

        @pl.when(i == 0)
        def _():
            loss_ref[...] = jnp.zeros_like(loss_ref)
            dg_ref[...] = jnp.zeros_like(dg_ref)

        gv = g_ref[...]
        r, xhat, y = _rms_parts(h_ref[...], gv)
        row = i * tm + lax.broadcasted_iota(jnp.int32, (tm, 1), 0)
        err = jnp.where((row >= lo) & (row < hi), y - t_ref[...], 0.0)
        loss_ref[...] += jnp.full(loss_ref.shape, 0.5 * jnp.sum(jnp.mean(err * err, axis=-1, keepdims=True)), F32)
        dy = err * (1.0 / d)
        dg_ref[...] += jnp.sum(dy * xhat, axis=0, keepdims=True)
        dh_ref[...] = _rms_bwd(dy, gv, r, xhat)

    tile = pl.BlockSpec((tm, d), lambda i: (i, 0))
    return pl.pallas_call(
        body, name=name, grid=(t // tm,), in_specs=[tile, tile, _VM],
        out_specs=[tile, pl.BlockSpec((SUBLANES, 128), lambda i: (0, 0)), pl.BlockSpec((1, d), lambda i: (0, 0))],
        out_shape=[jax.ShapeDtypeStruct((t, d), F32), jax.ShapeDtypeStruct((SUBLANES, 128), F32),
                   jax.ShapeDtypeStruct((1, d), F32)],
        compiler_params=_params(),
    )(h, tgt, g)


def _dw(a, b, name):
    t, m = a.shape
    n = b.shape[1]
    bn = next(k for k in (512, 256, n) if n % k == 0)

    def body(a_ref, b_ref, o_ref):
        o_ref[...] = _dot_tn(a_ref[...], b_ref[...])

    return pl.pallas_call(
        body, name=name, grid=(n // bn,),
        in_specs=[_VM, pl.BlockSpec((t, bn), lambda j: (0, j))], out_specs=pl.BlockSpec((m, bn), lambda j: (0, j)),
        out_shape=jax.ShapeDtypeStruct((m, n), F32), compiler_params=_params(),
    )(a, b)


def _s5_discretise(a_re, a_im, log_dt, b_re, b_im):
    dt = jnp.exp(log_dt)
    mag = jnp.exp(a_re * dt)
    lam_re = mag * jnp.cos(a_im * dt)
    lam_im = mag * jnp.sin(a_im * dt)
    den = a_re * a_re + a_im * a_im
    q_re = ((lam_re - 1.0) * a_re + lam_im * a_im) / den
    q_im = (lam_im * a_re - (lam_re - 1.0) * a_im) / den
    bb_re = q_re[:, None, :] * b_re - q_im[:, None, :] * b_im
    bb_im = q_re[:, None, :] * b_im + q_im[:, None, :] * b_re
    return lam_re, lam_im, bb_re, bb_im


def _s5_params_fwd(a_re, a_im, log_dt, b_re, b_im):
    g, p = a_re.shape
    c = b_re.shape[1]

    def body(are_ref, aim_ref, ldt_ref, bre_ref, bim_ref, pwr_ref, pwi_ref, bbr_ref, bbi_ref):
        lr, li, bbr, bbi = _s5_discretise(are_ref[...], aim_ref[...], ldt_ref[...], bre_ref[...], bim_ref[...])
        bbr_ref[...] = bbr
        bbi_ref[...] = bbi
        pr, pi = lr, li
        pwr_ref[0] = pr
        pwi_ref[0] = pi
        for k in range(1, SUBLANES):
            pr, pi = pr * lr - pi * li, pr * li + pi * lr
            pwr_ref[k] = pr
            pwi_ref[k] = pi

    return pl.pallas_call(
        body, name="s5_params_fwd",
        out_shape=[jax.ShapeDtypeStruct((SUBLANES, g, p), F32), jax.ShapeDtypeStruct((SUBLANES, g, p), F32),
                   jax.ShapeDtypeStruct((g, c, p), F32), jax.ShapeDtypeStruct((g, c, p), F32)],
    )(a_re, a_im, log_dt, b_re, b_im)


def _s5_params_bwd(a_re, a_im, log_dt, b_re, b_im, dlam, dbb_re, dbb_im):
    g, p = a_re.shape
    c = b_re.shape[1]

    def body(are_ref, aim_ref, ldt_ref, bre_ref, bim_ref, dlam_ref, dbr_ref, dbi_ref,
             dare_ref, daim_ref, dldt_ref, dbre_ref, dbim_ref):
        dlr = jnp.sum(dlam_ref[0], axis=0)
        dli = jnp.sum(dlam_ref[1], axis=0)
        _, vjp = jax.vjp(_s5_discretise, are_ref[...], aim_ref[...], ldt_ref[...], bre_ref[...], bim_ref[...])
        dare, daim, dldt, dbre, dbim = vjp((dlr, dli, dbr_ref[...], dbi_ref[...]))
        dare_ref[...] = dare
        daim_ref[...] = daim
        dldt_ref[...] = dldt
        dbre_ref[...] = dbre
        dbim_ref[...] = dbim

    return pl.pallas_call(
        body, name="s5_params_bwd",
        out_shape=[jax.ShapeDtypeStruct((g, p), F32), jax.ShapeDtypeStruct((g, p), F32),
                   jax.ShapeDtypeStruct((g, 1), F32), jax.ShapeDtypeStruct((g, c, p), F32),
                   jax.ShapeDtypeStruct((g, c, p), F32)],
    )(a_re, a_im, log_dt, b_re, b_im, dlam, dbb_re, dbb_im)


def _scan_chunks(gp):
    hg = gp // 2
    w = min(SCAN_LANES, hg)
    return w, [(half * hg + k * w, half * gp + k * w, half * gp + hg + k * w) for half in range(2) for k in range(hg // w)]


def _cmul_acc(xr, xi, tr, ti, sr, si):
    return xr + tr * sr - ti * si, xi + tr * si + ti * sr


def _scan_fwd(buf_ref, row0, tm, ltab_ref, cin_ref, cout_ref, gp):
    w, chunks = _scan_chunks(gp)
    for lo_t, lo_r, lo_i in chunks:
        def body(r, carry, lo_t=lo_t, lo_r=lo_r, lo_i=lo_i):
            cr, ci = carry
            row = pl.multiple_of(row0 + r * SUBLANES, SUBLANES)
            xr = buf_ref[pl.ds(row, SUBLANES), lo_r:lo_r + w]
            xi = buf_ref[pl.ds(row, SUBLANES), lo_i:lo_i + w]
            for tab, shift in ((0, 1), (2, 2), (4, 4)):
                xr, xi = _cmul_acc(xr, xi, ltab_ref[tab, :, lo_t:lo_t + w], ltab_ref[tab + 1, :, lo_t:lo_t + w],
                                   pltpu.roll(xr, shift, 0), pltpu.roll(xi, shift, 0))
            xr, xi = _cmul_acc(xr, xi, ltab_ref[6, :, lo_t:lo_t + w], ltab_ref[7, :, lo_t:lo_t + w], cr, ci)
            buf_ref[pl.ds(row, SUBLANES), lo_r:lo_r + w] = xr
            buf_ref[pl.ds(row, SUBLANES), lo_i:lo_i + w] = xi
            last = SUBLANES - 1
            return (jnp.broadcast_to(xr[last:last + 1], (SUBLANES, w)), jnp.broadcast_to(xi[last:last + 1], (SUBLANES, w)))

        cr, ci = lax.fori_loop(0, tm // SUBLANES, body,
                               (cin_ref[0:SUBLANES, lo_r:lo_r + w], cin_ref[0:SUBLANES, lo_i:lo_i + w]))
        if cout_ref is not None:
            cout_ref[0:SUBLANES, lo_r:lo_r + w] = cr
            cout_ref[0:SUBLANES, lo_i:lo_i + w] = ci


def _scan_rev(g_ref, hext_ref, tm, ltab_ref, gc_ref, dlam_ref, gp):
    w, chunks = _scan_chunks(gp)
    nb = tm // SUBLANES
    for lo_t, lo_r, lo_i in chunks:
        def body(k, carry, lo_t=lo_t, lo_r=lo_r, lo_i=lo_i):
            cr, ci, ar, ai = carry
            row = pl.multiple_of((nb - 1 - k) * SUBLANES, SUBLANES)
            xr = g_ref[pl.ds(row, SUBLANES), lo_r:lo_r + w]
            xi = g_ref[pl.ds(row, SUBLANES), lo_i:lo_i + w]
            for tab, shift in ((8, 7), (10, 6), (12, 4)):
                xr, xi = _cmul_acc(xr, xi, ltab_ref[tab, :, lo_t:lo_t + w], ltab_ref[tab + 1, :, lo_t:lo_t + w],
                                   pltpu.roll(xr, shift, 0), pltpu.roll(xi, shift, 0))
            xr, xi = _cmul_acc(xr, xi, ltab_ref[14, :, lo_t:lo_t + w], ltab_ref[15, :, lo_t:lo_t + w], cr, ci)
            g_ref[pl.ds(row, SUBLANES), lo_r:lo_r + w] = xr
            g_ref[pl.ds(row, SUBLANES), lo_i:lo_i + w] = xi
            first = lax.broadcasted_iota(jnp.int32, (SUBLANES, w), 0) == 0
            prev = pl.ds(row, SUBLANES)
            here = pl.ds(row + SUBLANES, SUBLANES)
            hpr = jnp.where(first, pltpu.roll(hext_ref[prev, lo_r:lo_r + w], 1, 0), pltpu.roll(hext_ref[here, lo_r:lo_r + w], 1, 0))
            hpi = jnp.where(first, pltpu.roll(hext_ref[prev, lo_i:lo_i + w], 1, 0), pltpu.roll(hext_ref[here, lo_i:lo_i + w], 1, 0))
            ar = ar + xr * hpr + xi * hpi
            ai = ai - xr * hpi + xi * hpr
            return (jnp.broadcast_to(xr[0:1], (SUBLANES, w)), jnp.broadcast_to(xi[0:1], (SUBLANES, w)), ar, ai)

        cr, ci, ar, ai = lax.fori_loop(
            0, nb, body, (gc_ref[:, lo_r:lo_r + w], gc_ref[:, lo_i:lo_i + w], dlam_ref[:, lo_r:lo_r + w], dlam_ref[:, lo_i:lo_i + w]))
        gc_ref[:, lo_r:lo_r + w] = cr
        gc_ref[:, lo_i:lo_i + w] = ci
        dlam_ref[:, lo_r:lo_r + w] = ar
        dlam_ref[:, lo_i:lo_i + w] = ai


def _mix_math(h, gm, win_ref, bg, bc_ref, cc_ref, dsk, wglu_ref, cw, wco_ref, ltab_ref,
              hbuf_ref, hrow0, st_in_ref, st_out_ref, cext_ref, dims):
    d, ds, dc, gp = dims
    tm = h.shape[0]
    dsh = ds // 2
    r, xhat, n32 = _rms_parts(h, gm)
    u = n32.astype(BF16)
    o1, o2, o3 = ds + dc, ds + 2 * dc, ds + 3 * dc
    us = _dot(u, win_ref[:, 0:ds])
    v = _dot(u, win_ref[:, ds:o1])
    gb = _dot(u, win_ref[:, o1:o2])
    gcv = _dot(u, win_ref[:, o2:o3])
    gs = jax.nn.sigmoid(_dot(u, win_ref[:, o3:o3 + d]) + bg[:, 0:d])
    gcg = jax.nn.sigmoid(_dot(u, win_ref[:, o3 + d:o3 + 2 * d]) + bg[:, d:2 * d])
    us16 = us.astype(BF16)
    rows = slice(hrow0, hrow0 + tm)
    for half in range(2):
        hbuf_ref[rows, half * gp:(half + 1) * gp] = _dot(us16[:, half * dsh:(half + 1) * dsh], bc_ref[half])
    _scan_fwd(hbuf_ref, hrow0, tm, ltab_ref, st_in_ref, st_out_ref, gp)
    y5 = jnp.concatenate(
        [_dot(hbuf_ref[rows, half * gp:(half + 1) * gp].astype(BF16), cc_ref[half]) for half in range(2)], axis=1) + dsk * us
    ge16 = jax.nn.gelu(y5).astype(BF16)
    z = _dot(ge16, wglu_ref[...])
    z1, sz = z[:, 0:d], jax.nn.sigmoid(z[:, d:2 * d])
    ys = z1 * sz
    cin = gcv * v
    cext_ref[SUBLANES:SUBLANES + tm, :] = cin
    cv = cw[0:1] * cext_ref[SUBLANES - 2:SUBLANES - 2 + tm, :] + cw[1:2] * cext_ref[SUBLANES - 1:SUBLANES - 1 + tm, :] + cw[2:3] * cin
    cg16 = (gb * cv).astype(BF16)
    yc = _dot(cg16, wco_ref[...])
    mixed = gs * ys + gcg * yc
    return dict(r=r, xhat=xhat, u=u, us=us, us16=us16, v=v, gb=gb, gcv=gcv, gs=gs, gcg=gcg, y5=y5, ge16=ge16,
                z1=z1, sz=sz, ys=ys, cin=cin, cv=cv, cg16=cg16, yc=yc, mixed=mixed)


def _mix_fwd(h, gm, win, bg, bc, cc, dsk, wglu, cw, wco, wo, ltab, dims):
    d, ds, dc, gp = dims
    t = h.shape[0]
    tm = MIX_TILE
    nt = t // tm

    def body(h_ref, gm_ref, win_ref, bg_ref, bc_ref, cc_ref, dsk_ref, wglu_ref, cw_ref, wco_ref, wo_ref, ltab_ref,
             h2_ref, st_ref, cvs_ref, hbuf_ref, carry_ref, cext_ref):
        @pl.when(pl.program_id(0) == 0)
        def _():
            carry_ref[...] = jnp.zeros_like(carry_ref)
            cext_ref[0:SUBLANES, :] = jnp.zeros((SUBLANES, dc), F32)

        st_ref[0] = carry_ref[...]
        cvs_ref[0] = cext_ref[0:SUBLANES, :]
        hv = h_ref[...]
        m = _mix_math(hv, gm_ref[...], win_ref, bg_ref[...], bc_ref, cc_ref, dsk_ref[...], wglu_ref, cw_ref[...], wco_ref,
                      ltab_ref, hbuf_ref, 0, carry_ref, carry_ref, cext_ref, dims)
        h2_ref[...] = hv + _dot(m["mixed"].astype(BF16), wo_ref[...])
        cext_ref[0:SUBLANES, :] = cext_ref[tm:tm + SUBLANES, :]

    tile = pl.BlockSpec((tm, d), lambda i: (i, 0))
    return pl.pallas_call(
        body, name="mix_fwd", grid=(nt,),
        in_specs=[tile] + [_VM] * 11,
        out_specs=[tile, pl.BlockSpec((1, SUBLANES, 2 * gp), lambda i: (i, 0, 0)), pl.BlockSpec((1, SUBLANES, dc), lambda i: (i, 0, 0))],
        out_shape=[jax.ShapeDtypeStruct((t, d), F32), jax.ShapeDtypeStruct((nt, SUBLANES, 2 * gp), F32),
                   jax.ShapeDtypeStruct((nt, SUBLANES, dc), F32)],
        scratch_shapes=[pltpu.VMEM((tm, 2 * gp), F32), pltpu.VMEM((SUBLANES, 2 * gp), F32), pltpu.VMEM((SUBLANES + tm, dc), F32)],
        compiler_params=_params(),
    )(h, gm, win, bg, bc, cc, dsk, wglu, cw, wco, wo, ltab)


def _mix_bwd(h, dh2, st, cvs, gm, win, bg, bc, cc, dsk, wglu, cw, wco, wo, ltab, dims):
    d, ds, dc, gp = dims
    t = h.shape[0]
    tm = MIX_TILE
    nt = t // tm
    dsh = ds // 2
    ncols = ds + 3 * dc + 2 * d

    def body(h_ref, dh2_ref, st_ref, cvs_ref, gm_ref, win_ref, bg_ref, bc_ref, cc_ref, dsk_ref, wglu_ref, cw_ref, wco_ref,
             wo_ref, ltab_ref,
             dh1_ref, u_ref, dp_ref, ge_ref, dz_ref, cg_ref, dyc_ref, mx_ref, dh216_ref,
             dgm_ref, dbg_ref, ddsk_ref, dcw_ref, dlam_ref, dbc_ref, dcc_ref,
             hext_ref, gbuf_ref, gcarry_ref, cext_ref, dcvext_ref):
        @pl.when(pl.program_id(0) == 0)
        def _():
            for ref in (dgm_ref, dbg_ref, ddsk_ref, dcw_ref, dlam_ref, dbc_ref, dcc_ref, gcarry_ref):
                ref[...] = jnp.zeros_like(ref)
            dcvext_ref[tm:tm + SUBLANES, :] = jnp.zeros((SUBLANES, dc), F32)

        hext_ref[0:SUBLANES, :] = st_ref[0]
        cext_ref[0:SUBLANES, :] = cvs_ref[0]
        hv = h_ref[...]
        gmv = gm_ref[...]
        cw_v = cw_ref[...]
        dskv = dsk_ref[...]
        m = _mix_math(hv, gmv, win_ref, bg_ref[...], bc_ref, cc_ref, dskv, wglu_ref, cw_v, wco_ref,
                      ltab_ref, hext_ref, SUBLANES, hext_ref, None, cext_ref, dims)
        dh2v = dh2_ref[...]
        dh216 = dh2v.astype(BF16)
        dmixed = _dot_nt(dh216, wo_ref[...])
        gs, gcg, ys, yc, sz = m["gs"], m["gcg"], m["ys"], m["yc"], m["sz"]
        dys = dmixed * gs
        dyc16 = (dmixed * gcg).astype(BF16)
        dpgs = dmixed * ys * gs * (1.0 - gs)
        dpgc = dmixed * yc * gcg * (1.0 - gcg)
        dz16 = jnp.concatenate([dys * sz, dys * m["z1"] * sz * (1.0 - sz)], axis=1).astype(BF16)
        dge = _dot_nt(dz16, wglu_ref[...])
        dy5 = jax.vjp(jax.nn.gelu, m["y5"])[1](dge)[0]
        dy516 = dy5.astype(BF16)
        for half in range(2):
            gbuf_ref[:, half * gp:(half + 1) * gp] = _dot_nt(dy516[:, half * dsh:(half + 1) * dsh], cc_ref[half])
        _scan_rev(gbuf_ref, hext_ref, tm, ltab_ref, gcarry_ref, dlam_ref, gp)
        dus = []
        for half in range(2):
            g16 = gbuf_ref[:, half * gp:(half + 1) * gp].astype(BF16)
            dus.append(_dot_nt(g16, bc_ref[half]))
            dbc_ref[half] += _dot_tn(m["us16"][:, half * dsh:(half + 1) * dsh], g16)
            h16 = hext_ref[SUBLANES:SUBLANES + tm, half * gp:(half + 1) * gp].astype(BF16)
            dcc_ref[half] += _dot_tn(h16, dy516[:, half * dsh:(half + 1) * dsh])
        dus = jnp.concatenate(dus, axis=1) + dskv * dy5
        ddsk_ref[...] += jnp.sum(dy5 * m["us"], axis=0, keepdims=True)
        dcg = _dot_nt(dyc16, wco_ref[...])
        dgb = dcg * m["cv"]
        dcv = dcg * m["gb"]
        dcvext_ref[0:tm, :] = dcv
        dcin = cw_v[2:3] * dcv + cw_v[1:2] * dcvext_ref[1:1 + tm, :] + cw_v[0:1] * dcvext_ref[2:2 + tm, :]
        dcw_ref[0:1, :] += jnp.sum(dcv * cext_ref[SUBLANES - 2:SUBLANES - 2 + tm, :], axis=0, keepdims=True)
        dcw_ref[1:2, :] += jnp.sum(dcv * cext_ref[SUBLANES - 1:SUBLANES - 1 + tm, :], axis=0, keepdims=True)
        dcw_ref[2:3, :] += jnp.sum(dcv * m["cin"], axis=0, keepdims=True)
        dcvext_ref[tm:tm + SUBLANES, :] = dcvext_ref[0:SUBLANES, :]
        dp16 = jnp.concatenate([dus, dcin * m["gcv"], dgb, dcin * m["v"], dpgs, dpgc], axis=1).astype(BF16)
        du = _dot_nt(dp16, win_ref[...])
        dh1_ref[...] = dh2v + _rms_bwd(du, gmv, m["r"], m["xhat"])
        dgm_ref[...] += jnp.sum(du * m["xhat"], axis=0, keepdims=True)
        dbg_ref[...] += jnp.concatenate([jnp.sum(dpgs, axis=0, keepdims=True), jnp.sum(dpgc, axis=0, keepdims=True)], axis=1)
        u_ref[...] = m["u"]
        dp_ref[...] = dp16
        ge_ref[...] = m["ge16"]
        dz_ref[...] = dz16
        cg_ref[...] = m["cg16"]
        dyc_ref[...] = dyc16
        mx_ref[...] = m["mixed"].astype(BF16)
        dh216_ref[...] = dh216

    def rev(cols):
        return pl.BlockSpec((tm, cols), lambda j: (nt - 1 - j, 0))

    def rev3(cols):
        return pl.BlockSpec((1, SUBLANES, cols), lambda j: (nt - 1 - j, 0, 0))

    def bf(cols):
        return jax.ShapeDtypeStruct((t, cols), BF16)

    return pl.pallas_call(
        body, name="mix_bwd", grid=(nt,),
        in_specs=[rev(d), rev(d), rev3(2 * gp), rev3(dc)] + [_VM] * 11,
        out_specs=[rev(d), rev(d), rev(ncols), rev(ds), rev(2 * d), rev(dc), rev(d), rev(d), rev(d)] + [_VM] * 7,
        out_shape=[jax.ShapeDtypeStruct((t, d), F32), bf(d), bf(ncols), bf(ds), bf(2 * d), bf(dc), bf(d), bf(d), bf(d),
                   jax.ShapeDtypeStruct((1, d), F32), jax.ShapeDtypeStruct((1, 2 * d), F32), jax.ShapeDtypeStruct((1, ds), F32),
                   jax.ShapeDtypeStruct((SUBLANES, dc), F32), jax.ShapeDtypeStruct((SUBLANES, 2 * gp), F32),
                   jax.ShapeDtypeStruct((2, dsh, gp), F32), jax.ShapeDtypeStruct((2, gp, dsh), F32)],
        scratch_shapes=[pltpu.VMEM((SUBLANES + tm, 2 * gp), F32), pltpu.VMEM((tm, 2 * gp), F32),
                        pltpu.VMEM((SUBLANES, 2 * gp), F32), pltpu.VMEM((SUBLANES + tm, dc), F32),
                        pltpu.VMEM((tm + SUBLANES, dc), F32)],
        compiler_params=_params(),
    )(h, dh2, st, cvs, gm, win, bg, bc, cc, dsk, wglu, cw, wco, wo, ltab)


def _pad_rows(a, rows, axis=0):
    pad = [(0, 0)] * a.ndim
    pad[axis] = (0, rows - a.shape[axis])
    return jnp.pad(a, pad)


def _as_rows(a):
    flat = a.reshape(-1)
    n = -(-flat.shape[0] // SLAB_COLS) * SLAB_COLS
    return jnp.pad(flat, (0, n - flat.shape[0])).reshape(-1, SLAB_COLS)


def _pack(arrs):
    rows = jnp.concatenate([_as_rows(a) for a in arrs], axis=0)
    return _pad_rows(rows, -(-rows.shape[0] // 16) * 16)


def _unpack(slab, shapes):
    out, r = [], 0
    for shp in shapes:
        size = 1
        for s in shp:
            size *= s
        n = -(-size // SLAB_COLS)
        out.append(slab[r:r + n].reshape(-1)[:size].reshape(shp))
        r += n
    return out


def _block_diag(blocks):
    n, a, b = blocks.shape
    eye = jnp.eye(n, dtype=blocks.dtype)
    return (blocks[:, :, None, :] * eye[:, None, :, None]).reshape(n * a, n * b)


def _diag_blocks(mat, n):
    a, b = mat.shape[0] // n, mat.shape[1] // n
    eye = jnp.eye(n, dtype=mat.dtype)
    return jnp.sum(mat.reshape(n, a, n, b) * eye[:, None, :, None], axis=2)


BIG = (("ffn1_w_gate", "col"), ("ffn1_w_up", "col"), ("ffn1_w_down", "row"), ("w_in", "col"), ("ssm_w_glu", "col"),
       ("conv_w_out", "col"), ("w_o", "row"), ("ffn2_w_gate", "col"), ("ffn2_w_up", "col"), ("ffn2_w_down", "row"))
REPLICATED = ("g_ffn1", "g_mix", "b_gate", "ssm_a_re", "ssm_a_im", "ssm_log_dt", "ssm_b_re", "ssm_b_im", "ssm_c_re",
              "ssm_c_im", "ssm_d", "g_ffn2", "g_final")
WEIGHTS = ("meta_tokens", "g_ffn1", "ffn1_w_gate", "ffn1_w_up", "ffn1_w_down", "g_mix", "w_in", "b_gate", "ssm_a_re",
           "ssm_a_im", "ssm_log_dt", "ssm_b_re", "ssm_b_im", "ssm_c_re", "ssm_c_im", "ssm_d", "ssm_w_glu", "conv_w",
           "conv_w_out", "w_o", "g_ffn2", "ffn2_w_gate", "ffn2_w_up", "ffn2_w_down", "g_final")
N_EARLY = 3


def _full_from_blocks(blocks, kind):
    n, r, c = blocks.shape
    if kind == "col":
        return jnp.transpose(blocks, (1, 0, 2)).reshape(r, n * c)
    return blocks.reshape(n * r, c)


def _blocks_from_full(full, kind):
    if kind == "col":
        r, nc = full.shape
        return jnp.transpose(full.reshape(r, NDEV, nc // NDEV), (1, 0, 2))
    nr, c = full.shape
    return full.reshape(NDEV, nr // NDEV, c)


def kernel(x, meta_tokens, g_ffn1, ffn1_w_gate, ffn1_w_up, ffn1_w_down, g_mix, w_in, b_gate, ssm_a_re, ssm_a_im, ssm_log_dt, ssm_b_re, ssm_b_im, ssm_c_re, ssm_c_im, ssm_d, ssm_w_glu, conv_w, conv_w_out, w_o, g_ffn2, ffn2_w_gate, ffn2_w_up, ffn2_w_down, g_final, loss_target, m_meta_tokens, m_g_ffn1, m_ffn1_w_gate, m_ffn1_w_up, m_ffn1_w_down, m_g_mix, m_w_in, m_b_gate, m_ssm_a_re, m_ssm_a_im, m_ssm_log_dt, m_ssm_b_re, m_ssm_b_im, m_ssm_c_re, m_ssm_c_im, m_ssm_d, m_ssm_w_glu, m_conv_w, m_conv_w_out, m_w_o, m_g_ffn2, m_ffn2_w_gate, m_ffn2_w_up, m_ffn2_w_down, m_g_final, v_meta_tokens, v_g_ffn1, v_ffn1_w_gate, v_ffn1_w_up, v_ffn1_w_down, v_g_mix, v_w_in, v_b_gate, v_ssm_a_re, v_ssm_a_im, v_ssm_log_dt, v_ssm_b_re, v_ssm_b_im, v_ssm_c_re, v_ssm_c_im, v_ssm_d, v_ssm_w_glu, v_conv_w, v_conv_w_out, v_w_o, v_g_ffn2, v_ffn2_w_gate, v_ffn2_w_up, v_ffn2_w_down, v_g_final):
    args = dict(locals())
    w = {n: args[n] for n in WEIGHTS}
    mom_m = {n: args["m_" + n] for n in WEIGHTS}
    mom_v = {n: args["v_" + n] for n in WEIGHTS}

    seq, d = x.shape[1], x.shape[2]
    n_meta = meta_tokens.shape[0]
    ds = ssm_d.shape[1]
    n_grp, n_state = ssm_a_re.shape[1], ssm_a_re.shape[2]
    gp = n_grp * n_state
    dc = conv_w.shape[3] * NDEV
    dims = (d, ds, dc, gp)
    t_real = n_meta + seq
    t_pad = -(-t_real // TOKEN_TILE) * TOKEN_TILE
    me_chip = 2 * lax.axis_index("x") + lax.axis_index("y")
    me_core = lax.axis_index("c")
    me = 2 * me_chip + me_core
    mcols, ccols = d // NDEV, dc // NDEV

    cw_shard = _pad_rows(_pad_rows(conv_w.reshape(3, ccols), SUBLANES), 128, axis=1)
    shard16 = {name: w[name][0].astype(BF16) for name, _ in BIG}
    early, late = BIG[:N_EARLY], BIG[N_EARLY:]
    got = _exchange(_gather_ride([shard16[name] for name, _ in early] + [meta_tokens, cw_shard]), "gather_first")
    full = {name: _full_from_blocks(got[i], kind) for i, (name, kind) in enumerate(early)}
    meta_full = _full_from_blocks(got[-2], "col")
    cw_rows = _pad_rows(_full_from_blocks(got[-1][:, 0:3, 0:ccols], "col"), SUBLANES)

    a_re, a_im, ldt = ssm_a_re[0], ssm_a_im[0], ssm_log_dt[0].reshape(n_grp, 1)
    b_re_t = jnp.transpose(ssm_b_re[0], (0, 2, 1))
    b_im_t = jnp.transpose(ssm_b_im[0], (0, 2, 1))
    pw_r, pw_i, bb_r, bb_i = _s5_params_fwd(a_re, a_im, ldt, b_re_t, b_im_t)
    pw_r = pw_r.reshape(SUBLANES, gp)
    pw_i = pw_i.reshape(SUBLANES, gp)
    sub = jnp.arange(SUBLANES)[:, None]

    def fwd_tab(p, k):
        return jnp.where(sub >= k, p[k - 1][None, :], 0.0)

    def rev_tab(p, k):
        return jnp.where(sub <= SUBLANES - 1 - k, p[k - 1][None, :], 0.0)

    ltab = jnp.stack(
        [fwd_tab(pw_r, 1), fwd_tab(pw_i, 1), fwd_tab(pw_r, 2), fwd_tab(pw_i, 2), fwd_tab(pw_r, 4), fwd_tab(pw_i, 4), pw_r, pw_i,
         rev_tab(pw_r, 1), -rev_tab(pw_i, 1), rev_tab(pw_r, 2), -rev_tab(pw_i, 2), rev_tab(pw_r, 4), -rev_tab(pw_i, 4),
         pw_r[::-1], -pw_i[::-1]], axis=0)
    gh = n_grp // 2
    bc = jnp.stack([jnp.concatenate([_block_diag(bb_r[h * gh:(h + 1) * gh]), _block_diag(bb_i[h * gh:(h + 1) * gh])], axis=1)
                    for h in range(2)]).astype(BF16)
    c_re_t = jnp.transpose(ssm_c_re[0], (0, 2, 1))
    c_im_t = jnp.transpose(ssm_c_im[0], (0, 2, 1))
    cc = jnp.stack([jnp.concatenate([_block_diag(c_re_t[h * gh:(h + 1) * gh]), -_block_diag(c_im_t[h * gh:(h + 1) * gh])], axis=0)
                    for h in range(2)]).astype(BF16)

    zpad = jnp.zeros((t_pad - t_real, d), F32)
    h0 = jnp.concatenate([meta_full, x[0], zpad], axis=0)
    tgt = jnp.concatenate([jnp.zeros((n_meta, d), F32), loss_target[0], zpad], axis=0)
    (h1, a1, b1), got = _ffn_fwd(h0, g_ffn1, full["ffn1_w_gate"], full["ffn1_w_up"], full["ffn1_w_down"], "ffn1_fwd",
                                 ride=_gather_ride([shard16[name] for name, _ in late]))
    full.update({name: _full_from_blocks(got[i], kind) for i, (name, kind) in enumerate(late)})
    mix_w = (g_mix, full["w_in"], b_gate, bc, cc, ssm_d, full["ssm_w_glu"], cw_rows, full["conv_w_out"], full["w_o"], ltab)
    h2, st, cvs = _mix_fwd(h1, *mix_w, dims)
    (h3, a2, b2), _ = _ffn_fwd(h2, g_ffn2, full["ffn2_w_gate"], full["ffn2_w_up"], full["ffn2_w_down"], "ffn2_fwd")
    dh3, loss_blk, dg_final = _head(h3, tgt, g_final.reshape(1, d), n_meta, t_real, "loss_head")
    loss = lax.psum(loss_blk[0, 0], AXES)

    (dh2, dg_ffn2, n2, da2, db2, s2, do2), _ = _ffn_bwd(
        h2, dh3, a2, b2, g_ffn2, full["ffn2_w_gate"], full["ffn2_w_up"], full["ffn2_w_down"], "ffn2_bwd")
    (dh1, u16, dp16, ge16, dz16, cg16, dyc16, mx16, dh216, dg_mix, dbg, ddsk, dcw, dlam, dbc, dcc) = _mix_bwd(h1, dh2, st, cvs, *mix_w, dims)
    dblocks = {
        "w_in": _blocks_from_full(_dw(u16, dp16, "dw_in"), "col"),
        "ssm_w_glu": _blocks_from_full(_dw(ge16, dz16, "dw_glu"), "col"),
        "conv_w_out": _blocks_from_full(_dw(cg16, dyc16, "dw_conv_out"), "col"),
        "w_o": _blocks_from_full(_dw(mx16, dh216, "dw_o"), "row"),
        "ffn2_w_gate": _blocks_from_full(_dw(n2, da2, "dw_ffn2_gate"), "col"),
        "ffn2_w_up": _blocks_from_full(_dw(n2, db2, "dw_ffn2_up"), "col"),
        "ffn2_w_down": jnp.transpose(_blocks_from_full(_dw(do2, s2, "dw_ffn2_down"), "col"), (0, 2, 1)),
    }

    def pair_sums(names, tag):
        gs = [dblocks[name].astype(BF16) for name in names]
        from_sibling = _exchange(_pair_ride(gs), "reduce_pair_" + tag)
        return [_add_pairs(g, me_core, b, "reduce_pair_add_" + name) for g, b, name in zip(gs, from_sibling, names)]

    late_names = [name for name, _ in late]
    pairs = dict(zip(late_names, pair_sums(late_names, "late")))
    (dh0, dg_ffn1, n1, da1, db1, s1, do1), got = _ffn_bwd(
        h0, dh1, a1, b1, g_ffn1, full["ffn1_w_gate"], full["ffn1_w_up"], full["ffn1_w_down"], "ffn1_bwd",
        ride=_chips_ride([pairs[name] for name in late_names]))
    from_chips = dict(zip(late_names, got))
    dblocks.update({
        "ffn1_w_gate": _blocks_from_full(_dw(n1, da1, "dw_ffn1_gate"), "col"),
        "ffn1_w_up": _blocks_from_full(_dw(n1, db1, "dw_ffn1_up"), "col"),
        "ffn1_w_down": jnp.transpose(_blocks_from_full(_dw(do1, s1, "dw_ffn1_down"), "col"), (0, 2, 1)),
    })
    early_names = [name for name, _ in early]
    pairs.update(zip(early_names, pair_sums(early_names, "early")))
    from_chips.update(zip(early_names, _exchange(_chips_ride([pairs[name] for name in early_names]), "reduce_chips_early")))

    dlam4 = dlam.reshape(SUBLANES, 2, 2, gh, n_state)
    dlam_in = jnp.transpose(dlam4, (2, 0, 1, 3, 4)).reshape(2, SUBLANES, n_grp, n_state)
    hg = gp // 2
    dbb_r = jnp.concatenate([_diag_blocks(dbc[h][:, :hg], gh) for h in range(2)], axis=0)
    dbb_i = jnp.concatenate([_diag_blocks(dbc[h][:, hg:], gh) for h in range(2)], axis=0)
    da_re, da_im, dldt, dbre_t, dbim_t = _s5_params_bwd(a_re, a_im, ldt, b_re_t, b_im_t, dlam_in, dbb_r, dbb_i)
    dc_re = jnp.concatenate([_diag_blocks(dcc[h][:hg], gh) for h in range(2)], axis=0)
    dc_im = -jnp.concatenate([_diag_blocks(dcc[h][hg:], gh) for h in range(2)], axis=0)

    grads_rep = {
        "g_ffn1": dg_ffn1, "g_mix": dg_mix, "b_gate": dbg, "ssm_a_re": da_re[None], "ssm_a_im": da_im[None],
        "ssm_log_dt": dldt.reshape(1, n_grp), "ssm_b_re": jnp.transpose(dbre_t, (0, 2, 1))[None],
        "ssm_b_im": jnp.transpose(dbim_t, (0, 2, 1))[None], "ssm_c_re": jnp.transpose(dc_re, (0, 2, 1))[None],
        "ssm_c_im": jnp.transpose(dc_im, (0, 2, 1))[None], "ssm_d": ddsk, "g_ffn2": dg_ffn2, "g_final": dg_final.reshape(d),
    }

    out_g, out_d, out_m, out_v = {}, {}, {}, {}
    for name, _ in BIG:
        fc = from_chips[name]
        out_g[name], out_d[name], out_m[name], out_v[name] = _adamw(
            w[name], mom_m[name], mom_v[name], [(pairs[name], None), (fc, 0), (fc, 1), (fc, 2)], me_chip, "adamw_" + name)

    rep_shapes = [w[n].shape for n in REPLICATED]
    small_g_shapes = rep_shapes + [(n_meta, d), (3, dc)]
    gsmall = _pack([grads_rep[n] for n in REPLICATED] + [dh0[0:n_meta], dcw[0:3]])
    gall = _exchange(_gather_ride([gsmall]), "gather_small_grads")[0]
    zer = [jnp.zeros((n_meta, d), F32), jnp.zeros((3, dc), F32)]
    gr, dr, mr, vr = [o[0] for o in _adamw(
        _pack([w[n] for n in REPLICATED] + zer)[None], _pack([mom_m[n] for n in REPLICATED] + zer)[None],
        _pack([mom_v[n] for n in REPLICATED] + zer)[None], [(gall, b) for b in range(NDEV)], None, "adamw_replicated")]
    g_list = _unpack(gr, small_g_shapes)
    out_g.update(zip(REPLICATED, g_list[:len(REPLICATED)]))
    out_d.update(zip(REPLICATED, _unpack(dr, rep_shapes)))
    out_m.update(zip(REPLICATED, _unpack(mr, rep_shapes)))
    out_v.update(zip(REPLICATED, _unpack(vr, rep_shapes)))

    g_meta = lax.dynamic_slice_in_dim(g_list[-2], me * mcols, mcols, axis=1)
    g_cw = lax.dynamic_slice_in_dim(g_list[-1], me * ccols, ccols, axis=1).reshape(conv_w.shape)
    tiny = ("meta_tokens", "conv_w")
    tiny_shapes = [meta_tokens.shape, conv_w.shape]
    gt, dt_, mt, vt = [o[0] for o in _adamw(
        _pack([w[n] for n in tiny])[None], _pack([mom_m[n] for n in tiny])[None], _pack([mom_v[n] for n in tiny])[None],
        [(_pack([g_meta, g_cw])[None], 0)], None, "adamw_tiny")]
    out_g.update(zip(tiny, _unpack(gt, tiny_shapes)))
    out_d.update(zip(tiny, _unpack(dt_, tiny_shapes)))
    out_m.update(zip(tiny, _unpack(mt, tiny_shapes)))
    out_v.update(zip(tiny, _unpack(vt, tiny_shapes)))

    grad_x = dh0[n_meta:t_real][None]
    return (loss, grad_x, *[out_g[n] for n in WEIGHTS], *[out_d[n] for n in WEIGHTS],
            *[out_m[n] for n in WEIGHTS], *[out_v[n] for n in WEIGHTS])
```

```python
import functools

import jax
import jax.numpy as jnp
from jax import lax
from jax.experimental import pallas as pl
from jax.experimental.pallas import tpu as pltpu

F32 = jnp.float32
BF16 = jnp.bfloat16
MESH = pl.DeviceIdType.MESH
AXES = ("x", "y", "c")
NDEV = 8
SLAB_COLS = 1024
RMS_EPS = 1e-6
TOKEN_TILE = 256
MIX_TILE = 128
SUBLANES = 8
SCAN_LANES = 512
FFN_CHUNK = 1024
VMEM_LIMIT_BYTES = 56 * 1024 * 1024

ADAM_LR = 0.001
ADAM_B1 = 0.9
ADAM_B2 = 0.999
ADAM_EPS = 1e-08
ADAM_WD = 0.01
ADAM_STEP = 10

_VM = pl.BlockSpec(memory_space=pltpu.VMEM)
_ANY = pl.BlockSpec(memory_space=pl.ANY)


def _params(sem=("arbitrary",)):
    return pltpu.CompilerParams(dimension_semantics=sem, vmem_limit_bytes=VMEM_LIMIT_BYTES)


def _dot(a, b):
    return jnp.dot(a, b, preferred_element_type=F32)


def _dot_nt(a, b):
    return lax.dot_general(a, b, (((1,), (1,)), ((), ())), preferred_element_type=F32)


def _dot_tn(a, b):
    return lax.dot_general(a, b, (((0,), (0,)), ((), ())), preferred_element_type=F32)


def _chunks(n, step):
    return [(s, min(s + step, n)) for s in range(0, n, step)]


def _gather_plan(x_refs, out_refs, send_sems, recv_sems, local_sems):
    n = len(x_refs)
    x, y, c = lax.axis_index("x"), lax.axis_index("y"), lax.axis_index("c")
    me, sibling = (x, y, c), (x, y, 1 - c)
    chips = [(1 - x, y), (x, 1 - y), (1 - x, 1 - y)]

    def copy(i, k, block, to, src=None):
        slot = out_refs[i].at[4 * block[0] + 2 * block[1] + block[2]]
        return pltpu.make_async_remote_copy(
            src_ref=slot if src is None else src, dst_ref=slot,
            send_sem=send_sems.at[7 * i + k], recv_sem=recv_sems.at[7 * i + k], device_id=to, device_id_type=MESH)

    def mine():
        return [pltpu.make_async_copy(x_refs[i], out_refs[i].at[4 * x + 2 * y + c], local_sems.at[i]) for i in range(n)]

    def first():
        out = []
        for i in range(n):
            out.append(copy(i, 0, me, sibling, src=x_refs[i]))
            out += [copy(i, 1 + j, me, (*chip, c), src=x_refs[i]) for j, chip in enumerate(chips)]
        return out

    def start():
        for cp in mine() + first():
            cp.start()

    def finish():
        passed = []
        for j, chip in enumerate(chips):
            for i in range(n):
                copy(i, 1 + j, (*chip, c), me).wait_recv()
                cp = copy(i, 4 + j, (*chip, c), sibling)
                cp.start()
                passed.append(cp)
        for i in range(n):
            copy(i, 0, sibling, me).wait_recv()
            for j, chip in enumerate(chips):
                copy(i, 4 + j, (*chip, 1 - c), me).wait_recv()
        for cp in first() + passed:
            cp.wait_send()
        for cp in mine():
            cp.wait()

    return start, finish


def _pair_plan(g_refs, out_refs, send_sems, recv_sems):
    x, y, c = lax.axis_index("x"), lax.axis_index("y"), lax.axis_index("c")

    def copies():
        return [pltpu.make_async_remote_copy(
            src_ref=g_refs[i].at[2 * j + (1 - c)], dst_ref=out_refs[i].at[j],
            send_sem=send_sems.at[4 * i + j], recv_sem=recv_sems.at[4 * i + j],
            device_id=(x, y, 1 - c), device_id_type=MESH) for i in range(len(g_refs)) for j in range(4)]

    def start():
        for cp in copies():
            cp.start()

    def finish():
        for cp in copies():
            cp.wait()

    return start, finish


def _chips_plan(p_refs, out_refs, send_sems, recv_sems):
    x, y, c = lax.axis_index("x"), lax.axis_index("y"), lax.axis_index("c")

    def copies():
        return [pltpu.make_async_remote_copy(
            src_ref=p_refs[i].at[2 * px + py], dst_ref=out_refs[i].at[k],
            send_sem=send_sems.at[3 * i + k], recv_sem=recv_sems.at[3 * i + k],
            device_id=(px, py, c), device_id_type=MESH)
            for i in range(len(p_refs)) for k, (px, py) in enumerate([(1 - x, y), (x, 1 - y), (1 - x, 1 - y)])]

    def start():
        for cp in copies():
            cp.start()

    def finish():
        for cp in copies():
            cp.wait()

    return start, finish


def _gather_ride(shards):
    n = len(shards)
    return dict(plan=_gather_plan, arrays=list(shards),
                out_shape=[jax.ShapeDtypeStruct((NDEV, *s.shape), s.dtype) for s in shards],
                sems=[pltpu.SemaphoreType.DMA((7 * n,)), pltpu.SemaphoreType.DMA((7 * n,)), pltpu.SemaphoreType.DMA((n,))])


def _pair_ride(blocks):
    n = len(blocks)
    return dict(plan=_pair_plan, arrays=list(blocks),
                out_shape=[jax.ShapeDtypeStruct((4, *b.shape[1:]), b.dtype) for b in blocks],
                sems=[pltpu.SemaphoreType.DMA((4 * n,)), pltpu.SemaphoreType.DMA((4 * n,))])


def _chips_ride(partials):
    n = len(partials)
    return dict(plan=_chips_plan, arrays=list(partials),
                out_shape=[jax.ShapeDtypeStruct((3, *p.shape[1:]), p.dtype) for p in partials],
                sems=[pltpu.SemaphoreType.DMA((3 * n,)), pltpu.SemaphoreType.DMA((3 * n,))])


def _exchange(ride, name):
    n = len(ride["arrays"])

    def body(*refs):
        start, finish = ride["plan"](refs[:n], refs[n:2 * n], *refs[2 * n:])
        start()
        finish()

    return pl.pallas_call(
        body, name=name, out_shape=ride["out_shape"], in_specs=[_ANY] * n, out_specs=[_ANY] * n, scratch_shapes=ride["sems"],
    )(*ride["arrays"])


def _grid_call(body, name, steps, in_specs, out_specs, out_shape, scratch_shapes, args, ride=None):
    if ride is None:
        outs = pl.pallas_call(body, name=name, grid=(steps,), in_specs=in_specs, out_specs=out_specs, out_shape=out_shape,
                              scratch_shapes=scratch_shapes, compiler_params=_params())(*args)
        return list(outs), []
    n_in, n_out, n_scr, n_ride, n_sems = len(in_specs), len(out_specs), len(scratch_shapes), len(ride["arrays"]), len(ride["sems"])

    def carrying(*refs):
        ins, r_in = refs[:n_in], refs[n_in:n_in + n_ride]
        o0 = n_in + n_ride
        outs, r_out = refs[o0:o0 + n_out], refs[o0 + n_out:o0 + n_out + n_ride]
        s0 = o0 + n_out + n_ride
        scratch, sems = refs[s0:s0 + n_scr], refs[s0 + n_scr:s0 + n_scr + n_sems]
        start, finish = ride["plan"](r_in, r_out, *sems)
        pl.when(pl.program_id(0) == 0)(start)
        body(*ins, *outs, *scratch)
        pl.when(pl.program_id(0) == steps - 1)(finish)

    outs = pl.pallas_call(
        carrying, name=name, grid=(steps,), in_specs=list(in_specs) + [_ANY] * n_ride, out_specs=list(out_specs) + [_ANY] * n_ride,
        out_shape=list(out_shape) + ride["out_shape"], scratch_shapes=list(scratch_shapes) + ride["sems"],
        compiler_params=_params())(*args, *ride["arrays"])
    return list(outs[:n_out]), list(outs[n_out:])


def _row_block(rows):
    return rows if rows <= 512 else next(k for k in (512, 256, 128, rows) if rows % k == 0)


def _add_pairs(gs, core, b, name):
    k, r, n = b.shape
    tr = _row_block(r)

    def body(core_ref, a_ref, b_ref, o_ref):
        o_ref[0] = (a_ref[0, 0].astype(F32) + b_ref[0].astype(F32)).astype(o_ref.dtype)

    spec = pl.BlockSpec((1, tr, n), lambda j, i, c: (j, i, 0))
    return pl.pallas_call(
        body, name=name,
        grid_spec=pltpu.PrefetchScalarGridSpec(
            num_scalar_prefetch=1, grid=(k, r // tr),
            in_specs=[pl.BlockSpec((1, 1, tr, n), lambda j, i, c: (j, c[0], i, 0)), spec], out_specs=spec),
        out_shape=jax.ShapeDtypeStruct(b.shape, b.dtype), compiler_params=_params(("arbitrary", "arbitrary")),
    )(core.reshape(1), gs.reshape(k, 2, r, n), b)


def _adamw(w, m, v, parts, sel, name):
    _, r, n = w.shape
    tr = _row_block(r)
    nparts = len(parts)
    bc1 = 1.0 - ADAM_B1 ** ADAM_STEP
    bc2 = 1.0 - ADAM_B2 ** ADAM_STEP

    def body(sel_ref, *refs):
        w_ref, m_ref, v_ref = refs[:3]
        p_refs = refs[3:3 + nparts]
        g_ref, d_ref, nm_ref, nv_ref = refs[3 + nparts:]
        g = p_refs[0][...].astype(F32)
        for p in p_refs[1:]:
            g = g + p[...].astype(F32)
        nm = ADAM_B1 * m_ref[...] + (1.0 - ADAM_B1) * g
        nv = ADAM_B2 * v_ref[...] + (1.0 - ADAM_B2) * (g * g)
        m_hat = nm / bc1
        v_hat = nv / bc2
        g_ref[...] = g
        d_ref[...] = -ADAM_LR * (m_hat / (jnp.sqrt(v_hat) + ADAM_EPS) + ADAM_WD * w_ref[...])
        nm_ref[...] = nm
        nv_ref[...] = nv

    def part_spec(idx):
        if idx is None:
            return pl.BlockSpec((1, tr, n), lambda i, s: (s[0], i, 0))
        return pl.BlockSpec((1, tr, n), lambda i, s, idx=idx: (idx, i, 0))

    spec = pl.BlockSpec((1, tr, n), lambda i, s: (0, i, 0))
    out = jax.ShapeDtypeStruct((1, r, n), F32)
    return pl.pallas_call(
        body, name=name,
        grid_spec=pltpu.PrefetchScalarGridSpec(
            num_scalar_prefetch=1, grid=(r // tr,),
            in_specs=[spec] * 3 + [part_spec(idx) for _, idx in parts], out_specs=[spec] * 4),
        out_shape=[out] * 4, compiler_params=_params(),
    )(jnp.zeros((1,), jnp.int32) if sel is None else sel.reshape(1), w, m, v, *[p for p, _ in parts])


def _rms_parts(h, g):
    r = lax.rsqrt(jnp.mean(h * h, axis=-1, keepdims=True) + RMS_EPS)
    xhat = h * r
    return r, xhat, xhat * g


def _rms_bwd(dn, g, r, xhat):
    dxh = dn * g
    return r * (dxh - xhat * jnp.mean(dxh * xhat, axis=-1, keepdims=True))


def _ffn_fwd(h, g, wg, wu, wd, name, ride=None):
    t, d = h.shape
    f = wg.shape[1]
    tm = TOKEN_TILE
    chunks = _chunks(f, FFN_CHUNK)

    def body(h_ref, g_ref, wg_ref, wu_ref, wd_ref, o_ref, a_ref, b_ref):
        hv = h_ref[...]
        n = _rms_parts(hv, g_ref[...])[2].astype(BF16)
        acc = jnp.zeros((tm, d), F32)
        for s, e in chunks:
            a = _dot(n, wg_ref[:, s:e])
            b = _dot(n, wu_ref[:, s:e])
            a_ref[:, s:e] = a.astype(BF16)
            b_ref[:, s:e] = b.astype(BF16)
            acc = acc + _dot((a * jax.nn.sigmoid(a) * b).astype(BF16), wd_ref[s:e, :])
        o_ref[...] = hv + 0.5 * acc

    tile = pl.BlockSpec((tm, d), lambda i: (i, 0))
    wide = pl.BlockSpec((tm, f), lambda i: (i, 0))
    return _grid_call(
        body, name, t // tm, [tile, _VM, _VM, _VM, _VM], [tile, wide, wide],
        [jax.ShapeDtypeStruct((t, d), F32), jax.ShapeDtypeStruct((t, f), BF16), jax.ShapeDtypeStruct((t, f), BF16)],
        [], (h, g, wg, wu, wd), ride)


def _ffn_bwd(h, dh_out, a16, b16, g, wg, wu, wd, name, ride=None):
    t, d = h.shape
    f = wg.shape[1]
    tm = TOKEN_TILE
    chunks = _chunks(f, FFN_CHUNK)

    def body(h_ref, dho_ref, a_ref, b_ref, g_ref, wg_ref, wu_ref, wd_ref, dh_ref, dg_ref, n_ref, da_ref, db_ref, s_ref, do_ref):
        @pl.when(pl.program_id(0) == 0)
        def _():
            dg_ref[...] = jnp.zeros_like(dg_ref)

        hv = h_ref[...]
        gv = g_ref[...]
        r, xhat, n32 = _rms_parts(hv, gv)
        dho = dho_ref[...]
        do = (0.5 * dho).astype(BF16)
        dn = jnp.zeros((tm, d), F32)
        for s, e in chunks:
            a = a_ref[:, s:e].astype(F32)
            b = b_ref[:, s:e].astype(F32)
            sig = jax.nn.sigmoid(a)
            sa = a * sig
            ds = _dot_nt(do, wd_ref[s:e, :])
            da = (ds * b * (sig * (1.0 + a * (1.0 - sig)))).astype(BF16)
            db = (ds * sa).astype(BF16)
            s_ref[:, s:e] = (sa * b).astype(BF16)
            da_ref[:, s:e] = da
            db_ref[:, s:e] = db
            dn = dn + _dot_nt(da, wg_ref[:, s:e]) + _dot_nt(db, wu_ref[:, s:e])
        dh_ref[...] = dho + _rms_bwd(dn, gv, r, xhat)
        dg_ref[...] += jnp.sum(dn * xhat, axis=0, keepdims=True)
        n_ref[...] = n32.astype(BF16)
        do_ref[...] = do

    tile = pl.BlockSpec((tm, d), lambda i: (i, 0))
    wide = pl.BlockSpec((tm, f), lambda i: (i, 0))
    one = pl.BlockSpec((1, d), lambda i: (0, 0))
    return _grid_call(
        body, name, t // tm, [tile, tile, wide, wide, _VM, _VM, _VM, _VM], [tile, one, tile, wide, wide, wide, tile],
        [jax.ShapeDtypeStruct((t, d), F32), jax.ShapeDtypeStruct((1, d), F32),
         jax.ShapeDtypeStruct((t, d), BF16), jax.ShapeDtypeStruct((t, f), BF16),
         jax.ShapeDtypeStruct((t, f), BF16), jax.ShapeDtypeStruct((t, f), BF16),
         jax.ShapeDtypeStruct((t, d), BF16)],
        [], (h, dh_out, a16, b16, g, wg, wu, wd), ride)


def _head(h, tgt, g, lo, hi, name):
    t, d = h.shape
    tm = TOKEN_TILE

    def body(h_ref, t_ref, g_ref, dh_ref, loss_ref, dg_ref):
        i = pl.program_id(0)

        @pl.when(i == 0)
        def _():
            loss_ref[...] = jnp.zeros_like(loss_ref)
            dg_ref[...] = jnp.zeros_like(dg_ref)

        gv = g_ref[...]
        r, xhat, y = _rms_parts(h_ref[...], gv)
        row = i * tm + lax.broadcasted_iota(jnp.int32, (tm, 1), 0)
        err = jnp.where((row >= lo) & (row < hi), y - t_ref[...], 0.0)
        loss_ref[...] += jnp.full(loss_ref.shape, 0.5 * jnp.sum(jnp.mean(err * err, axis=-1, keepdims=True)), F32)
        dy = err * (1.0 / d)
        dg_ref[...] += jnp.sum(dy * xhat, axis=0, keepdims=True)
        dh_ref[...] = _rms_bwd(dy, gv, r, xhat)

    tile = pl.BlockSpec((tm, d), lambda i: (i, 0))
    return pl.pallas_call(
        body, name=name, grid=(t // tm,), in_specs=[tile, tile, _VM],
        out_specs=[tile, pl.BlockSpec((SUBLANES, 128), lambda i: (0, 0)), pl.BlockSpec((1, d), lambda i: (0, 0))],
        out_shape=[jax.ShapeDtypeStruct((t, d), F32), jax.ShapeDtypeStruct((SUBLANES, 128), F32),
                   jax.ShapeDtypeStruct((1, d), F32)],
        compiler_params=_params(),
    )(h, tgt, g)


def _dw(a, b, name):
    t, m = a.shape
    n = b.shape[1]
    bn = next(k for k in (512, 256, n) if n % k == 0)

    def body(a_ref, b_ref, o_ref):
        o_ref[...] = _dot_tn(a_ref[...], b_ref[...])

    return pl.pallas_call(
        body, name=name, grid=(n // bn,),
        in_specs=[_VM, pl.BlockSpec((t, bn), lambda j: (0, j))], out_specs=pl.BlockSpec((m, bn), lambda j: (0, j)),
        out_shape=jax.ShapeDtypeStruct((m, n), F32), compiler_params=_params(),
    )(a, b)


def _s5_discretise(a_re, a_im, log_dt, b_re, b_im):
    dt = jnp.exp(log_dt)
    mag = jnp.exp(a_re * dt)
    lam_re = mag * jnp.cos(a_im * dt)
    lam_im = mag * jnp.sin(a_im * dt)
    den = a_re * a_re + a_im * a_im
    q_re = ((lam_re - 1.0) * a_re + lam_im * a_im) / den
    q_im = (lam_im * a_re - (lam_re - 1.0) * a_im) / den
    bb_re = q_re[:, None, :] * b_re - q_im[:, None, :] * b_im
    bb_im = q_re[:, None, :] * b_im + q_im[:, None, :] * b_re
    return lam_re, lam_im, bb_re, bb_im


def _s5_params_fwd(a_re, a_im, log_dt, b_re, b_im):
    g, p = a_re.shape
    c = b_re.shape[1]

    def body(are_ref, aim_ref, ldt_ref, bre_ref, bim_ref, pwr_ref, pwi_ref, bbr_ref, bbi_ref):
        lr, li, bbr, bbi = _s5_discretise(are_ref[...], aim_ref[...], ldt_ref[...], bre_ref[...], bim_ref[...])
        bbr_ref[...] = bbr
        bbi_ref[...] = bbi
        pr, pi = lr, li
        pwr_ref[0] = pr
        pwi_ref[0] = pi
        for k in range(1, SUBLANES):
            pr, pi = pr * lr - pi * li, pr * li + pi * lr
            pwr_ref[k] = pr
            pwi_ref[k] = pi

    return pl.pallas_call(
        body, name="s5_params_fwd",
        out_shape=[jax.ShapeDtypeStruct((SUBLANES, g, p), F32), jax.ShapeDtypeStruct((SUBLANES, g, p), F32),
                   jax.ShapeDtypeStruct((g, c, p), F32), jax.ShapeDtypeStruct((g, c, p), F32)],
    )(a_re, a_im, log_dt, b_re, b_im)


def _s5_params_bwd(a_re, a_im, log_dt, b_re, b_im, dlam, dbb_re, dbb_im):
    g, p = a_re.shape
    c = b_re.shape[1]

    def body(are_ref, aim_ref, ldt_ref, bre_ref, bim_ref, dlam_ref, dbr_ref, dbi_ref,
             dare_ref, daim_ref, dldt_ref, dbre_ref, dbim_ref):
        dlr = jnp.sum(dlam_ref[0], axis=0)
        dli = jnp.sum(dlam_ref[1], axis=0)
        _, vjp = jax.vjp(_s5_discretise, are_ref[...], aim_ref[...], ldt_ref[...], bre_ref[...], bim_ref[...])
        dare, daim, dldt, dbre, dbim = vjp((dlr, dli, dbr_ref[...], dbi_ref[...]))
        dare_ref[...] = dare
        daim_ref[...] = daim
        dldt_ref[...] = dldt
        dbre_ref[...] = dbre
        dbim_ref[...] = dbim

    return pl.pallas_call(
        body, name="s5_params_bwd",
        out_shape=[jax.ShapeDtypeStruct((g, p), F32), jax.ShapeDtypeStruct((g, p), F32),
                   jax.ShapeDtypeStruct((g, 1), F32), jax.ShapeDtypeStruct((g, c, p), F32),
                   jax.ShapeDtypeStruct((g, c, p), F32)],
    )(a_re, a_im, log_dt, b_re, b_im, dlam, dbb_re, dbb_im)


def _scan_chunks(gp):
    hg = gp // 2
    w = min(SCAN_LANES, hg)
    return w, [(half * hg + k * w, half * gp + k * w, half * gp + hg + k * w) for half in range(2) for k in range(hg // w)]


def _cmul_acc(xr, xi, tr, ti, sr, si):
    return xr + tr * sr - ti * si, xi + tr * si + ti * sr


def _scan_fwd(buf_ref, row0, tm, ltab_ref, cin_ref, cout_ref, gp):
    w, chunks = _scan_chunks(gp)
    for lo_t, lo_r, lo_i in chunks:
        def body(r, carry, lo_t=lo_t, lo_r=lo_r, lo_i=lo_i):
            cr, ci = carry
            row = pl.multiple_of(row0 + r * SUBLANES, SUBLANES)
            xr = buf_ref[pl.ds(row, SUBLANES), lo_r:lo_r + w]
            xi = buf_ref[pl.ds(row, SUBLANES), lo_i:lo_i + w]
            for tab, shift in ((0, 1), (2, 2), (4, 4)):
                xr, xi = _cmul_acc(xr, xi, ltab_ref[tab, :, lo_t:lo_t + w], ltab_ref[tab + 1, :, lo_t:lo_t + w],
                                   pltpu.roll(xr, shift, 0), pltpu.roll(xi, shift, 0))
            xr, xi = _cmul_acc(xr, xi, ltab_ref[6, :, lo_t:lo_t + w], ltab_ref[7, :, lo_t:lo_t + w], cr, ci)
            buf_ref[pl.ds(row, SUBLANES), lo_r:lo_r + w] = xr
            buf_ref[pl.ds(row, SUBLANES), lo_i:lo_i + w] = xi
            last = SUBLANES - 1
            return (jnp.broadcast_to(xr[last:last + 1], (SUBLANES, w)), jnp.broadcast_to(xi[last:last + 1], (SUBLANES, w)))

        cr, ci = lax.fori_loop(0, tm // SUBLANES, body,
                               (cin_ref[0:SUBLANES, lo_r:lo_r + w], cin_ref[0:SUBLANES, lo_i:lo_i + w]))
        if cout_ref is not None:
            cout_ref[0:SUBLANES, lo_r:lo_r + w] = cr
            cout_ref[0:SUBLANES, lo_i:lo_i + w] = ci


def _scan_rev(g_ref, hext_ref, tm, ltab_ref, gc_ref, dlam_ref, gp):
    w, chunks = _scan_chunks(gp)
    nb = tm // SUBLANES
    for lo_t, lo_r, lo_i in chunks:
        def body(k, carry, lo_t=lo_t, lo_r=lo_r, lo_i=lo_i):
            cr, ci, ar, ai = carry
            row = pl.multiple_of((nb - 1 - k) * SUBLANES, SUBLANES)
            xr = g_ref[pl.ds(row, SUBLANES), lo_r:lo_r + w]
            xi = g_ref[pl.ds(row, SUBLANES), lo_i:lo_i + w]
            for tab, shift in ((8, 7), (10, 6), (12, 4)):
                xr, xi = _cmul_acc(xr, xi, ltab_ref[tab, :, lo_t:lo_t + w], ltab_ref[tab + 1, :, lo_t:lo_t + w],
                                   pltpu.roll(xr, shift, 0), pltpu.roll(xi, shift, 0))
            xr, xi = _cmul_acc(xr, xi, ltab_ref[14, :, lo_t:lo_t + w], ltab_ref[15, :, lo_t:lo_t + w], cr, ci)
            g_ref[pl.ds(row, SUBLANES), lo_r:lo_r + w] = xr
            g_ref[pl.ds(row, SUBLANES), lo_i:lo_i + w] = xi
            first = lax.broadcasted_iota(jnp.int32, (SUBLANES, w), 0) == 0
            prev = pl.ds(row, SUBLANES)
            here = pl.ds(row + SUBLANES, SUBLANES)
            hpr = jnp.where(first, pltpu.roll(hext_ref[prev, lo_r:lo_r + w], 1, 0), pltpu.roll(hext_ref[here, lo_r:lo_r + w], 1, 0))
            hpi = jnp.where(first, pltpu.roll(hext_ref[prev, lo_i:lo_i + w], 1, 0), pltpu.roll(hext_ref[here, lo_i:lo_i + w], 1, 0))
            ar = ar + xr * hpr + xi * hpi
            ai = ai - xr * hpi + xi * hpr
            return (jnp.broadcast_to(xr[0:1], (SUBLANES, w)), jnp.broadcast_to(xi[0:1], (SUBLANES, w)), ar, ai)

        cr, ci, ar, ai = lax.fori_loop(
            0, nb, body, (gc_ref[:, lo_r:lo_r + w], gc_ref[:, lo_i:lo_i + w], dlam_ref[:, lo_r:lo_r + w], dlam_ref[:, lo_i:lo_i + w]))
        gc_ref[:, lo_r:lo_r + w] = cr
        gc_ref[:, lo_i:lo_i + w] = ci
        dlam_ref[:, lo_r:lo_r + w] = ar
        dlam_ref[:, lo_i:lo_i + w] = ai


def _conv_taps(cw, cext_ref, cin, tm):
    return (cw[0:1] * cext_ref[SUBLANES - 2:SUBLANES - 2 + tm, :] + cw[1:2] * cext_ref[SUBLANES - 1:SUBLANES - 1 + tm, :]
            + cw[2:3] * cin)


def _mix_fwd(h, gm, win, bg, bc, cc, dsk, wglu, cw, wco, wo, ltab, dims):
    d, ds, dc, gp = dims
    t = h.shape[0]
    tm = MIX_TILE
    nt = t // tm
    dsh = ds // 2
    o1, o2, o3 = ds + dc, ds + 2 * dc, ds + 3 * dc
    ncols = o3 + 2 * d

    def body(h_ref, gm_ref, win_ref, bg_ref, bc_ref, cc_ref, dsk_ref, wglu_ref, cw_ref, wco_ref, wo_ref, ltab_ref,
             h2_ref, st_ref, cvs_ref, p_ref, hs_ref, y5_ref, z_ref, yc_ref, hbuf_ref, carry_ref, cext_ref):
        @pl.when(pl.program_id(0) == 0)
        def _():
            carry_ref[...] = jnp.zeros_like(carry_ref)
            cext_ref[0:SUBLANES, :] = jnp.zeros((SUBLANES, dc), F32)

        st_ref[0] = carry_ref[...]
        cvs_ref[0] = cext_ref[0:SUBLANES, :]
        hv = h_ref[...]
        bg = bg_ref[...]
        u = _rms_parts(hv, gm_ref[...])[2].astype(BF16)
        us = _dot(u, win_ref[:, 0:ds])
        v = _dot(u, win_ref[:, ds:o1])
        gb = _dot(u, win_ref[:, o1:o2])
        gcv = _dot(u, win_ref[:, o2:o3])
        gs = jax.nn.sigmoid(_dot(u, win_ref[:, o3:o3 + d]) + bg[:, 0:d])
        gcg = jax.nn.sigmoid(_dot(u, win_ref[:, o3 + d:o3 + 2 * d]) + bg[:, d:2 * d])
        us16 = us.astype(BF16)
        p_ref[:, 0:ds] = us16
        p_ref[:, ds:o1] = v.astype(BF16)
        p_ref[:, o1:o2] = gb.astype(BF16)
        p_ref[:, o2:o3] = gcv.astype(BF16)
        p_ref[:, o3:o3 + d] = gs.astype(BF16)
        p_ref[:, o3 + d:ncols] = gcg.astype(BF16)
        for half in range(2):
            hbuf_ref[:, half * gp:(half + 1) * gp] = _dot(us16[:, half * dsh:(half + 1) * dsh], bc_ref[half])
        _scan_fwd(hbuf_ref, 0, tm, ltab_ref, carry_ref, carry_ref, gp)
        hs_ref[...] = hbuf_ref[...].astype(BF16)
        y5 = jnp.concatenate([_dot(hs_ref[:, half * gp:(half + 1) * gp], cc_ref[half]) for half in range(2)], axis=1) + dsk_ref[...] * us
        y5_ref[...] = y5.astype(BF16)
        z = _dot(jax.nn.gelu(y5).astype(BF16), wglu_ref[...])
        z_ref[...] = z.astype(BF16)
        ys = z[:, 0:d] * jax.nn.sigmoid(z[:, d:2 * d])
        cin = gcv * v
        cext_ref[SUBLANES:SUBLANES + tm, :] = cin
        yc = _dot((gb * _conv_taps(cw_ref[...], cext_ref, cin, tm)).astype(BF16), wco_ref[...])
        yc_ref[...] = yc.astype(BF16)
        h2_ref[...] = hv + _dot((gs * ys + gcg * yc).astype(BF16), wo_ref[...])
        cext_ref[0:SUBLANES, :] = cext_ref[tm:tm + SUBLANES, :]

    def tile(cols):
        return pl.BlockSpec((tm, cols), lambda i: (i, 0))

    def bf(cols):
        return jax.ShapeDtypeStruct((t, cols), BF16)

    return pl.pallas_call(
        body, name="mix_fwd", grid=(nt,),
        in_specs=[tile(d)] + [_VM] * 11,
        out_specs=[tile(d), pl.BlockSpec((1, SUBLANES, 2 * gp), lambda i: (i, 0, 0)), pl.BlockSpec((1, SUBLANES, dc), lambda i: (i, 0, 0)),
                   tile(ncols), tile(2 * gp), tile(ds), tile(2 * d), tile(d)],
        out_shape=[jax.ShapeDtypeStruct((t, d), F32), jax.ShapeDtypeStruct((nt, SUBLANES, 2 * gp), F32),
                   jax.ShapeDtypeStruct((nt, SUBLANES, dc), F32), bf(ncols), bf(2 * gp), bf(ds), bf(2 * d), bf(d)],
        scratch_shapes=[pltpu.VMEM((tm, 2 * gp), F32), pltpu.VMEM((SUBLANES, 2 * gp), F32), pltpu.VMEM((SUBLANES + tm, dc), F32)],
        compiler_params=_params(),
    )(h, gm, win, bg, bc, cc, dsk, wglu, cw, wco, wo, ltab)


def _mix_bwd(h, dh2, saved, gm, win, bc, cc, dsk, wglu, cw, wco, wo, ltab, dims):
    d, ds, dc, gp = dims
    t = h.shape[0]
    tm = MIX_TILE
    nt = t // tm
    dsh = ds // 2
    o1, o2, o3 = ds + dc, ds + 2 * dc, ds + 3 * dc
    ncols = o3 + 2 * d

    def body(h_ref, dh2_ref, st_ref, cvs_ref, p_ref, hs_ref, y5_ref, z_ref, yc_ref,
             gm_ref, win_ref, bc_ref, cc_ref, dsk_ref, wglu_ref, cw_ref, wco_ref, wo_ref, ltab_ref,
             dh1_ref, u_ref, dp_ref, ge_ref, dz_ref, cg_ref, dyc_ref, mx_ref, dh216_ref,
             dgm_ref, dbg_ref, ddsk_ref, dcw_ref, dlam_ref, dbc_ref, dcc_ref,
             hext_ref, gbuf_ref, gcarry_ref, cext_ref, dcvext_ref):
        @pl.when(pl.program_id(0) == 0)
        def _():
            for ref in (dgm_ref, dbg_ref, ddsk_ref, dcw_ref, dlam_ref, dbc_ref, dcc_ref, gcarry_ref):
                ref[...] = jnp.zeros_like(ref)
            dcvext_ref[tm:tm + SUBLANES, :] = jnp.zeros((SUBLANES, dc), F32)

        hext_ref[0:SUBLANES, :] = st_ref[0]
        hext_ref[SUBLANES:SUBLANES + tm, :] = hs_ref[...].astype(F32)
        cext_ref[0:SUBLANES, :] = cvs_ref[0]
        gmv = gm_ref[...]
        cw_v = cw_ref[...]
        dskv = dsk_ref[...]
        r, xhat, n32 = _rms_parts(h_ref[...], gmv)
        us = p_ref[:, 0:ds].astype(F32)
        v = p_ref[:, ds:o1].astype(F32)
        gb = p_ref[:, o1:o2].astype(F32)
        gcv = p_ref[:, o2:o3].astype(F32)
        gs = p_ref[:, o3:o3 + d].astype(F32)
        gcg = p_ref[:, o3 + d:ncols].astype(F32)
        z1 = z_ref[:, 0:d].astype(F32)
        sz = jax.nn.sigmoid(z_ref[:, d:2 * d].astype(F32))
        ys = z1 * sz
        yc = yc_ref[...].astype(F32)
        y5 = y5_ref[...].astype(F32)
        ge, gelu_vjp = jax.vjp(jax.nn.gelu, y5)
        cin = gcv * v
        cext_ref[SUBLANES:SUBLANES + tm, :] = cin
        cv = _conv_taps(cw_v, cext_ref, cin, tm)

        dh2v = dh2_ref[...]
        dh216 = dh2v.astype(BF16)
        dmixed = _dot_nt(dh216, wo_ref[...])
        dys = dmixed * gs
        dyc16 = (dmixed * gcg).astype(BF16)
        dpgs = dmixed * ys * gs * (1.0 - gs)
        dpgc = dmixed * yc * gcg * (1.0 - gcg)
        dz16 = jnp.concatenate([dys * sz, dys * z1 * sz * (1.0 - sz)], axis=1).astype(BF16)
        dy5 = gelu_vjp(_dot_nt(dz16, wglu_ref[...]))[0]
        dy516 = dy5.astype(BF16)
        for half in range(2):
            gbuf_ref[:, half * gp:(half + 1) * gp] = _dot_nt(dy516[:, half * dsh:(half + 1) * dsh], cc_ref[half])
        _scan_rev(gbuf_ref, hext_ref, tm, ltab_ref, gcarry_ref, dlam_ref, gp)
        dus = []
        for half in range(2):
            g16 = gbuf_ref[:, half * gp:(half + 1) * gp].astype(BF16)
            dus.append(_dot_nt(g16, bc_ref[half]))
            dbc_ref[half] += _dot_tn(p_ref[:, half * dsh:(half + 1) * dsh], g16)
            dcc_ref[half] += _dot_tn(hs_ref[:, half * gp:(half + 1) * gp], dy516[:, half * dsh:(half + 1) * dsh])
        dus = jnp.concatenate(dus, axis=1) + dskv * dy5
        ddsk_ref[...] += jnp.sum(dy5 * us, axis=0, keepdims=True)
        dcg = _dot_nt(dyc16, wco_ref[...])
        dgb = dcg * cv
        dcv = dcg * gb
        dcvext_ref[0:tm, :] = dcv
        dcin = cw_v[2:3] * dcv + cw_v[1:2] * dcvext_ref[1:1 + tm, :] + cw_v[0:1] * dcvext_ref[2:2 + tm, :]
        dcw_ref[0:1, :] += jnp.sum(dcv * cext_ref[SUBLANES - 2:SUBLANES - 2 + tm, :], axis=0, keepdims=True)
        dcw_ref[1:2, :] += jnp.sum(dcv * cext_ref[SUBLANES - 1:SUBLANES - 1 + tm, :], axis=0, keepdims=True)
        dcw_ref[2:3, :] += jnp.sum(dcv * cin, axis=0, keepdims=True)
        dcvext_ref[tm:tm + SUBLANES, :] = dcvext_ref[0:SUBLANES, :]
        dp16 = jnp.concatenate([dus, dcin * gcv, dgb, dcin * v, dpgs, dpgc], axis=1).astype(BF16)
        du = _dot_nt(dp16, win_ref[...])
        dh1_ref[...] = dh2v + _rms_bwd(du, gmv, r, xhat)
        dgm_ref[...] += jnp.sum(du * xhat, axis=0, keepdims=True)
        dbg_ref[...] += jnp.concatenate([jnp.sum(dpgs, axis=0, keepdims=True), jnp.sum(dpgc, axis=0, keepdims=True)], axis=1)
        u_ref[...] = n32.astype(BF16)
        dp_ref[...] = dp16
        ge_ref[...] = ge.astype(BF16)
        dz_ref[...] = dz16
        cg_ref[...] = (gb * cv).astype(BF16)
        dyc_ref[...] = dyc16
        mx_ref[...] = (gs * ys + gcg * yc).astype(BF16)
        dh216_ref[...] = dh216

    def rev(cols):
        return pl.BlockSpec((tm, cols), lambda j: (nt - 1 - j, 0))

    def rev3(cols):
        return pl.BlockSpec((1, SUBLANES, cols), lambda j: (nt - 1 - j, 0, 0))

    def bf(cols):
        return jax.ShapeDtypeStruct((t, cols), BF16)

    st, cvs, p16, hs16, y516, z16, yc16 = saved
    return pl.pallas_call(
        body, name="mix_bwd", grid=(nt,),
        in_specs=[rev(d), rev(d), rev3(2 * gp), rev3(dc), rev(ncols), rev(2 * gp), rev(ds), rev(2 * d), rev(d)] + [_VM] * 10,
        out_specs=[rev(d), rev(d), rev(ncols), rev(ds), rev(2 * d), rev(dc), rev(d), rev(d), rev(d)] + [_VM] * 7,
        out_shape=[jax.ShapeDtypeStruct((t, d), F32), bf(d), bf(ncols), bf(ds), bf(2 * d), bf(dc), bf(d), bf(d), bf(d),
                   jax.ShapeDtypeStruct((1, d), F32), jax.ShapeDtypeStruct((1, 2 * d), F32), jax.ShapeDtypeStruct((1, ds), F32),
                   jax.ShapeDtypeStruct((SUBLANES, dc), F32), jax.ShapeDtypeStruct((SUBLANES, 2 * gp), F32),
                   jax.ShapeDtypeStruct((2, dsh, gp), F32), jax.ShapeDtypeStruct((2, gp, dsh), F32)],
        scratch_shapes=[pltpu.VMEM((SUBLANES + tm, 2 * gp), F32), pltpu.VMEM((tm, 2 * gp), F32),
                        pltpu.VMEM((SUBLANES, 2 * gp), F32), pltpu.VMEM((SUBLANES + tm, dc), F32),
                        pltpu.VMEM((tm + SUBLANES, dc), F32)],
        compiler_params=_params(),
    )(h, dh2, st, cvs, p16, hs16, y516, z16, yc16, gm, win, bc, cc, dsk, wglu, cw, wco, wo, ltab)


def _pad_rows(a, rows, axis=0):
    pad = [(0, 0)] * a.ndim
    pad[axis] = (0, rows - a.shape[axis])
    return jnp.pad(a, pad)


def _as_rows(a):
    flat = a.reshape(-1)
    n = -(-flat.shape[0] // SLAB_COLS) * SLAB_COLS
    return jnp.pad(flat, (0, n - flat.shape[0])).reshape(-1, SLAB_COLS)


def _pack(arrs):
    rows = jnp.concatenate([_as_rows(a) for a in arrs], axis=0)
    return _pad_rows(rows, -(-rows.shape[0] // 16) * 16)


def _unpack(slab, shapes):
    out, r = [], 0
    for shp in shapes:
        size = 1
        for s in shp:
            size *= s
        n = -(-size // SLAB_COLS)
        out.append(slab[r:r + n].reshape(-1)[:size].reshape(shp))
        r += n
    return out


def _block_diag(blocks):
    n, a, b = blocks.shape
    eye = jnp.eye(n, dtype=blocks.dtype)
    return (blocks[:, :, None, :] * eye[:, None, :, None]).reshape(n * a, n * b)


def _diag_blocks(mat, n):
    a, b = mat.shape[0] // n, mat.shape[1] // n
    eye = jnp.eye(n, dtype=mat.dtype)
    return jnp.sum(mat.reshape(n, a, n, b) * eye[:, None, :, None], axis=2)


BIG = (("ffn1_w_gate", "col"), ("ffn1_w_up", "col"), ("ffn1_w_down", "row"), ("w_in", "col"), ("ssm_w_glu", "col"),
       ("conv_w_out", "col"), ("w_o", "row"), ("ffn2_w_gate", "col"), ("ffn2_w_up", "col"), ("ffn2_w_down", "row"))
REPLICATED = ("g_ffn1", "g_mix", "b_gate", "ssm_a_re", "ssm_a_im", "ssm_log_dt", "ssm_b_re", "ssm_b_im", "ssm_c_re",
              "ssm_c_im", "ssm_d", "g_ffn2", "g_final")
WEIGHTS = ("meta_tokens", "g_ffn1", "ffn1_w_gate", "ffn1_w_up", "ffn1_w_down", "g_mix", "w_in", "b_gate", "ssm_a_re",
           "ssm_a_im", "ssm_log_dt", "ssm_b_re", "ssm_b_im", "ssm_c_re", "ssm_c_im", "ssm_d", "ssm_w_glu", "conv_w",
           "conv_w_out", "w_o", "g_ffn2", "ffn2_w_gate", "ffn2_w_up", "ffn2_w_down", "g_final")
N_EARLY = 3


def _full_from_blocks(blocks, kind):
    n, r, c = blocks.shape
    if kind == "col":
        return jnp.transpose(blocks, (1, 0, 2)).reshape(r, n * c)
    return blocks.reshape(n * r, c)


def _blocks_from_full(full, kind):
    if kind == "col":
        r, nc = full.shape
        return jnp.transpose(full.reshape(r, NDEV, nc // NDEV), (1, 0, 2))
    nr, c = full.shape
    return full.reshape(NDEV, nr // NDEV, c)


def kernel(x, meta_tokens, g_ffn1, ffn1_w_gate, ffn1_w_up, ffn1_w_down, g_mix, w_in, b_gate, ssm_a_re, ssm_a_im, ssm_log_dt, ssm_b_re, ssm_b_im, ssm_c_re, ssm_c_im, ssm_d, ssm_w_glu, conv_w, conv_w_out, w_o, g_ffn2, ffn2_w_gate, ffn2_w_up, ffn2_w_down, g_final, loss_target, m_meta_tokens, m_g_ffn1, m_ffn1_w_gate, m_ffn1_w_up, m_ffn1_w_down, m_g_mix, m_w_in, m_b_gate, m_ssm_a_re, m_ssm_a_im, m_ssm_log_dt, m_ssm_b_re, m_ssm_b_im, m_ssm_c_re, m_ssm_c_im, m_ssm_d, m_ssm_w_glu, m_conv_w, m_conv_w_out, m_w_o, m_g_ffn2, m_ffn2_w_gate, m_ffn2_w_up, m_ffn2_w_down, m_g_final, v_meta_tokens, v_g_ffn1, v_ffn1_w_gate, v_ffn1_w_up, v_ffn1_w_down, v_g_mix, v_w_in, v_b_gate, v_ssm_a_re, v_ssm_a_im, v_ssm_log_dt, v_ssm_b_re, v_ssm_b_im, v_ssm_c_re, v_ssm_c_im, v_ssm_d, v_ssm_w_glu, v_conv_w, v_conv_w_out, v_w_o, v_g_ffn2, v_ffn2_w_gate, v_ffn2_w_up, v_ffn2_w_down, v_g_final):
    args = dict(locals())
    w = {n: args[n] for n in WEIGHTS}
    mom_m = {n: args["m_" + n] for n in WEIGHTS}
    mom_v = {n: args["v_" + n] for n in WEIGHTS}

    seq, d = x.shape[1], x.shape[2]
    n_meta = meta_tokens.shape[0]
    ds = ssm_d.shape[1]
    n_grp, n_state = ssm_a_re.shape[1], ssm_a_re.shape[2]
    gp = n_grp * n_state
    dc = conv_w.shape[3] * NDEV
    dims = (d, ds, dc, gp)
    t_real = n_meta + seq
    t_pad = -(-t_real // TOKEN_TILE) * TOKEN_TILE
    me_chip = 2 * lax.axis_index("x") + lax.axis_index("y")
    me_core = lax.axis_index("c")
    me = 2 * me_chip + me_core
    mcols, ccols = d // NDEV, dc // NDEV

    cw_shard = _pad_rows(_pad_rows(conv_w.reshape(3, ccols), SUBLANES), 128, axis=1)
    shard16 = {name: w[name][0].astype(BF16) for name, _ in BIG}
    early, late = BIG[:N_EARLY], BIG[N_EARLY:]
    got = _exchange(_gather_ride([shard16[name] for name, _ in early] + [meta_tokens, cw_shard]), "gather_first")
    full = {name: _full_from_blocks(got[i], kind) for i, (name, kind) in enumerate(early)}
    meta_full = _full_from_blocks(got[-2], "col")
    cw_rows = _pad_rows(_full_from_blocks(got[-1][:, 0:3, 0:ccols], "col"), SUBLANES)

    a_re, a_im, ldt = ssm_a_re[0], ssm_a_im[0], ssm_log_dt[0].reshape(n_grp, 1)
    b_re_t = jnp.transpose(ssm_b_re[0], (0, 2, 1))
    b_im_t = jnp.transpose(ssm_b_im[0], (0, 2, 1))
    pw_r, pw_i, bb_r, bb_i = _s5_params_fwd(a_re, a_im, ldt, b_re_t, b_im_t)
    pw_r = pw_r.reshape(SUBLANES, gp)
    pw_i = pw_i.reshape(SUBLANES, gp)
    sub = jnp.arange(SUBLANES)[:, None]

    def fwd_tab(p, k):
        return jnp.where(sub >= k, p[k - 1][None, :], 0.0)

    def rev_tab(p, k):
        return jnp.where(sub <= SUBLANES - 1 - k, p[k - 1][None, :], 0.0)

    ltab = jnp.stack(
        [fwd_tab(pw_r, 1), fwd_tab(pw_i, 1), fwd_tab(pw_r, 2), fwd_tab(pw_i, 2), fwd_tab(pw_r, 4), fwd_tab(pw_i, 4), pw_r, pw_i,
         rev_tab(pw_r, 1), -rev_tab(pw_i, 1), rev_tab(pw_r, 2), -rev_tab(pw_i, 2), rev_tab(pw_r, 4), -rev_tab(pw_i, 4),
         pw_r[::-1], -pw_i[::-1]], axis=0)
    gh = n_grp // 2
    bc = jnp.stack([jnp.concatenate([_block_diag(bb_r[h * gh:(h + 1) * gh]), _block_diag(bb_i[h * gh:(h + 1) * gh])], axis=1)
                    for h in range(2)]).astype(BF16)
    c_re_t = jnp.transpose(ssm_c_re[0], (0, 2, 1))
    c_im_t = jnp.transpose(ssm_c_im[0], (0, 2, 1))
    cc = jnp.stack([jnp.concatenate([_block_diag(c_re_t[h * gh:(h + 1) * gh]), -_block_diag(c_im_t[h * gh:(h + 1) * gh])], axis=0)
                    for h in range(2)]).astype(BF16)

    zpad = jnp.zeros((t_pad - t_real, d), F32)
    h0 = jnp.concatenate([meta_full, x[0], zpad], axis=0)
    tgt = jnp.concatenate([jnp.zeros((n_meta, d), F32), loss_target[0], zpad], axis=0)
    (h1, a1, b1), got = _ffn_fwd(h0, g_ffn1, full["ffn1_w_gate"], full["ffn1_w_up"], full["ffn1_w_down"], "ffn1_fwd",
                                 ride=_gather_ride([shard16[name] for name, _ in late]))
    full.update({name: _full_from_blocks(got[i], kind) for i, (name, kind) in enumerate(late)})
    h2, *saved = _mix_fwd(h1, g_mix, full["w_in"], b_gate, bc, cc, ssm_d, full["ssm_w_glu"], cw_rows, full["conv_w_out"],
                          full["w_o"], ltab, dims)
    (h3, a2, b2), _ = _ffn_fwd(h2, g_ffn2, full["ffn2_w_gate"], full["ffn2_w_up"], full["ffn2_w_down"], "ffn2_fwd")
    dh3, loss_blk, dg_final = _head(h3, tgt, g_final.reshape(1, d), n_meta, t_real, "loss_head")
    loss = lax.psum(loss_blk[0, 0], AXES)

    (dh2, dg_ffn2, n2, da2, db2, s2, do2), _ = _ffn_bwd(
        h2, dh3, a2, b2, g_ffn2, full["ffn2_w_gate"], full["ffn2_w_up"], full["ffn2_w_down"], "ffn2_bwd")
    (dh1, u16, dp16, ge16, dz16, cg16, dyc16, mx16, dh216, dg_mix, dbg, ddsk, dcw, dlam, dbc, dcc) = _mix_bwd(
        h1, dh2, saved, g_mix, full["w_in"], bc, cc, ssm_d, full["ssm_w_glu"], cw_rows, full["conv_w_out"], full["w_o"], ltab, dims)
    dblocks = {
        "w_in": _blocks_from_full(_dw(u16, dp16, "dw_in"), "col"),
        "ssm_w_glu": _blocks_from_full(_dw(ge16, dz16, "dw_glu"), "col"),
        "conv_w_out": _blocks_from_full(_dw(cg16, dyc16, "dw_conv_out"), "col"),
        "w_o": _blocks_from_full(_dw(mx16, dh216, "dw_o"), "row"),
        "ffn2_w_gate": _blocks_from_full(_dw(n2, da2, "dw_ffn2_gate"), "col"),
        "ffn2_w_up": _blocks_from_full(_dw(n2, db2, "dw_ffn2_up"), "col"),
        "ffn2_w_down": jnp.transpose(_blocks_from_full(_dw(do2, s2, "dw_ffn2_down"), "col"), (0, 2, 1)),
    }

    def pair_sums(names, tag):
        gs = [dblocks[name].astype(BF16) for name in names]
        from_sibling = _exchange(_pair_ride(gs), "reduce_pair_" + tag)
        return [_add_pairs(g, me_core, b, "reduce_pair_add_" + name) for g, b, name in zip(gs, from_sibling, names)]

    late_names = [name for name, _ in late]
    pairs = dict(zip(late_names, pair_sums(late_names, "late")))
    (dh0, dg_ffn1, n1, da1, db1, s1, do1), got = _ffn_bwd(
        h0, dh1, a1, b1, g_ffn1, full["ffn1_w_gate"], full["ffn1_w_up"], full["ffn1_w_down"], "ffn1_bwd",
        ride=_chips_ride([pairs[name] for name in late_names]))
    from_chips = dict(zip(late_names, got))
    dblocks.update({
        "ffn1_w_gate": _blocks_from_full(_dw(n1, da1, "dw_ffn1_gate"), "col"),
        "ffn1_w_up": _blocks_from_full(_dw(n1, db1, "dw_ffn1_up"), "col"),
        "ffn1_w_down": jnp.transpose(_blocks_from_full(_dw(do1, s1, "dw_ffn1_down"), "col"), (0, 2, 1)),
    })
    early_names = [name for name, _ in early]
    pairs.update(zip(early_names, pair_sums(early_names, "early")))
    from_chips.update(zip(early_names, _exchange(_chips_ride([pairs[name] for name in early_names]), "reduce_chips_early")))

    dlam4 = dlam.reshape(SUBLANES, 2, 2, gh, n_state)
    dlam_in = jnp.transpose(dlam4, (2, 0, 1, 3, 4)).reshape(2, SUBLANES, n_grp, n_state)
    hg = gp // 2
    dbb_r = jnp.concatenate([_diag_blocks(dbc[h][:, :hg], gh) for h in range(2)], axis=0)
    dbb_i = jnp.concatenate([_diag_blocks(dbc[h][:, hg:], gh) for h in range(2)], axis=0)
    da_re, da_im, dldt, dbre_t, dbim_t = _s5_params_bwd(a_re, a_im, ldt, b_re_t, b_im_t, dlam_in, dbb_r, dbb_i)
    dc_re = jnp.concatenate([_diag_blocks(dcc[h][:hg], gh) for h in range(2)], axis=0)
    dc_im = -jnp.concatenate([_diag_blocks(dcc[h][hg:], gh) for h in range(2)], axis=0)

    grads_rep = {
        "g_ffn1": dg_ffn1, "g_mix": dg_mix, "b_gate": dbg, "ssm_a_re": da_re[None], "ssm_a_im": da_im[None],
        "ssm_log_dt": dldt.reshape(1, n_grp), "ssm_b_re": jnp.transpose(dbre_t, (0, 2, 1))[None],
        "ssm_b_im": jnp.transpose(dbim_t, (0, 2, 1))[None], "ssm_c_re": jnp.transpose(dc_re, (0, 2, 1))[None],
        "ssm_c_im": jnp.transpose(dc_im, (0, 2, 1))[None], "ssm_d": ddsk, "g_ffn2": dg_ffn2, "g_final": dg_final.reshape(d),
    }

    out_g, out_d, out_m, out_v = {}, {}, {}, {}
    for name, _ in BIG:
        fc = from_chips[name]
        out_g[name], out_d[name], out_m[name], out_v[name] = _adamw(
            w[name], mom_m[name], mom_v[name], [(pairs[name], None), (fc, 0), (fc, 1), (fc, 2)], me_chip, "adamw_" + name)

    rep_shapes = [w[n].shape for n in REPLICATED]
    small_g_shapes = rep_shapes + [(n_meta, d), (3, dc)]
    gsmall = _pack([grads_rep[n] for n in REPLICATED] + [dh0[0:n_meta], dcw[0:3]])
    gall = _exchange(_gather_ride([gsmall]), "gather_small_grads")[0]
    zer = [jnp.zeros((n_meta, d), F32), jnp.zeros((3, dc), F32)]
    gr, dr, mr, vr = [o[0] for o in _adamw(
        _pack([w[n] for n in REPLICATED] + zer)[None], _pack([mom_m[n] for n in REPLICATED] + zer)[None],
        _pack([mom_v[n] for n in REPLICATED] + zer)[None], [(gall, b) for b in range(NDEV)], None, "adamw_replicated")]
    g_list = _unpack(gr, small_g_shapes)
    out_g.update(zip(REPLICATED, g_list[:len(REPLICATED)]))
    out_d.update(zip(REPLICATED, _unpack(dr, rep_shapes)))
    out_m.update(zip(REPLICATED, _unpack(mr, rep_shapes)))
    out_v.update(zip(REPLICATED, _unpack(vr, rep_shapes)))

    g_meta = lax.dynamic_slice_in_dim(g_list[-2], me * mcols, mcols, axis=1)
    g_cw = lax.dynamic_slice_in_dim(g_list[-1], me * ccols, ccols, axis=1).reshape(conv_w.shape)
    tiny = ("meta_tokens", "conv_w")
    tiny_shapes = [meta_tokens.shape, conv_w.shape]
    gt, dt_, mt, vt = [o[0] for o in _adamw(
        _pack([w[n] for n in tiny])[None], _pack([mom_m[n] for n in tiny])[None], _pack([mom_v[n] for n in tiny])[None],
        [(_pack([g_meta, g_cw])[None], 0)], None, "adamw_tiny")]
    out_g.update(zip(tiny, _unpack(gt, tiny_shapes)))
    out_d.update(zip(tiny, _unpack(dt_, tiny_shapes)))
    out_m.update(zip(tiny, _unpack(mt, tiny_shapes)))
    out_v.update(zip(tiny, _unpack(vt, tiny_shapes)))

    grad_x = dh0[n_meta:t_real][None]
    return (loss, grad_x, *[out_g[n] for n in WEIGHTS], *[out_d[n] for n in WEIGHTS],
            *[out_m[n] for n in WEIGHTS], *[out_v[n] for n in WEIGHTS])
```

```python
import functools

import jax
import jax.numpy as jnp
from jax import lax
from jax.experimental import pallas as pl
from jax.experimental.pallas import tpu as pltpu

F32 = jnp.float32
BF16 = jnp.bfloat16
MESH = pl.DeviceIdType.MESH
AXES = ("x", "y", "c")
NDEV = 8
SLAB_COLS = 1024
RMS_EPS = 1e-6
TOKEN_TILE = 256
MIX_TILE = 128
SUBLANES = 8
SCAN_LANES = 512
FFN_CHUNK = 1024
VMEM_LIMIT_BYTES = 56 * 1024 * 1024

ADAM_LR = 0.001
ADAM_B1 = 0.9
ADAM_B2 = 0.999
ADAM_EPS = 1e-08
ADAM_WD = 0.01
ADAM_STEP = 10

_VM = pl.BlockSpec(memory_space=pltpu.VMEM)
_ANY = pl.BlockSpec(memory_space=pl.ANY)


def _params(sem=("arbitrary",)):
    return pltpu.CompilerParams(dimension_semantics=sem, vmem_limit_bytes=VMEM_LIMIT_BYTES)


def _dot(a, b):
    return jnp.dot(a, b, preferred_element_type=F32)


def _dot_nt(a, b):
    return lax.dot_general(a, b, (((1,), (1,)), ((), ())), preferred_element_type=F32)


def _dot_tn(a, b):
    return lax.dot_general(a, b, (((0,), (0,)), ((), ())), preferred_element_type=F32)


def _chunks(n, step):
    return [(s, min(s + step, n)) for s in range(0, n, step)]


def _gather_plan(x_refs, out_refs, send_sems, recv_sems, local_sems):
    n = len(x_refs)
    x, y, c = lax.axis_index("x"), lax.axis_index("y"), lax.axis_index("c")
    me, sibling = (x, y, c), (x, y, 1 - c)
    chips = [(1 - x, y), (x, 1 - y), (1 - x, 1 - y)]

    def copy(i, k, block, to, src=None):
        slot = out_refs[i].at[4 * block[0] + 2 * block[1] + block[2]]
        return pltpu.make_async_remote_copy(
            src_ref=slot if src is None else src, dst_ref=slot,
            send_sem=send_sems.at[7 * i + k], recv_sem=recv_sems.at[7 * i + k], device_id=to, device_id_type=MESH)

    def mine():
        return [pltpu.make_async_copy(x_refs[i], out_refs[i].at[4 * x + 2 * y + c], local_sems.at[i]) for i in range(n)]

    def first():
        out = []
        for i in range(n):
            out.append(copy(i, 0, me, sibling, src=x_refs[i]))
            out += [copy(i, 1 + j, me, (*chip, c), src=x_refs[i]) for j, chip in enumerate(chips)]
        return out

    def start():
        for cp in mine() + first():
            cp.start()

    def finish():
        passed = []
        for j, chip in enumerate(chips):
            for i in range(n):
                copy(i, 1 + j, (*chip, c), me).wait_recv()
                cp = copy(i, 4 + j, (*chip, c), sibling)
                cp.start()
                passed.append(cp)
        for i in range(n):
            copy(i, 0, sibling, me).wait_recv()
            for j, chip in enumerate(chips):
                copy(i, 4 + j, (*chip, 1 - c), me).wait_recv()
        for cp in first() + passed:
            cp.wait_send()
        for cp in mine():
            cp.wait()

    return start, finish


def _pair_plan(g_refs, out_refs, send_sems, recv_sems):
    x, y, c = lax.axis_index("x"), lax.axis_index("y"), lax.axis_index("c")

    def copies():
        return [pltpu.make_async_remote_copy(
            src_ref=g_refs[i].at[2 * j + (1 - c)], dst_ref=out_refs[i].at[j],
            send_sem=send_sems.at[4 * i + j], recv_sem=recv_sems.at[4 * i + j],
            device_id=(x, y, 1 - c), device_id_type=MESH) for i in range(len(g_refs)) for j in range(4)]

    def start():
        for cp in copies():
            cp.start()

    def finish():
        for cp in copies():
            cp.wait()

    return start, finish


def _chips_plan(p_refs, out_refs, send_sems, recv_sems):
    x, y, c = lax.axis_index("x"), lax.axis_index("y"), lax.axis_index("c")

    def copies():
        return [pltpu.make_async_remote_copy(
            src_ref=p_refs[i].at[2 * px + py], dst_ref=out_refs[i].at[k],
            send_sem=send_sems.at[3 * i + k], recv_sem=recv_sems.at[3 * i + k],
            device_id=(px, py, c), device_id_type=MESH)
            for i in range(len(p_refs)) for k, (px, py) in enumerate([(1 - x, y), (x, 1 - y), (1 - x, 1 - y)])]

    def start():
        for cp in copies():
            cp.start()

    def finish():
        for cp in copies():
            cp.wait()

    return start, finish


def _gather_ride(shards):
    n = len(shards)
    return dict(plan=_gather_plan, arrays=list(shards),
                out_shape=[jax.ShapeDtypeStruct((NDEV, *s.shape), s.dtype) for s in shards],
                sems=[pltpu.SemaphoreType.DMA((7 * n,)), pltpu.SemaphoreType.DMA((7 * n,)), pltpu.SemaphoreType.DMA((n,))])


def _pair_ride(blocks):
    n = len(blocks)
    return dict(plan=_pair_plan, arrays=list(blocks),
                out_shape=[jax.ShapeDtypeStruct((4, *b.shape[1:]), b.dtype) for b in blocks],
                sems=[pltpu.SemaphoreType.DMA((4 * n,)), pltpu.SemaphoreType.DMA((4 * n,))])


def _chips_ride(partials):
    n = len(partials)
    return dict(plan=_chips_plan, arrays=list(partials),
                out_shape=[jax.ShapeDtypeStruct((3, *p.shape[1:]), p.dtype) for p in partials],
                sems=[pltpu.SemaphoreType.DMA((3 * n,)), pltpu.SemaphoreType.DMA((3 * n,))])


def _exchange(ride, name):
    n = len(ride["arrays"])

    def body(*refs):
        start, finish = ride["plan"](refs[:n], refs[n:2 * n], *refs[2 * n:])
        start()
        finish()

    return pl.pallas_call(
        body, name=name, out_shape=ride["out_shape"], in_specs=[_ANY] * n, out_specs=[_ANY] * n, scratch_shapes=ride["sems"],
    )(*ride["arrays"])


def _grid_call(body, name, steps, in_specs, out_specs, out_shape, scratch_shapes, args, ride=None):
    if ride is None:
        outs = pl.pallas_call(body, name=name, grid=(steps,), in_specs=in_specs, out_specs=out_specs, out_shape=out_shape,
                              scratch_shapes=scratch_shapes, compiler_params=_params())(*args)
        return list(outs), []
    n_in, n_out, n_scr, n_ride, n_sems = len(in_specs), len(out_specs), len(scratch_shapes), len(ride["arrays"]), len(ride["sems"])

    def carrying(*refs):
        ins, r_in = refs[:n_in], refs[n_in:n_in + n_ride]
        o0 = n_in + n_ride
        outs, r_out = refs[o0:o0 + n_out], refs[o0 + n_out:o0 + n_out + n_ride]
        s0 = o0 + n_out + n_ride
        scratch, sems = refs[s0:s0 + n_scr], refs[s0 + n_scr:s0 + n_scr + n_sems]
        start, finish = ride["plan"](r_in, r_out, *sems)
        pl.when(pl.program_id(0) == 0)(start)
        body(*ins, *outs, *scratch)
        pl.when(pl.program_id(0) == steps - 1)(finish)

    outs = pl.pallas_call(
        carrying, name=name, grid=(steps,), in_specs=list(in_specs) + [_ANY] * n_ride, out_specs=list(out_specs) + [_ANY] * n_ride,
        out_shape=list(out_shape) + ride["out_shape"], scratch_shapes=list(scratch_shapes) + ride["sems"],
        compiler_params=_params())(*args, *ride["arrays"])
    return list(outs[:n_out]), list(outs[n_out:])


def _row_block(rows):
    return rows if rows <= 512 else next(k for k in (512, 256, 128, rows) if rows % k == 0)


def _add_pairs(gs, core, b, name):
    k, r, n = b.shape
    tr = _row_block(r)

    def body(core_ref, a_ref, b_ref, o_ref):
        o_ref[0] = (a_ref[0, 0].astype(F32) + b_ref[0].astype(F32)).astype(o_ref.dtype)

    spec = pl.BlockSpec((1, tr, n), lambda j, i, c: (j, i, 0))
    return pl.pallas_call(
        body, name=name,
        grid_spec=pltpu.PrefetchScalarGridSpec(
            num_scalar_prefetch=1, grid=(k, r // tr),
            in_specs=[pl.BlockSpec((1, 1, tr, n), lambda j, i, c: (j, c[0], i, 0)), spec], out_specs=spec),
        out_shape=jax.ShapeDtypeStruct(b.shape, b.dtype), compiler_params=_params(("arbitrary", "arbitrary")),
    )(core.reshape(1), gs.reshape(k, 2, r, n), b)


def _adamw(w, m, v, parts, sel, name):
    _, r, n = w.shape
    tr = _row_block(r)
    nparts = len(parts)
    bc1 = 1.0 - ADAM_B1 ** ADAM_STEP
    bc2 = 1.0 - ADAM_B2 ** ADAM_STEP

    def body(sel_ref, *refs):
        w_ref, m_ref, v_ref = refs[:3]
        p_refs = refs[3:3 + nparts]
        g_ref, d_ref, nm_ref, nv_ref = refs[3 + nparts:]
        g = p_refs[0][...].astype(F32)
        for p in p_refs[1:]:
            g = g + p[...].astype(F32)
        nm = ADAM_B1 * m_ref[...] + (1.0 - ADAM_B1) * g
        nv = ADAM_B2 * v_ref[...] + (1.0 - ADAM_B2) * (g * g)
        m_hat = nm / bc1
        v_hat = nv / bc2
        g_ref[...] = g
        d_ref[...] = -ADAM_LR * (m_hat / (jnp.sqrt(v_hat) + ADAM_EPS) + ADAM_WD * w_ref[...])
        nm_ref[...] = nm
        nv_ref[...] = nv

    def part_spec(idx):
        if idx is None:
            return pl.BlockSpec((1, tr, n), lambda i, s: (s[0], i, 0))
        return pl.BlockSpec((1, tr, n), lambda i, s, idx=idx: (idx, i, 0))

    spec = pl.BlockSpec((1, tr, n), lambda i, s: (0, i, 0))
    out = jax.ShapeDtypeStruct((1, r, n), F32)
    return pl.pallas_call(
        body, name=name,
        grid_spec=pltpu.PrefetchScalarGridSpec(
            num_scalar_prefetch=1, grid=(r // tr,),
            in_specs=[spec] * 3 + [part_spec(idx) for _, idx in parts], out_specs=[spec] * 4),
        out_shape=[out] * 4, compiler_params=_params(),
    )(jnp.zeros((1,), jnp.int32) if sel is None else sel.reshape(1), w, m, v, *[p for p, _ in parts])


def _rms_parts(h, g):
    r = lax.rsqrt(jnp.mean(h * h, axis=-1, keepdims=True) + RMS_EPS)
    xhat = h * r
    return r, xhat, xhat * g


def _rms_bwd(dn, g, r, xhat):
    dxh = dn * g
    return r * (dxh - xhat * jnp.mean(dxh * xhat, axis=-1, keepdims=True))


def _loss_tile(h, tgt, g, lo, hi, loss_ref, dg_ref):
    tm, d = h.shape
    i = pl.program_id(0)

    @pl.when(i == 0)
    def _():
        loss_ref[...] = jnp.zeros_like(loss_ref)
        dg_ref[...] = jnp.zeros_like(dg_ref)

    r, xhat, y = _rms_parts(h, g)
    row = i * tm + lax.broadcasted_iota(jnp.int32, (tm, 1), 0)
    err = jnp.where((row >= lo) & (row < hi), y - tgt, 0.0)
    loss_ref[...] += jnp.full(loss_ref.shape, 0.5 * jnp.sum(jnp.mean(err * err, axis=-1, keepdims=True)), F32)
    dy = err * (1.0 / d)
    dg_ref[...] += jnp.sum(dy * xhat, axis=0, keepdims=True)
    return _rms_bwd(dy, g, r, xhat)


def _ffn_fwd(h, g, wg, wu, wd, name, ride=None, head=None):
    t, d = h.shape
    f = wg.shape[1]
    tm = TOKEN_TILE
    chunks = _chunks(f, FFN_CHUNK)

    def body(h_ref, g_ref, wg_ref, wu_ref, wd_ref, *rest):
        t_ref, gh_ref = rest[:2] if head else (None, None)
        o_ref, a_ref, b_ref = rest[2:5] if head else rest
        hv = h_ref[...]
        n = _rms_parts(hv, g_ref[...])[2].astype(BF16)
        acc = jnp.zeros((tm, d), F32)
        for s, e in chunks:
            a = _dot(n, wg_ref[:, s:e])
            b = _dot(n, wu_ref[:, s:e])
            a_ref[:, s:e] = a.astype(BF16)
            b_ref[:, s:e] = b.astype(BF16)
            acc = acc + _dot((a * jax.nn.sigmoid(a) * b).astype(BF16), wd_ref[s:e, :])
        out = hv + 0.5 * acc
        o_ref[...] = _loss_tile(out, t_ref[...], gh_ref[...], head[2], head[3], rest[5], rest[6]) if head else out

    tile = pl.BlockSpec((tm, d), lambda i: (i, 0))
    wide = pl.BlockSpec((tm, f), lambda i: (i, 0))
    in_specs, args = [tile, _VM, _VM, _VM, _VM], (h, g, wg, wu, wd)
    out_specs = [tile, wide, wide]
    out_shape = [jax.ShapeDtypeStruct((t, d), F32), jax.ShapeDtypeStruct((t, f), BF16), jax.ShapeDtypeStruct((t, f), BF16)]
    if head:
        in_specs, args = in_specs + [tile, _VM], args + (head[0], head[1])
        out_specs = out_specs + [pl.BlockSpec((SUBLANES, 128), lambda i: (0, 0)), pl.BlockSpec((1, d), lambda i: (0, 0))]
        out_shape = out_shape + [jax.ShapeDtypeStruct((SUBLANES, 128), F32), jax.ShapeDtypeStruct((1, d), F32)]
    return _grid_call(body, name, t // tm, in_specs, out_specs, out_shape, [], args, ride)


def _ffn_bwd(h, dh_out, a16, b16, g, wg, wu, wd, name, ride=None):
    t, d = h.shape
    f = wg.shape[1]
    tm = TOKEN_TILE
    chunks = _chunks(f, FFN_CHUNK)

    def body(h_ref, dho_ref, a_ref, b_ref, g_ref, wg_ref, wu_ref, wd_ref, dh_ref, dg_ref, n_ref, da_ref, db_ref, s_ref, do_ref):
        @pl.when(pl.program_id(0) == 0)
        def _():
            dg_ref[...] = jnp.zeros_like(dg_ref)

        hv = h_ref[...]
        gv = g_ref[...]
        r, xhat, n32 = _rms_parts(hv, gv)
        dho = dho_ref[...]
        do = (0.5 * dho).astype(BF16)
        dn = jnp.zeros((tm, d), F32)
        for s, e in chunks:
            a = a_ref[:, s:e].astype(F32)
            b = b_ref[:, s:e].astype(F32)
            sig = jax.nn.sigmoid(a)
            sa = a * sig
            ds = _dot_nt(do, wd_ref[s:e, :])
            da = (ds * b * (sig * (1.0 + a * (1.0 - sig)))).astype(BF16)
            db = (ds * sa).astype(BF16)
            s_ref[:, s:e] = (sa * b).astype(BF16)
            da_ref[:, s:e] = da
            db_ref[:, s:e] = db
            dn = dn + _dot_nt(da, wg_ref[:, s:e]) + _dot_nt(db, wu_ref[:, s:e])
        dh_ref[...] = dho + _rms_bwd(dn, gv, r, xhat)
        dg_ref[...] += jnp.sum(dn * xhat, axis=0, keepdims=True)
        n_ref[...] = n32.astype(BF16)
        do_ref[...] = do

    tile = pl.BlockSpec((tm, d), lambda i: (i, 0))
    wide = pl.BlockSpec((tm, f), lambda i: (i, 0))
    one = pl.BlockSpec((1, d), lambda i: (0, 0))
    return _grid_call(
        body, name, t // tm, [tile, tile, wide, wide, _VM, _VM, _VM, _VM], [tile, one, tile, wide, wide, wide, tile],
        [jax.ShapeDtypeStruct((t, d), F32), jax.ShapeDtypeStruct((1, d), F32),
         jax.ShapeDtypeStruct((t, d), BF16), jax.ShapeDtypeStruct((t, f), BF16),
         jax.ShapeDtypeStruct((t, f), BF16), jax.ShapeDtypeStruct((t, f), BF16),
         jax.ShapeDtypeStruct((t, d), BF16)],
        [], (h, dh_out, a16, b16, g, wg, wu, wd), ride)


def _dw(a, b, name, ride=None):
    t, m = a.shape
    n = b.shape[1]
    bn = next(k for k in (512, 256, n) if n % k == 0)

    def body(a_ref, b_ref, o_ref):
        o_ref[...] = _dot_tn(a_ref[...], b_ref[...])

    (out,), got = _grid_call(
        body, name, n // bn, [_VM, pl.BlockSpec((t, bn), lambda j: (0, j))], [pl.BlockSpec((m, bn), lambda j: (0, j))],
        [jax.ShapeDtypeStruct((m, n), F32)], [], (a, b), ride)
    return (out, got) if ride else out


def _to_bf16(arrays, name):
    k = len(arrays)

    def body(*refs):
        for x_ref, o_ref in zip(refs[:k], refs[k:]):
            o_ref[...] = x_ref[...].astype(BF16)

    return pl.pallas_call(
        body, name=name, out_shape=[jax.ShapeDtypeStruct(a.shape, BF16) for a in arrays],
        compiler_params=pltpu.CompilerParams(vmem_limit_bytes=VMEM_LIMIT_BYTES),
    )(*arrays)


def _s5_discretise(a_re, a_im, log_dt, b_re, b_im):
    dt = jnp.exp(log_dt)
    mag = jnp.exp(a_re * dt)
    lam_re = mag * jnp.cos(a_im * dt)
    lam_im = mag * jnp.sin(a_im * dt)
    den = a_re * a_re + a_im * a_im
    q_re = ((lam_re - 1.0) * a_re + lam_im * a_im) / den
    q_im = (lam_im * a_re - (lam_re - 1.0) * a_im) / den
    bb_re = q_re[:, None, :] * b_re - q_im[:, None, :] * b_im
    bb_im = q_re[:, None, :] * b_im + q_im[:, None, :] * b_re
    return lam_re, lam_im, bb_re, bb_im


def _s5_params_fwd(a_re, a_im, log_dt, b_re, b_im):
    g, p = a_re.shape
    c = b_re.shape[1]

    def body(are_ref, aim_ref, ldt_ref, bre_ref, bim_ref, pwr_ref, pwi_ref, bbr_ref, bbi_ref):
        lr, li, bbr, bbi = _s5_discretise(are_ref[...], aim_ref[...], ldt_ref[...], bre_ref[...], bim_ref[...])
        bbr_ref[...] = bbr
        bbi_ref[...] = bbi
        pr, pi = lr, li
        pwr_ref[0] = pr
        pwi_ref[0] = pi
        for k in range(1, SUBLANES):
            pr, pi = pr * lr - pi * li, pr * li + pi * lr
            pwr_ref[k] = pr
            pwi_ref[k] = pi

    return pl.pallas_call(
        body, name="s5_params_fwd",
        out_shape=[jax.ShapeDtypeStruct((SUBLANES, g, p), F32), jax.ShapeDtypeStruct((SUBLANES, g, p), F32),
                   jax.ShapeDtypeStruct((g, c, p), F32), jax.ShapeDtypeStruct((g, c, p), F32)],
    )(a_re, a_im, log_dt, b_re, b_im)


def _s5_params_bwd(a_re, a_im, log_dt, b_re, b_im, dlam, dbb_re, dbb_im):
    g, p = a_re.shape
    c = b_re.shape[1]

    def body(are_ref, aim_ref, ldt_ref, bre_ref, bim_ref, dlam_ref, dbr_ref, dbi_ref,
             dare_ref, daim_ref, dldt_ref, dbre_ref, dbim_ref):
        dlr = jnp.sum(dlam_ref[0], axis=0)
        dli = jnp.sum(dlam_ref[1], axis=0)
        _, vjp = jax.vjp(_s5_discretise, are_ref[...], aim_ref[...], ldt_ref[...], bre_ref[...], bim_ref[...])
        dare, daim, dldt, dbre, dbim = vjp((dlr, dli, dbr_ref[...], dbi_ref[...]))
        dare_ref[...] = dare
        daim_ref[...] = daim
        dldt_ref[...] = dldt
        dbre_ref[...] = dbre
        dbim_ref[...] = dbim

    return pl.pallas_call(
        body, name="s5_params_bwd",
        out_shape=[jax.ShapeDtypeStruct((g, p), F32), jax.ShapeDtypeStruct((g, p), F32),
                   jax.ShapeDtypeStruct((g, 1), F32), jax.ShapeDtypeStruct((g, c, p), F32),
                   jax.ShapeDtypeStruct((g, c, p), F32)],
    )(a_re, a_im, log_dt, b_re, b_im, dlam, dbb_re, dbb_im)


def _scan_chunks(gp):
    hg = gp // 2
    w = min(SCAN_LANES, hg)
    return w, [(half * hg + k * w, half * gp + k * w, half * gp + hg + k * w) for half in range(2) for k in range(hg // w)]


def _cmul_acc(xr, xi, tr, ti, sr, si):
    return xr + tr * sr - ti * si, xi + tr * si + ti * sr


def _scan_fwd(buf_ref, row0, tm, ltab_ref, cin_ref, cout_ref, gp):
    w, chunks = _scan_chunks(gp)
    for lo_t, lo_r, lo_i in chunks:
        def body(r, carry, lo_t=lo_t, lo_r=lo_r, lo_i=lo_i):
            cr, ci = carry
            row = pl.multiple_of(row0 + r * SUBLANES, SUBLANES)
            xr = buf_ref[pl.ds(row, SUBLANES), lo_r:lo_r + w]
            xi = buf_ref[pl.ds(row, SUBLANES), lo_i:lo_i + w]
            for tab, shift in ((0, 1), (2, 2), (4, 4)):
                xr, xi = _cmul_acc(xr, xi, ltab_ref[tab, :, lo_t:lo_t + w], ltab_ref[tab + 1, :, lo_t:lo_t + w],
                                   pltpu.roll(xr, shift, 0), pltpu.roll(xi, shift, 0))
            xr, xi = _cmul_acc(xr, xi, ltab_ref[6, :, lo_t:lo_t + w], ltab_ref[7, :, lo_t:lo_t + w], cr, ci)
            buf_ref[pl.ds(row, SUBLANES), lo_r:lo_r + w] = xr
            buf_ref[pl.ds(row, SUBLANES), lo_i:lo_i + w] = xi
            last = SUBLANES - 1
            return (jnp.broadcast_to(xr[last:last + 1], (SUBLANES, w)), jnp.broadcast_to(xi[last:last + 1], (SUBLANES, w)))

        cr, ci = lax.fori_loop(0, tm // SUBLANES, body,
                               (cin_ref[0:SUBLANES, lo_r:lo_r + w], cin_ref[0:SUBLANES, lo_i:lo_i + w]))
        if cout_ref is not None:
            cout_ref[0:SUBLANES, lo_r:lo_r + w] = cr
            cout_ref[0:SUBLANES, lo_i:lo_i + w] = ci


def _scan_rev(g_ref, hext_ref, tm, ltab_ref, gc_ref, dlam_ref, gp):
    w, chunks = _scan_chunks(gp)
    nb = tm // SUBLANES
    for lo_t, lo_r, lo_i in chunks:
        def body(k, carry, lo_t=lo_t, lo_r=lo_r, lo_i=lo_i):
            cr, ci, ar, ai = carry
            row = pl.multiple_of((nb - 1 - k) * SUBLANES, SUBLANES)
            xr = g_ref[pl.ds(row, SUBLANES), lo_r:lo_r + w]
            xi = g_ref[pl.ds(row, SUBLANES), lo_i:lo_i + w]
            for tab, shift in ((8, 7), (10, 6), (12, 4)):
                xr, xi = _cmul_acc(xr, xi, ltab_ref[tab, :, lo_t:lo_t + w], ltab_ref[tab + 1, :, lo_t:lo_t + w],
                                   pltpu.roll(xr, shift, 0), pltpu.roll(xi, shift, 0))
            xr, xi = _cmul_acc(xr, xi, ltab_ref[14, :, lo_t:lo_t + w], ltab_ref[15, :, lo_t:lo_t + w], cr, ci)
            g_ref[pl.ds(row, SUBLANES), lo_r:lo_r + w] = xr
            g_ref[pl.ds(row, SUBLANES), lo_i:lo_i + w] = xi
            first = lax.broadcasted_iota(jnp.int32, (SUBLANES, w), 0) == 0
            prev = pl.ds(row, SUBLANES)
            here = pl.ds(row + SUBLANES, SUBLANES)
            hpr = jnp.where(first, pltpu.roll(hext_ref[prev, lo_r:lo_r + w], 1, 0), pltpu.roll(hext_ref[here, lo_r:lo_r + w], 1, 0))
            hpi = jnp.where(first, pltpu.roll(hext_ref[prev, lo_i:lo_i + w], 1, 0), pltpu.roll(hext_ref[here, lo_i:lo_i + w], 1, 0))
            ar = ar + xr * hpr + xi * hpi
            ai = ai - xr * hpi + xi * hpr
            return (jnp.broadcast_to(xr[0:1], (SUBLANES, w)), jnp.broadcast_to(xi[0:1], (SUBLANES, w)), ar, ai)

        cr, ci, ar, ai = lax.fori_loop(
            0, nb, body, (gc_ref[:, lo_r:lo_r + w], gc_ref[:, lo_i:lo_i + w], dlam_ref[:, lo_r:lo_r + w], dlam_ref[:, lo_i:lo_i + w]))
        gc_ref[:, lo_r:lo_r + w] = cr
        gc_ref[:, lo_i:lo_i + w] = ci
        dlam_ref[:, lo_r:lo_r + w] = ar
        dlam_ref[:, lo_i:lo_i + w] = ai


def _conv_taps(cw, cext_ref, cin, tm):
    return (cw[0:1] * cext_ref[SUBLANES - 2:SUBLANES - 2 + tm, :] + cw[1:2] * cext_ref[SUBLANES - 1:SUBLANES - 1 + tm, :]
            + cw[2:3] * cin)


def _mix_fwd(h, gm, win, bg, bc, cc, dsk, wglu, cw, wco, wo, ltab, dims):
    d, ds, dc, gp = dims
    t = h.shape[0]
    tm = MIX_TILE
    nt = t // tm
    dsh = ds // 2
    o1, o2, o3 = ds + dc, ds + 2 * dc, ds + 3 * dc
    ncols = o3 + 2 * d

    def body(h_ref, gm_ref, win_ref, bg_ref, bc_ref, cc_ref, dsk_ref, wglu_ref, cw_ref, wco_ref, wo_ref, ltab_ref,
             h2_ref, st_ref, cvs_ref, p_ref, hs_ref, y5_ref, z_ref, yc_ref, hbuf_ref, carry_ref, cext_ref):
        @pl.when(pl.program_id(0) == 0)
        def _():
            carry_ref[...] = jnp.zeros_like(carry_ref)
            cext_ref[0:SUBLANES, :] = jnp.zeros((SUBLANES, dc), F32)

        st_ref[0] = carry_ref[...]
        cvs_ref[0] = cext_ref[0:SUBLANES, :]
        hv = h_ref[...]
        bg = bg_ref[...]
        u = _rms_parts(hv, gm_ref[...])[2].astype(BF16)
        us = _dot(u, win_ref[:, 0:ds])
        v = _dot(u, win_ref[:, ds:o1])
        gb = _dot(u, win_ref[:, o1:o2])
        gcv = _dot(u, win_ref[:, o2:o3])
        gs = jax.nn.sigmoid(_dot(u, win_ref[:, o3:o3 + d]) + bg[:, 0:d])
        gcg = jax.nn.sigmoid(_dot(u, win_ref[:, o3 + d:o3 + 2 * d]) + bg[:, d:2 * d])
        us16 = us.astype(BF16)
        p_ref[:, 0:ds] = us16
        p_ref[:, ds:o1] = v.astype(BF16)
        p_ref[:, o1:o2] = gb.astype(BF16)
        p_ref[:, o2:o3] = gcv.astype(BF16)
        p_ref[:, o3:o3 + d] = gs.astype(BF16)
        p_ref[:, o3 + d:ncols] = gcg.astype(BF16)
        for half in range(2):
            hbuf_ref[:, half * gp:(half + 1) * gp] = _dot(us16[:, half * dsh:(half + 1) * dsh], bc_ref[half])
        _scan_fwd(hbuf_ref, 0, tm, ltab_ref, carry_ref, carry_ref, gp)
        hs_ref[...] = hbuf_ref[...].astype(BF16)
        y5 = jnp.concatenate([_dot(hs_ref[:, half * gp:(half + 1) * gp], cc_ref[half]) for half in range(2)], axis=1) + dsk_ref[...] * us
        y5_ref[...] = y5.astype(BF16)
        z = _dot(jax.nn.gelu(y5).astype(BF16), wglu_ref[...])
        z_ref[...] = z.astype(BF16)
        ys = z[:, 0:d] * jax.nn.sigmoid(z[:, d:2 * d])
        cin = gcv * v
        cext_ref[SUBLANES:SUBLANES + tm, :] = cin
        yc = _dot((gb * _conv_taps(cw_ref[...], cext_ref, cin, tm)).astype(BF16), wco_ref[...])
        yc_ref[...] = yc.astype(BF16)
        h2_ref[...] = hv + _dot((gs * ys + gcg * yc).astype(BF16), wo_ref[...])
        cext_ref[0:SUBLANES, :] = cext_ref[tm:tm + SUBLANES, :]

    def tile(cols):
        return pl.BlockSpec((tm, cols), lambda i: (i, 0))

    def bf(cols):
        return jax.ShapeDtypeStruct((t, cols), BF16)

    return pl.pallas_call(
        body, name="mix_fwd", grid=(nt,),
        in_specs=[tile(d)] + [_VM] * 11,
        out_specs=[tile(d), pl.BlockSpec((1, SUBLANES, 2 * gp), lambda i: (i, 0, 0)), pl.BlockSpec((1, SUBLANES, dc), lambda i: (i, 0, 0)),
                   tile(ncols), tile(2 * gp), tile(ds), tile(2 * d), tile(d)],
        out_shape=[jax.ShapeDtypeStruct((t, d), F32), jax.ShapeDtypeStruct((nt, SUBLANES, 2 * gp), F32),
                   jax.ShapeDtypeStruct((nt, SUBLANES, dc), F32), bf(ncols), bf(2 * gp), bf(ds), bf(2 * d), bf(d)],
        scratch_shapes=[pltpu.VMEM((tm, 2 * gp), F32), pltpu.VMEM((SUBLANES, 2 * gp), F32), pltpu.VMEM((SUBLANES + tm, dc), F32)],
        compiler_params=_params(),
    )(h, gm, win, bg, bc, cc, dsk, wglu, cw, wco, wo, ltab)


def _mix_bwd(h, dh2, saved, gm, win, bc, cc, dsk, wglu, cw, wco, wo, ltab, dims):
    d, ds, dc, gp = dims
    t = h.shape[0]
    tm = MIX_TILE
    nt = t // tm
    dsh = ds // 2
    o1, o2, o3 = ds + dc, ds + 2 * dc, ds + 3 * dc
    ncols = o3 + 2 * d

    def body(h_ref, dh2_ref, st_ref, cvs_ref, p_ref, hs_ref, y5_ref, z_ref, yc_ref,
             gm_ref, win_ref, bc_ref, cc_ref, dsk_ref, wglu_ref, cw_ref, wco_ref, wo_ref, ltab_ref,
             dh1_ref, u_ref, dp_ref, ge_ref, dz_ref, cg_ref, dyc_ref, mx_ref, dh216_ref,
             dgm_ref, dbg_ref, ddsk_ref, dcw_ref, dlam_ref, dbc_ref, dcc_ref,
             hext_ref, gbuf_ref, gcarry_ref, cext_ref, dcvext_ref):
        @pl.when(pl.program_id(0) == 0)
        def _():
            for ref in (dgm_ref, dbg_ref, ddsk_ref, dcw_ref, dlam_ref, dbc_ref, dcc_ref, gcarry_ref):
                ref[...] = jnp.zeros_like(ref)
            dcvext_ref[tm:tm + SUBLANES, :] = jnp.zeros((SUBLANES, dc), F32)

        hext_ref[0:SUBLANES, :] = st_ref[0]
        hext_ref[SUBLANES:SUBLANES + tm, :] = hs_ref[...].astype(F32)
        cext_ref[0:SUBLANES, :] = cvs_ref[0]
        gmv = gm_ref[...]
        cw_v = cw_ref[...]
        dskv = dsk_ref[...]
        r, xhat, n32 = _rms_parts(h_ref[...], gmv)
        us = p_ref[:, 0:ds].astype(F32)
        v = p_ref[:, ds:o1].astype(F32)
        gb = p_ref[:, o1:o2].astype(F32)
        gcv = p_ref[:, o2:o3].astype(F32)
        gs = p_ref[:, o3:o3 + d].astype(F32)
        gcg = p_ref[:, o3 + d:ncols].astype(F32)
        z1 = z_ref[:, 0:d].astype(F32)
        sz = jax.nn.sigmoid(z_ref[:, d:2 * d].astype(F32))
        ys = z1 * sz
        yc = yc_ref[...].astype(F32)
        y5 = y5_ref[...].astype(F32)
        ge, gelu_vjp = jax.vjp(jax.nn.gelu, y5)
        cin = gcv * v
        cext_ref[SUBLANES:SUBLANES + tm, :] = cin
        cv = _conv_taps(cw_v, cext_ref, cin, tm)

        dh2v = dh2_ref[...]
        dh216 = dh2v.astype(BF16)
        dmixed = _dot_nt(dh216, wo_ref[...])
        dys = dmixed * gs
        dyc16 = (dmixed * gcg).astype(BF16)
        dpgs = dmixed * ys * gs * (1.0 - gs)
        dpgc = dmixed * yc * gcg * (1.0 - gcg)
        dz16 = jnp.concatenate([dys * sz, dys * z1 * sz * (1.0 - sz)], axis=1).astype(BF16)
        dy5 = gelu_vjp(_dot_nt(dz16, wglu_ref[...]))[0]
        dy516 = dy5.astype(BF16)
        for half in range(2):
            gbuf_ref[:, half * gp:(half + 1) * gp] = _dot_nt(dy516[:, half * dsh:(half + 1) * dsh], cc_ref[half])
        _scan_rev(gbuf_ref, hext_ref, tm, ltab_ref, gcarry_ref, dlam_ref, gp)
        dus = []
        for half in range(2):
            g16 = gbuf_ref[:, half * gp:(half + 1) * gp].astype(BF16)
            dus.append(_dot_nt(g16, bc_ref[half]))
            dbc_ref[half] += _dot_tn(p_ref[:, half * dsh:(half + 1) * dsh], g16)
            dcc_ref[half] += _dot_tn(hs_ref[:, half * gp:(half + 1) * gp], dy516[:, half * dsh:(half + 1) * dsh])
        dus = jnp.concatenate(dus, axis=1) + dskv * dy5
        ddsk_ref[...] += jnp.sum(dy5 * us, axis=0, keepdims=True)
        dcg = _dot_nt(dyc16, wco_ref[...])
        dgb = dcg * cv
        dcv = dcg * gb
        dcvext_ref[0:tm, :] = dcv
        dcin = cw_v[2:3] * dcv + cw_v[1:2] * dcvext_ref[1:1 + tm, :] + cw_v[0:1] * dcvext_ref[2:2 + tm, :]
        dcw_ref[0:1, :] += jnp.sum(dcv * cext_ref[SUBLANES - 2:SUBLANES - 2 + tm, :], axis=0, keepdims=True)
        dcw_ref[1:2, :] += jnp.sum(dcv * cext_ref[SUBLANES - 1:SUBLANES - 1 + tm, :], axis=0, keepdims=True)
        dcw_ref[2:3, :] += jnp.sum(dcv * cin, axis=0, keepdims=True)
        dcvext_ref[tm:tm + SUBLANES, :] = dcvext_ref[0:SUBLANES, :]
        dp16 = jnp.concatenate([dus, dcin * gcv, dgb, dcin * v, dpgs, dpgc], axis=1).astype(BF16)
        du = _dot_nt(dp16, win_ref[...])
        dh1_ref[...] = dh2v + _rms_bwd(du, gmv, r, xhat)
        dgm_ref[...] += jnp.sum(du * xhat, axis=0, keepdims=True)
        dbg_ref[...] += jnp.concatenate([jnp.sum(dpgs, axis=0, keepdims=True), jnp.sum(dpgc, axis=0, keepdims=True)], axis=1)
        u_ref[...] = n32.astype(BF16)
        dp_ref[...] = dp16
        ge_ref[...] = ge.astype(BF16)
        dz_ref[...] = dz16
        cg_ref[...] = (gb * cv).astype(BF16)
        dyc_ref[...] = dyc16
        mx_ref[...] = (gs * ys + gcg * yc).astype(BF16)
        dh216_ref[...] = dh216

    def rev(cols):
        return pl.BlockSpec((tm, cols), lambda j: (nt - 1 - j, 0))

    def rev3(cols):
        return pl.BlockSpec((1, SUBLANES, cols), lambda j: (nt - 1 - j, 0, 0))

    def bf(cols):
        return jax.ShapeDtypeStruct((t, cols), BF16)

    st, cvs, p16, hs16, y516, z16, yc16 = saved
    return pl.pallas_call(
        body, name="mix_bwd", grid=(nt,),
        in_specs=[rev(d), rev(d), rev3(2 * gp), rev3(dc), rev(ncols), rev(2 * gp), rev(ds), rev(2 * d), rev(d)] + [_VM] * 10,
        out_specs=[rev(d), rev(d), rev(ncols), rev(ds), rev(2 * d), rev(dc), rev(d), rev(d), rev(d)] + [_VM] * 7,
        out_shape=[jax.ShapeDtypeStruct((t, d), F32), bf(d), bf(ncols), bf(ds), bf(2 * d), bf(dc), bf(d), bf(d), bf(d),
                   jax.ShapeDtypeStruct((1, d), F32), jax.ShapeDtypeStruct((1, 2 * d), F32), jax.ShapeDtypeStruct((1, ds), F32),
                   jax.ShapeDtypeStruct((SUBLANES, dc), F32), jax.ShapeDtypeStruct((SUBLANES, 2 * gp), F32),
                   jax.ShapeDtypeStruct((2, dsh, gp), F32), jax.ShapeDtypeStruct((2, gp, dsh), F32)],
        scratch_shapes=[pltpu.VMEM((SUBLANES + tm, 2 * gp), F32), pltpu.VMEM((tm, 2 * gp), F32),
                        pltpu.VMEM((SUBLANES, 2 * gp), F32), pltpu.VMEM((SUBLANES + tm, dc), F32),
                        pltpu.VMEM((tm + SUBLANES, dc), F32)],
        compiler_params=_params(),
    )(h, dh2, st, cvs, p16, hs16, y516, z16, yc16, gm, win, bc, cc, dsk, wglu, cw, wco, wo, ltab)


def _pad_rows(a, rows, axis=0):
    pad = [(0, 0)] * a.ndim
    pad[axis] = (0, rows - a.shape[axis])
    return jnp.pad(a, pad)


def _as_rows(a):
    flat = a.reshape(-1)
    n = -(-flat.shape[0] // SLAB_COLS) * SLAB_COLS
    return jnp.pad(flat, (0, n - flat.shape[0])).reshape(-1, SLAB_COLS)


def _pack(arrs):
    rows = jnp.concatenate([_as_rows(a) for a in arrs], axis=0)
    return _pad_rows(rows, -(-rows.shape[0] // 16) * 16)


def _unpack(slab, shapes):
    out, r = [], 0
    for shp in shapes:
        size = 1
        for s in shp:
            size *= s
        n = -(-size // SLAB_COLS)
        out.append(slab[r:r + n].reshape(-1)[:size].reshape(shp))
        r += n
    return out


def _block_diag(blocks):
    n, a, b = blocks.shape
    eye = jnp.eye(n, dtype=blocks.dtype)
    return (blocks[:, :, None, :] * eye[:, None, :, None]).reshape(n * a, n * b)


def _diag_blocks(mat, n):
    a, b = mat.shape[0] // n, mat.shape[1] // n
    eye = jnp.eye(n, dtype=mat.dtype)
    return jnp.sum(mat.reshape(n, a, n, b) * eye[:, None, :, None], axis=2)


BIG = (("ffn1_w_gate", "col"), ("ffn1_w_up", "col"), ("ffn1_w_down", "row"), ("w_in", "col"), ("ssm_w_glu", "col"),
       ("conv_w_out", "col"), ("w_o", "row"), ("ffn2_w_gate", "col"), ("ffn2_w_up", "col"), ("ffn2_w_down", "row"))
REPLICATED = ("g_ffn1", "g_mix", "b_gate", "ssm_a_re", "ssm_a_im", "ssm_log_dt", "ssm_b_re", "ssm_b_im", "ssm_c_re",
              "ssm_c_im", "ssm_d", "g_ffn2", "g_final")
WEIGHTS = ("meta_tokens", "g_ffn1", "ffn1_w_gate", "ffn1_w_up", "ffn1_w_down", "g_mix", "w_in", "b_gate", "ssm_a_re",
           "ssm_a_im", "ssm_log_dt", "ssm_b_re", "ssm_b_im", "ssm_c_re", "ssm_c_im", "ssm_d", "ssm_w_glu", "conv_w",
           "conv_w_out", "w_o", "g_ffn2", "ffn2_w_gate", "ffn2_w_up", "ffn2_w_down", "g_final")
N_EARLY = 3


def _full_from_blocks(blocks, kind):
    n, r, c = blocks.shape
    if kind == "col":
        return jnp.transpose(blocks, (1, 0, 2)).reshape(r, n * c)
    return blocks.reshape(n * r, c)


def _blocks_from_full(full, kind):
    if kind == "col":
        r, nc = full.shape
        return jnp.transpose(full.reshape(r, NDEV, nc // NDEV), (1, 0, 2))
    nr, c = full.shape
    return full.reshape(NDEV, nr // NDEV, c)


def kernel(x, meta_tokens, g_ffn1, ffn1_w_gate, ffn1_w_up, ffn1_w_down, g_mix, w_in, b_gate, ssm_a_re, ssm_a_im, ssm_log_dt, ssm_b_re, ssm_b_im, ssm_c_re, ssm_c_im, ssm_d, ssm_w_glu, conv_w, conv_w_out, w_o, g_ffn2, ffn2_w_gate, ffn2_w_up, ffn2_w_down, g_final, loss_target, m_meta_tokens, m_g_ffn1, m_ffn1_w_gate, m_ffn1_w_up, m_ffn1_w_down, m_g_mix, m_w_in, m_b_gate, m_ssm_a_re, m_ssm_a_im, m_ssm_log_dt, m_ssm_b_re, m_ssm_b_im, m_ssm_c_re, m_ssm_c_im, m_ssm_d, m_ssm_w_glu, m_conv_w, m_conv_w_out, m_w_o, m_g_ffn2, m_ffn2_w_gate, m_ffn2_w_up, m_ffn2_w_down, m_g_final, v_meta_tokens, v_g_ffn1, v_ffn1_w_gate, v_ffn1_w_up, v_ffn1_w_down, v_g_mix, v_w_in, v_b_gate, v_ssm_a_re, v_ssm_a_im, v_ssm_log_dt, v_ssm_b_re, v_ssm_b_im, v_ssm_c_re, v_ssm_c_im, v_ssm_d, v_ssm_w_glu, v_conv_w, v_conv_w_out, v_w_o, v_g_ffn2, v_ffn2_w_gate, v_ffn2_w_up, v_ffn2_w_down, v_g_final):
    args = dict(locals())
    w = {n: args[n] for n in WEIGHTS}
    mom_m = {n: args["m_" + n] for n in WEIGHTS}
    mom_v = {n: args["v_" + n] for n in WEIGHTS}

    seq, d = x.shape[1], x.shape[2]
    n_meta = meta_tokens.shape[0]
    ds = ssm_d.shape[1]
    n_grp, n_state = ssm_a_re.shape[1], ssm_a_re.shape[2]
    gp = n_grp * n_state
    dc = conv_w.shape[3] * NDEV
    dims = (d, ds, dc, gp)
    t_real = n_meta + seq
    t_pad = -(-t_real // TOKEN_TILE) * TOKEN_TILE
    me_chip = 2 * lax.axis_index("x") + lax.axis_index("y")
    me_core = lax.axis_index("c")
    me = 2 * me_chip + me_core
    mcols, ccols = d // NDEV, dc // NDEV

    cw_shard = _pad_rows(_pad_rows(conv_w.reshape(3, ccols), SUBLANES), 128, axis=1)
    shard16 = dict(zip([name for name, _ in BIG], _to_bf16([w[name][0] for name, _ in BIG], "weights_to_bf16")))
    early, late = BIG[:N_EARLY], BIG[N_EARLY:]
    got = _exchange(_gather_ride([shard16[name] for name, _ in early] + [meta_tokens, cw_shard]), "gather_first")
    full = {name: _full_from_blocks(got[i], kind) for i, (name, kind) in enumerate(early)}
    meta_full = _full_from_blocks(got[-2], "col")
    cw_rows = _pad_rows(_full_from_blocks(got[-1][:, 0:3, 0:ccols], "col"), SUBLANES)

    a_re, a_im, ldt = ssm_a_re[0], ssm_a_im[0], ssm_log_dt[0].reshape(n_grp, 1)
    b_re_t = jnp.transpose(ssm_b_re[0], (0, 2, 1))
    b_im_t = jnp.transpose(ssm_b_im[0], (0, 2, 1))
    pw_r, pw_i, bb_r, bb_i = _s5_params_fwd(a_re, a_im, ldt, b_re_t, b_im_t)
    pw_r = pw_r.reshape(SUBLANES, gp)
    pw_i = pw_i.reshape(SUBLANES, gp)
    sub = jnp.arange(SUBLANES)[:, None]

    def fwd_tab(p, k):
        return jnp.where(sub >= k, p[k - 1][None, :], 0.0)

    def rev_tab(p, k):
        return jnp.where(sub <= SUBLANES - 1 - k, p[k - 1][None, :], 0.0)

    ltab = jnp.stack(
        [fwd_tab(pw_r, 1), fwd_tab(pw_i, 1), fwd_tab(pw_r, 2), fwd_tab(pw_i, 2), fwd_tab(pw_r, 4), fwd_tab(pw_i, 4), pw_r, pw_i,
         rev_tab(pw_r, 1), -rev_tab(pw_i, 1), rev_tab(pw_r, 2), -rev_tab(pw_i, 2), rev_tab(pw_r, 4), -rev_tab(pw_i, 4),
         pw_r[::-1], -pw_i[::-1]], axis=0)
    gh = n_grp // 2
    bc = jnp.stack([jnp.concatenate([_block_diag(bb_r[h * gh:(h + 1) * gh]), _block_diag(bb_i[h * gh:(h + 1) * gh])], axis=1)
                    for h in range(2)]).astype(BF16)
    c_re_t = jnp.transpose(ssm_c_re[0], (0, 2, 1))
    c_im_t = jnp.transpose(ssm_c_im[0], (0, 2, 1))
    cc = jnp.stack([jnp.concatenate([_block_diag(c_re_t[h * gh:(h + 1) * gh]), -_block_diag(c_im_t[h * gh:(h + 1) * gh])], axis=0)
                    for h in range(2)]).astype(BF16)

    zpad = jnp.zeros((t_pad - t_real, d), F32)
    h0 = jnp.concatenate([meta_full, x[0], zpad], axis=0)
    tgt = jnp.concatenate([jnp.zeros((n_meta, d), F32), loss_target[0], zpad], axis=0)
    (h1, a1, b1), got = _ffn_fwd(h0, g_ffn1, full["ffn1_w_gate"], full["ffn1_w_up"], full["ffn1_w_down"], "ffn1_fwd",
                                 ride=_gather_ride([shard16[name] for name, _ in late]))
    full.update({name: _full_from_blocks(got[i], kind) for i, (name, kind) in enumerate(late)})
    h2, *saved = _mix_fwd(h1, g_mix, full["w_in"], b_gate, bc, cc, ssm_d, full["ssm_w_glu"], cw_rows, full["conv_w_out"],
                          full["w_o"], ltab, dims)
    (dh3, a2, b2, loss_blk, dg_final), _ = _ffn_fwd(h2, g_ffn2, full["ffn2_w_gate"], full["ffn2_w_up"], full["ffn2_w_down"], "ffn2_fwd",
                                                 head=(tgt, g_final.reshape(1, d), n_meta, t_real))
    loss = lax.psum(loss_blk[0, 0], AXES)

    (dh2, dg_ffn2, n2, da2, db2, s2, do2), _ = _ffn_bwd(
        h2, dh3, a2, b2, g_ffn2, full["ffn2_w_gate"], full["ffn2_w_up"], full["ffn2_w_down"], "ffn2_bwd")
    (dh1, u16, dp16, ge16, dz16, cg16, dyc16, mx16, dh216, dg_mix, dbg, ddsk, dcw, dlam, dbc, dcc) = _mix_bwd(
        h1, dh2, saved, g_mix, full["w_in"], bc, cc, ssm_d, full["ssm_w_glu"], cw_rows, full["conv_w_out"], full["w_o"], ltab, dims)
    dblocks = {
        "w_in": _blocks_from_full(_dw(u16, dp16, "dw_in"), "col"),
        "ssm_w_glu": _blocks_from_full(_dw(ge16, dz16, "dw_glu"), "col"),
        "conv_w_out": _blocks_from_full(_dw(cg16, dyc16, "dw_conv_out"), "col"),
        "w_o": _blocks_from_full(_dw(mx16, dh216, "dw_o"), "row"),
        "ffn2_w_gate": _blocks_from_full(_dw(n2, da2, "dw_ffn2_gate"), "col"),
        "ffn2_w_up": _blocks_from_full(_dw(n2, db2, "dw_ffn2_up"), "col"),
        "ffn2_w_down": jnp.transpose(_blocks_from_full(_dw(do2, s2, "dw_ffn2_down"), "col"), (0, 2, 1)),
    }

    def pair_sums(names, tag):
        gs = [dblocks[name].astype(BF16) for name in names]
        from_sibling = _exchange(_pair_ride(gs), "reduce_pair_" + tag)
        return [_add_pairs(g, me_core, b, "reduce_pair_add_" + name) for g, b, name in zip(gs, from_sibling, names)]

    late_names = [name for name, _ in late]
    pairs = dict(zip(late_names, pair_sums(late_names, "late")))
    (dh0, dg_ffn1, n1, da1, db1, s1, do1), got = _ffn_bwd(
        h0, dh1, a1, b1, g_ffn1, full["ffn1_w_gate"], full["ffn1_w_up"], full["ffn1_w_down"], "ffn1_bwd",
        ride=_chips_ride([pairs[name] for name in late_names]))
    from_chips = dict(zip(late_names, got))
    dlam4 = dlam.reshape(SUBLANES, 2, 2, gh, n_state)
    dlam_in = jnp.transpose(dlam4, (2, 0, 1, 3, 4)).reshape(2, SUBLANES, n_grp, n_state)
    hg = gp // 2
    dbb_r = jnp.concatenate([_diag_blocks(dbc[h][:, :hg], gh) for h in range(2)], axis=0)
    dbb_i = jnp.concatenate([_diag_blocks(dbc[h][:, hg:], gh) for h in range(2)], axis=0)
    da_re, da_im, dldt, dbre_t, dbim_t = _s5_params_bwd(a_re, a_im, ldt, b_re_t, b_im_t, dlam_in, dbb_r, dbb_i)
    dc_re = jnp.concatenate([_diag_blocks(dcc[h][:hg], gh) for h in range(2)], axis=0)
    dc_im = -jnp.concatenate([_diag_blocks(dcc[h][hg:], gh) for h in range(2)], axis=0)

    grads_rep = {
        "g_ffn1": dg_ffn1, "g_mix": dg_mix, "b_gate": dbg, "ssm_a_re": da_re[None], "ssm_a_im": da_im[None],
        "ssm_log_dt": dldt.reshape(1, n_grp), "ssm_b_re": jnp.transpose(dbre_t, (0, 2, 1))[None],
        "ssm_b_im": jnp.transpose(dbim_t, (0, 2, 1))[None], "ssm_c_re": jnp.transpose(dc_re, (0, 2, 1))[None],
        "ssm_c_im": jnp.transpose(dc_im, (0, 2, 1))[None], "ssm_d": ddsk, "g_ffn2": dg_ffn2, "g_final": dg_final.reshape(d),
    }

    rep_shapes = [w[n].shape for n in REPLICATED]
    small_g_shapes = rep_shapes + [(n_meta, d), (3, dc)]
    gsmall = _pack([grads_rep[n] for n in REPLICATED] + [dh0[0:n_meta], dcw[0:3]])
    dw_gate, (gall,) = _dw(n1, da1, "dw_ffn1_gate", ride=_gather_ride([gsmall]))
    dblocks.update({
        "ffn1_w_gate": _blocks_from_full(dw_gate, "col"),
        "ffn1_w_up": _blocks_from_full(_dw(n1, db1, "dw_ffn1_up"), "col"),
        "ffn1_w_down": jnp.transpose(_blocks_from_full(_dw(do1, s1, "dw_ffn1_down"), "col"), (0, 2, 1)),
    })
    early_names = [name for name, _ in early]
    pairs.update(zip(early_names, pair_sums(early_names, "early")))
    from_chips.update(zip(early_names, _exchange(_chips_ride([pairs[name] for name in early_names]), "reduce_chips_early")))

    out_g, out_d, out_m, out_v = {}, {}, {}, {}
    for name, _ in BIG:
        fc = from_chips[name]
        out_g[name], out_d[name], out_m[name], out_v[name] = _adamw(
            w[name], mom_m[name], mom_v[name], [(pairs[name], None), (fc, 0), (fc, 1), (fc, 2)], me_chip, "adamw_" + name)

    zer = [jnp.zeros((n_meta, d), F32), jnp.zeros((3, dc), F32)]
    gr, dr, mr, vr = [o[0] for o in _adamw(
        _pack([w[n] for n in REPLICATED] + zer)[None], _pack([mom_m[n] for n in REPLICATED] + zer)[None],
        _pack([mom_v[n] for n in REPLICATED] + zer)[None], [(gall, b) for b in range(NDEV)], None, "adamw_replicated")]
    g_list = _unpack(gr, small_g_shapes)
    out_g.update(zip(REPLICATED, g_list[:len(REPLICATED)]))
    out_d.update(zip(REPLICATED, _unpack(dr, rep_shapes)))
    out_m.update(zip(REPLICATED, _unpack(mr, rep_shapes)))
    out_v.update(zip(REPLICATED, _unpack(vr, rep_shapes)))

    g_meta = lax.dynamic_slice_in_dim(g_list[-2], me * mcols, mcols, axis=1)
    g_cw = lax.dynamic_slice_in_dim(g_list[-1], me * ccols, ccols, axis=1).reshape(conv_w.shape)
    tiny = ("meta_tokens", "conv_w")
    tiny_shapes = [meta_tokens.shape, conv_w.shape]
    gt, dt_, mt, vt = [o[0] for o in _adamw(
        _pack([w[n] for n in tiny])[None], _pack([mom_m[n] for n in tiny])[None], _pack([mom_v[n] for n in tiny])[None],
        [(_pack([g_meta, g_cw])[None], 0)], None, "adamw_tiny")]
    out_g.update(zip(tiny, _unpack(gt, tiny_shapes)))
    out_d.update(zip(tiny, _unpack(dt_, tiny_shapes)))
    out_m.update(zip(tiny, _unpack(mt, tiny_shapes)))
    out_v.update(zip(tiny, _unpack(vt, tiny_shapes)))

    grad_x = dh0[n_meta:t_real][None]
    return (loss, grad_x, *[out_g[n] for n in WEIGHTS], *[out_d[n] for n in WEIGHTS],
            *[out_m[n] for n in WEIGHTS], *[out_v[n] for n in WEIGHTS])
```

```python
import functools

import jax
import jax.numpy as jnp
from jax import lax
from jax.experimental import pallas as pl
from jax.experimental.pallas import tpu as pltpu

F32 = jnp.float32
BF16 = jnp.bfloat16
MESH = pl.DeviceIdType.MESH
AXES = ("x", "y", "c")
NDEV = 8
SLAB_COLS = 1024
RMS_EPS = 1e-6
TOKEN_TILE = 320
MIX_TILE = 128
ROW_ALIGN = 128
SUBLANES = 8
SCAN_LANES = 512
FFN_CHUNK = 1024
VMEM_LIMIT_BYTES = 56 * 1024 * 1024

ADAM_LR = 0.001
ADAM_B1 = 0.9
ADAM_B2 = 0.999
ADAM_EPS = 1e-08
ADAM_WD = 0.01
ADAM_STEP = 10

_VM = pl.BlockSpec(memory_space=pltpu.VMEM)
_ANY = pl.BlockSpec(memory_space=pl.ANY)


def _params(sem=("arbitrary",)):
    return pltpu.CompilerParams(dimension_semantics=sem, vmem_limit_bytes=VMEM_LIMIT_BYTES)


def _dot(a, b):
    return jnp.dot(a, b, preferred_element_type=F32)


def _dot_nt(a, b):
    return lax.dot_general(a, b, (((1,), (1,)), ((), ())), preferred_element_type=F32)


def _dot_tn(a, b):
    return lax.dot_general(a, b, (((0,), (0,)), ((), ())), preferred_element_type=F32)


def _tile(rows, most):
    return next(k for k in range(most - most % 16, 0, -16) if rows % k == 0)


def _chunks(n, step):
    return [(s, min(s + step, n)) for s in range(0, n, step)]


def _gather_plan(x_refs, out_refs, send_sems, recv_sems, local_sems):
    n = len(x_refs)
    x, y, c = lax.axis_index("x"), lax.axis_index("y"), lax.axis_index("c")
    me, sibling = (x, y, c), (x, y, 1 - c)
    chips = [(1 - x, y), (x, 1 - y), (1 - x, 1 - y)]

    def copy(i, k, block, to, src=None):
        slot = out_refs[i].at[4 * block[0] + 2 * block[1] + block[2]]
        return pltpu.make_async_remote_copy(
            src_ref=slot if src is None else src, dst_ref=slot,
            send_sem=send_sems.at[7 * i + k], recv_sem=recv_sems.at[7 * i + k], device_id=to, device_id_type=MESH)

    def mine():
        return [pltpu.make_async_copy(x_refs[i], out_refs[i].at[4 * x + 2 * y + c], local_sems.at[i]) for i in range(n)]

    def first():
        out = []
        for i in range(n):
            out.append(copy(i, 0, me, sibling, src=x_refs[i]))
            out += [copy(i, 1 + j, me, (*chip, c), src=x_refs[i]) for j, chip in enumerate(chips)]
        return out

    def start():
        for cp in mine() + first():
            cp.start()

    def finish():
        passed = []
        for j, chip in enumerate(chips):
            for i in range(n):
                copy(i, 1 + j, (*chip, c), me).wait_recv()
                cp = copy(i, 4 + j, (*chip, c), sibling)
                cp.start()
                passed.append(cp)
        for i in range(n):
            copy(i, 0, sibling, me).wait_recv()
            for j, chip in enumerate(chips):
                copy(i, 4 + j, (*chip, 1 - c), me).wait_recv()
        for cp in first() + passed:
            cp.wait_send()
        for cp in mine():
            cp.wait()

    return start, finish


def _pair_plan(g_refs, out_refs, send_sems, recv_sems):
    x, y, c = lax.axis_index("x"), lax.axis_index("y"), lax.axis_index("c")

    def copies():
        return [pltpu.make_async_remote_copy(
            src_ref=g_refs[i].at[2 * j + (1 - c)], dst_ref=out_refs[i].at[j],
            send_sem=send_sems.at[4 * i + j], recv_sem=recv_sems.at[4 * i + j],
            device_id=(x, y, 1 - c), device_id_type=MESH) for i in range(len(g_refs)) for j in range(4)]

    def start():
        for cp in copies():
            cp.start()

    def finish():
        for cp in copies():
            cp.wait()

    return start, finish


def _chips_plan(p_refs, out_refs, send_sems, recv_sems):
    x, y, c = lax.axis_index("x"), lax.axis_index("y"), lax.axis_index("c")

    def copies():
        return [pltpu.make_async_remote_copy(
            src_ref=p_refs[i].at[2 * px + py], dst_ref=out_refs[i].at[k],
            send_sem=send_sems.at[3 * i + k], recv_sem=recv_sems.at[3 * i + k],
            device_id=(px, py, c), device_id_type=MESH)
            for i in range(len(p_refs)) for k, (px, py) in enumerate([(1 - x, y), (x, 1 - y), (1 - x, 1 - y)])]

    def start():
        for cp in copies():
            cp.start()

    def finish():
        for cp in copies():
            cp.wait()

    return start, finish


def _gather_ride(shards):
    n = len(shards)
    return dict(plan=_gather_plan, arrays=list(shards),
                out_shape=[jax.ShapeDtypeStruct((NDEV, *s.shape), s.dtype) for s in shards],
                sems=[pltpu.SemaphoreType.DMA((7 * n,)), pltpu.SemaphoreType.DMA((7 * n,)), pltpu.SemaphoreType.DMA((n,))])


def _pair_ride(blocks):
    n = len(blocks)
    return dict(plan=_pair_plan, arrays=list(blocks),
                out_shape=[jax.ShapeDtypeStruct((4, *b.shape[1:]), b.dtype) for b in blocks],
                sems=[pltpu.SemaphoreType.DMA((4 * n,)), pltpu.SemaphoreType.DMA((4 * n,))])


def _chips_ride(partials):
    n = len(partials)
    return dict(plan=_chips_plan, arrays=list(partials),
                out_shape=[jax.ShapeDtypeStruct((3, *p.shape[1:]), p.dtype) for p in partials],
                sems=[pltpu.SemaphoreType.DMA((3 * n,)), pltpu.SemaphoreType.DMA((3 * n,))])


def _exchange(ride, name):
    n = len(ride["arrays"])

    def body(*refs):
        start, finish = ride["plan"](refs[:n], refs[n:2 * n], *refs[2 * n:])
        start()
        finish()

    return pl.pallas_call(
        body, name=name, out_shape=ride["out_shape"], in_specs=[_ANY] * n, out_specs=[_ANY] * n, scratch_shapes=ride["sems"],
    )(*ride["arrays"])


def _grid_call(body, name, steps, in_specs, out_specs, out_shape, scratch_shapes, args, ride=None):
    if ride is None:
        outs = pl.pallas_call(body, name=name, grid=(steps,), in_specs=in_specs, out_specs=out_specs, out_shape=out_shape,
                              scratch_shapes=scratch_shapes, compiler_params=_params())(*args)
        return list(outs), []
    n_in, n_out, n_scr, n_ride, n_sems = len(in_specs), len(out_specs), len(scratch_shapes), len(ride["arrays"]), len(ride["sems"])

    def carrying(*refs):
        ins, r_in = refs[:n_in], refs[n_in:n_in + n_ride]
        o0 = n_in + n_ride
        outs, r_out = refs[o0:o0 + n_out], refs[o0 + n_out:o0 + n_out + n_ride]
        s0 = o0 + n_out + n_ride
        scratch, sems = refs[s0:s0 + n_scr], refs[s0 + n_scr:s0 + n_scr + n_sems]
        start, finish = ride["plan"](r_in, r_out, *sems)
        pl.when(pl.program_id(0) == 0)(start)
        body(*ins, *outs, *scratch)
        pl.when(pl.program_id(0) == steps - 1)(finish)

    outs = pl.pallas_call(
        carrying, name=name, grid=(steps,), in_specs=list(in_specs) + [_ANY] * n_ride, out_specs=list(out_specs) + [_ANY] * n_ride,
        out_shape=list(out_shape) + ride["out_shape"], scratch_shapes=list(scratch_shapes) + ride["sems"],
        compiler_params=_params())(*args, *ride["arrays"])
    return list(outs[:n_out]), list(outs[n_out:])


def _row_block(rows):
    return rows if rows <= 512 else next(k for k in (512, 256, 128, rows) if rows % k == 0)


def _add_pairs(gs, core, b, name):
    k, r, n = b.shape
    tr = _row_block(r)

    def body(core_ref, a_ref, b_ref, o_ref):
        o_ref[0] = (a_ref[0, 0].astype(F32) + b_ref[0].astype(F32)).astype(o_ref.dtype)

    spec = pl.BlockSpec((1, tr, n), lambda j, i, c: (j, i, 0))
    return pl.pallas_call(
        body, name=name,
        grid_spec=pltpu.PrefetchScalarGridSpec(
            num_scalar_prefetch=1, grid=(k, r // tr),
            in_specs=[pl.BlockSpec((1, 1, tr, n), lambda j, i, c: (j, c[0], i, 0)), spec], out_specs=spec),
        out_shape=jax.ShapeDtypeStruct(b.shape, b.dtype), compiler_params=_params(("arbitrary", "arbitrary")),
    )(core.reshape(1), gs.reshape(k, 2, r, n), b)


def _adamw(w, m, v, parts, sel, name):
    _, r, n = w.shape
    tr = _row_block(r)
    nparts = len(parts)
    bc1 = 1.0 - ADAM_B1 ** ADAM_STEP
    bc2 = 1.0 - ADAM_B2 ** ADAM_STEP

    def body(sel_ref, *refs):
        w_ref, m_ref, v_ref = refs[:3]
        p_refs = refs[3:3 + nparts]
        g_ref, d_ref, nm_ref, nv_ref = refs[3 + nparts:]
        g = p_refs[0][...].astype(F32)
        for p in p_refs[1:]:
            g = g + p[...].astype(F32)
        nm = ADAM_B1 * m_ref[...] + (1.0 - ADAM_B1) * g
        nv = ADAM_B2 * v_ref[...] + (1.0 - ADAM_B2) * (g * g)
        m_hat = nm / bc1
        v_hat = nv / bc2
        g_ref[...] = g
        d_ref[...] = -ADAM_LR * (m_hat / (jnp.sqrt(v_hat) + ADAM_EPS) + ADAM_WD * w_ref[...])
        nm_ref[...] = nm
        nv_ref[...] = nv

    def part_spec(idx):
        if idx is None:
            return pl.BlockSpec((1, tr, n), lambda i, s: (s[0], i, 0))
        return pl.BlockSpec((1, tr, n), lambda i, s, idx=idx: (idx, i, 0))

    spec = pl.BlockSpec((1, tr, n), lambda i, s: (0, i, 0))
    out = jax.ShapeDtypeStruct((1, r, n), F32)
    return pl.pallas_call(
        body, name=name,
        grid_spec=pltpu.PrefetchScalarGridSpec(
            num_scalar_prefetch=1, grid=(r // tr,),
            in_specs=[spec] * 3 + [part_spec(idx) for _, idx in parts], out_specs=[spec] * 4),
        out_shape=[out] * 4, compiler_params=_params(),
    )(jnp.zeros((1,), jnp.int32) if sel is None else sel.reshape(1), w, m, v, *[p for p, _ in parts])


def _rms_parts(h, g):
    r = lax.rsqrt(jnp.mean(h * h, axis=-1, keepdims=True) + RMS_EPS)
    xhat = h * r
    return r, xhat, xhat * g


def _rms_bwd(dn, g, r, xhat):
    dxh = dn * g
    return r * (dxh - xhat * jnp.mean(dxh * xhat, axis=-1, keepdims=True))


def _loss_tile(h, tgt, g, lo, hi, loss_ref, dg_ref):
    tm, d = h.shape
    i = pl.program_id(0)

    @pl.when(i == 0)
    def _():
        loss_ref[...] = jnp.zeros_like(loss_ref)
        dg_ref[...] = jnp.zeros_like(dg_ref)

    r, xhat, y = _rms_parts(h, g)
    row = i * tm + lax.broadcasted_iota(jnp.int32, (tm, 1), 0)
    err = jnp.where((row >= lo) & (row < hi), y - tgt, 0.0)
    loss_ref[...] += jnp.full(loss_ref.shape, 0.5 * jnp.sum(jnp.mean(err * err, axis=-1, keepdims=True)), F32)
    dy = err * (1.0 / d)
    dg_ref[...] += jnp.sum(dy * xhat, axis=0, keepdims=True)
    return _rms_bwd(dy, g, r, xhat)


def _ffn_fwd(h, g, wg, wu, wd, name, ride=None, head=None):
    t, d = h.shape
    f = wg.shape[1]
    tm = _tile(t, TOKEN_TILE)
    chunks = _chunks(f, FFN_CHUNK)

    def body(h_ref, g_ref, wg_ref, wu_ref, wd_ref, *rest):
        t_ref, gh_ref = rest[:2] if head else (None, None)
        o_ref, a_ref, b_ref = rest[2:5] if head else rest
        hv = h_ref[...]
        n = _rms_parts(hv, g_ref[...])[2].astype(BF16)
        acc = jnp.zeros((tm, d), F32)
        for s, e in chunks:
            a = _dot(n, wg_ref[:, s:e])
            b = _dot(n, wu_ref[:, s:e])
            a_ref[:, s:e] = a.astype(BF16)
            b_ref[:, s:e] = b.astype(BF16)
            acc = acc + _dot((a * jax.nn.sigmoid(a) * b).astype(BF16), wd_ref[s:e, :])
        out = hv + 0.5 * acc
        o_ref[...] = _loss_tile(out, t_ref[...], gh_ref[...], head[2], head[3], rest[5], rest[6]) if head else out

    tile = pl.BlockSpec((tm, d), lambda i: (i, 0))
    wide = pl.BlockSpec((tm, f), lambda i: (i, 0))
    in_specs, args = [tile, _VM, _VM, _VM, _VM], (h, g, wg, wu, wd)
    out_specs = [tile, wide, wide]
    out_shape = [jax.ShapeDtypeStruct((t, d), F32), jax.ShapeDtypeStruct((t, f), BF16), jax.ShapeDtypeStruct((t, f), BF16)]
    if head:
        in_specs, args = in_specs + [tile, _VM], args + (head[0], head[1])
        out_specs = out_specs + [pl.BlockSpec((SUBLANES, 128), lambda i: (0, 0)), pl.BlockSpec((1, d), lambda i: (0, 0))]
        out_shape = out_shape + [jax.ShapeDtypeStruct((SUBLANES, 128), F32), jax.ShapeDtypeStruct((1, d), F32)]
    return _grid_call(body, name, t // tm, in_specs, out_specs, out_shape, [], args, ride)


def _ffn_bwd(h, dh_out, a16, b16, g, wg, wu, wd, name, ride=None):
    t, d = h.shape
    f = wg.shape[1]
    tm = _tile(t, TOKEN_TILE)
    chunks = _chunks(f, FFN_CHUNK)

    def body(h_ref, dho_ref, a_ref, b_ref, g_ref, wg_ref, wu_ref, wd_ref, dh_ref, dg_ref, n_ref, da_ref, db_ref, s_ref, do_ref):
        @pl.when(pl.program_id(0) == 0)
        def _():
            dg_ref[...] = jnp.zeros_like(dg_ref)

        hv = h_ref[...]
        gv = g_ref[...]
        r, xhat, n32 = _rms_parts(hv, gv)
        dho = dho_ref[...]
        do = (0.5 * dho).astype(BF16)
        dn = jnp.zeros((tm, d), F32)
        for s, e in chunks:
            a = a_ref[:, s:e].astype(F32)
            b = b_ref[:, s:e].astype(F32)
            sig = jax.nn.sigmoid(a)
            sa = a * sig
            ds = _dot_nt(do, wd_ref[s:e, :])
            da = (ds * b * (sig * (1.0 + a * (1.0 - sig)))).astype(BF16)
            db = (ds * sa).astype(BF16)
            s_ref[:, s:e] = (sa * b).astype(BF16)
            da_ref[:, s:e] = da
            db_ref[:, s:e] = db
            dn = dn + _dot_nt(da, wg_ref[:, s:e]) + _dot_nt(db, wu_ref[:, s:e])
        dh_ref[...] = dho + _rms_bwd(dn, gv, r, xhat)
        dg_ref[...] += jnp.sum(dn * xhat, axis=0, keepdims=True)
        n_ref[...] = n32.astype(BF16)
        do_ref[...] = do

    tile = pl.BlockSpec((tm, d), lambda i: (i, 0))
    wide = pl.BlockSpec((tm, f), lambda i: (i, 0))
    one = pl.BlockSpec((1, d), lambda i: (0, 0))
    return _grid_call(
        body, name, t // tm, [tile, tile, wide, wide, _VM, _VM, _VM, _VM], [tile, one, tile, wide, wide, wide, tile],
        [jax.ShapeDtypeStruct((t, d), F32), jax.ShapeDtypeStruct((1, d), F32),
         jax.ShapeDtypeStruct((t, d), BF16), jax.ShapeDtypeStruct((t, f), BF16),
         jax.ShapeDtypeStruct((t, f), BF16), jax.ShapeDtypeStruct((t, f), BF16),
         jax.ShapeDtypeStruct((t, d), BF16)],
        [], (h, dh_out, a16, b16, g, wg, wu, wd), ride)


def _dw(a, b, name, ride=None):
    t, m = a.shape
    n = b.shape[1]
    bn = next(k for k in (512, 256, n) if n % k == 0)

    def body(a_ref, b_ref, o_ref):
        o_ref[...] = _dot_tn(a_ref[...], b_ref[...]).astype(BF16)

    (out,), got = _grid_call(
        body, name, n // bn, [_VM, pl.BlockSpec((t, bn), lambda j: (0, j))], [pl.BlockSpec((m, bn), lambda j: (0, j))],
        [jax.ShapeDtypeStruct((m, n), BF16)], [], (a, b), ride)
    return (out, got) if ride else out


def _to_bf16(arrays, name):
    k = len(arrays)

    def body(*refs):
        for x_ref, o_ref in zip(refs[:k], refs[k:]):
            o_ref[...] = x_ref[...].astype(BF16)

    return pl.pallas_call(
        body, name=name, out_shape=[jax.ShapeDtypeStruct(a.shape, BF16) for a in arrays],
        compiler_params=pltpu.CompilerParams(vmem_limit_bytes=VMEM_LIMIT_BYTES),
    )(*arrays)


def _s5_discretise(a_re, a_im, log_dt, b_re, b_im):
    dt = jnp.exp(log_dt)
    mag = jnp.exp(a_re * dt)
    lam_re = mag * jnp.cos(a_im * dt)
    lam_im = mag * jnp.sin(a_im * dt)
    den = a_re * a_re + a_im * a_im
    q_re = ((lam_re - 1.0) * a_re + lam_im * a_im) / den
    q_im = (lam_im * a_re - (lam_re - 1.0) * a_im) / den
    bb_re = q_re[:, None, :] * b_re - q_im[:, None, :] * b_im
    bb_im = q_re[:, None, :] * b_im + q_im[:, None, :] * b_re
    return lam_re, lam_im, bb_re, bb_im


def _s5_params_fwd(a_re, a_im, log_dt, b_re, b_im):
    g, p = a_re.shape
    c = b_re.shape[1]

    def body(are_ref, aim_ref, ldt_ref, bre_ref, bim_ref, pwr_ref, pwi_ref, bbr_ref, bbi_ref):
        lr, li, bbr, bbi = _s5_discretise(are_ref[...], aim_ref[...], ldt_ref[...], bre_ref[...], bim_ref[...])
        bbr_ref[...] = bbr
        bbi_ref[...] = bbi
        pr, pi = lr, li
        pwr_ref[0] = pr
        pwi_ref[0] = pi
        for k in range(1, SUBLANES):
            pr, pi = pr * lr - pi * li, pr * li + pi * lr
            pwr_ref[k] = pr
            pwi_ref[k] = pi

    return pl.pallas_call(
        body, name="s5_params_fwd",
        out_shape=[jax.ShapeDtypeStruct((SUBLANES, g, p), F32), jax.ShapeDtypeStruct((SUBLANES, g, p), F32),
                   jax.ShapeDtypeStruct((g, c, p), F32), jax.ShapeDtypeStruct((g, c, p), F32)],
    )(a_re, a_im, log_dt, b_re, b_im)


def _s5_params_bwd(a_re, a_im, log_dt, b_re, b_im, dlam, dbb_re, dbb_im):
    g, p = a_re.shape
    c = b_re.shape[1]

    def body(are_ref, aim_ref, ldt_ref, bre_ref, bim_ref, dlam_ref, dbr_ref, dbi_ref,
             dare_ref, daim_ref, dldt_ref, dbre_ref, dbim_ref):
        dlr = jnp.sum(dlam_ref[0], axis=0)
        dli = jnp.sum(dlam_ref[1], axis=0)
        _, vjp = jax.vjp(_s5_discretise, are_ref[...], aim_ref[...], ldt_ref[...], bre_ref[...], bim_ref[...])
        dare, daim, dldt, dbre, dbim = vjp((dlr, dli, dbr_ref[...], dbi_ref[...]))
        dare_ref[...] = dare
        daim_ref[...] = daim
        dldt_ref[...] = dldt
        dbre_ref[...] = dbre
        dbim_ref[...] = dbim

    return pl.pallas_call(
        body, name="s5_params_bwd",
        out_shape=[jax.ShapeDtypeStruct((g, p), F32), jax.ShapeDtypeStruct((g, p), F32),
                   jax.ShapeDtypeStruct((g, 1), F32), jax.ShapeDtypeStruct((g, c, p), F32),
                   jax.ShapeDtypeStruct((g, c, p), F32)],
    )(a_re, a_im, log_dt, b_re, b_im, dlam, dbb_re, dbb_im)


def _scan_chunks(gp):
    hg = gp // 2
    w = min(SCAN_LANES, hg)
    return w, [(half * hg + k * w, half * gp + k * w, half * gp + hg + k * w) for half in range(2) for k in range(hg // w)]


def _cmul_acc(xr, xi, tr, ti, sr, si):
    return xr + tr * sr - ti * si, xi + tr * si + ti * sr


def _scan_fwd(buf_ref, row0, tm, ltab_ref, cin_ref, cout_ref, gp):
    w, chunks = _scan_chunks(gp)
    for lo_t, lo_r, lo_i in chunks:
        def body(r, carry, lo_t=lo_t, lo_r=lo_r, lo_i=lo_i):
            cr, ci = carry
            row = pl.multiple_of(row0 + r * SUBLANES, SUBLANES)
            xr = buf_ref[pl.ds(row, SUBLANES), lo_r:lo_r + w]
            xi = buf_ref[pl.ds(row, SUBLANES), lo_i:lo_i + w]
            for tab, shift in ((0, 1), (2, 2), (4, 4)):
                xr, xi = _cmul_acc(xr, xi, ltab_ref[tab, :, lo_t:lo_t + w], ltab_ref[tab + 1, :, lo_t:lo_t + w],
                                   pltpu.roll(xr, shift, 0), pltpu.roll(xi, shift, 0))
            xr, xi = _cmul_acc(xr, xi, ltab_ref[6, :, lo_t:lo_t + w], ltab_ref[7, :, lo_t:lo_t + w], cr, ci)
            buf_ref[pl.ds(row, SUBLANES), lo_r:lo_r + w] = xr
            buf_ref[pl.ds(row, SUBLANES), lo_i:lo_i + w] = xi
            last = SUBLANES - 1
            return (jnp.broadcast_to(xr[last:last + 1], (SUBLANES, w)), jnp.broadcast_to(xi[last:last + 1], (SUBLANES, w)))

        cr, ci = lax.fori_loop(0, tm // SUBLANES, body,
                               (cin_ref[0:SUBLANES, lo_r:lo_r + w], cin_ref[0:SUBLANES, lo_i:lo_i + w]))
        if cout_ref is not None:
            cout_ref[0:SUBLANES, lo_r:lo_r + w] = cr
            cout_ref[0:SUBLANES, lo_i:lo_i + w] = ci


def _scan_rev(g_ref, hext_ref, tm, ltab_ref, gc_ref, dlam_ref, gp):
    w, chunks = _scan_chunks(gp)
    nb = tm // SUBLANES
    for lo_t, lo_r, lo_i in chunks:
        def body(k, carry, lo_t=lo_t, lo_r=lo_r, lo_i=lo_i):
            cr, ci, ar, ai = carry
            row = pl.multiple_of((nb - 1 - k) * SUBLANES, SUBLANES)
            xr = g_ref[pl.ds(row, SUBLANES), lo_r:lo_r + w]
            xi = g_ref[pl.ds(row, SUBLANES), lo_i:lo_i + w]
            for tab, shift in ((8, 7), (10, 6), (12, 4)):
                xr, xi = _cmul_acc(xr, xi, ltab_ref[tab, :, lo_t:lo_t + w], ltab_ref[tab + 1, :, lo_t:lo_t + w],
                                   pltpu.roll(xr, shift, 0), pltpu.roll(xi, shift, 0))
            xr, xi = _cmul_acc(xr, xi, ltab_ref[14, :, lo_t:lo_t + w], ltab_ref[15, :, lo_t:lo_t + w], cr, ci)
            g_ref[pl.ds(row, SUBLANES), lo_r:lo_r + w] = xr
            g_ref[pl.ds(row, SUBLANES), lo_i:lo_i + w] = xi
            first = lax.broadcasted_iota(jnp.int32, (SUBLANES, w), 0) == 0
            prev = pl.ds(row, SUBLANES)
            here = pl.ds(row + SUBLANES, SUBLANES)
            hpr = jnp.where(first, pltpu.roll(hext_ref[prev, lo_r:lo_r + w], 1, 0), pltpu.roll(hext_ref[here, lo_r:lo_r + w], 1, 0))
            hpi = jnp.where(first, pltpu.roll(hext_ref[prev, lo_i:lo_i + w], 1, 0), pltpu.roll(hext_ref[here, lo_i:lo_i + w], 1, 0))
            ar = ar + xr * hpr + xi * hpi
            ai = ai - xr * hpi + xi * hpr
            return (jnp.broadcast_to(xr[0:1], (SUBLANES, w)), jnp.broadcast_to(xi[0:1], (SUBLANES, w)), ar, ai)

        cr, ci, ar, ai = lax.fori_loop(
            0, nb, body, (gc_ref[:, lo_r:lo_r + w], gc_ref[:, lo_i:lo_i + w], dlam_ref[:, lo_r:lo_r + w], dlam_ref[:, lo_i:lo_i + w]))
        gc_ref[:, lo_r:lo_r + w] = cr
        gc_ref[:, lo_i:lo_i + w] = ci
        dlam_ref[:, lo_r:lo_r + w] = ar
        dlam_ref[:, lo_i:lo_i + w] = ai


def _conv_taps(cw, cext_ref, cin, tm):
    return (cw[0:1] * cext_ref[SUBLANES - 2:SUBLANES - 2 + tm, :] + cw[1:2] * cext_ref[SUBLANES - 1:SUBLANES - 1 + tm, :]
            + cw[2:3] * cin)


def _mix_fwd(h, gm, win, bg, bc, cc, dsk, wglu, cw, wco, wo, ltab, dims):
    d, ds, dc, gp = dims
    t = h.shape[0]
    tm = _tile(t, MIX_TILE)
    nt = t // tm
    dsh = ds // 2
    o1, o2, o3 = ds + dc, ds + 2 * dc, ds + 3 * dc
    ncols = o3 + 2 * d

    def body(h_ref, gm_ref, win_ref, bg_ref, bc_ref, cc_ref, dsk_ref, wglu_ref, cw_ref, wco_ref, wo_ref, ltab_ref,
             h2_ref, st_ref, cvs_ref, p_ref, hs_ref, y5_ref, z_ref, yc_ref, hbuf_ref, carry_ref, cext_ref):
        @pl.when(pl.program_id(0) == 0)
        def _():
            carry_ref[...] = jnp.zeros_like(carry_ref)
            cext_ref[0:SUBLANES, :] = jnp.zeros((SUBLANES, dc), F32)

        st_ref[0] = carry_ref[...]
        cvs_ref[0] = cext_ref[0:SUBLANES, :]
        hv = h_ref[...]
        bg = bg_ref[...]
        u = _rms_parts(hv, gm_ref[...])[2].astype(BF16)
        us = _dot(u, win_ref[:, 0:ds])
        v = _dot(u, win_ref[:, ds:o1])
        gb = _dot(u, win_ref[:, o1:o2])
        gcv = _dot(u, win_ref[:, o2:o3])
        gs = jax.nn.sigmoid(_dot(u, win_ref[:, o3:o3 + d]) + bg[:, 0:d])
        gcg = jax.nn.sigmoid(_dot(u, win_ref[:, o3 + d:o3 + 2 * d]) + bg[:, d:2 * d])
        us16 = us.astype(BF16)
        p_ref[:, 0:ds] = us16
        p_ref[:, ds:o1] = v.astype(BF16)
        p_ref[:, o1:o2] = gb.astype(BF16)
        p_ref[:, o2:o3] = gcv.astype(BF16)
        p_ref[:, o3:o3 + d] = gs.astype(BF16)
        p_ref[:, o3 + d:ncols] = gcg.astype(BF16)
        for half in range(2):
            hbuf_ref[:, half * gp:(half + 1) * gp] = _dot(us16[:, half * dsh:(half + 1) * dsh], bc_ref[half])
        _scan_fwd(hbuf_ref, 0, tm, ltab_ref, carry_ref, carry_ref, gp)
        hs_ref[...] = hbuf_ref[...].astype(BF16)
        y5 = jnp.concatenate([_dot(hs_ref[:, half * gp:(half + 1) * gp], cc_ref[half]) for half in range(2)], axis=1) + dsk_ref[...] * us
        y5_ref[...] = y5.astype(BF16)
        z = _dot(jax.nn.gelu(y5).astype(BF16), wglu_ref[...])
        z_ref[...] = z.astype(BF16)
        ys = z[:, 0:d] * jax.nn.sigmoid(z[:, d:2 * d])
        cin = gcv * v
        cext_ref[SUBLANES:SUBLANES + tm, :] = cin
        yc = _dot((gb * _conv_taps(cw_ref[...], cext_ref, cin, tm)).astype(BF16), wco_ref[...])
        yc_ref[...] = yc.astype(BF16)
        h2_ref[...] = hv + _dot((gs * ys + gcg * yc).astype(BF16), wo_ref[...])
        cext_ref[0:SUBLANES, :] = cext_ref[tm:tm + SUBLANES, :]

    def tile(cols):
        return pl.BlockSpec((tm, cols), lambda i: (i, 0))

    def bf(cols):
        return jax.ShapeDtypeStruct((t, cols), BF16)

    return pl.pallas_call(
        body, name="mix_fwd", grid=(nt,),
        in_specs=[tile(d)] + [_VM] * 11,
        out_specs=[tile(d), pl.BlockSpec((1, SUBLANES, 2 * gp), lambda i: (i, 0, 0)), pl.BlockSpec((1, SUBLANES, dc), lambda i: (i, 0, 0)),
                   tile(ncols), tile(2 * gp), tile(ds), tile(2 * d), tile(d)],
        out_shape=[jax.ShapeDtypeStruct((t, d), F32), jax.ShapeDtypeStruct((nt, SUBLANES, 2 * gp), F32),
                   jax.ShapeDtypeStruct((nt, SUBLANES, dc), F32), bf(ncols), bf(2 * gp), bf(ds), bf(2 * d), bf(d)],
        scratch_shapes=[pltpu.VMEM((tm, 2 * gp), F32), pltpu.VMEM((SUBLANES, 2 * gp), F32), pltpu.VMEM((SUBLANES + tm, dc), F32)],
        compiler_params=_params(),
    )(h, gm, win, bg, bc, cc, dsk, wglu, cw, wco, wo, ltab)


def _mix_bwd(h, dh2, saved, gm, win, bc, cc, dsk, wglu, cw, wco, wo, ltab, dims):
    d, ds, dc, gp = dims
    t = h.shape[0]
    tm = _tile(t, MIX_TILE)
    nt = t // tm
    dsh = ds // 2
    o1, o2, o3 = ds + dc, ds + 2 * dc, ds + 3 * dc
    ncols = o3 + 2 * d

    def body(h_ref, dh2_ref, st_ref, cvs_ref, p_ref, hs_ref, y5_ref, z_ref, yc_ref,
             gm_ref, win_ref, bc_ref, cc_ref, dsk_ref, wglu_ref, cw_ref, wco_ref, wo_ref, ltab_ref,
             dh1_ref, u_ref, dp_ref, ge_ref, dz_ref, cg_ref, dyc_ref, mx_ref, dh216_ref,
             dgm_ref, dbg_ref, ddsk_ref, dcw_ref, dlam_ref, dbc_ref, dcc_ref,
             hext_ref, gbuf_ref, gcarry_ref, cext_ref, dcvext_ref):
        @pl.when(pl.program_id(0) == 0)
        def _():
            for ref in (dgm_ref, dbg_ref, ddsk_ref, dcw_ref, dlam_ref, dbc_ref, dcc_ref, gcarry_ref):
                ref[...] = jnp.zeros_like(ref)
            dcvext_ref[tm:tm + SUBLANES, :] = jnp.zeros((SUBLANES, dc), F32)

        hext_ref[0:SUBLANES, :] = st_ref[0]
        hext_ref[SUBLANES:SUBLANES + tm, :] = hs_ref[...].astype(F32)
        cext_ref[0:SUBLANES, :] = cvs_ref[0]
        gmv = gm_ref[...]
        cw_v = cw_ref[...]
        dskv = dsk_ref[...]
        r, xhat, n32 = _rms_parts(h_ref[...], gmv)
        us = p_ref[:, 0:ds].astype(F32)
        v = p_ref[:, ds:o1].astype(F32)
        gb = p_ref[:, o1:o2].astype(F32)
        gcv = p_ref[:, o2:o3].astype(F32)
        gs = p_ref[:, o3:o3 + d].astype(F32)
        gcg = p_ref[:, o3 + d:ncols].astype(F32)
        z1 = z_ref[:, 0:d].astype(F32)
        sz = jax.nn.sigmoid(z_ref[:, d:2 * d].astype(F32))
        ys = z1 * sz
        yc = yc_ref[...].astype(F32)
        y5 = y5_ref[...].astype(F32)
        ge, gelu_vjp = jax.vjp(jax.nn.gelu, y5)
        cin = gcv * v
        cext_ref[SUBLANES:SUBLANES + tm, :] = cin
        cv = _conv_taps(cw_v, cext_ref, cin, tm)

        dh2v = dh2_ref[...]
        dh216 = dh2v.astype(BF16)
        dmixed = _dot_nt(dh216, wo_ref[...])
        dys = dmixed * gs
        dyc16 = (dmixed * gcg).astype(BF16)
        dpgs = dmixed * ys * gs * (1.0 - gs)
        dpgc = dmixed * yc * gcg * (1.0 - gcg)
        dz16 = jnp.concatenate([dys * sz, dys * z1 * sz * (1.0 - sz)], axis=1).astype(BF16)
        dy5 = gelu_vjp(_dot_nt(dz16, wglu_ref[...]))[0]
        dy516 = dy5.astype(BF16)
        for half in range(2):
            gbuf_ref[:, half * gp:(half + 1) * gp] = _dot_nt(dy516[:, half * dsh:(half + 1) * dsh], cc_ref[half])
        _scan_rev(gbuf_ref, hext_ref, tm, ltab_ref, gcarry_ref, dlam_ref, gp)
        dus = []
        for half in range(2):
            g16 = gbuf_ref[:, half * gp:(half + 1) * gp].astype(BF16)
            dus.append(_dot_nt(g16, bc_ref[half]))
            dbc_ref[half] += _dot_tn(p_ref[:, half * dsh:(half + 1) * dsh], g16)
            dcc_ref[half] += _dot_tn(hs_ref[:, half * gp:(half + 1) * gp], dy516[:, half * dsh:(half + 1) * dsh])
        dus = jnp.concatenate(dus, axis=1) + dskv * dy5
        ddsk_ref[...] += jnp.sum(dy5 * us, axis=0, keepdims=True)
        dcg = _dot_nt(dyc16, wco_ref[...])
        dgb = dcg * cv
        dcv = dcg * gb
        dcvext_ref[0:tm, :] = dcv
        dcin = cw_v[2:3] * dcv + cw_v[1:2] * dcvext_ref[1:1 + tm, :] + cw_v[0:1] * dcvext_ref[2:2 + tm, :]
        dcw_ref[0:1, :] += jnp.sum(dcv * cext_ref[SUBLANES - 2:SUBLANES - 2 + tm, :], axis=0, keepdims=True)
        dcw_ref[1:2, :] += jnp.sum(dcv * cext_ref[SUBLANES - 1:SUBLANES - 1 + tm, :], axis=0, keepdims=True)
        dcw_ref[2:3, :] += jnp.sum(dcv * cin, axis=0, keepdims=True)
        dcvext_ref[tm:tm + SUBLANES, :] = dcvext_ref[0:SUBLANES, :]
        dp16 = jnp.concatenate([dus, dcin * gcv, dgb, dcin * v, dpgs, dpgc], axis=1).astype(BF16)
        du = _dot_nt(dp16, win_ref[...])
        dh1_ref[...] = dh2v + _rms_bwd(du, gmv, r, xhat)
        dgm_ref[...] += jnp.sum(du * xhat, axis=0, keepdims=True)
        dbg_ref[...] += jnp.concatenate([jnp.sum(dpgs, axis=0, keepdims=True), jnp.sum(dpgc, axis=0, keepdims=True)], axis=1)
        u_ref[...] = n32.astype(BF16)
        dp_ref[...] = dp16
        ge_ref[...] = ge.astype(BF16)
        dz_ref[...] = dz16
        cg_ref[...] = (gb * cv).astype(BF16)
        dyc_ref[...] = dyc16
        mx_ref[...] = (gs * ys + gcg * yc).astype(BF16)
        dh216_ref[...] = dh216

    def rev(cols):
        return pl.BlockSpec((tm, cols), lambda j: (nt - 1 - j, 0))

    def rev3(cols):
        return pl.BlockSpec((1, SUBLANES, cols), lambda j: (nt - 1 - j, 0, 0))

    def bf(cols):
        return jax.ShapeDtypeStruct((t, cols), BF16)

    st, cvs, p16, hs16, y516, z16, yc16 = saved
    return pl.pallas_call(
        body, name="mix_bwd", grid=(nt,),
        in_specs=[rev(d), rev(d), rev3(2 * gp), rev3(dc), rev(ncols), rev(2 * gp), rev(ds), rev(2 * d), rev(d)] + [_VM] * 10,
        out_specs=[rev(d), rev(d), rev(ncols), rev(ds), rev(2 * d), rev(dc), rev(d), rev(d), rev(d)] + [_VM] * 7,
        out_shape=[jax.ShapeDtypeStruct((t, d), F32), bf(d), bf(ncols), bf(ds), bf(2 * d), bf(dc), bf(d), bf(d), bf(d),
                   jax.ShapeDtypeStruct((1, d), F32), jax.ShapeDtypeStruct((1, 2 * d), F32), jax.ShapeDtypeStruct((1, ds), F32),
                   jax.ShapeDtypeStruct((SUBLANES, dc), F32), jax.ShapeDtypeStruct((SUBLANES, 2 * gp), F32),
                   jax.ShapeDtypeStruct((2, dsh, gp), F32), jax.ShapeDtypeStruct((2, gp, dsh), F32)],
        scratch_shapes=[pltpu.VMEM((SUBLANES + tm, 2 * gp), F32), pltpu.VMEM((tm, 2 * gp), F32),
                        pltpu.VMEM((SUBLANES, 2 * gp), F32), pltpu.VMEM((SUBLANES + tm, dc), F32),
                        pltpu.VMEM((tm + SUBLANES, dc), F32)],
        compiler_params=_params(),
    )(h, dh2, st, cvs, p16, hs16, y516, z16, yc16, gm, win, bc, cc, dsk, wglu, cw, wco, wo, ltab)


def _pad_rows(a, rows, axis=0):
    pad = [(0, 0)] * a.ndim
    pad[axis] = (0, rows - a.shape[axis])
    return jnp.pad(a, pad)


def _as_rows(a):
    flat = a.reshape(-1)
    n = -(-flat.shape[0] // SLAB_COLS) * SLAB_COLS
    return jnp.pad(flat, (0, n - flat.shape[0])).reshape(-1, SLAB_COLS)


def _pack(arrs):
    rows = jnp.concatenate([_as_rows(a) for a in arrs], axis=0)
    return _pad_rows(rows, -(-rows.shape[0] // 16) * 16)


def _unpack(slab, shapes):
    out, r = [], 0
    for shp in shapes:
        size = 1
        for s in shp:
            size *= s
        n = -(-size // SLAB_COLS)
        out.append(slab[r:r + n].reshape(-1)[:size].reshape(shp))
        r += n
    return out


def _block_diag(blocks):
    n, a, b = blocks.shape
    eye = jnp.eye(n, dtype=blocks.dtype)
    return (blocks[:, :, None, :] * eye[:, None, :, None]).reshape(n * a, n * b)


def _diag_blocks(mat, n):
    a, b = mat.shape[0] // n, mat.shape[1] // n
    eye = jnp.eye(n, dtype=mat.dtype)
    return jnp.sum(mat.reshape(n, a, n, b) * eye[:, None, :, None], axis=2)


BIG = (("ffn1_w_gate", "col"), ("ffn1_w_up", "col"), ("ffn1_w_down", "row"), ("w_in", "col"), ("ssm_w_glu", "col"),
       ("conv_w_out", "col"), ("w_o", "row"), ("ffn2_w_gate", "col"), ("ffn2_w_up", "col"), ("ffn2_w_down", "row"))
REPLICATED = ("g_ffn1", "g_mix", "b_gate", "ssm_a_re", "ssm_a_im", "ssm_log_dt", "ssm_b_re", "ssm_b_im", "ssm_c_re",
              "ssm_c_im", "ssm_d", "g_ffn2", "g_final")
WEIGHTS = ("meta_tokens", "g_ffn1", "ffn1_w_gate", "ffn1_w_up", "ffn1_w_down", "g_mix", "w_in", "b_gate", "ssm_a_re",
           "ssm_a_im", "ssm_log_dt", "ssm_b_re", "ssm_b_im", "ssm_c_re", "ssm_c_im", "ssm_d", "ssm_w_glu", "conv_w",
           "conv_w_out", "w_o", "g_ffn2", "ffn2_w_gate", "ffn2_w_up", "ffn2_w_down", "g_final")
N_EARLY = 3


def _full_from_blocks(blocks, kind):
    n, r, c = blocks.shape
    if kind == "col":
        return jnp.transpose(blocks, (1, 0, 2)).reshape(r, n * c)
    return blocks.reshape(n * r, c)


def _blocks_from_full(full, kind):
    if kind == "col":
        r, nc = full.shape
        return jnp.transpose(full.reshape(r, NDEV, nc // NDEV), (1, 0, 2))
    nr, c = full.shape
    return full.reshape(NDEV, nr // NDEV, c)


def kernel(x, meta_tokens, g_ffn1, ffn1_w_gate, ffn1_w_up, ffn1_w_down, g_mix, w_in, b_gate, ssm_a_re, ssm_a_im, ssm_log_dt, ssm_b_re, ssm_b_im, ssm_c_re, ssm_c_im, ssm_d, ssm_w_glu, conv_w, conv_w_out, w_o, g_ffn2, ffn2_w_gate, ffn2_w_up, ffn2_w_down, g_final, loss_target, m_meta_tokens, m_g_ffn1, m_ffn1_w_gate, m_ffn1_w_up, m_ffn1_w_down, m_g_mix, m_w_in, m_b_gate, m_ssm_a_re, m_ssm_a_im, m_ssm_log_dt, m_ssm_b_re, m_ssm_b_im, m_ssm_c_re, m_ssm_c_im, m_ssm_d, m_ssm_w_glu, m_conv_w, m_conv_w_out, m_w_o, m_g_ffn2, m_ffn2_w_gate, m_ffn2_w_up, m_ffn2_w_down, m_g_final, v_meta_tokens, v_g_ffn1, v_ffn1_w_gate, v_ffn1_w_up, v_ffn1_w_down, v_g_mix, v_w_in, v_b_gate, v_ssm_a_re, v_ssm_a_im, v_ssm_log_dt, v_ssm_b_re, v_ssm_b_im, v_ssm_c_re, v_ssm_c_im, v_ssm_d, v_ssm_w_glu, v_conv_w, v_conv_w_out, v_w_o, v_g_ffn2, v_ffn2_w_gate, v_ffn2_w_up, v_ffn2_w_down, v_g_final):
    args = dict(locals())
    w = {n: args[n] for n in WEIGHTS}
    mom_m = {n: args["m_" + n] for n in WEIGHTS}
    mom_v = {n: args["v_" + n] for n in WEIGHTS}

    seq, d = x.shape[1], x.shape[2]
    n_meta = meta_tokens.shape[0]
    ds = ssm_d.shape[1]
    n_grp, n_state = ssm_a_re.shape[1], ssm_a_re.shape[2]
    gp = n_grp * n_state
    dc = conv_w.shape[3] * NDEV
    dims = (d, ds, dc, gp)
    t_real = n_meta + seq
    t_pad = -(-t_real // ROW_ALIGN) * ROW_ALIGN
    me_chip = 2 * lax.axis_index("x") + lax.axis_index("y")
    me_core = lax.axis_index("c")
    me = 2 * me_chip + me_core
    mcols, ccols = d // NDEV, dc // NDEV

    cw_shard = _pad_rows(_pad_rows(conv_w.reshape(3, ccols), SUBLANES), 128, axis=1)
    shard16 = dict(zip([name for name, _ in BIG], _to_bf16([w[name][0] for name, _ in BIG], "weights_to_bf16")))
    early, late = BIG[:N_EARLY], BIG[N_EARLY:]
    got = _exchange(_gather_ride([shard16[name] for name, _ in early] + [meta_tokens, cw_shard]), "gather_first")
    full = {name: _full_from_blocks(got[i], kind) for i, (name, kind) in enumerate(early)}
    meta_full = _full_from_blocks(got[-2], "col")
    cw_rows = _pad_rows(_full_from_blocks(got[-1][:, 0:3, 0:ccols], "col"), SUBLANES)

    a_re, a_im, ldt = ssm_a_re[0], ssm_a_im[0], ssm_log_dt[0].reshape(n_grp, 1)
    b_re_t = jnp.transpose(ssm_b_re[0], (0, 2, 1))
    b_im_t = jnp.transpose(ssm_b_im[0], (0, 2, 1))
    pw_r, pw_i, bb_r, bb_i = _s5_params_fwd(a_re, a_im, ldt, b_re_t, b_im_t)
    pw_r = pw_r.reshape(SUBLANES, gp)
    pw_i = pw_i.reshape(SUBLANES, gp)
    sub = jnp.arange(SUBLANES)[:, None]

    def fwd_tab(p, k):
        return jnp.where(sub >= k, p[k - 1][None, :], 0.0)

    def rev_tab(p, k):
        return jnp.where(sub <= SUBLANES - 1 - k, p[k - 1][None, :], 0.0)

    ltab = jnp.stack(
        [fwd_tab(pw_r, 1), fwd_tab(pw_i, 1), fwd_tab(pw_r, 2), fwd_tab(pw_i, 2), fwd_tab(pw_r, 4), fwd_tab(pw_i, 4), pw_r, pw_i,
         rev_tab(pw_r, 1), -rev_tab(pw_i, 1), rev_tab(pw_r, 2), -rev_tab(pw_i, 2), rev_tab(pw_r, 4), -rev_tab(pw_i, 4),
         pw_r[::-1], -pw_i[::-1]], axis=0)
    gh = n_grp // 2
    bc = jnp.stack([jnp.concatenate([_block_diag(bb_r[h * gh:(h + 1) * gh]), _block_diag(bb_i[h * gh:(h + 1) * gh])], axis=1)
                    for h in range(2)]).astype(BF16)
    c_re_t = jnp.transpose(ssm_c_re[0], (0, 2, 1))
    c_im_t = jnp.transpose(ssm_c_im[0], (0, 2, 1))
    cc = jnp.stack([jnp.concatenate([_block_diag(c_re_t[h * gh:(h + 1) * gh]), -_block_diag(c_im_t[h * gh:(h + 1) * gh])], axis=0)
                    for h in range(2)]).astype(BF16)

    zpad = jnp.zeros((t_pad - t_real, d), F32)
    h0 = jnp.concatenate([meta_full, x[0], zpad], axis=0)
    tgt = jnp.concatenate([jnp.zeros((n_meta, d), F32), loss_target[0], zpad], axis=0)
    (h1, a1, b1), got = _ffn_fwd(h0, g_ffn1, full["ffn1_w_gate"], full["ffn1_w_up"], full["ffn1_w_down"], "ffn1_fwd",
                                 ride=_gather_ride([shard16[name] for name, _ in late]))
    full.update({name: _full_from_blocks(got[i], kind) for i, (name, kind) in enumerate(late)})
    h2, *saved = _mix_fwd(h1, g_mix, full["w_in"], b_gate, bc, cc, ssm_d, full["ssm_w_glu"], cw_rows, full["conv_w_out"],
                          full["w_o"], ltab, dims)
    (dh3, a2, b2, loss_blk, dg_final), _ = _ffn_fwd(h2, g_ffn2, full["ffn2_w_gate"], full["ffn2_w_up"], full["ffn2_w_down"], "ffn2_fwd",
                                                 head=(tgt, g_final.reshape(1, d), n_meta, t_real))
    loss = lax.psum(loss_blk[0, 0], AXES)

    (dh2, dg_ffn2, n2, da2, db2, s2, do2), _ = _ffn_bwd(
        h2, dh3, a2, b2, g_ffn2, full["ffn2_w_gate"], full["ffn2_w_up"], full["ffn2_w_down"], "ffn2_bwd")
    (dh1, u16, dp16, ge16, dz16, cg16, dyc16, mx16, dh216, dg_mix, dbg, ddsk, dcw, dlam, dbc, dcc) = _mix_bwd(
        h1, dh2, saved, g_mix, full["w_in"], bc, cc, ssm_d, full["ssm_w_glu"], cw_rows, full["conv_w_out"], full["w_o"], ltab, dims)
    dblocks = {
        "w_in": _blocks_from_full(_dw(u16, dp16, "dw_in"), "col"),
        "ssm_w_glu": _blocks_from_full(_dw(ge16, dz16, "dw_glu"), "col"),
        "conv_w_out": _blocks_from_full(_dw(cg16, dyc16, "dw_conv_out"), "col"),
        "w_o": _blocks_from_full(_dw(mx16, dh216, "dw_o"), "row"),
        "ffn2_w_gate": _blocks_from_full(_dw(n2, da2, "dw_ffn2_gate"), "col"),
        "ffn2_w_up": _blocks_from_full(_dw(n2, db2, "dw_ffn2_up"), "col"),
        "ffn2_w_down": jnp.transpose(_blocks_from_full(_dw(do2, s2, "dw_ffn2_down"), "col"), (0, 2, 1)),
    }

    def pair_sums(names, tag):
        gs = [dblocks[name] for name in names]
        from_sibling = _exchange(_pair_ride(gs), "reduce_pair_" + tag)
        return [_add_pairs(g, me_core, b, "reduce_pair_add_" + name) for g, b, name in zip(gs, from_sibling, names)]

    late_names = [name for name, _ in late]
    pairs = dict(zip(late_names, pair_sums(late_names, "late")))
    (dh0, dg_ffn1, n1, da1, db1, s1, do1), got = _ffn_bwd(
        h0, dh1, a1, b1, g_ffn1, full["ffn1_w_gate"], full["ffn1_w_up"], full["ffn1_w_down"], "ffn1_bwd",
        ride=_chips_ride([pairs[name] for name in late_names]))
    from_chips = dict(zip(late_names, got))
    dlam4 = dlam.reshape(SUBLANES, 2, 2, gh, n_state)
    dlam_in = jnp.transpose(dlam4, (2, 0, 1, 3, 4)).reshape(2, SUBLANES, n_grp, n_state)
    hg = gp // 2
    dbb_r = jnp.concatenate([_diag_blocks(dbc[h][:, :hg], gh) for h in range(2)], axis=0)
    dbb_i = jnp.concatenate([_diag_blocks(dbc[h][:, hg:], gh) for h in range(2)], axis=0)
    da_re, da_im, dldt, dbre_t, dbim_t = _s5_params_bwd(a_re, a_im, ldt, b_re_t, b_im_t, dlam_in, dbb_r, dbb_i)
    dc_re = jnp.concatenate([_diag_blocks(dcc[h][:hg], gh) for h in range(2)], axis=0)
    dc_im = -jnp.concatenate([_diag_blocks(dcc[h][hg:], gh) for h in range(2)], axis=0)

    grads_rep = {
        "g_ffn1": dg_ffn1, "g_mix": dg_mix, "b_gate": dbg, "ssm_a_re": da_re[None], "ssm_a_im": da_im[None],
        "ssm_log_dt": dldt.reshape(1, n_grp), "ssm_b_re": jnp.transpose(dbre_t, (0, 2, 1))[None],
        "ssm_b_im": jnp.transpose(dbim_t, (0, 2, 1))[None], "ssm_c_re": jnp.transpose(dc_re, (0, 2, 1))[None],
        "ssm_c_im": jnp.transpose(dc_im, (0, 2, 1))[None], "ssm_d": ddsk, "g_ffn2": dg_ffn2, "g_final": dg_final.reshape(d),
    }

    rep_shapes = [w[n].shape for n in REPLICATED]
    small_g_shapes = rep_shapes + [(n_meta, d), (3, dc)]
    gsmall = _pack([grads_rep[n] for n in REPLICATED] + [dh0[0:n_meta], dcw[0:3]])
    dw_gate, (gall,) = _dw(n1, da1, "dw_ffn1_gate", ride=_gather_ride([gsmall]))
    dblocks.update({
        "ffn1_w_gate": _blocks_from_full(dw_gate, "col"),
        "ffn1_w_up": _blocks_from_full(_dw(n1, db1, "dw_ffn1_up"), "col"),
        "ffn1_w_down": jnp.transpose(_blocks_from_full(_dw(do1, s1, "dw_ffn1_down"), "col"), (0, 2, 1)),
    })
    early_names = [name for name, _ in early]
    pairs.update(zip(early_names, pair_sums(early_names, "early")))
    from_chips.update(zip(early_names, _exchange(_chips_ride([pairs[name] for name in early_names]), "reduce_chips_early")))

    out_g, out_d, out_m, out_v = {}, {}, {}, {}
    for name, _ in BIG:
        fc = from_chips[name]
        out_g[name], out_d[name], out_m[name], out_v[name] = _adamw(
            w[name], mom_m[name], mom_v[name], [(pairs[name], None), (fc, 0), (fc, 1), (fc, 2)], me_chip, "adamw_" + name)

    zer = [jnp.zeros((n_meta, d), F32), jnp.zeros((3, dc), F32)]
    gr, dr, mr, vr = [o[0] for o in _adamw(
        _pack([w[n] for n in REPLICATED] + zer)[None], _pack([mom_m[n] for n in REPLICATED] + zer)[None],
        _pack([mom_v[n] for n in REPLICATED] + zer)[None], [(gall, b) for b in range(NDEV)], None, "adamw_replicated")]
    g_list = _unpack(gr, small_g_shapes)
    out_g.update(zip(REPLICATED, g_list[:len(REPLICATED)]))
    out_d.update(zip(REPLICATED, _unpack(dr, rep_shapes)))
    out_m.update(zip(REPLICATED, _unpack(mr, rep_shapes)))
    out_v.update(zip(REPLICATED, _unpack(vr, rep_shapes)))

    g_meta = lax.dynamic_slice_in_dim(g_list[-2], me * mcols, mcols, axis=1)
    g_cw = lax.dynamic_slice_in_dim(g_list[-1], me * ccols, ccols, axis=1).reshape(conv_w.shape)
    tiny = ("meta_tokens", "conv_w")
    tiny_shapes = [meta_tokens.shape, conv_w.shape]
    gt, dt_, mt, vt = [o[0] for o in _adamw(
        _pack([w[n] for n in tiny])[None], _pack([mom_m[n] for n in tiny])[None], _pack([mom_v[n] for n in tiny])[None],
        [(_pack([g_meta, g_cw])[None], 0)], None, "adamw_tiny")]
    out_g.update(zip(tiny, _unpack(gt, tiny_shapes)))
    out_d.update(zip(tiny, _unpack(dt_, tiny_shapes)))
    out_m.update(zip(tiny, _unpack(mt, tiny_shapes)))
    out_v.update(zip(tiny, _unpack(vt, tiny_shapes)))

    grad_x = dh0[n_meta:t_real][None]
    return (loss, grad_x, *[out_g[n] for n in WEIGHTS], *[out_d[n] for n in WEIGHTS],
            *[out_m[n] for n in WEIGHTS], *[out_v[n] for n in WEIGHTS])
```

```python
import functools

import jax
import jax.numpy as jnp
from jax import lax
from jax.experimental import pallas as pl
from jax.experimental.pallas import tpu as pltpu

F32 = jnp.float32
BF16 = jnp.bfloat16
MESH = pl.DeviceIdType.MESH
AXES = ("x", "y", "c")
NDEV = 8
SLAB_COLS = 1024
RMS_EPS = 1e-6
TOKEN_TILE = 320
MIX_TILE = 128
ROW_ALIGN = 128
SUBLANES = 8
SCAN_LANES = 512
FFN_CHUNK = 1024
VMEM_LIMIT_BYTES = 56 * 1024 * 1024

ADAM_LR = 0.001
ADAM_B1 = 0.9
ADAM_B2 = 0.999
ADAM_EPS = 1e-08
ADAM_WD = 0.01
ADAM_STEP = 10

_VM = pl.BlockSpec(memory_space=pltpu.VMEM)
_ANY = pl.BlockSpec(memory_space=pl.ANY)


def _params(sem=("arbitrary",)):
    return pltpu.CompilerParams(dimension_semantics=sem, vmem_limit_bytes=VMEM_LIMIT_BYTES)


def _dot(a, b):
    return jnp.dot(a, b, preferred_element_type=F32)


def _dot_nt(a, b):
    return lax.dot_general(a, b, (((1,), (1,)), ((), ())), preferred_element_type=F32)


def _dot_tn(a, b):
    return lax.dot_general(a, b, (((0,), (0,)), ((), ())), preferred_element_type=F32)


def _tile(rows, most):
    return next(k for k in range(most - most % 16, 0, -16) if rows % k == 0)


def _chunks(n, step):
    return [(s, min(s + step, n)) for s in range(0, n, step)]


def _gather_plan(x_refs, out_refs, send_sems, recv_sems, local_sems):
    n = len(x_refs)
    x, y, c = lax.axis_index("x"), lax.axis_index("y"), lax.axis_index("c")
    me, sibling = (x, y, c), (x, y, 1 - c)
    chips = [(1 - x, y), (x, 1 - y), (1 - x, 1 - y)]

    def copy(i, k, block, to, src=None):
        slot = out_refs[i].at[4 * block[0] + 2 * block[1] + block[2]]
        return pltpu.make_async_remote_copy(
            src_ref=slot if src is None else src, dst_ref=slot,
            send_sem=send_sems.at[7 * i + k], recv_sem=recv_sems.at[7 * i + k], device_id=to, device_id_type=MESH)

    def mine():
        return [pltpu.make_async_copy(x_refs[i], out_refs[i].at[4 * x + 2 * y + c], local_sems.at[i]) for i in range(n)]

    def first():
        out = []
        for i in range(n):
            out.append(copy(i, 0, me, sibling, src=x_refs[i]))
            out += [copy(i, 1 + j, me, (*chip, c), src=x_refs[i]) for j, chip in enumerate(chips)]
        return out

    def start():
        for cp in mine() + first():
            cp.start()

    def finish():
        passed = []
        for j, chip in enumerate(chips):
            for i in range(n):
                copy(i, 1 + j, (*chip, c), me).wait_recv()
                cp = copy(i, 4 + j, (*chip, c), sibling)
                cp.start()
                passed.append(cp)
        for i in range(n):
            copy(i, 0, sibling, me).wait_recv()
            for j, chip in enumerate(chips):
                copy(i, 4 + j, (*chip, 1 - c), me).wait_recv()
        for cp in first() + passed:
            cp.wait_send()
        for cp in mine():
            cp.wait()

    return start, finish


def _pair_plan(g_refs, out_refs, send_sems, recv_sems):
    x, y, c = lax.axis_index("x"), lax.axis_index("y"), lax.axis_index("c")

    def copies():
        return [pltpu.make_async_remote_copy(
            src_ref=g_refs[i].at[2 * j + (1 - c)], dst_ref=out_refs[i].at[j],
            send_sem=send_sems.at[4 * i + j], recv_sem=recv_sems.at[4 * i + j],
            device_id=(x, y, 1 - c), device_id_type=MESH) for i in range(len(g_refs)) for j in range(4)]

    def start():
        for cp in copies():
            cp.start()

    def finish():
        for cp in copies():
            cp.wait()

    return start, finish


def _chips_plan(p_refs, out_refs, send_sems, recv_sems):
    x, y, c = lax.axis_index("x"), lax.axis_index("y"), lax.axis_index("c")

    def copies():
        return [pltpu.make_async_remote_copy(
            src_ref=p_refs[i].at[2 * px + py], dst_ref=out_refs[i].at[k],
            send_sem=send_sems.at[3 * i + k], recv_sem=recv_sems.at[3 * i + k],
            device_id=(px, py, c), device_id_type=MESH)
            for i in range(len(p_refs)) for k, (px, py) in enumerate([(1 - x, y), (x, 1 - y), (1 - x, 1 - y)])]

    def start():
        for cp in copies():
            cp.start()

    def finish():
        for cp in copies():
            cp.wait()

    return start, finish


def _gather_ride(shards):
    n = len(shards)
    return dict(plan=_gather_plan, arrays=list(shards),
                out_shape=[jax.ShapeDtypeStruct((NDEV, *s.shape), s.dtype) for s in shards],
                sems=[pltpu.SemaphoreType.DMA((7 * n,)), pltpu.SemaphoreType.DMA((7 * n,)), pltpu.SemaphoreType.DMA((n,))])


def _pair_ride(blocks):
    n = len(blocks)
    return dict(plan=_pair_plan, arrays=list(blocks),
                out_shape=[jax.ShapeDtypeStruct((4, *b.shape[1:]), b.dtype) for b in blocks],
                sems=[pltpu.SemaphoreType.DMA((4 * n,)), pltpu.SemaphoreType.DMA((4 * n,))])


def _chips_ride(partials):
    n = len(partials)
    return dict(plan=_chips_plan, arrays=list(partials),
                out_shape=[jax.ShapeDtypeStruct((3, *p.shape[1:]), p.dtype) for p in partials],
                sems=[pltpu.SemaphoreType.DMA((3 * n,)), pltpu.SemaphoreType.DMA((3 * n,))])


def _exchange(ride, name):
    n = len(ride["arrays"])

    def body(*refs):
        start, finish = ride["plan"](refs[:n], refs[n:2 * n], *refs[2 * n:])
        start()
        finish()

    return pl.pallas_call(
        body, name=name, out_shape=ride["out_shape"], in_specs=[_ANY] * n, out_specs=[_ANY] * n, scratch_shapes=ride["sems"],
    )(*ride["arrays"])


def _grid_call(body, name, steps, in_specs, out_specs, out_shape, scratch_shapes, args, ride=None):
    if ride is None:
        outs = pl.pallas_call(body, name=name, grid=(steps,), in_specs=in_specs, out_specs=out_specs, out_shape=out_shape,
                              scratch_shapes=scratch_shapes, compiler_params=_params())(*args)
        return list(outs), []
    n_in, n_out, n_scr, n_ride, n_sems = len(in_specs), len(out_specs), len(scratch_shapes), len(ride["arrays"]), len(ride["sems"])

    def carrying(*refs):
        ins, r_in = refs[:n_in], refs[n_in:n_in + n_ride]
        o0 = n_in + n_ride
        outs, r_out = refs[o0:o0 + n_out], refs[o0 + n_out:o0 + n_out + n_ride]
        s0 = o0 + n_out + n_ride
        scratch, sems = refs[s0:s0 + n_scr], refs[s0 + n_scr:s0 + n_scr + n_sems]
        start, finish = ride["plan"](r_in, r_out, *sems)
        pl.when(pl.program_id(0) == 0)(start)
        body(*ins, *outs, *scratch)
        pl.when(pl.program_id(0) == steps - 1)(finish)

    outs = pl.pallas_call(
        carrying, name=name, grid=(steps,), in_specs=list(in_specs) + [_ANY] * n_ride, out_specs=list(out_specs) + [_ANY] * n_ride,
        out_shape=list(out_shape) + ride["out_shape"], scratch_shapes=list(scratch_shapes) + ride["sems"],
        compiler_params=_params())(*args, *ride["arrays"])
    return list(outs[:n_out]), list(outs[n_out:])


def _row_block(rows):
    return rows if rows <= 512 else next(k for k in (512, 256, 128, rows) if rows % k == 0)


def _add_pairs(gs, core, b, name):
    k, r, n = b.shape
    tr = _row_block(r)

    def body(core_ref, a_ref, b_ref, o_ref):
        o_ref[0] = (a_ref[0, 0].astype(F32) + b_ref[0].astype(F32)).astype(o_ref.dtype)

    spec = pl.BlockSpec((1, tr, n), lambda j, i, c: (j, i, 0))
    return pl.pallas_call(
        body, name=name,
        grid_spec=pltpu.PrefetchScalarGridSpec(
            num_scalar_prefetch=1, grid=(k, r // tr),
            in_specs=[pl.BlockSpec((1, 1, tr, n), lambda j, i, c: (j, c[0], i, 0)), spec], out_specs=spec),
        out_shape=jax.ShapeDtypeStruct(b.shape, b.dtype), compiler_params=_params(("arbitrary", "arbitrary")),
    )(core.reshape(1), gs.reshape(k, 2, r, n), b)


def _adamw(w, m, v, parts, sel, name):
    _, r, n = w.shape
    tr = _row_block(r)
    nparts = len(parts)
    bc1 = 1.0 - ADAM_B1 ** ADAM_STEP
    bc2 = 1.0 - ADAM_B2 ** ADAM_STEP

    def body(sel_ref, *refs):
        w_ref, m_ref, v_ref = refs[:3]
        p_refs = refs[3:3 + nparts]
        g_ref, d_ref, nm_ref, nv_ref = refs[3 + nparts:]
        g = p_refs[0][...].astype(F32)
        for p in p_refs[1:]:
            g = g + p[...].astype(F32)
        nm = ADAM_B1 * m_ref[...] + (1.0 - ADAM_B1) * g
        nv = ADAM_B2 * v_ref[...] + (1.0 - ADAM_B2) * (g * g)
        m_hat = nm / bc1
        v_hat = nv / bc2
        g_ref[...] = g
        d_ref[...] = -ADAM_LR * (m_hat / (jnp.sqrt(v_hat) + ADAM_EPS) + ADAM_WD * w_ref[...])
        nm_ref[...] = nm
        nv_ref[...] = nv

    def part_spec(idx):
        if idx is None:
            return pl.BlockSpec((1, tr, n), lambda i, s: (s[0], i, 0))
        return pl.BlockSpec((1, tr, n), lambda i, s, idx=idx: (idx, i, 0))

    spec = pl.BlockSpec((1, tr, n), lambda i, s: (0, i, 0))
    out = jax.ShapeDtypeStruct((1, r, n), F32)
    return pl.pallas_call(
        body, name=name,
        grid_spec=pltpu.PrefetchScalarGridSpec(
            num_scalar_prefetch=1, grid=(r // tr,),
            in_specs=[spec] * 3 + [part_spec(idx) for _, idx in parts], out_specs=[spec] * 4),
        out_shape=[out] * 4, compiler_params=_params(),
    )(jnp.zeros((1,), jnp.int32) if sel is None else sel.reshape(1), w, m, v, *[p for p, _ in parts])


def _rms_parts(h, g):
    r = lax.rsqrt(jnp.mean(h * h, axis=-1, keepdims=True) + RMS_EPS)
    xhat = h * r
    return r, xhat, xhat * g


def _rms_bwd(dn, g, r, xhat):
    dxh = dn * g
    return r * (dxh - xhat * jnp.mean(dxh * xhat, axis=-1, keepdims=True))


def _loss_tile(h, tgt, g, lo, hi, loss_ref, dg_ref):
    tm, d = h.shape
    i = pl.program_id(0)

    @pl.when(i == 0)
    def _():
        loss_ref[...] = jnp.zeros_like(loss_ref)
        dg_ref[...] = jnp.zeros_like(dg_ref)

    r, xhat, y = _rms_parts(h, g)
    row = i * tm + lax.broadcasted_iota(jnp.int32, (tm, 1), 0)
    err = jnp.where((row >= lo) & (row < hi), y - tgt, 0.0)
    loss_ref[...] += jnp.full(loss_ref.shape, 0.5 * jnp.sum(jnp.mean(err * err, axis=-1, keepdims=True)), F32)
    dy = err * (1.0 / d)
    dg_ref[...] += jnp.sum(dy * xhat, axis=0, keepdims=True)
    return _rms_bwd(dy, g, r, xhat)


def _ffn_fwd(h, g, wg, wu, wd, name, ride=None, head=None):
    t, d = h.shape
    f = wg.shape[1]
    tm = _tile(t, TOKEN_TILE)
    chunks = _chunks(f, FFN_CHUNK)

    def body(h_ref, g_ref, wg_ref, wu_ref, wd_ref, *rest):
        t_ref, gh_ref = rest[:2] if head else (None, None)
        o_ref, a_ref, b_ref = rest[2:5] if head else rest
        hv = h_ref[...]
        n = _rms_parts(hv, g_ref[...])[2].astype(BF16)
        acc = jnp.zeros((tm, d), F32)
        for s, e in chunks:
            a = _dot(n, wg_ref[:, s:e])
            b = _dot(n, wu_ref[:, s:e])
            a_ref[:, s:e] = a.astype(BF16)
            b_ref[:, s:e] = b.astype(BF16)
            acc = acc + _dot((a * jax.nn.sigmoid(a) * b).astype(BF16), wd_ref[s:e, :])
        out = hv + 0.5 * acc
        o_ref[...] = _loss_tile(out, t_ref[...], gh_ref[...], head[2], head[3], rest[5], rest[6]) if head else out

    tile = pl.BlockSpec((tm, d), lambda i: (i, 0))
    wide = pl.BlockSpec((tm, f), lambda i: (i, 0))
    in_specs, args = [tile, _VM, _VM, _VM, _VM], (h, g, wg, wu, wd)
    out_specs = [tile, wide, wide]
    out_shape = [jax.ShapeDtypeStruct((t, d), F32), jax.ShapeDtypeStruct((t, f), BF16), jax.ShapeDtypeStruct((t, f), BF16)]
    if head:
        in_specs, args = in_specs + [tile, _VM], args + (head[0], head[1])
        out_specs = out_specs + [pl.BlockSpec((SUBLANES, 128), lambda i: (0, 0)), pl.BlockSpec((1, d), lambda i: (0, 0))]
        out_shape = out_shape + [jax.ShapeDtypeStruct((SUBLANES, 128), F32), jax.ShapeDtypeStruct((1, d), F32)]
    return _grid_call(body, name, t // tm, in_specs, out_specs, out_shape, [], args, ride)


def _ffn_bwd(h, dh_out, a16, b16, g, wg, wu, wd, name, ride=None):
    t, d = h.shape
    f = wg.shape[1]
    tm = _tile(t, TOKEN_TILE)
    chunks = _chunks(f, FFN_CHUNK)

    def body(h_ref, dho_ref, a_ref, b_ref, g_ref, wg_ref, wu_ref, wd_ref, dh_ref, dg_ref, n_ref, da_ref, db_ref, s_ref, do_ref):
        @pl.when(pl.program_id(0) == 0)
        def _():
            dg_ref[...] = jnp.zeros_like(dg_ref)

        hv = h_ref[...]
        gv = g_ref[...]
        r, xhat, n32 = _rms_parts(hv, gv)
        dho = dho_ref[...]
        do = (0.5 * dho).astype(BF16)
        dn = jnp.zeros((tm, d), F32)
        for s, e in chunks:
            a = a_ref[:, s:e].astype(F32)
            b = b_ref[:, s:e].astype(F32)
            sig = jax.nn.sigmoid(a)
            sa = a * sig
            ds = _dot_nt(do, wd_ref[s:e, :])
            da = (ds * b * (sig * (1.0 + a * (1.0 - sig)))).astype(BF16)
            db = (ds * sa).astype(BF16)
            s_ref[:, s:e] = (sa * b).astype(BF16)
            da_ref[:, s:e] = da
            db_ref[:, s:e] = db
            dn = dn + _dot_nt(da, wg_ref[:, s:e]) + _dot_nt(db, wu_ref[:, s:e])
        dh_ref[...] = dho + _rms_bwd(dn, gv, r, xhat)
        dg_ref[...] += jnp.sum(dn * xhat, axis=0, keepdims=True)
        n_ref[...] = n32.astype(BF16)
        do_ref[...] = do

    tile = pl.BlockSpec((tm, d), lambda i: (i, 0))
    wide = pl.BlockSpec((tm, f), lambda i: (i, 0))
    one = pl.BlockSpec((1, d), lambda i: (0, 0))
    return _grid_call(
        body, name, t // tm, [tile, tile, wide, wide, _VM, _VM, _VM, _VM], [tile, one, tile, wide, wide, wide, tile],
        [jax.ShapeDtypeStruct((t, d), F32), jax.ShapeDtypeStruct((1, d), F32),
         jax.ShapeDtypeStruct((t, d), BF16), jax.ShapeDtypeStruct((t, f), BF16),
         jax.ShapeDtypeStruct((t, f), BF16), jax.ShapeDtypeStruct((t, f), BF16),
         jax.ShapeDtypeStruct((t, d), BF16)],
        [], (h, dh_out, a16, b16, g, wg, wu, wd), ride)


def _dw(a, b, name, ride=None):
    t, m = a.shape
    n = b.shape[1]
    bn = next(k for k in (512, 256, n) if n % k == 0)

    def body(a_ref, b_ref, o_ref):
        o_ref[...] = _dot_tn(a_ref[...], b_ref[...]).astype(BF16)

    (out,), got = _grid_call(
        body, name, n // bn, [_VM, pl.BlockSpec((t, bn), lambda j: (0, j))], [pl.BlockSpec((m, bn), lambda j: (0, j))],
        [jax.ShapeDtypeStruct((m, n), BF16)], [], (a, b), ride)
    return (out, got) if ride else out


def _to_bf16(arrays, name):
    k = len(arrays)

    def body(*refs):
        for x_ref, o_ref in zip(refs[:k], refs[k:]):
            o_ref[...] = x_ref[...].astype(BF16)

    return pl.pallas_call(
        body, name=name, out_shape=[jax.ShapeDtypeStruct(a.shape, BF16) for a in arrays],
        compiler_params=pltpu.CompilerParams(vmem_limit_bytes=VMEM_LIMIT_BYTES),
    )(*arrays)


def _s5_discretise(a_re, a_im, log_dt, b_re, b_im):
    dt = jnp.exp(log_dt)
    mag = jnp.exp(a_re * dt)
    lam_re = mag * jnp.cos(a_im * dt)
    lam_im = mag * jnp.sin(a_im * dt)
    den = a_re * a_re + a_im * a_im
    q_re = ((lam_re - 1.0) * a_re + lam_im * a_im) / den
    q_im = (lam_im * a_re - (lam_re - 1.0) * a_im) / den
    bb_re = q_re[:, None, :] * b_re - q_im[:, None, :] * b_im
    bb_im = q_re[:, None, :] * b_im + q_im[:, None, :] * b_re
    return lam_re, lam_im, bb_re, bb_im


def _s5_params_fwd(a_re, a_im, log_dt, b_re, b_im):
    g, p = a_re.shape
    c = b_re.shape[1]

    def body(are_ref, aim_ref, ldt_ref, bre_ref, bim_ref, pwr_ref, pwi_ref, bbr_ref, bbi_ref):
        lr, li, bbr, bbi = _s5_discretise(are_ref[...], aim_ref[...], ldt_ref[...], bre_ref[...], bim_ref[...])
        bbr_ref[...] = bbr
        bbi_ref[...] = bbi
        pr, pi = lr, li
        pwr_ref[0] = pr
        pwi_ref[0] = pi
        for k in range(1, SUBLANES):
            pr, pi = pr * lr - pi * li, pr * li + pi * lr
            pwr_ref[k] = pr
            pwi_ref[k] = pi

    return pl.pallas_call(
        body, name="s5_params_fwd",
        out_shape=[jax.ShapeDtypeStruct((SUBLANES, g, p), F32), jax.ShapeDtypeStruct((SUBLANES, g, p), F32),
                   jax.ShapeDtypeStruct((g, c, p), F32), jax.ShapeDtypeStruct((g, c, p), F32)],
    )(a_re, a_im, log_dt, b_re, b_im)


def _s5_params_bwd(a_re, a_im, log_dt, b_re, b_im, dlam, dbb_re, dbb_im):
    g, p = a_re.shape
    c = b_re.shape[1]

    def body(are_ref, aim_ref, ldt_ref, bre_ref, bim_ref, dlam_ref, dbr_ref, dbi_ref,
             dare_ref, daim_ref, dldt_ref, dbre_ref, dbim_ref):
        dlr = jnp.sum(dlam_ref[0], axis=0)
        dli = jnp.sum(dlam_ref[1], axis=0)
        _, vjp = jax.vjp(_s5_discretise, are_ref[...], aim_ref[...], ldt_ref[...], bre_ref[...], bim_ref[...])
        dare, daim, dldt, dbre, dbim = vjp((dlr, dli, dbr_ref[...], dbi_ref[...]))
        dare_ref[...] = dare
        daim_ref[...] = daim
        dldt_ref[...] = dldt
        dbre_ref[...] = dbre
        dbim_ref[...] = dbim

    return pl.pallas_call(
        body, name="s5_params_bwd",
        out_shape=[jax.ShapeDtypeStruct((g, p), F32), jax.ShapeDtypeStruct((g, p), F32),
                   jax.ShapeDtypeStruct((g, 1), F32), jax.ShapeDtypeStruct((g, c, p), F32),
                   jax.ShapeDtypeStruct((g, c, p), F32)],
    )(a_re, a_im, log_dt, b_re, b_im, dlam, dbb_re, dbb_im)


def _scan_chunks(gp):
    hg = gp // 2
    w = min(SCAN_LANES, hg)
    return w, [(half * hg + k * w, half * gp + k * w, half * gp + hg + k * w) for half in range(2) for k in range(hg // w)]


def _cmul_acc(xr, xi, tr, ti, sr, si):
    return xr + tr * sr - ti * si, xi + tr * si + ti * sr


def _scan_fwd(buf_ref, row0, tm, ltab_ref, cin_ref, cout_ref, gp):
    w, chunks = _scan_chunks(gp)
    for lo_t, lo_r, lo_i in chunks:
        def body(r, carry, lo_t=lo_t, lo_r=lo_r, lo_i=lo_i):
            cr, ci = carry
            row = pl.multiple_of(row0 + r * SUBLANES, SUBLANES)
            xr = buf_ref[pl.ds(row, SUBLANES), lo_r:lo_r + w]
            xi = buf_ref[pl.ds(row, SUBLANES), lo_i:lo_i + w]
            for tab, shift in ((0, 1), (2, 2), (4, 4)):
                xr, xi = _cmul_acc(xr, xi, ltab_ref[tab, :, lo_t:lo_t + w], ltab_ref[tab + 1, :, lo_t:lo_t + w],
                                   pltpu.roll(xr, shift, 0), pltpu.roll(xi, shift, 0))
            xr, xi = _cmul_acc(xr, xi, ltab_ref[6, :, lo_t:lo_t + w], ltab_ref[7, :, lo_t:lo_t + w], cr, ci)
            buf_ref[pl.ds(row, SUBLANES), lo_r:lo_r + w] = xr
            buf_ref[pl.ds(row, SUBLANES), lo_i:lo_i + w] = xi
            last = SUBLANES - 1
            return (jnp.broadcast_to(xr[last:last + 1], (SUBLANES, w)), jnp.broadcast_to(xi[last:last + 1], (SUBLANES, w)))

        cr, ci = lax.fori_loop(0, tm // SUBLANES, body,
                               (cin_ref[0:SUBLANES, lo_r:lo_r + w], cin_ref[0:SUBLANES, lo_i:lo_i + w]))
        if cout_ref is not None:
            cout_ref[0:SUBLANES, lo_r:lo_r + w] = cr
            cout_ref[0:SUBLANES, lo_i:lo_i + w] = ci


def _scan_rev(g_ref, hext_ref, tm, ltab_ref, gc_ref, dlam_ref, gp):
    w, chunks = _scan_chunks(gp)
    nb = tm // SUBLANES
    for lo_t, lo_r, lo_i in chunks:
        def body(k, carry, lo_t=lo_t, lo_r=lo_r, lo_i=lo_i):
            cr, ci, ar, ai = carry
            row = pl.multiple_of((nb - 1 - k) * SUBLANES, SUBLANES)
            xr = g_ref[pl.ds(row, SUBLANES), lo_r:lo_r + w]
            xi = g_ref[pl.ds(row, SUBLANES), lo_i:lo_i + w]
            for tab, shift in ((8, 7), (10, 6), (12, 4)):
                xr, xi = _cmul_acc(xr, xi, ltab_ref[tab, :, lo_t:lo_t + w], ltab_ref[tab + 1, :, lo_t:lo_t + w],
                                   pltpu.roll(xr, shift, 0), pltpu.roll(xi, shift, 0))
            xr, xi = _cmul_acc(xr, xi, ltab_ref[14, :, lo_t:lo_t + w], ltab_ref[15, :, lo_t:lo_t + w], cr, ci)
            g_ref[pl.ds(row, SUBLANES), lo_r:lo_r + w] = xr
            g_ref[pl.ds(row, SUBLANES), lo_i:lo_i + w] = xi
            first = lax.broadcasted_iota(jnp.int32, (SUBLANES, w), 0) == 0
            prev = pl.ds(row, SUBLANES)
            here = pl.ds(row + SUBLANES, SUBLANES)
            hpr = jnp.where(first, pltpu.roll(hext_ref[prev, lo_r:lo_r + w], 1, 0), pltpu.roll(hext_ref[here, lo_r:lo_r + w], 1, 0))
            hpi = jnp.where(first, pltpu.roll(hext_ref[prev, lo_i:lo_i + w], 1, 0), pltpu.roll(hext_ref[here, lo_i:lo_i + w], 1, 0))
            ar = ar + xr * hpr + xi * hpi
            ai = ai - xr * hpi + xi * hpr
            return (jnp.broadcast_to(xr[0:1], (SUBLANES, w)), jnp.broadcast_to(xi[0:1], (SUBLANES, w)), ar, ai)

        cr, ci, ar, ai = lax.fori_loop(
            0, nb, body, (gc_ref[:, lo_r:lo_r + w], gc_ref[:, lo_i:lo_i + w], dlam_ref[:, lo_r:lo_r + w], dlam_ref[:, lo_i:lo_i + w]))
        gc_ref[:, lo_r:lo_r + w] = cr
        gc_ref[:, lo_i:lo_i + w] = ci
        dlam_ref[:, lo_r:lo_r + w] = ar
        dlam_ref[:, lo_i:lo_i + w] = ai


def _conv_taps(cw, cext_ref, cin, tm):
    return (cw[0:1] * cext_ref[SUBLANES - 2:SUBLANES - 2 + tm, :] + cw[1:2] * cext_ref[SUBLANES - 1:SUBLANES - 1 + tm, :]
            + cw[2:3] * cin)


def _mix_fwd(h, gm, win, bg, bc, cc, dsk, wglu, cw, wco, wo, ltab, dims):
    d, ds, dc, gp = dims
    t = h.shape[0]
    tm = _tile(t, MIX_TILE)
    nt = t // tm
    dsh = ds // 2
    o1, o2, o3 = ds + dc, ds + 2 * dc, ds + 3 * dc
    ncols = o3 + 2 * d

    def body(h_ref, gm_ref, win_ref, bg_ref, bc_ref, cc_ref, dsk_ref, wglu_ref, cw_ref, wco_ref, wo_ref, ltab_ref,
             h2_ref, p_ref, hs_ref, y5_ref, z_ref, yc_ref, hbuf_ref, carry_ref, cext_ref):
        @pl.when(pl.program_id(0) == 0)
        def _():
            carry_ref[...] = jnp.zeros_like(carry_ref)
            cext_ref[0:SUBLANES, :] = jnp.zeros((SUBLANES, dc), F32)

        hv = h_ref[...]
        bg = bg_ref[...]
        u = _rms_parts(hv, gm_ref[...])[2].astype(BF16)
        us = _dot(u, win_ref[:, 0:ds])
        v = _dot(u, win_ref[:, ds:o1])
        gb = _dot(u, win_ref[:, o1:o2])
        gcv = _dot(u, win_ref[:, o2:o3])
        gs = jax.nn.sigmoid(_dot(u, win_ref[:, o3:o3 + d]) + bg[:, 0:d])
        gcg = jax.nn.sigmoid(_dot(u, win_ref[:, o3 + d:o3 + 2 * d]) + bg[:, d:2 * d])
        us16 = us.astype(BF16)
        p_ref[:, 0:ds] = us16
        p_ref[:, ds:o1] = v.astype(BF16)
        p_ref[:, o1:o2] = gb.astype(BF16)
        p_ref[:, o2:o3] = gcv.astype(BF16)
        p_ref[:, o3:o3 + d] = gs.astype(BF16)
        p_ref[:, o3 + d:ncols] = gcg.astype(BF16)
        for half in range(2):
            hbuf_ref[:, half * gp:(half + 1) * gp] = _dot(us16[:, half * dsh:(half + 1) * dsh], bc_ref[half])
        _scan_fwd(hbuf_ref, 0, tm, ltab_ref, carry_ref, carry_ref, gp)
        hs_ref[...] = hbuf_ref[...].astype(BF16)
        y5 = jnp.concatenate([_dot(hs_ref[:, half * gp:(half + 1) * gp], cc_ref[half]) for half in range(2)], axis=1) + dsk_ref[...] * us
        y5_ref[...] = y5.astype(BF16)
        z = _dot(jax.nn.gelu(y5).astype(BF16), wglu_ref[...])
        z_ref[...] = z.astype(BF16)
        ys = z[:, 0:d] * jax.nn.sigmoid(z[:, d:2 * d])
        cin = gcv * v
        cext_ref[SUBLANES:SUBLANES + tm, :] = cin
        yc = _dot((gb * _conv_taps(cw_ref[...], cext_ref, cin, tm)).astype(BF16), wco_ref[...])
        yc_ref[...] = yc.astype(BF16)
        h2_ref[...] = hv + _dot((gs * ys + gcg * yc).astype(BF16), wo_ref[...])
        cext_ref[0:SUBLANES, :] = cext_ref[tm:tm + SUBLANES, :]

    def tile(cols):
        return pl.BlockSpec((tm, cols), lambda i: (i, 0))

    def bf(cols):
        return jax.ShapeDtypeStruct((t, cols), BF16)

    return pl.pallas_call(
        body, name="mix_fwd", grid=(nt,),
        in_specs=[tile(d)] + [_VM] * 11,
        out_specs=[tile(d), tile(ncols), tile(2 * gp), tile(ds), tile(2 * d), tile(d)],
        out_shape=[jax.ShapeDtypeStruct((t, d), F32), bf(ncols), bf(2 * gp), bf(ds), bf(2 * d), bf(d)],
        scratch_shapes=[pltpu.VMEM((tm, 2 * gp), F32), pltpu.VMEM((SUBLANES, 2 * gp), F32), pltpu.VMEM((SUBLANES + tm, dc), F32)],
        compiler_params=_params(),
    )(h, gm, win, bg, bc, cc, dsk, wglu, cw, wco, wo, ltab)


HALO = 16


def _mix_bwd_gates(dh2, p16, y516, z16, yc16, wglu, cw, wco, wo, dims):
    d, ds, dc, gp = dims
    t = dh2.shape[0]
    tm = _tile(t, TOKEN_TILE)
    nt = t // tm
    o1, o2, o3 = ds + dc, ds + 2 * dc, ds + 3 * dc
    ncols = o3 + 2 * d

    def body(dh2_ref, p_ref, halo_ref, y5_ref, z_ref, yc_ref, wglu_ref, cw_ref, wco_ref, wo_ref,
             dp_ref, dy5_ref, ge_ref, dz_ref, cg_ref, dyc_ref, mx_ref, dh216_ref, dbg_ref, dcw_ref, cext_ref, dcvext_ref):
        j = pl.program_id(0)

        @pl.when(j == 0)
        def _():
            dbg_ref[...] = jnp.zeros_like(dbg_ref)
            dcw_ref[...] = jnp.zeros_like(dcw_ref)
            dcvext_ref[tm:tm + SUBLANES, :] = jnp.zeros((SUBLANES, dc), F32)

        before = halo_ref[:, o2:o3].astype(F32) * halo_ref[:, ds:o1].astype(F32)
        cext_ref[0:SUBLANES, :] = jnp.where(j == nt - 1, 0.0, before[HALO - SUBLANES:HALO])
        cw_v = cw_ref[...]
        v = p_ref[:, ds:o1].astype(F32)
        gb = p_ref[:, o1:o2].astype(F32)
        gcv = p_ref[:, o2:o3].astype(F32)
        gs = p_ref[:, o3:o3 + d].astype(F32)
        gcg = p_ref[:, o3 + d:ncols].astype(F32)
        z1 = z_ref[:, 0:d].astype(F32)
        sz = jax.nn.sigmoid(z_ref[:, d:2 * d].astype(F32))
        ys = z1 * sz
        yc = yc_ref[...].astype(F32)
        ge, gelu_vjp = jax.vjp(jax.nn.gelu, y5_ref[...].astype(F32))
        cin = gcv * v
        cext_ref[SUBLANES:SUBLANES + tm, :] = cin
        cv = _conv_taps(cw_v, cext_ref, cin, tm)

        dh216 = dh2_ref[...].astype(BF16)
        dmixed = _dot_nt(dh216, wo_ref[...])
        dys = dmixed * gs
        dyc16 = (dmixed * gcg).astype(BF16)
        dpgs = dmixed * ys * gs * (1.0 - gs)
        dpgc = dmixed * yc * gcg * (1.0 - gcg)
        dz16 = jnp.concatenate([dys * sz, dys * z1 * sz * (1.0 - sz)], axis=1).astype(BF16)
        dy5_ref[...] = gelu_vjp(_dot_nt(dz16, wglu_ref[...]))[0].astype(BF16)
        dcg = _dot_nt(dyc16, wco_ref[...])
        dcv = dcg * gb
        dcvext_ref[0:tm, :] = dcv
        dcin = cw_v[2:3] * dcv + cw_v[1:2] * dcvext_ref[1:1 + tm, :] + cw_v[0:1] * dcvext_ref[2:2 + tm, :]
        dcw_ref[0:1, :] += jnp.sum(dcv * cext_ref[SUBLANES - 2:SUBLANES - 2 + tm, :], axis=0, keepdims=True)
        dcw_ref[1:2, :] += jnp.sum(dcv * cext_ref[SUBLANES - 1:SUBLANES - 1 + tm, :], axis=0, keepdims=True)
        dcw_ref[2:3, :] += jnp.sum(dcv * cin, axis=0, keepdims=True)
        dcvext_ref[tm:tm + SUBLANES, :] = dcvext_ref[0:SUBLANES, :]
        dbg_ref[...] += jnp.concatenate([jnp.sum(dpgs, axis=0, keepdims=True), jnp.sum(dpgc, axis=0, keepdims=True)], axis=1)
        dp_ref[:, 0:ds] = jnp.zeros((tm, ds), BF16)
        dp_ref[:, ds:o1] = (dcin * gcv).astype(BF16)
        dp_ref[:, o1:o2] = (dcg * cv).astype(BF16)
        dp_ref[:, o2:o3] = (dcin * v).astype(BF16)
        dp_ref[:, o3:o3 + d] = dpgs.astype(BF16)
        dp_ref[:, o3 + d:ncols] = dpgc.astype(BF16)
        ge_ref[...] = ge.astype(BF16)
        dz_ref[...] = dz16
        cg_ref[...] = (gb * cv).astype(BF16)
        dyc_ref[...] = dyc16
        mx_ref[...] = (gs * ys + gcg * yc).astype(BF16)
        dh216_ref[...] = dh216

    def rev(cols):
        return pl.BlockSpec((tm, cols), lambda j: (nt - 1 - j, 0))

    def bf(cols):
        return jax.ShapeDtypeStruct((t, cols), BF16)

    halo = pl.BlockSpec((HALO, ncols), lambda j: (jnp.maximum((nt - 1 - j) * (tm // HALO) - 1, 0), 0))
    return pl.pallas_call(
        body, name="mix_bwd_gates", grid=(nt,),
        in_specs=[rev(d), rev(ncols), halo, rev(ds), rev(2 * d), rev(d), _VM, _VM, _VM, _VM],
        out_specs=[rev(ncols), rev(ds), rev(ds), rev(2 * d), rev(dc), rev(d), rev(d), rev(d), _VM, _VM],
        out_shape=[bf(ncols), bf(ds), bf(ds), bf(2 * d), bf(dc), bf(d), bf(d), bf(d),
                   jax.ShapeDtypeStruct((1, 2 * d), F32), jax.ShapeDtypeStruct((SUBLANES, dc), F32)],
        scratch_shapes=[pltpu.VMEM((SUBLANES + tm, dc), F32), pltpu.VMEM((tm + SUBLANES, dc), F32)],
        compiler_params=_params(),
    )(dh2, p16, p16, y516, z16, yc16, wglu, cw, wco, wo)


def _mix_bwd_scan(dy516, hs16, p16, bc, cc, dsk, ltab, dims):
    d, ds, dc, gp = dims
    t = dy516.shape[0]
    tm = _tile(t, TOKEN_TILE)
    nt = t // tm
    dsh = ds // 2

    def body(dy5_ref, hs_ref, halo_ref, us_ref, bc_ref, cc_ref, dsk_ref, ltab_ref,
             dus_ref, ddsk_ref, dlam_ref, dbc_ref, dcc_ref, hext_ref, gbuf_ref, gcarry_ref):
        j = pl.program_id(0)

        @pl.when(j == 0)
        def _():
            for ref in (ddsk_ref, dlam_ref, dbc_ref, dcc_ref, gcarry_ref):
                ref[...] = jnp.zeros_like(ref)

        before = jnp.where(j == nt - 1, 0.0, halo_ref[...].astype(F32)[HALO - 1:HALO])
        hext_ref[0:SUBLANES, :] = jnp.broadcast_to(before, (SUBLANES, 2 * gp))
        hext_ref[SUBLANES:SUBLANES + tm, :] = hs_ref[...].astype(F32)
        dy516v = dy5_ref[...]
        for half in range(2):
            gbuf_ref[:, half * gp:(half + 1) * gp] = _dot_nt(dy516v[:, half * dsh:(half + 1) * dsh], cc_ref[half])
        _scan_rev(gbuf_ref, hext_ref, tm, ltab_ref, gcarry_ref, dlam_ref, gp)
        dus = []
        for half in range(2):
            g16 = gbuf_ref[:, half * gp:(half + 1) * gp].astype(BF16)
            dus.append(_dot_nt(g16, bc_ref[half]))
            dbc_ref[half] += _dot_tn(us_ref[:, half * dsh:(half + 1) * dsh], g16)
            dcc_ref[half] += _dot_tn(hs_ref[:, half * gp:(half + 1) * gp], dy516v[:, half * dsh:(half + 1) * dsh])
        dy5 = dy516v.astype(F32)
        dus_ref[...] = (jnp.concatenate(dus, axis=1) + dsk_ref[...] * dy5).astype(BF16)
        ddsk_ref[...] += jnp.sum(dy5 * us_ref[...].astype(F32), axis=0, keepdims=True)

    def rev(cols):
        return pl.BlockSpec((tm, cols), lambda j: (nt - 1 - j, 0))

    halo = pl.BlockSpec((HALO, 2 * gp), lambda j: (jnp.maximum((nt - 1 - j) * (tm // HALO) - 1, 0), 0))
    return pl.pallas_call(
        body, name="mix_bwd_scan", grid=(nt,),
        in_specs=[rev(ds), rev(2 * gp), halo, rev(ds), _VM, _VM, _VM, _VM],
        out_specs=[rev(ds), _VM, _VM, _VM, _VM],
        out_shape=[jax.ShapeDtypeStruct((t, ds), BF16), jax.ShapeDtypeStruct((1, ds), F32),
                   jax.ShapeDtypeStruct((SUBLANES, 2 * gp), F32),
                   jax.ShapeDtypeStruct((2, dsh, gp), F32), jax.ShapeDtypeStruct((2, gp, dsh), F32)],
        scratch_shapes=[pltpu.VMEM((SUBLANES + tm, 2 * gp), F32), pltpu.VMEM((tm, 2 * gp), F32), pltpu.VMEM((SUBLANES, 2 * gp), F32)],
        compiler_params=_params(),
    )(dy516, hs16, hs16, p16, bc, cc, dsk, ltab)


def _mix_bwd_in(h, dh2, dp16, dus16, gm, win, dims):
    d, ds, dc, gp = dims
    t = h.shape[0]
    tm = _tile(t, TOKEN_TILE)
    ncols = dp16.shape[1]

    def body(h_ref, dh2_ref, dp_ref, dus_ref, gm_ref, win_ref, dh1_ref, u_ref, dpf_ref, dgm_ref):
        @pl.when(pl.program_id(0) == 0)
        def _():
            dgm_ref[...] = jnp.zeros_like(dgm_ref)

        gmv = gm_ref[...]
        r, xhat, n32 = _rms_parts(h_ref[...], gmv)
        du = _dot_nt(dus_ref[...], win_ref[:, 0:ds]) + _dot_nt(dp_ref[:, ds:ncols], win_ref[:, ds:ncols])
        dh1_ref[...] = dh2_ref[...] + _rms_bwd(du, gmv, r, xhat)
        dgm_ref[...] += jnp.sum(du * xhat, axis=0, keepdims=True)
        u_ref[...] = n32.astype(BF16)
        dpf_ref[:, 0:ds] = dus_ref[...]
        dpf_ref[:, ds:ncols] = dp_ref[:, ds:ncols]

    def tile(cols):
        return pl.BlockSpec((tm, cols), lambda i: (i, 0))

    return pl.pallas_call(
        body, name="mix_bwd_in", grid=(t // tm,),
        in_specs=[tile(d), tile(d), tile(ncols), tile(ds), _VM, _VM],
        out_specs=[tile(d), tile(d), tile(ncols), pl.BlockSpec((1, d), lambda i: (0, 0))],
        out_shape=[jax.ShapeDtypeStruct((t, d), F32), jax.ShapeDtypeStruct((t, d), BF16),
                   jax.ShapeDtypeStruct((t, ncols), BF16), jax.ShapeDtypeStruct((1, d), F32)],
        compiler_params=_params(),
    )(h, dh2, dp16, dus16, gm, win)


def _pad_rows(a, rows, axis=0):
    pad = [(0, 0)] * a.ndim
    pad[axis] = (0, rows - a.shape[axis])
    return jnp.pad(a, pad)


def _as_rows(a):
    flat = a.reshape(-1)
    n = -(-flat.shape[0] // SLAB_COLS) * SLAB_COLS
    return jnp.pad(flat, (0, n - flat.shape[0])).reshape(-1, SLAB_COLS)


def _pack(arrs):
    rows = jnp.concatenate([_as_rows(a) for a in arrs], axis=0)
    return _pad_rows(rows, -(-rows.shape[0] // 16) * 16)


def _unpack(slab, shapes):
    out, r = [], 0
    for shp in shapes:
        size = 1
        for s in shp:
            size *= s
        n = -(-size // SLAB_COLS)
        out.append(slab[r:r + n].reshape(-1)[:size].reshape(shp))
        r += n
    return out


def _block_diag(blocks):
    n, a, b = blocks.shape
    eye = jnp.eye(n, dtype=blocks.dtype)
    return (blocks[:, :, None, :] * eye[:, None, :, None]).reshape(n * a, n * b)


def _diag_blocks(mat, n):
    a, b = mat.shape[0] // n, mat.shape[1] // n
    eye = jnp.eye(n, dtype=mat.dtype)
    return jnp.sum(mat.reshape(n, a, n, b) * eye[:, None, :, None], axis=2)


BIG = (("ffn1_w_gate", "col"), ("ffn1_w_up", "col"), ("ffn1_w_down", "row"), ("w_in", "col"), ("ssm_w_glu", "col"),
       ("conv_w_out", "col"), ("w_o", "row"), ("ffn2_w_gate", "col"), ("ffn2_w_up", "col"), ("ffn2_w_down", "row"))
REPLICATED = ("g_ffn1", "g_mix", "b_gate", "ssm_a_re", "ssm_a_im", "ssm_log_dt", "ssm_b_re", "ssm_b_im", "ssm_c_re",
              "ssm_c_im", "ssm_d", "g_ffn2", "g_final")
WEIGHTS = ("meta_tokens", "g_ffn1", "ffn1_w_gate", "ffn1_w_up", "ffn1_w_down", "g_mix", "w_in", "b_gate", "ssm_a_re",
           "ssm_a_im", "ssm_log_dt", "ssm_b_re", "ssm_b_im", "ssm_c_re", "ssm_c_im", "ssm_d", "ssm_w_glu", "conv_w",
           "conv_w_out", "w_o", "g_ffn2", "ffn2_w_gate", "ffn2_w_up", "ffn2_w_down", "g_final")
N_EARLY = 3


def _full_from_blocks(blocks, kind):
    n, r, c = blocks.shape
    if kind == "col":
        return jnp.transpose(blocks, (1, 0, 2)).reshape(r, n * c)
    return blocks.reshape(n * r, c)


def _blocks_from_full(full, kind):
    if kind == "col":
        r, nc = full.shape
        return jnp.transpose(full.reshape(r, NDEV, nc // NDEV), (1, 0, 2))
    nr, c = full.shape
    return full.reshape(NDEV, nr // NDEV, c)


def kernel(x, meta_tokens, g_ffn1, ffn1_w_gate, ffn1_w_up, ffn1_w_down, g_mix, w_in, b_gate, ssm_a_re, ssm_a_im, ssm_log_dt, ssm_b_re, ssm_b_im, ssm_c_re, ssm_c_im, ssm_d, ssm_w_glu, conv_w, conv_w_out, w_o, g_ffn2, ffn2_w_gate, ffn2_w_up, ffn2_w_down, g_final, loss_target, m_meta_tokens, m_g_ffn1, m_ffn1_w_gate, m_ffn1_w_up, m_ffn1_w_down, m_g_mix, m_w_in, m_b_gate, m_ssm_a_re, m_ssm_a_im, m_ssm_log_dt, m_ssm_b_re, m_ssm_b_im, m_ssm_c_re, m_ssm_c_im, m_ssm_d, m_ssm_w_glu, m_conv_w, m_conv_w_out, m_w_o, m_g_ffn2, m_ffn2_w_gate, m_ffn2_w_up, m_ffn2_w_down, m_g_final, v_meta_tokens, v_g_ffn1, v_ffn1_w_gate, v_ffn1_w_up, v_ffn1_w_down, v_g_mix, v_w_in, v_b_gate, v_ssm_a_re, v_ssm_a_im, v_ssm_log_dt, v_ssm_b_re, v_ssm_b_im, v_ssm_c_re, v_ssm_c_im, v_ssm_d, v_ssm_w_glu, v_conv_w, v_conv_w_out, v_w_o, v_g_ffn2, v_ffn2_w_gate, v_ffn2_w_up, v_ffn2_w_down, v_g_final):
    args = dict(locals())
    w = {n: args[n] for n in WEIGHTS}
    mom_m = {n: args["m_" + n] for n in WEIGHTS}
    mom_v = {n: args["v_" + n] for n in WEIGHTS}

    seq, d = x.shape[1], x.shape[2]
    n_meta = meta_tokens.shape[0]
    ds = ssm_d.shape[1]
    n_grp, n_state = ssm_a_re.shape[1], ssm_a_re.shape[2]
    gp = n_grp * n_state
    dc = conv_w.shape[3] * NDEV
    dims = (d, ds, dc, gp)
    t_real = n_meta + seq
    t_pad = -(-t_real // ROW_ALIGN) * ROW_ALIGN
    me_chip = 2 * lax.axis_index("x") + lax.axis_index("y")
    me_core = lax.axis_index("c")
    me = 2 * me_chip + me_core
    mcols, ccols = d // NDEV, dc // NDEV

    cw_shard = _pad_rows(_pad_rows(conv_w.reshape(3, ccols), SUBLANES), 128, axis=1)
    shard16 = dict(zip([name for name, _ in BIG], _to_bf16([w[name][0] for name, _ in BIG], "weights_to_bf16")))
    early, late = BIG[:N_EARLY], BIG[N_EARLY:]
    got = _exchange(_gather_ride([shard16[name] for name, _ in early] + [meta_tokens, cw_shard]), "gather_first")
    full = {name: _full_from_blocks(got[i], kind) for i, (name, kind) in enumerate(early)}
    meta_full = _full_from_blocks(got[-2], "col")
    cw_rows = _pad_rows(_full_from_blocks(got[-1][:, 0:3, 0:ccols], "col"), SUBLANES)

    a_re, a_im, ldt = ssm_a_re[0], ssm_a_im[0], ssm_log_dt[0].reshape(n_grp, 1)
    b_re_t = jnp.transpose(ssm_b_re[0], (0, 2, 1))
    b_im_t = jnp.transpose(ssm_b_im[0], (0, 2, 1))
    pw_r, pw_i, bb_r, bb_i = _s5_params_fwd(a_re, a_im, ldt, b_re_t, b_im_t)
    pw_r = pw_r.reshape(SUBLANES, gp)
    pw_i = pw_i.reshape(SUBLANES, gp)
    sub = jnp.arange(SUBLANES)[:, None]

    def fwd_tab(p, k):
        return jnp.where(sub >= k, p[k - 1][None, :], 0.0)

    def rev_tab(p, k):
        return jnp.where(sub <= SUBLANES - 1 - k, p[k - 1][None, :], 0.0)

    ltab = jnp.stack(
        [fwd_tab(pw_r, 1), fwd_tab(pw_i, 1), fwd_tab(pw_r, 2), fwd_tab(pw_i, 2), fwd_tab(pw_r, 4), fwd_tab(pw_i, 4), pw_r, pw_i,
         rev_tab(pw_r, 1), -rev_tab(pw_i, 1), rev_tab(pw_r, 2), -rev_tab(pw_i, 2), rev_tab(pw_r, 4), -rev_tab(pw_i, 4),
         pw_r[::-1], -pw_i[::-1]], axis=0)
    gh = n_grp // 2
    bc = jnp.stack([jnp.concatenate([_block_diag(bb_r[h * gh:(h + 1) * gh]), _block_diag(bb_i[h * gh:(h + 1) * gh])], axis=1)
                    for h in range(2)]).astype(BF16)
    c_re_t = jnp.transpose(ssm_c_re[0], (0, 2, 1))
    c_im_t = jnp.transpose(ssm_c_im[0], (0, 2, 1))
    cc = jnp.stack([jnp.concatenate([_block_diag(c_re_t[h * gh:(h + 1) * gh]), -_block_diag(c_im_t[h * gh:(h + 1) * gh])], axis=0)
                    for h in range(2)]).astype(BF16)

    zpad = jnp.zeros((t_pad - t_real, d), F32)
    h0 = jnp.concatenate([meta_full, x[0], zpad], axis=0)
    tgt = jnp.concatenate([jnp.zeros((n_meta, d), F32), loss_target[0], zpad], axis=0)
    (h1, a1, b1), got = _ffn_fwd(h0, g_ffn1, full["ffn1_w_gate"], full["ffn1_w_up"], full["ffn1_w_down"], "ffn1_fwd",
                                 ride=_gather_ride([shard16[name] for name, _ in late]))
    full.update({name: _full_from_blocks(got[i], kind) for i, (name, kind) in enumerate(late)})
    h2, *saved = _mix_fwd(h1, g_mix, full["w_in"], b_gate, bc, cc, ssm_d, full["ssm_w_glu"], cw_rows, full["conv_w_out"],
                          full["w_o"], ltab, dims)
    (dh3, a2, b2, loss_blk, dg_final), _ = _ffn_fwd(h2, g_ffn2, full["ffn2_w_gate"], full["ffn2_w_up"], full["ffn2_w_down"], "ffn2_fwd",
                                                 head=(tgt, g_final.reshape(1, d), n_meta, t_real))
    loss = lax.psum(loss_blk[0, 0], AXES)

    (dh2, dg_ffn2, n2, da2, db2, s2, do2), _ = _ffn_bwd(
        h2, dh3, a2, b2, g_ffn2, full["ffn2_w_gate"], full["ffn2_w_up"], full["ffn2_w_down"], "ffn2_bwd")
    p16, hs16, y516, z16, yc16 = saved
    dp_part, dy516, ge16, dz16, cg16, dyc16, mx16, dh216, dbg, dcw = _mix_bwd_gates(
        dh2, p16, y516, z16, yc16, full["ssm_w_glu"], cw_rows, full["conv_w_out"], full["w_o"], dims)
    dus16, ddsk, dlam, dbc, dcc = _mix_bwd_scan(dy516, hs16, p16, bc, cc, ssm_d, ltab, dims)
    dh1, u16, dp16, dg_mix = _mix_bwd_in(h1, dh2, dp_part, dus16, g_mix, full["w_in"], dims)
    dblocks = {
        "w_in": _blocks_from_full(_dw(u16, dp16, "dw_in"), "col"),
        "ssm_w_glu": _blocks_from_full(_dw(ge16, dz16, "dw_glu"), "col"),
        "conv_w_out": _blocks_from_full(_dw(cg16, dyc16, "dw_conv_out"), "col"),
        "w_o": _blocks_from_full(_dw(mx16, dh216, "dw_o"), "row"),
        "ffn2_w_gate": _blocks_from_full(_dw(n2, da2, "dw_ffn2_gate"), "col"),
        "ffn2_w_up": _blocks_from_full(_dw(n2, db2, "dw_ffn2_up"), "col"),
        "ffn2_w_down": jnp.transpose(_blocks_from_full(_dw(do2, s2, "dw_ffn2_down"), "col"), (0, 2, 1)),
    }

    def pair_sums(names, tag):
        gs = [dblocks[name] for name in names]
        from_sibling = _exchange(_pair_ride(gs), "reduce_pair_" + tag)
        return [_add_pairs(g, me_core, b, "reduce_pair_add_" + name) for g, b, name in zip(gs, from_sibling, names)]

    late_names = [name for name, _ in late]
    pairs = dict(zip(late_names, pair_sums(late_names, "late")))
    (dh0, dg_ffn1, n1, da1, db1, s1, do1), got = _ffn_bwd(
        h0, dh1, a1, b1, g_ffn1, full["ffn1_w_gate"], full["ffn1_w_up"], full["ffn1_w_down"], "ffn1_bwd",
        ride=_chips_ride([pairs[name] for name in late_names]))
    from_chips = dict(zip(late_names, got))
    dlam4 = dlam.reshape(SUBLANES, 2, 2, gh, n_state)
    dlam_in = jnp.transpose(dlam4, (2, 0, 1, 3, 4)).reshape(2, SUBLANES, n_grp, n_state)
    hg = gp // 2
    dbb_r = jnp.concatenate([_diag_blocks(dbc[h][:, :hg], gh) for h in range(2)], axis=0)
    dbb_i = jnp.concatenate([_diag_blocks(dbc[h][:, hg:], gh) for h in range(2)], axis=0)
    da_re, da_im, dldt, dbre_t, dbim_t = _s5_params_bwd(a_re, a_im, ldt, b_re_t, b_im_t, dlam_in, dbb_r, dbb_i)
    dc_re = jnp.concatenate([_diag_blocks(dcc[h][:hg], gh) for h in range(2)], axis=0)
    dc_im = -jnp.concatenate([_diag_blocks(dcc[h][hg:], gh) for h in range(2)], axis=0)

    grads_rep = {
        "g_ffn1": dg_ffn1, "g_mix": dg_mix, "b_gate": dbg, "ssm_a_re": da_re[None], "ssm_a_im": da_im[None],
        "ssm_log_dt": dldt.reshape(1, n_grp), "ssm_b_re": jnp.transpose(dbre_t, (0, 2, 1))[None],
        "ssm_b_im": jnp.transpose(dbim_t, (0, 2, 1))[None], "ssm_c_re": jnp.transpose(dc_re, (0, 2, 1))[None],
        "ssm_c_im": jnp.transpose(dc_im, (0, 2, 1))[None], "ssm_d": ddsk, "g_ffn2": dg_ffn2, "g_final": dg_final.reshape(d),
    }

    rep_shapes = [w[n].shape for n in REPLICATED]
    small_g_shapes = rep_shapes + [(n_meta, d), (3, dc)]
    gsmall = _pack([grads_rep[n] for n in REPLICATED] + [dh0[0:n_meta], dcw[0:3]])
    dw_gate, (gall,) = _dw(n1, da1, "dw_ffn1_gate", ride=_gather_ride([gsmall]))
    dblocks.update({
        "ffn1_w_gate": _blocks_from_full(dw_gate, "col"),
        "ffn1_w_up": _blocks_from_full(_dw(n1, db1, "dw_ffn1_up"), "col"),
        "ffn1_w_down": jnp.transpose(_blocks_from_full(_dw(do1, s1, "dw_ffn1_down"), "col"), (0, 2, 1)),
    })
    early_names = [name for name, _ in early]
    pairs.update(zip(early_names, pair_sums(early_names, "early")))
    from_chips.update(zip(early_names, _exchange(_chips_ride([pairs[name] for name in early_names]), "reduce_chips_early")))

    out_g, out_d, out_m, out_v = {}, {}, {}, {}
    for name, _ in BIG:
        fc = from_chips[name]
        out_g[name], out_d[name], out_m[name], out_v[name] = _adamw(
            w[name], mom_m[name], mom_v[name], [(pairs[name], None), (fc, 0), (fc, 1), (fc, 2)], me_chip, "adamw_" + name)

    zer = [jnp.zeros((n_meta, d), F32), jnp.zeros((3, dc), F32)]
    gr, dr, mr, vr = [o[0] for o in _adamw(
        _pack([w[n] for n in REPLICATED] + zer)[None], _pack([mom_m[n] for n in REPLICATED] + zer)[None],
        _pack([mom_v[n] for n in REPLICATED] + zer)[None], [(gall, b) for b in range(NDEV)], None, "adamw_replicated")]
    g_list = _unpack(gr, small_g_shapes)
    out_g.update(zip(REPLICATED, g_list[:len(REPLICATED)]))
    out_d.update(zip(REPLICATED, _unpack(dr, rep_shapes)))
    out_m.update(zip(REPLICATED, _unpack(mr, rep_shapes)))
    out_v.update(zip(REPLICATED, _unpack(vr, rep_shapes)))

    g_meta = lax.dynamic_slice_in_dim(g_list[-2], me * mcols, mcols, axis=1)
    g_cw = lax.dynamic_slice_in_dim(g_list[-1], me * ccols, ccols, axis=1).reshape(conv_w.shape)
    tiny = ("meta_tokens", "conv_w")
    tiny_shapes = [meta_tokens.shape, conv_w.shape]
    gt, dt_, mt, vt = [o[0] for o in _adamw(
        _pack([w[n] for n in tiny])[None], _pack([mom_m[n] for n in tiny])[None], _pack([mom_v[n] for n in tiny])[None],
        [(_pack([g_meta, g_cw])[None], 0)], None, "adamw_tiny")]
    out_g.update(zip(tiny, _unpack(gt, tiny_shapes)))
    out_d.update(zip(tiny, _unpack(dt_, tiny_shapes)))
    out_m.update(zip(tiny, _unpack(mt, tiny_shapes)))
    out_v.update(zip(tiny, _unpack(vt, tiny_shapes)))

    grad_x = dh0[n_meta:t_real][None]
    return (loss, grad_x, *[out_g[n] for n in WEIGHTS], *[out_d[n] for n in WEIGHTS],
            *[out_m[n] for n in WEIGHTS], *[out_v[n] for n in WEIGHTS])
```

```python
import functools

import jax
import jax.numpy as jnp
from jax import lax
from jax.experimental import pallas as pl
from jax.experimental.pallas import tpu as pltpu

F32 = jnp.float32
BF16 = jnp.bfloat16
MESH = pl.DeviceIdType.MESH
NDEV = 8
SLAB_COLS = 1024
RMS_EPS = 1e-6
TOKEN_TILE = 320
MIX_TILE = 320
ROW_ALIGN = 128
SUBLANES = 8
SCAN_LANES = 512
FFN_CHUNK = 4096
VMEM_LIMIT_BYTES = 56 * 1024 * 1024

ADAM_LR = 0.001
ADAM_B1 = 0.9
ADAM_B2 = 0.999
ADAM_EPS = 1e-08
ADAM_WD = 0.01
ADAM_STEP = 10

_VM = pl.BlockSpec(memory_space=pltpu.VMEM)
_ANY = pl.BlockSpec(memory_space=pl.ANY)


def _params(sem=("arbitrary",)):
    return pltpu.CompilerParams(dimension_semantics=sem, vmem_limit_bytes=VMEM_LIMIT_BYTES)


def _dot(a, b):
    return jnp.dot(a, b, preferred_element_type=F32)


def _dot_nt(a, b):
    return lax.dot_general(a, b, (((1,), (1,)), ((), ())), preferred_element_type=F32)


def _dot_tn(a, b):
    return lax.dot_general(a, b, (((0,), (0,)), ((), ())), preferred_element_type=F32)


def _tile(rows, most):
    return next(k for k in range(most - most % 16, 0, -16) if rows % k == 0)


def _chunks(n, step):
    return [(s, min(s + step, n)) for s in range(0, n, step)]


def _gather_plan(x_refs, out_refs, send_sems, recv_sems, local_sems):
    n = len(x_refs)
    x, y, c = lax.axis_index("x"), lax.axis_index("y"), lax.axis_index("c")
    me, sibling = (x, y, c), (x, y, 1 - c)
    chips = [(1 - x, y), (x, 1 - y), (1 - x, 1 - y)]

    def copy(i, k, block, to, src=None):
        slot = out_refs[i].at[4 * block[0] + 2 * block[1] + block[2]]
        return pltpu.make_async_remote_copy(
            src_ref=slot if src is None else src, dst_ref=slot,
            send_sem=send_sems.at[7 * i + k], recv_sem=recv_sems.at[7 * i + k], device_id=to, device_id_type=MESH)

    def mine():
        return [pltpu.make_async_copy(x_refs[i], out_refs[i].at[4 * x + 2 * y + c], local_sems.at[i]) for i in range(n)]

    def first():
        out = []
        for i in range(n):
            out.append(copy(i, 0, me, sibling, src=x_refs[i]))
            out += [copy(i, 1 + j, me, (*chip, c), src=x_refs[i]) for j, chip in enumerate(chips)]
        return out

    def start():
        for cp in mine() + first():
            cp.start()

    def finish():
        passed = []
        for j, chip in enumerate(chips):
            for i in range(n):
                copy(i, 1 + j, (*chip, c), me).wait_recv()
                cp = copy(i, 4 + j, (*chip, c), sibling)
                cp.start()
                passed.append(cp)
        for i in range(n):
            copy(i, 0, sibling, me).wait_recv()
            for j, chip in enumerate(chips):
                copy(i, 4 + j, (*chip, 1 - c), me).wait_recv()
        for cp in first() + passed:
            cp.wait_send()
        for cp in mine():
            cp.wait()

    return start, finish


def _pair_plan(g_refs, out_refs, send_sems, recv_sems):
    x, y, c = lax.axis_index("x"), lax.axis_index("y"), lax.axis_index("c")

    def copies():
        return [pltpu.make_async_remote_copy(
            src_ref=g_refs[i].at[2 * j + (1 - c)], dst_ref=out_refs[i].at[j],
            send_sem=send_sems.at[4 * i + j], recv_sem=recv_sems.at[4 * i + j],
            device_id=(x, y, 1 - c), device_id_type=MESH) for i in range(len(g_refs)) for j in range(4)]

    def start():
        for cp in copies():
            cp.start()

    def finish():
        for cp in copies():
            cp.wait()

    return start, finish


def _chips_plan(p_refs, out_refs, send_sems, recv_sems):
    x, y, c = lax.axis_index("x"), lax.axis_index("y"), lax.axis_index("c")

    def copies():
        return [pltpu.make_async_remote_copy(
            src_ref=p_refs[i].at[2 * px + py], dst_ref=out_refs[i].at[k],
            send_sem=send_sems.at[3 * i + k], recv_sem=recv_sems.at[3 * i + k],
            device_id=(px, py, c), device_id_type=MESH)
            for i in range(len(p_refs)) for k, (px, py) in enumerate([(1 - x, y), (x, 1 - y), (1 - x, 1 - y)])]

    def start():
        for cp in copies():
            cp.start()

    def finish():
        for cp in copies():
            cp.wait()

    return start, finish


def _gather_ride(shards):
    n = len(shards)
    return dict(plan=_gather_plan, arrays=list(shards),
                out_shape=[jax.ShapeDtypeStruct((NDEV, *s.shape), s.dtype) for s in shards],
                sems=[pltpu.SemaphoreType.DMA((7 * n,)), pltpu.SemaphoreType.DMA((7 * n,)), pltpu.SemaphoreType.DMA((n,))])


def _pair_ride(blocks):
    n = len(blocks)
    return dict(plan=_pair_plan, arrays=list(blocks),
                out_shape=[jax.ShapeDtypeStruct((4, *b.shape[1:]), b.dtype) for b in blocks],
                sems=[pltpu.SemaphoreType.DMA((4 * n,)), pltpu.SemaphoreType.DMA((4 * n,))])


def _chips_ride(partials):
    n = len(partials)
    return dict(plan=_chips_plan, arrays=list(partials),
                out_shape=[jax.ShapeDtypeStruct((3, *p.shape[1:]), p.dtype) for p in partials],
                sems=[pltpu.SemaphoreType.DMA((3 * n,)), pltpu.SemaphoreType.DMA((3 * n,))])


def _exchange(ride, name):
    n = len(ride["arrays"])

    def body(*refs):
        start, finish = ride["plan"](refs[:n], refs[n:2 * n], *refs[2 * n:])
        start()
        finish()

    return pl.pallas_call(
        body, name=name, out_shape=ride["out_shape"], in_specs=[_ANY] * n, out_specs=[_ANY] * n, scratch_shapes=ride["sems"],
    )(*ride["arrays"])


def _grid_call(body, name, steps, in_specs, out_specs, out_shape, scratch_shapes, args, ride=None):
    if ride is None:
        outs = pl.pallas_call(body, name=name, grid=(steps,), in_specs=in_specs, out_specs=out_specs, out_shape=out_shape,
                              scratch_shapes=scratch_shapes, compiler_params=_params())(*args)
        return list(outs), []
    n_in, n_out, n_scr, n_ride, n_sems = len(in_specs), len(out_specs), len(scratch_shapes), len(ride["arrays"]), len(ride["sems"])

    def carrying(*refs):
        ins, r_in = refs[:n_in], refs[n_in:n_in + n_ride]
        o0 = n_in + n_ride
        outs, r_out = refs[o0:o0 + n_out], refs[o0 + n_out:o0 + n_out + n_ride]
        s0 = o0 + n_out + n_ride
        scratch, sems = refs[s0:s0 + n_scr], refs[s0 + n_scr:s0 + n_scr + n_sems]
        start, finish = ride["plan"](r_in, r_out, *sems)
        pl.when(pl.program_id(0) == 0)(start)
        body(*ins, *outs, *scratch)
        pl.when(pl.program_id(0) == steps - 1)(finish)

    outs = pl.pallas_call(
        carrying, name=name, grid=(steps,), in_specs=list(in_specs) + [_ANY] * n_ride, out_specs=list(out_specs) + [_ANY] * n_ride,
        out_shape=list(out_shape) + ride["out_shape"], scratch_shapes=list(scratch_shapes) + ride["sems"],
        compiler_params=_params())(*args, *ride["arrays"])
    return list(outs[:n_out]), list(outs[n_out:])


def _row_block(rows):
    return rows if rows <= 512 else next(k for k in (512, 256, 128, rows) if rows % k == 0)


def _add_pairs(gs, core, b, name):
    k, r, n = b.shape
    tr = _row_block(r)

    def body(core_ref, a_ref, b_ref, o_ref):
        o_ref[0] = (a_ref[0, 0].astype(F32) + b_ref[0].astype(F32)).astype(o_ref.dtype)

    spec = pl.BlockSpec((1, tr, n), lambda j, i, c: (j, i, 0))
    return pl.pallas_call(
        body, name=name,
        grid_spec=pltpu.PrefetchScalarGridSpec(
            num_scalar_prefetch=1, grid=(k, r // tr),
            in_specs=[pl.BlockSpec((1, 1, tr, n), lambda j, i, c: (j, c[0], i, 0)), spec], out_specs=spec),
        out_shape=jax.ShapeDtypeStruct(b.shape, b.dtype), compiler_params=_params(("arbitrary", "arbitrary")),
    )(core.reshape(1), gs.reshape(k, 2, r, n), b)


def _adamw(w, m, v, parts, sel, name):
    _, r, n = w.shape
    tr = _row_block(r)
    nparts = len(parts)
    bc1 = 1.0 - ADAM_B1 ** ADAM_STEP
    bc2 = 1.0 - ADAM_B2 ** ADAM_STEP

    def body(sel_ref, *refs):
        w_ref, m_ref, v_ref = refs[:3]
        p_refs = refs[3:3 + nparts]
        g_ref, d_ref, nm_ref, nv_ref = refs[3 + nparts:]
        g = p_refs[0][...].astype(F32)
        for p in p_refs[1:]:
            g = g + p[...].astype(F32)
        nm = ADAM_B1 * m_ref[...] + (1.0 - ADAM_B1) * g
        nv = ADAM_B2 * v_ref[...] + (1.0 - ADAM_B2) * (g * g)
        m_hat = nm / bc1
        v_hat = nv / bc2
        g_ref[...] = g
        d_ref[...] = -ADAM_LR * (m_hat / (jnp.sqrt(v_hat) + ADAM_EPS) + ADAM_WD * w_ref[...])
        nm_ref[...] = nm
        nv_ref[...] = nv

    def part_spec(idx):
        if idx is None:
            return pl.BlockSpec((1, tr, n), lambda i, s: (s[0], i, 0))
        return pl.BlockSpec((1, tr, n), lambda i, s, idx=idx: (idx, i, 0))

    spec = pl.BlockSpec((1, tr, n), lambda i, s: (0, i, 0))
    out = jax.ShapeDtypeStruct((1, r, n), F32)
    return pl.pallas_call(
        body, name=name,
        grid_spec=pltpu.PrefetchScalarGridSpec(
            num_scalar_prefetch=1, grid=(r // tr,),
            in_specs=[spec] * 3 + [part_spec(idx) for _, idx in parts], out_specs=[spec] * 4),
        out_shape=[out] * 4, compiler_params=_params(),
    )(jnp.zeros((1,), jnp.int32) if sel is None else sel.reshape(1), w, m, v, *[p for p, _ in parts])


def _rms_parts(h, g):
    r = lax.rsqrt(jnp.mean(h * h, axis=-1, keepdims=True) + RMS_EPS)
    xhat = h * r
    return r, xhat, xhat * g


def _rms_bwd(dn, g, r, xhat):
    dxh = dn * g
    return r * (dxh - xhat * jnp.mean(dxh * xhat, axis=-1, keepdims=True))


def _loss_tile(h, tgt, g, lo, hi, loss_ref, dg_ref):
    tm, d = h.shape
    i = pl.program_id(0)

    @pl.when(i == 0)
    def _():
        loss_ref[...] = jnp.zeros_like(loss_ref)
        dg_ref[...] = jnp.zeros_like(dg_ref)

    r, xhat, y = _rms_parts(h, g)
    row = i * tm + lax.broadcasted_iota(jnp.int32, (tm, 1), 0)
    err = jnp.where((row >= lo) & (row < hi), y - tgt, 0.0)
    loss_ref[...] += jnp.full(loss_ref.shape, 0.5 * jnp.sum(jnp.mean(err * err, axis=-1, keepdims=True)), F32)
    dy = err * (1.0 / d)
    dg_ref[...] += jnp.sum(dy * xhat, axis=0, keepdims=True)
    return _rms_bwd(dy, g, r, xhat)


def _ffn_fwd(h, g, wg, wu, wd, name, ride=None, head=None):
    t, d = h.shape
    f = wg.shape[1]
    tm = _tile(t, TOKEN_TILE)
    chunks = _chunks(f, FFN_CHUNK)

    def body(h_ref, g_ref, wg_ref, wu_ref, wd_ref, *rest):
        t_ref, gh_ref = rest[:2] if head else (None, None)
        o_ref, a_ref, b_ref = rest[2:5] if head else rest
        hv = h_ref[...]
        n = _rms_parts(hv, g_ref[...])[2].astype(BF16)
        acc = jnp.zeros((tm, d), F32)
        for s, e in chunks:
            a = _dot(n, wg_ref[:, s:e])
            b = _dot(n, wu_ref[:, s:e])
            a_ref[:, s:e] = a.astype(BF16)
            b_ref[:, s:e] = b.astype(BF16)
            acc = acc + _dot((a * jax.nn.sigmoid(a) * b).astype(BF16), wd_ref[s:e, :])
        out = hv + 0.5 * acc
        o_ref[...] = _loss_tile(out, t_ref[...], gh_ref[...], head[2], head[3], rest[5], rest[6]) if head else out

    tile = pl.BlockSpec((tm, d), lambda i: (i, 0))
    wide = pl.BlockSpec((tm, f), lambda i: (i, 0))
    in_specs, args = [tile, _VM, _VM, _VM, _VM], (h, g, wg, wu, wd)
    out_specs = [tile, wide, wide]
    out_shape = [jax.ShapeDtypeStruct((t, d), F32), jax.ShapeDtypeStruct((t, f), BF16), jax.ShapeDtypeStruct((t, f), BF16)]
    if head:
        in_specs, args = in_specs + [tile, _VM], args + (head[0], head[1])
        out_specs = out_specs + [pl.BlockSpec((SUBLANES, 128), lambda i: (0, 0)), pl.BlockSpec((1, d), lambda i: (0, 0))]
        out_shape = out_shape + [jax.ShapeDtypeStruct((SUBLANES, 128), F32), jax.ShapeDtypeStruct((1, d), F32)]
    return _grid_call(body, name, t // tm, in_specs, out_specs, out_shape, [], args, ride)


def _ffn_bwd(h, dh_out, a16, b16, g, wg, wu, wd, name, ride=None):
    t, d = h.shape
    f = wg.shape[1]
    tm = _tile(t, TOKEN_TILE)
    chunks = _chunks(f, FFN_CHUNK)

    def body(h_ref, dho_ref, a_ref, b_ref, g_ref, wg_ref, wu_ref, wd_ref, dh_ref, dg_ref, n_ref, da_ref, db_ref, s_ref, do_ref):
        @pl.when(pl.program_id(0) == 0)
        def _():
            dg_ref[...] = jnp.zeros_like(dg_ref)

        hv = h_ref[...]
        gv = g_ref[...]
        r, xhat, n32 = _rms_parts(hv, gv)
        dho = dho_ref[...]
        do = (0.5 * dho).astype(BF16)
        dn = jnp.zeros((tm, d), F32)
        for s, e in chunks:
            a = a_ref[:, s:e].astype(F32)
            b = b_ref[:, s:e].astype(F32)
            sig = jax.nn.sigmoid(a)
            sa = a * sig
            ds = _dot_nt(do, wd_ref[s:e, :])
            da = (ds * b * (sig * (1.0 + a * (1.0 - sig)))).astype(BF16)
            db = (ds * sa).astype(BF16)
            s_ref[:, s:e] = (sa * b).astype(BF16)
            da_ref[:, s:e] = da
            db_ref[:, s:e] = db
            dn = dn + _dot_nt(da, wg_ref[:, s:e]) + _dot_nt(db, wu_ref[:, s:e])
        dh_ref[...] = dho + _rms_bwd(dn, gv, r, xhat)
        dg_ref[...] += jnp.sum(dn * xhat, axis=0, keepdims=True)
        n_ref[...] = n32.astype(BF16)
        do_ref[...] = do

    tile = pl.BlockSpec((tm, d), lambda i: (i, 0))
    wide = pl.BlockSpec((tm, f), lambda i: (i, 0))
    one = pl.BlockSpec((1, d), lambda i: (0, 0))
    return _grid_call(
        body, name, t // tm, [tile, tile, wide, wide, _VM, _VM, _VM, _VM], [tile, one, tile, wide, wide, wide, tile],
        [jax.ShapeDtypeStruct((t, d), F32), jax.ShapeDtypeStruct((1, d), F32),
         jax.ShapeDtypeStruct((t, d), BF16), jax.ShapeDtypeStruct((t, f), BF16),
         jax.ShapeDtypeStruct((t, f), BF16), jax.ShapeDtypeStruct((t, f), BF16),
         jax.ShapeDtypeStruct((t, d), BF16)],
        [], (h, dh_out, a16, b16, g, wg, wu, wd), ride)


def _dw(a, b, name, ride=None):
    t, m = a.shape
    n = b.shape[1]
    bn = next(k for k in (512, 256, n) if n % k == 0)

    def body(a_ref, b_ref, o_ref):
        o_ref[...] = _dot_tn(a_ref[...], b_ref[...]).astype(BF16)

    (out,), got = _grid_call(
        body, name, n // bn, [_VM, pl.BlockSpec((t, bn), lambda j: (0, j))], [pl.BlockSpec((m, bn), lambda j: (0, j))],
        [jax.ShapeDtypeStruct((m, n), BF16)], [], (a, b), ride)
    return (out, got) if ride else out


def _to_bf16(arrays, name):
    k = len(arrays)

    def body(*refs):
        for x_ref, o_ref in zip(refs[:k], refs[k:]):
            o_ref[...] = x_ref[...].astype(BF16)

    return pl.pallas_call(
        body, name=name, out_shape=[jax.ShapeDtypeStruct(a.shape, BF16) for a in arrays],
        compiler_params=pltpu.CompilerParams(vmem_limit_bytes=VMEM_LIMIT_BYTES),
    )(*arrays)


def _s5_discretise(a_re, a_im, log_dt, b_re, b_im):
    dt = jnp.exp(log_dt)
    mag = jnp.exp(a_re * dt)
    lam_re = mag * jnp.cos(a_im * dt)
    lam_im = mag * jnp.sin(a_im * dt)
    den = a_re * a_re + a_im * a_im
    q_re = ((lam_re - 1.0) * a_re + lam_im * a_im) / den
    q_im = (lam_im * a_re - (lam_re - 1.0) * a_im) / den
    bb_re = q_re[:, None, :] * b_re - q_im[:, None, :] * b_im
    bb_im = q_re[:, None, :] * b_im + q_im[:, None, :] * b_re
    return lam_re, lam_im, bb_re, bb_im


def _s5_params_fwd(a_re, a_im, log_dt, b_re, b_im):
    g, p = a_re.shape
    c = b_re.shape[1]

    def body(are_ref, aim_ref, ldt_ref, bre_ref, bim_ref, pwr_ref, pwi_ref, bbr_ref, bbi_ref):
        lr, li, bbr, bbi = _s5_discretise(are_ref[...], aim_ref[...], ldt_ref[...], bre_ref[...], bim_ref[...])
        bbr_ref[...] = bbr
        bbi_ref[...] = bbi
        pr, pi = lr, li
        pwr_ref[0] = pr
        pwi_ref[0] = pi
        for k in range(1, SUBLANES):
            pr, pi = pr * lr - pi * li, pr * li + pi * lr
            pwr_ref[k] = pr
            pwi_ref[k] = pi

    return pl.pallas_call(
        body, name="s5_params_fwd",
        out_shape=[jax.ShapeDtypeStruct((SUBLANES, g, p), F32), jax.ShapeDtypeStruct((SUBLANES, g, p), F32),
                   jax.ShapeDtypeStruct((g, c, p), F32), jax.ShapeDtypeStruct((g, c, p), F32)],
    )(a_re, a_im, log_dt, b_re, b_im)


def _s5_params_bwd(a_re, a_im, log_dt, b_re, b_im, dlam, dbb_re, dbb_im):
    g, p = a_re.shape
    c = b_re.shape[1]

    def body(are_ref, aim_ref, ldt_ref, bre_ref, bim_ref, dlam_ref, dbr_ref, dbi_ref,
             dare_ref, daim_ref, dldt_ref, dbre_ref, dbim_ref):
        dlr = jnp.sum(dlam_ref[0], axis=0)
        dli = jnp.sum(dlam_ref[1], axis=0)
        _, vjp = jax.vjp(_s5_discretise, are_ref[...], aim_ref[...], ldt_ref[...], bre_ref[...], bim_ref[...])
        dare, daim, dldt, dbre, dbim = vjp((dlr, dli, dbr_ref[...], dbi_ref[...]))
        dare_ref[...] = dare
        daim_ref[...] = daim
        dldt_ref[...] = dldt
        dbre_ref[...] = dbre
        dbim_ref[...] = dbim

    return pl.pallas_call(
        body, name="s5_params_bwd",
        out_shape=[jax.ShapeDtypeStruct((g, p), F32), jax.ShapeDtypeStruct((g, p), F32),
                   jax.ShapeDtypeStruct((g, 1), F32), jax.ShapeDtypeStruct((g, c, p), F32),
                   jax.ShapeDtypeStruct((g, c, p), F32)],
    )(a_re, a_im, log_dt, b_re, b_im, dlam, dbb_re, dbb_im)


def _scan_chunks(gp):
    hg = gp // 2
    w = min(SCAN_LANES, hg)
    return w, [(half * hg + k * w, half * gp + k * w, half * gp + hg + k * w) for half in range(2) for k in range(hg // w)]


def _cmul_acc(xr, xi, tr, ti, sr, si):
    return xr + tr * sr - ti * si, xi + tr * si + ti * sr


def _scan_fwd(buf_ref, row0, tm, ltab_ref, cin_ref, cout_ref, gp):
    w, chunks = _scan_chunks(gp)
    for lo_t, lo_r, lo_i in chunks:
        def body(r, carry, lo_t=lo_t, lo_r=lo_r, lo_i=lo_i):
            cr, ci = carry
            row = pl.multiple_of(row0 + r * SUBLANES, SUBLANES)
            xr = buf_ref[pl.ds(row, SUBLANES), lo_r:lo_r + w]
            xi = buf_ref[pl.ds(row, SUBLANES), lo_i:lo_i + w]
            for tab, shift in ((0, 1), (2, 2), (4, 4)):
                xr, xi = _cmul_acc(xr, xi, ltab_ref[tab, :, lo_t:lo_t + w], ltab_ref[tab + 1, :, lo_t:lo_t + w],
                                   pltpu.roll(xr, shift, 0), pltpu.roll(xi, shift, 0))
            xr, xi = _cmul_acc(xr, xi, ltab_ref[6, :, lo_t:lo_t + w], ltab_ref[7, :, lo_t:lo_t + w], cr, ci)
            buf_ref[pl.ds(row, SUBLANES), lo_r:lo_r + w] = xr
            buf_ref[pl.ds(row, SUBLANES), lo_i:lo_i + w] = xi
            last = SUBLANES - 1
            return (jnp.broadcast_to(xr[last:last + 1], (SUBLANES, w)), jnp.broadcast_to(xi[last:last + 1], (SUBLANES, w)))

        cr, ci = lax.fori_loop(0, tm // SUBLANES, body,
                               (cin_ref[0:SUBLANES, lo_r:lo_r + w], cin_ref[0:SUBLANES, lo_i:lo_i + w]))
        if cout_ref is not None:
            cout_ref[0:SUBLANES, lo_r:lo_r + w] = cr
            cout_ref[0:SUBLANES, lo_i:lo_i + w] = ci


def _scan_rev(g_ref, hext_ref, tm, ltab_ref, gc_ref, dlam_ref, gp):
    w, chunks = _scan_chunks(gp)
    nb = tm // SUBLANES
    for lo_t, lo_r, lo_i in chunks:
        def body(k, carry, lo_t=lo_t, lo_r=lo_r, lo_i=lo_i):
            cr, ci, ar, ai = carry
            row = pl.multiple_of((nb - 1 - k) * SUBLANES, SUBLANES)
            xr = g_ref[pl.ds(row, SUBLANES), lo_r:lo_r + w]
            xi = g_ref[pl.ds(row, SUBLANES), lo_i:lo_i + w]
            for tab, shift in ((8, 7), (10, 6), (12, 4)):
                xr, xi = _cmul_acc(xr, xi, ltab_ref[tab, :, lo_t:lo_t + w], ltab_ref[tab + 1, :, lo_t:lo_t + w],
                                   pltpu.roll(xr, shift, 0), pltpu.roll(xi, shift, 0))
            xr, xi = _cmul_acc(xr, xi, ltab_ref[14, :, lo_t:lo_t + w], ltab_ref[15, :, lo_t:lo_t + w], cr, ci)
            g_ref[pl.ds(row, SUBLANES), lo_r:lo_r + w] = xr
            g_ref[pl.ds(row, SUBLANES), lo_i:lo_i + w] = xi
            first = lax.broadcasted_iota(jnp.int32, (SUBLANES, w), 0) == 0
            prev = pl.ds(row, SUBLANES)
            here = pl.ds(row + SUBLANES, SUBLANES)
            hpr = jnp.where(first, pltpu.roll(hext_ref[prev, lo_r:lo_r + w], 1, 0), pltpu.roll(hext_ref[here, lo_r:lo_r + w], 1, 0))
            hpi = jnp.where(first, pltpu.roll(hext_ref[prev, lo_i:lo_i + w], 1, 0), pltpu.roll(hext_ref[here, lo_i:lo_i + w], 1, 0))
            ar = ar + xr * hpr + xi * hpi
            ai = ai - xr * hpi + xi * hpr
            return (jnp.broadcast_to(xr[0:1], (SUBLANES, w)), jnp.broadcast_to(xi[0:1], (SUBLANES, w)), ar, ai)

        cr, ci, ar, ai = lax.fori_loop(
            0, nb, body, (gc_ref[:, lo_r:lo_r + w], gc_ref[:, lo_i:lo_i + w], dlam_ref[:, lo_r:lo_r + w], dlam_ref[:, lo_i:lo_i + w]))
        gc_ref[:, lo_r:lo_r + w] = cr
        gc_ref[:, lo_i:lo_i + w] = ci
        dlam_ref[:, lo_r:lo_r + w] = ar
        dlam_ref[:, lo_i:lo_i + w] = ai


def _conv_taps(cw, cext_ref, cin, tm):
    return (cw[0:1] * cext_ref[SUBLANES - 2:SUBLANES - 2 + tm, :] + cw[1:2] * cext_ref[SUBLANES - 1:SUBLANES - 1 + tm, :]
            + cw[2:3] * cin)


def _mix_fwd(h, gm, win, bg, bc, cc, dsk, wglu, cw, wco, wo, ltab, dims):
    d, ds, dc, gp = dims
    t = h.shape[0]
    tm = _tile(t, MIX_TILE)
    nt = t // tm
    dsh = ds // 2
    o1, o2, o3 = ds + dc, ds + 2 * dc, ds + 3 * dc
    ncols = o3 + 2 * d

    def body(h_ref, gm_ref, win_ref, bg_ref, bc_ref, cc_ref, dsk_ref, wglu_ref, cw_ref, wco_ref, wo_ref, ltab_ref,
             h2_ref, p_ref, hs_ref, y5_ref, z_ref, yc_ref, hbuf_ref, carry_ref, cext_ref):
        @pl.when(pl.program_id(0) == 0)
        def _():
            carry_ref[...] = jnp.zeros_like(carry_ref)
            cext_ref[0:SUBLANES, :] = jnp.zeros((SUBLANES, dc), F32)

        hv = h_ref[...]
        bg = bg_ref[...]
        u = _rms_parts(hv, gm_ref[...])[2].astype(BF16)
        us = _dot(u, win_ref[:, 0:ds])
        v = _dot(u, win_ref[:, ds:o1])
        gb = _dot(u, win_ref[:, o1:o2])
        gcv = _dot(u, win_ref[:, o2:o3])
        gs = jax.nn.sigmoid(_dot(u, win_ref[:, o3:o3 + d]) + bg[:, 0:d])
        gcg = jax.nn.sigmoid(_dot(u, win_ref[:, o3 + d:o3 + 2 * d]) + bg[:, d:2 * d])
        us16 = us.astype(BF16)
        p_ref[:, 0:ds] = us16
        p_ref[:, ds:o1] = v.astype(BF16)
        p_ref[:, o1:o2] = gb.astype(BF16)
        p_ref[:, o2:o3] = gcv.astype(BF16)
        p_ref[:, o3:o3 + d] = gs.astype(BF16)
        p_ref[:, o3 + d:ncols] = gcg.astype(BF16)
        for half in range(2):
            hbuf_ref[:, half * gp:(half + 1) * gp] = _dot(us16[:, half * dsh:(half + 1) * dsh], bc_ref[half])
        _scan_fwd(hbuf_ref, 0, tm, ltab_ref, carry_ref, carry_ref, gp)
        hs_ref[...] = hbuf_ref[...].astype(BF16)
        y5 = jnp.concatenate([_dot(hs_ref[:, half * gp:(half + 1) * gp], cc_ref[half]) for half in range(2)], axis=1) + dsk_ref[...] * us
        y5_ref[...] = y5.astype(BF16)
        z = _dot(jax.nn.gelu(y5).astype(BF16), wglu_ref[...])
        z_ref[...] = z.astype(BF16)
        ys = z[:, 0:d] * jax.nn.sigmoid(z[:, d:2 * d])
        cin = gcv * v
        cext_ref[SUBLANES:SUBLANES + tm, :] = cin
        yc = _dot((gb * _conv_taps(cw_ref[...], cext_ref, cin, tm)).astype(BF16), wco_ref[...])
        yc_ref[...] = yc.astype(BF16)
        h2_ref[...] = hv + _dot((gs * ys + gcg * yc).astype(BF16), wo_ref[...])
        cext_ref[0:SUBLANES, :] = cext_ref[tm:tm + SUBLANES, :]

    def tile(cols):
        return pl.BlockSpec((tm, cols), lambda i: (i, 0))

    def bf(cols):
        return jax.ShapeDtypeStruct((t, cols), BF16)

    return pl.pallas_call(
        body, name="mix_fwd", grid=(nt,),
        in_specs=[tile(d)] + [_VM] * 11,
        out_specs=[tile(d), tile(ncols), tile(2 * gp), tile(ds), tile(2 * d), tile(d)],
        out_shape=[jax.ShapeDtypeStruct((t, d), F32), bf(ncols), bf(2 * gp), bf(ds), bf(2 * d), bf(d)],
        scratch_shapes=[pltpu.VMEM((tm, 2 * gp), F32), pltpu.VMEM((SUBLANES, 2 * gp), F32), pltpu.VMEM((SUBLANES + tm, dc), F32)],
        compiler_params=_params(),
    )(h, gm, win, bg, bc, cc, dsk, wglu, cw, wco, wo, ltab)


HALO = 16


def _mix_bwd_gates(dh2, p16, y516, z16, yc16, wglu, cw, wco, wo, dims):
    d, ds, dc, gp = dims
    t = dh2.shape[0]
    tm = _tile(t, TOKEN_TILE)
    nt = t // tm
    o1, o2, o3 = ds + dc, ds + 2 * dc, ds + 3 * dc
    ncols = o3 + 2 * d

    def body(dh2_ref, p_ref, halo_ref, y5_ref, z_ref, yc_ref, wglu_ref, cw_ref, wco_ref, wo_ref,
             dp_ref, dy5_ref, ge_ref, dz_ref, cg_ref, dyc_ref, mx_ref, dh216_ref, dbg_ref, dcw_ref, cext_ref, dcvext_ref):
        j = pl.program_id(0)

        @pl.when(j == 0)
        def _():
            dbg_ref[...] = jnp.zeros_like(dbg_ref)
            dcw_ref[...] = jnp.zeros_like(dcw_ref)
            dcvext_ref[tm:tm + SUBLANES, :] = jnp.zeros((SUBLANES, dc), F32)

        before = halo_ref[:, o2:o3].astype(F32) * halo_ref[:, ds:o1].astype(F32)
        cext_ref[0:SUBLANES, :] = jnp.where(j == nt - 1, 0.0, before[HALO - SUBLANES:HALO])
        cw_v = cw_ref[...]
        v = p_ref[:, ds:o1].astype(F32)
        gb = p_ref[:, o1:o2].astype(F32)
        gcv = p_ref[:, o2:o3].astype(F32)
        gs = p_ref[:, o3:o3 + d].astype(F32)
        gcg = p_ref[:, o3 + d:ncols].astype(F32)
        z1 = z_ref[:, 0:d].astype(F32)
        sz = jax.nn.sigmoid(z_ref[:, d:2 * d].astype(F32))
        ys = z1 * sz
        yc = yc_ref[...].astype(F32)
        ge, gelu_vjp = jax.vjp(jax.nn.gelu, y5_ref[...].astype(F32))
        cin = gcv * v
        cext_ref[SUBLANES:SUBLANES + tm, :] = cin
        cv = _conv_taps(cw_v, cext_ref, cin, tm)

        dh216 = dh2_ref[...].astype(BF16)
        dmixed = _dot_nt(dh216, wo_ref[...])
        dys = dmixed * gs
        dyc16 = (dmixed * gcg).astype(BF16)
        dpgs = dmixed * ys * gs * (1.0 - gs)
        dpgc = dmixed * yc * gcg * (1.0 - gcg)
        dz16 = jnp.concatenate([dys * sz, dys * z1 * sz * (1.0 - sz)], axis=1).astype(BF16)
        dy5_ref[...] = gelu_vjp(_dot_nt(dz16, wglu_ref[...]))[0].astype(BF16)
        dcg = _dot_nt(dyc16, wco_ref[...])
        dcv = dcg * gb
        dcvext_ref[0:tm, :] = dcv
        dcin = cw_v[2:3] * dcv + cw_v[1:2] * dcvext_ref[1:1 + tm, :] + cw_v[0:1] * dcvext_ref[2:2 + tm, :]
        dcw_ref[0:1, :] += jnp.sum(dcv * cext_ref[SUBLANES - 2:SUBLANES - 2 + tm, :], axis=0, keepdims=True)
        dcw_ref[1:2, :] += jnp.sum(dcv * cext_ref[SUBLANES - 1:SUBLANES - 1 + tm, :], axis=0, keepdims=True)
        dcw_ref[2:3, :] += jnp.sum(dcv * cin, axis=0, keepdims=True)
        dcvext_ref[tm:tm + SUBLANES, :] = dcvext_ref[0:SUBLANES, :]
        dbg_ref[...] += jnp.concatenate([jnp.sum(dpgs, axis=0, keepdims=True), jnp.sum(dpgc, axis=0, keepdims=True)], axis=1)
        dp_ref[:, 0:ds] = jnp.zeros((tm, ds), BF16)
        dp_ref[:, ds:o1] = (dcin * gcv).astype(BF16)
        dp_ref[:, o1:o2] = (dcg * cv).astype(BF16)
        dp_ref[:, o2:o3] = (dcin * v).astype(BF16)
        dp_ref[:, o3:o3 + d] = dpgs.astype(BF16)
        dp_ref[:, o3 + d:ncols] = dpgc.astype(BF16)
        ge_ref[...] = ge.astype(BF16)
        dz_ref[...] = dz16
        cg_ref[...] = (gb * cv).astype(BF16)
        dyc_ref[...] = dyc16
        mx_ref[...] = (gs * ys + gcg * yc).astype(BF16)
        dh216_ref[...] = dh216

    def rev(cols):
        return pl.BlockSpec((tm, cols), lambda j: (nt - 1 - j, 0))

    def bf(cols):
        return jax.ShapeDtypeStruct((t, cols), BF16)

    halo = pl.BlockSpec((HALO, ncols), lambda j: (jnp.maximum((nt - 1 - j) * (tm // HALO) - 1, 0), 0))
    return pl.pallas_call(
        body, name="mix_bwd_gates", grid=(nt,),
        in_specs=[rev(d), rev(ncols), halo, rev(ds), rev(2 * d), rev(d), _VM, _VM, _VM, _VM],
        out_specs=[rev(ncols), rev(ds), rev(ds), rev(2 * d), rev(dc), rev(d), rev(d), rev(d), _VM, _VM],
        out_shape=[bf(ncols), bf(ds), bf(ds), bf(2 * d), bf(dc), bf(d), bf(d), bf(d),
                   jax.ShapeDtypeStruct((1, 2 * d), F32), jax.ShapeDtypeStruct((SUBLANES, dc), F32)],
        scratch_shapes=[pltpu.VMEM((SUBLANES + tm, dc), F32), pltpu.VMEM((tm + SUBLANES, dc), F32)],
        compiler_params=_params(),
    )(dh2, p16, p16, y516, z16, yc16, wglu, cw, wco, wo)


def _mix_bwd_scan(dy516, hs16, p16, bc, cc, dsk, ltab, dims):
    d, ds, dc, gp = dims
    t = dy516.shape[0]
    tm = _tile(t, TOKEN_TILE)
    nt = t // tm
    dsh = ds // 2

    def body(dy5_ref, hs_ref, halo_ref, us_ref, bc_ref, cc_ref, dsk_ref, ltab_ref,
             dus_ref, ddsk_ref, dlam_ref, dbc_ref, dcc_ref, hext_ref, gbuf_ref, gcarry_ref):
        j = pl.program_id(0)

        @pl.when(j == 0)
        def _():
            for ref in (ddsk_ref, dlam_ref, dbc_ref, dcc_ref, gcarry_ref):
                ref[...] = jnp.zeros_like(ref)

        before = jnp.where(j == nt - 1, 0.0, halo_ref[...].astype(F32)[HALO - 1:HALO])
        hext_ref[0:SUBLANES, :] = jnp.broadcast_to(before, (SUBLANES, 2 * gp))
        hext_ref[SUBLANES:SUBLANES + tm, :] = hs_ref[...].astype(F32)
        dy516v = dy5_ref[...]
        for half in range(2):
            gbuf_ref[:, half * gp:(half + 1) * gp] = _dot_nt(dy516v[:, half * dsh:(half + 1) * dsh], cc_ref[half])
        _scan_rev(gbuf_ref, hext_ref, tm, ltab_ref, gcarry_ref, dlam_ref, gp)
        dus = []
        for half in range(2):
            g16 = gbuf_ref[:, half * gp:(half + 1) * gp].astype(BF16)
            dus.append(_dot_nt(g16, bc_ref[half]))
            dbc_ref[half] += _dot_tn(us_ref[:, half * dsh:(half + 1) * dsh], g16)
            dcc_ref[half] += _dot_tn(hs_ref[:, half * gp:(half + 1) * gp], dy516v[:, half * dsh:(half + 1) * dsh])
        dy5 = dy516v.astype(F32)
        dus_ref[...] = (jnp.concatenate(dus, axis=1) + dsk_ref[...] * dy5).astype(BF16)
        ddsk_ref[...] += jnp.sum(dy5 * us_ref[...].astype(F32), axis=0, keepdims=True)

    def rev(cols):
        return pl.BlockSpec((tm, cols), lambda j: (nt - 1 - j, 0))

    halo = pl.BlockSpec((HALO, 2 * gp), lambda j: (jnp.maximum((nt - 1 - j) * (tm // HALO) - 1, 0), 0))
    return pl.pallas_call(
        body, name="mix_bwd_scan", grid=(nt,),
        in_specs=[rev(ds), rev(2 * gp), halo, rev(ds), _VM, _VM, _VM, _VM],
        out_specs=[rev(ds), _VM, _VM, _VM, _VM],
        out_shape=[jax.ShapeDtypeStruct((t, ds), BF16), jax.ShapeDtypeStruct((1, ds), F32),
                   jax.ShapeDtypeStruct((SUBLANES, 2 * gp), F32),
                   jax.ShapeDtypeStruct((2, dsh, gp), F32), jax.ShapeDtypeStruct((2, gp, dsh), F32)],
        scratch_shapes=[pltpu.VMEM((SUBLANES + tm, 2 * gp), F32), pltpu.VMEM((tm, 2 * gp), F32), pltpu.VMEM((SUBLANES, 2 * gp), F32)],
        compiler_params=_params(),
    )(dy516, hs16, hs16, p16, bc, cc, dsk, ltab)


def _mix_bwd_in(h, dh2, dp16, dus16, gm, win, dims):
    d, ds, dc, gp = dims
    t = h.shape[0]
    tm = _tile(t, TOKEN_TILE)
    ncols = dp16.shape[1]

    def body(h_ref, dh2_ref, dp_ref, dus_ref, gm_ref, win_ref, dh1_ref, u_ref, dpf_ref, dgm_ref):
        @pl.when(pl.program_id(0) == 0)
        def _():
            dgm_ref[...] = jnp.zeros_like(dgm_ref)

        gmv = gm_ref[...]
        r, xhat, n32 = _rms_parts(h_ref[...], gmv)
        du = _dot_nt(dus_ref[...], win_ref[:, 0:ds]) + _dot_nt(dp_ref[:, ds:ncols], win_ref[:, ds:ncols])
        dh1_ref[...] = dh2_ref[...] + _rms_bwd(du, gmv, r, xhat)
        dgm_ref[...] += jnp.sum(du * xhat, axis=0, keepdims=True)
        u_ref[...] = n32.astype(BF16)
        dpf_ref[:, 0:ds] = dus_ref[...]
        dpf_ref[:, ds:ncols] = dp_ref[:, ds:ncols]

    def tile(cols):
        return pl.BlockSpec((tm, cols), lambda i: (i, 0))

    return pl.pallas_call(
        body, name="mix_bwd_in", grid=(t // tm,),
        in_specs=[tile(d), tile(d), tile(ncols), tile(ds), _VM, _VM],
        out_specs=[tile(d), tile(d), tile(ncols), pl.BlockSpec((1, d), lambda i: (0, 0))],
        out_shape=[jax.ShapeDtypeStruct((t, d), F32), jax.ShapeDtypeStruct((t, d), BF16),
                   jax.ShapeDtypeStruct((t, ncols), BF16), jax.ShapeDtypeStruct((1, d), F32)],
        compiler_params=_params(),
    )(h, dh2, dp16, dus16, gm, win)


def _pad_rows(a, rows, axis=0):
    pad = [(0, 0)] * a.ndim
    pad[axis] = (0, rows - a.shape[axis])
    return jnp.pad(a, pad)


def _as_rows(a):
    flat = a.reshape(-1)
    n = -(-flat.shape[0] // SLAB_COLS) * SLAB_COLS
    return jnp.pad(flat, (0, n - flat.shape[0])).reshape(-1, SLAB_COLS)


def _pack(arrs):
    rows = jnp.concatenate([_as_rows(a) for a in arrs], axis=0)
    return _pad_rows(rows, -(-rows.shape[0] // 16) * 16)


def _unpack(slab, shapes):
    out, r = [], 0
    for shp in shapes:
        size = 1
        for s in shp:
            size *= s
        n = -(-size // SLAB_COLS)
        out.append(slab[r:r + n].reshape(-1)[:size].reshape(shp))
        r += n
    return out


def _block_diag(blocks):
    n, a, b = blocks.shape
    eye = jnp.eye(n, dtype=blocks.dtype)
    return (blocks[:, :, None, :] * eye[:, None, :, None]).reshape(n * a, n * b)


def _diag_blocks(mat, n):
    a, b = mat.shape[0] // n, mat.shape[1] // n
    eye = jnp.eye(n, dtype=mat.dtype)
    return jnp.sum(mat.reshape(n, a, n, b) * eye[:, None, :, None], axis=2)


BIG = (("ffn1_w_gate", "col"), ("ffn1_w_up", "col"), ("ffn1_w_down", "row"), ("w_in", "col"), ("ssm_w_glu", "col"),
       ("conv_w_out", "col"), ("w_o", "row"), ("ffn2_w_gate", "col"), ("ffn2_w_up", "col"), ("ffn2_w_down", "row"))
REPLICATED = ("g_ffn1", "g_mix", "b_gate", "ssm_a_re", "ssm_a_im", "ssm_log_dt", "ssm_b_re", "ssm_b_im", "ssm_c_re",
              "ssm_c_im", "ssm_d", "g_ffn2", "g_final")
WEIGHTS = ("meta_tokens", "g_ffn1", "ffn1_w_gate", "ffn1_w_up", "ffn1_w_down", "g_mix", "w_in", "b_gate", "ssm_a_re",
           "ssm_a_im", "ssm_log_dt", "ssm_b_re", "ssm_b_im", "ssm_c_re", "ssm_c_im", "ssm_d", "ssm_w_glu", "conv_w",
           "conv_w_out", "w_o", "g_ffn2", "ffn2_w_gate", "ffn2_w_up", "ffn2_w_down", "g_final")
N_EARLY = 3


def _full_from_blocks(blocks, kind):
    n, r, c = blocks.shape
    if kind == "col":
        return jnp.transpose(blocks, (1, 0, 2)).reshape(r, n * c)
    return blocks.reshape(n * r, c)


def _blocks_from_full(full, kind):
    if kind == "col":
        r, nc = full.shape
        return jnp.transpose(full.reshape(r, NDEV, nc // NDEV), (1, 0, 2))
    nr, c = full.shape
    return full.reshape(NDEV, nr // NDEV, c)


def kernel(x, meta_tokens, g_ffn1, ffn1_w_gate, ffn1_w_up, ffn1_w_down, g_mix, w_in, b_gate, ssm_a_re, ssm_a_im, ssm_log_dt, ssm_b_re, ssm_b_im, ssm_c_re, ssm_c_im, ssm_d, ssm_w_glu, conv_w, conv_w_out, w_o, g_ffn2, ffn2_w_gate, ffn2_w_up, ffn2_w_down, g_final, loss_target, m_meta_tokens, m_g_ffn1, m_ffn1_w_gate, m_ffn1_w_up, m_ffn1_w_down, m_g_mix, m_w_in, m_b_gate, m_ssm_a_re, m_ssm_a_im, m_ssm_log_dt, m_ssm_b_re, m_ssm_b_im, m_ssm_c_re, m_ssm_c_im, m_ssm_d, m_ssm_w_glu, m_conv_w, m_conv_w_out, m_w_o, m_g_ffn2, m_ffn2_w_gate, m_ffn2_w_up, m_ffn2_w_down, m_g_final, v_meta_tokens, v_g_ffn1, v_ffn1_w_gate, v_ffn1_w_up, v_ffn1_w_down, v_g_mix, v_w_in, v_b_gate, v_ssm_a_re, v_ssm_a_im, v_ssm_log_dt, v_ssm_b_re, v_ssm_b_im, v_ssm_c_re, v_ssm_c_im, v_ssm_d, v_ssm_w_glu, v_conv_w, v_conv_w_out, v_w_o, v_g_ffn2, v_ffn2_w_gate, v_ffn2_w_up, v_ffn2_w_down, v_g_final):
    args = dict(locals())
    w = {n: args[n] for n in WEIGHTS}
    mom_m = {n: args["m_" + n] for n in WEIGHTS}
    mom_v = {n: args["v_" + n] for n in WEIGHTS}

    seq, d = x.shape[1], x.shape[2]
    n_meta = meta_tokens.shape[0]
    ds = ssm_d.shape[1]
    n_grp, n_state = ssm_a_re.shape[1], ssm_a_re.shape[2]
    gp = n_grp * n_state
    dc = conv_w.shape[3] * NDEV
    dims = (d, ds, dc, gp)
    t_real = n_meta + seq
    t_pad = -(-t_real // ROW_ALIGN) * ROW_ALIGN
    me_chip = 2 * lax.axis_index("x") + lax.axis_index("y")
    me_core = lax.axis_index("c")
    me = 2 * me_chip + me_core
    mcols, ccols = d // NDEV, dc // NDEV

    cw_shard = _pad_rows(_pad_rows(conv_w.reshape(3, ccols), SUBLANES), 128, axis=1)
    shard16 = dict(zip([name for name, _ in BIG], _to_bf16([w[name][0] for name, _ in BIG], "weights_to_bf16")))
    early, late = BIG[:N_EARLY], BIG[N_EARLY:]
    got = _exchange(_gather_ride([shard16[name] for name, _ in early] + [meta_tokens, cw_shard]), "gather_first")
    full = {name: _full_from_blocks(got[i], kind) for i, (name, kind) in enumerate(early)}
    meta_full = _full_from_blocks(got[-2], "col")
    cw_rows = _pad_rows(_full_from_blocks(got[-1][:, 0:3, 0:ccols], "col"), SUBLANES)

    a_re, a_im, ldt = ssm_a_re[0], ssm_a_im[0], ssm_log_dt[0].reshape(n_grp, 1)
    b_re_t = jnp.transpose(ssm_b_re[0], (0, 2, 1))
    b_im_t = jnp.transpose(ssm_b_im[0], (0, 2, 1))
    pw_r, pw_i, bb_r, bb_i = _s5_params_fwd(a_re, a_im, ldt, b_re_t, b_im_t)
    pw_r = pw_r.reshape(SUBLANES, gp)
    pw_i = pw_i.reshape(SUBLANES, gp)
    sub = jnp.arange(SUBLANES)[:, None]

    def fwd_tab(p, k):
        return jnp.where(sub >= k, p[k - 1][None, :], 0.0)

    def rev_tab(p, k):
        return jnp.where(sub <= SUBLANES - 1 - k, p[k - 1][None, :], 0.0)

    ltab = jnp.stack(
        [fwd_tab(pw_r, 1), fwd_tab(pw_i, 1), fwd_tab(pw_r, 2), fwd_tab(pw_i, 2), fwd_tab(pw_r, 4), fwd_tab(pw_i, 4), pw_r, pw_i,
         rev_tab(pw_r, 1), -rev_tab(pw_i, 1), rev_tab(pw_r, 2), -rev_tab(pw_i, 2), rev_tab(pw_r, 4), -rev_tab(pw_i, 4),
         pw_r[::-1], -pw_i[::-1]], axis=0)
    gh = n_grp // 2
    bc = jnp.stack([jnp.concatenate([_block_diag(bb_r[h * gh:(h + 1) * gh]), _block_diag(bb_i[h * gh:(h + 1) * gh])], axis=1)
                    for h in range(2)]).astype(BF16)
    c_re_t = jnp.transpose(ssm_c_re[0], (0, 2, 1))
    c_im_t = jnp.transpose(ssm_c_im[0], (0, 2, 1))
    cc = jnp.stack([jnp.concatenate([_block_diag(c_re_t[h * gh:(h + 1) * gh]), -_block_diag(c_im_t[h * gh:(h + 1) * gh])], axis=0)
                    for h in range(2)]).astype(BF16)

    zpad = jnp.zeros((t_pad - t_real, d), F32)
    h0 = jnp.concatenate([meta_full, x[0], zpad], axis=0)
    tgt = jnp.concatenate([jnp.zeros((n_meta, d), F32), loss_target[0], zpad], axis=0)
    (h1, a1, b1), got = _ffn_fwd(h0, g_ffn1, full["ffn1_w_gate"], full["ffn1_w_up"], full["ffn1_w_down"], "ffn1_fwd",
                                 ride=_gather_ride([shard16[name] for name, _ in late]))
    full.update({name: _full_from_blocks(got[i], kind) for i, (name, kind) in enumerate(late)})
    h2, *saved = _mix_fwd(h1, g_mix, full["w_in"], b_gate, bc, cc, ssm_d, full["ssm_w_glu"], cw_rows, full["conv_w_out"],
                          full["w_o"], ltab, dims)
    (dh3, a2, b2, loss_blk, dg_final), _ = _ffn_fwd(h2, g_ffn2, full["ffn2_w_gate"], full["ffn2_w_up"], full["ffn2_w_down"], "ffn2_fwd",
                                                 head=(tgt, g_final.reshape(1, d), n_meta, t_real))

    (dh2, dg_ffn2, n2, da2, db2, s2, do2), _ = _ffn_bwd(
        h2, dh3, a2, b2, g_ffn2, full["ffn2_w_gate"], full["ffn2_w_up"], full["ffn2_w_down"], "ffn2_bwd")
    p16, hs16, y516, z16, yc16 = saved
    dp_part, dy516, ge16, dz16, cg16, dyc16, mx16, dh216, dbg, dcw = _mix_bwd_gates(
        dh2, p16, y516, z16, yc16, full["ssm_w_glu"], cw_rows, full["conv_w_out"], full["w_o"], dims)
    dus16, ddsk, dlam, dbc, dcc = _mix_bwd_scan(dy516, hs16, p16, bc, cc, ssm_d, ltab, dims)
    dh1, u16, dp16, dg_mix = _mix_bwd_in(h1, dh2, dp_part, dus16, g_mix, full["w_in"], dims)
    dblocks = {
        "w_in": _blocks_from_full(_dw(u16, dp16, "dw_in"), "col"),
        "ssm_w_glu": _blocks_from_full(_dw(ge16, dz16, "dw_glu"), "col"),
        "conv_w_out": _blocks_from_full(_dw(cg16, dyc16, "dw_conv_out"), "col"),
        "w_o": _blocks_from_full(_dw(mx16, dh216, "dw_o"), "row"),
        "ffn2_w_gate": _blocks_from_full(_dw(n2, da2, "dw_ffn2_gate"), "col"),
        "ffn2_w_up": _blocks_from_full(_dw(n2, db2, "dw_ffn2_up"), "col"),
        "ffn2_w_down": jnp.transpose(_blocks_from_full(_dw(do2, s2, "dw_ffn2_down"), "col"), (0, 2, 1)),
    }

    def pair_sums(names, tag):
        gs = [dblocks[name] for name in names]
        from_sibling = _exchange(_pair_ride(gs), "reduce_pair_" + tag)
        return [_add_pairs(g, me_core, b, "reduce_pair_add_" + name) for g, b, name in zip(gs, from_sibling, names)]

    late_names = [name for name, _ in late]
    pairs = dict(zip(late_names, pair_sums(late_names, "late")))
    (dh0, dg_ffn1, n1, da1, db1, s1, do1), got = _ffn_bwd(
        h0, dh1, a1, b1, g_ffn1, full["ffn1_w_gate"], full["ffn1_w_up"], full["ffn1_w_down"], "ffn1_bwd",
        ride=_chips_ride([pairs[name] for name in late_names]))
    from_chips = dict(zip(late_names, got))
    dlam4 = dlam.reshape(SUBLANES, 2, 2, gh, n_state)
    dlam_in = jnp.transpose(dlam4, (2, 0, 1, 3, 4)).reshape(2, SUBLANES, n_grp, n_state)
    hg = gp // 2
    dbb_r = jnp.concatenate([_diag_blocks(dbc[h][:, :hg], gh) for h in range(2)], axis=0)
    dbb_i = jnp.concatenate([_diag_blocks(dbc[h][:, hg:], gh) for h in range(2)], axis=0)
    da_re, da_im, dldt, dbre_t, dbim_t = _s5_params_bwd(a_re, a_im, ldt, b_re_t, b_im_t, dlam_in, dbb_r, dbb_i)
    dc_re = jnp.concatenate([_diag_blocks(dcc[h][:hg], gh) for h in range(2)], axis=0)
    dc_im = -jnp.concatenate([_diag_blocks(dcc[h][hg:], gh) for h in range(2)], axis=0)

    grads_rep = {
        "g_ffn1": dg_ffn1, "g_mix": dg_mix, "b_gate": dbg, "ssm_a_re": da_re[None], "ssm_a_im": da_im[None],
        "ssm_log_dt": dldt.reshape(1, n_grp), "ssm_b_re": jnp.transpose(dbre_t, (0, 2, 1))[None],
        "ssm_b_im": jnp.transpose(dbim_t, (0, 2, 1))[None], "ssm_c_re": jnp.transpose(dc_re, (0, 2, 1))[None],
        "ssm_c_im": jnp.transpose(dc_im, (0, 2, 1))[None], "ssm_d": ddsk, "g_ffn2": dg_ffn2, "g_final": dg_final.reshape(d),
    }

    rep_shapes = [w[n].shape for n in REPLICATED]
    small_g_shapes = rep_shapes + [(n_meta, d), (3, dc), (1, 1)]
    gsmall = _pack([grads_rep[n] for n in REPLICATED] + [dh0[0:n_meta], dcw[0:3], loss_blk[0:1, 0:1]])
    dw_gate, (gall,) = _dw(n1, da1, "dw_ffn1_gate", ride=_gather_ride([gsmall]))
    dblocks.update({
        "ffn1_w_gate": _blocks_from_full(dw_gate, "col"),
        "ffn1_w_up": _blocks_from_full(_dw(n1, db1, "dw_ffn1_up"), "col"),
        "ffn1_w_down": jnp.transpose(_blocks_from_full(_dw(do1, s1, "dw_ffn1_down"), "col"), (0, 2, 1)),
    })
    early_names = [name for name, _ in early]
    pairs.update(zip(early_names, pair_sums(early_names, "early")))
    from_chips.update(zip(early_names, _exchange(_chips_ride([pairs[name] for name in early_names]), "reduce_chips_early")))

    out_g, out_d, out_m, out_v = {}, {}, {}, {}
    for name, _ in BIG:
        fc = from_chips[name]
        out_g[name], out_d[name], out_m[name], out_v[name] = _adamw(
            w[name], mom_m[name], mom_v[name], [(pairs[name], None), (fc, 0), (fc, 1), (fc, 2)], me_chip, "adamw_" + name)

    zer = [jnp.zeros((n_meta, d), F32), jnp.zeros((3, dc), F32), jnp.zeros((1, 1), F32)]
    gr, dr, mr, vr = [o[0] for o in _adamw(
        _pack([w[n] for n in REPLICATED] + zer)[None], _pack([mom_m[n] for n in REPLICATED] + zer)[None],
        _pack([mom_v[n] for n in REPLICATED] + zer)[None], [(gall, b) for b in range(NDEV)], None, "adamw_replicated")]
    g_list = _unpack(gr, small_g_shapes)
    out_g.update(zip(REPLICATED, g_list[:len(REPLICATED)]))
    out_d.update(zip(REPLICATED, _unpack(dr, rep_shapes)))
    out_m.update(zip(REPLICATED, _unpack(mr, rep_shapes)))
    out_v.update(zip(REPLICATED, _unpack(vr, rep_shapes)))

    loss = g_list[-1][0, 0]
    g_meta = lax.dynamic_slice_in_dim(g_list[-3], me * mcols, mcols, axis=1)
    g_cw = lax.dynamic_slice_in_dim(g_list[-2], me * ccols, ccols, axis=1).reshape(conv_w.shape)
    tiny = ("meta_tokens", "conv_w")
    tiny_shapes = [meta_tokens.shape, conv_w.shape]
    gt, dt_, mt, vt = [o[0] for o in _adamw(
        _pack([w[n] for n in tiny])[None], _pack([mom_m[n] for n in tiny])[None], _pack([mom_v[n] for n in tiny])[None],
        [(_pack([g_meta, g_cw])[None], 0)], None, "adamw_tiny")]
    out_g.update(zip(tiny, _unpack(gt, tiny_shapes)))
    out_d.update(zip(tiny, _unpack(dt_, tiny_shapes)))
    out_m.update(zip(tiny, _unpack(mt, tiny_shapes)))
    out_v.update(zip(tiny, _unpack(vt, tiny_shapes)))

    grad_x = dh0[n_meta:t_real][None]
    return (loss, grad_x, *[out_g[n] for n in WEIGHTS], *[out_d[n] for n in WEIGHTS],
            *[out_m[n] for n in WEIGHTS], *[out_v[n] for n in WEIGHTS])
```

```python
import functools

import jax
import jax.numpy as jnp
from jax import lax
from jax.experimental import pallas as pl
from jax.experimental.pallas import tpu as pltpu

F32 = jnp.float32
BF16 = jnp.bfloat16
MESH = pl.DeviceIdType.MESH
NDEV = 8
SLAB_COLS = 1024
RMS_EPS = 1e-6
TOKEN_TILE = 320
MIX_TILE = 320
ROW_ALIGN = 128
SUBLANES = 8
SCAN_LANES = 512
FFN_CHUNK = 4096
VMEM_LIMIT_BYTES = 56 * 1024 * 1024

ADAM_LR = 0.001
ADAM_B1 = 0.9
ADAM_B2 = 0.999
ADAM_EPS = 1e-08
ADAM_WD = 0.01
ADAM_STEP = 10

_VM = pl.BlockSpec(memory_space=pltpu.VMEM)
_ANY = pl.BlockSpec(memory_space=pl.ANY)


def _params(sem=("arbitrary",)):
    return pltpu.CompilerParams(dimension_semantics=sem, vmem_limit_bytes=VMEM_LIMIT_BYTES)


def _dot(a, b):
    return jnp.dot(a, b, preferred_element_type=F32)


def _dot_nt(a, b):
    return lax.dot_general(a, b, (((1,), (1,)), ((), ())), preferred_element_type=F32)


def _dot_tn(a, b):
    return lax.dot_general(a, b, (((0,), (0,)), ((), ())), preferred_element_type=F32)


def _tile(rows, most):
    return next(k for k in range(most - most % 16, 0, -16) if rows % k == 0)


def _chunks(n, step):
    return [(s, min(s + step, n)) for s in range(0, n, step)]


def _gather_plan(x_refs, out_refs, send_sems, recv_sems, local_sems):
    n = len(x_refs)
    x, y, c = lax.axis_index("x"), lax.axis_index("y"), lax.axis_index("c")
    me, sibling = (x, y, c), (x, y, 1 - c)
    chips = [(1 - x, y), (x, 1 - y), (1 - x, 1 - y)]

    def copy(i, k, block, to, src=None):
        slot = out_refs[i].at[4 * block[0] + 2 * block[1] + block[2]]
        return pltpu.make_async_remote_copy(
            src_ref=slot if src is None else src, dst_ref=slot,
            send_sem=send_sems.at[7 * i + k], recv_sem=recv_sems.at[7 * i + k], device_id=to, device_id_type=MESH)

    def mine():
        return [pltpu.make_async_copy(x_refs[i], out_refs[i].at[4 * x + 2 * y + c], local_sems.at[i]) for i in range(n)]

    def first():
        out = []
        for i in range(n):
            out.append(copy(i, 0, me, sibling, src=x_refs[i]))
            out += [copy(i, 1 + j, me, (*chip, c), src=x_refs[i]) for j, chip in enumerate(chips)]
        return out

    def start():
        for cp in mine() + first():
            cp.start()

    def finish():
        passed = []
        for j, chip in enumerate(chips):
            for i in range(n):
                copy(i, 1 + j, (*chip, c), me).wait_recv()
                cp = copy(i, 4 + j, (*chip, c), sibling)
                cp.start()
                passed.append(cp)
        for i in range(n):
            copy(i, 0, sibling, me).wait_recv()
            for j, chip in enumerate(chips):
                copy(i, 4 + j, (*chip, 1 - c), me).wait_recv()
        for cp in first() + passed:
            cp.wait_send()
        for cp in mine():
            cp.wait()

    return start, finish


def _pair_plan(g_refs, out_refs, send_sems, recv_sems):
    x, y, c = lax.axis_index("x"), lax.axis_index("y"), lax.axis_index("c")

    def copies():
        return [pltpu.make_async_remote_copy(
            src_ref=g_refs[i].at[2 * j + (1 - c)], dst_ref=out_refs[i].at[j],
            send_sem=send_sems.at[4 * i + j], recv_sem=recv_sems.at[4 * i + j],
            device_id=(x, y, 1 - c), device_id_type=MESH) for i in range(len(g_refs)) for j in range(4)]

    def start():
        for cp in copies():
            cp.start()

    def finish():
        for cp in copies():
            cp.wait()

    return start, finish


def _chips_plan(p_refs, out_refs, send_sems, recv_sems):
    x, y, c = lax.axis_index("x"), lax.axis_index("y"), lax.axis_index("c")

    def copies():
        return [pltpu.make_async_remote_copy(
            src_ref=p_refs[i].at[2 * px + py], dst_ref=out_refs[i].at[k],
            send_sem=send_sems.at[3 * i + k], recv_sem=recv_sems.at[3 * i + k],
            device_id=(px, py, c), device_id_type=MESH)
            for i in range(len(p_refs)) for k, (px, py) in enumerate([(1 - x, y), (x, 1 - y), (1 - x, 1 - y)])]

    def start():
        for cp in copies():
            cp.start()

    def finish():
        for cp in copies():
            cp.wait()

    return start, finish


def _gather_ride(shards):
    n = len(shards)
    return dict(plan=_gather_plan, arrays=list(shards),
                out_shape=[jax.ShapeDtypeStruct((NDEV, *s.shape), s.dtype) for s in shards],
                sems=[pltpu.SemaphoreType.DMA((7 * n,)), pltpu.SemaphoreType.DMA((7 * n,)), pltpu.SemaphoreType.DMA((n,))])


def _pair_ride(blocks):
    n = len(blocks)
    return dict(plan=_pair_plan, arrays=list(blocks),
                out_shape=[jax.ShapeDtypeStruct((4, *b.shape[1:]), b.dtype) for b in blocks],
                sems=[pltpu.SemaphoreType.DMA((4 * n,)), pltpu.SemaphoreType.DMA((4 * n,))])


def _chips_ride(partials):
    n = len(partials)
    return dict(plan=_chips_plan, arrays=list(partials),
                out_shape=[jax.ShapeDtypeStruct((3, *p.shape[1:]), p.dtype) for p in partials],
                sems=[pltpu.SemaphoreType.DMA((3 * n,)), pltpu.SemaphoreType.DMA((3 * n,))])


def _exchange(ride, name):
    n = len(ride["arrays"])

    def body(*refs):
        start, finish = ride["plan"](refs[:n], refs[n:2 * n], *refs[2 * n:])
        start()
        finish()

    return pl.pallas_call(
        body, name=name, out_shape=ride["out_shape"], in_specs=[_ANY] * n, out_specs=[_ANY] * n, scratch_shapes=ride["sems"],
    )(*ride["arrays"])


def _grid_call(body, name, steps, in_specs, out_specs, out_shape, scratch_shapes, args, ride=None):
    if ride is None:
        outs = pl.pallas_call(body, name=name, grid=(steps,), in_specs=in_specs, out_specs=out_specs, out_shape=out_shape,
                              scratch_shapes=scratch_shapes, compiler_params=_params())(*args)
        return list(outs), []
    n_in, n_out, n_scr, n_ride, n_sems = len(in_specs), len(out_specs), len(scratch_shapes), len(ride["arrays"]), len(ride["sems"])

    def carrying(*refs):
        ins, r_in = refs[:n_in], refs[n_in:n_in + n_ride]
        o0 = n_in + n_ride
        outs, r_out = refs[o0:o0 + n_out], refs[o0 + n_out:o0 + n_out + n_ride]
        s0 = o0 + n_out + n_ride
        scratch, sems = refs[s0:s0 + n_scr], refs[s0 + n_scr:s0 + n_scr + n_sems]
        start, finish = ride["plan"](r_in, r_out, *sems)
        pl.when(pl.program_id(0) == 0)(start)
        body(*ins, *outs, *scratch)
        pl.when(pl.program_id(0) == steps - 1)(finish)

    outs = pl.pallas_call(
        carrying, name=name, grid=(steps,), in_specs=list(in_specs) + [_ANY] * n_ride, out_specs=list(out_specs) + [_ANY] * n_ride,
        out_shape=list(out_shape) + ride["out_shape"], scratch_shapes=list(scratch_shapes) + ride["sems"],
        compiler_params=_params())(*args, *ride["arrays"])
    return list(outs[:n_out]), list(outs[n_out:])


def _row_block(rows):
    return rows if rows <= 512 else next(k for k in (512, 256, 128, rows) if rows % k == 0)


def _add_pairs(gs, core, b, name):
    k, r, n = b.shape
    tr = _row_block(r)

    def body(core_ref, a_ref, b_ref, o_ref):
        o_ref[0] = (a_ref[0, 0].astype(F32) + b_ref[0].astype(F32)).astype(o_ref.dtype)

    spec = pl.BlockSpec((1, tr, n), lambda j, i, c: (j, i, 0))
    return pl.pallas_call(
        body, name=name,
        grid_spec=pltpu.PrefetchScalarGridSpec(
            num_scalar_prefetch=1, grid=(k, r // tr),
            in_specs=[pl.BlockSpec((1, 1, tr, n), lambda j, i, c: (j, c[0], i, 0)), spec], out_specs=spec),
        out_shape=jax.ShapeDtypeStruct(b.shape, b.dtype), compiler_params=_params(("arbitrary", "arbitrary")),
    )(core.reshape(1), gs.reshape(k, 2, r, n), b)


def _adamw(w, m, v, parts, sel, name):
    _, r, n = w.shape
    tr = _row_block(r)
    nparts = len(parts)
    bc1 = 1.0 - ADAM_B1 ** ADAM_STEP
    bc2 = 1.0 - ADAM_B2 ** ADAM_STEP

    def body(sel_ref, *refs):
        w_ref, m_ref, v_ref = refs[:3]
        p_refs = refs[3:3 + nparts]
        g_ref, d_ref, nm_ref, nv_ref = refs[3 + nparts:]
        g = p_refs[0][...].astype(F32)
        for p in p_refs[1:]:
            g = g + p[...].astype(F32)
        nm = ADAM_B1 * m_ref[...] + (1.0 - ADAM_B1) * g
        nv = ADAM_B2 * v_ref[...] + (1.0 - ADAM_B2) * (g * g)
        m_hat = nm / bc1
        v_hat = nv / bc2
        g_ref[...] = g
        d_ref[...] = -ADAM_LR * (m_hat / (jnp.sqrt(v_hat) + ADAM_EPS) + ADAM_WD * w_ref[...])
        nm_ref[...] = nm
        nv_ref[...] = nv

    def part_spec(idx):
        if idx is None:
            return pl.BlockSpec((1, tr, n), lambda i, s: (s[0], i, 0))
        return pl.BlockSpec((1, tr, n), lambda i, s, idx=idx: (idx, i, 0))

    spec = pl.BlockSpec((1, tr, n), lambda i, s: (0, i, 0))
    out = jax.ShapeDtypeStruct((1, r, n), F32)
    return pl.pallas_call(
        body, name=name,
        grid_spec=pltpu.PrefetchScalarGridSpec(
            num_scalar_prefetch=1, grid=(r // tr,),
            in_specs=[spec] * 3 + [part_spec(idx) for _, idx in parts], out_specs=[spec] * 4),
        out_shape=[out] * 4, compiler_params=_params(),
    )(jnp.zeros((1,), jnp.int32) if sel is None else sel.reshape(1), w, m, v, *[p for p, _ in parts])


def _rms_parts(h, g):
    r = lax.rsqrt(jnp.mean(h * h, axis=-1, keepdims=True) + RMS_EPS)
    xhat = h * r
    return r, xhat, xhat * g


def _rms_bwd(dn, g, r, xhat):
    dxh = dn * g
    return r * (dxh - xhat * jnp.mean(dxh * xhat, axis=-1, keepdims=True))


def _loss_tile(h, tgt, g, lo, hi, loss_ref, dg_ref):
    tm, d = h.shape
    i = pl.program_id(0)

    @pl.when(i == 0)
    def _():
        loss_ref[...] = jnp.zeros_like(loss_ref)
        dg_ref[...] = jnp.zeros_like(dg_ref)

    r, xhat, y = _rms_parts(h, g)
    row = i * tm + lax.broadcasted_iota(jnp.int32, (tm, 1), 0)
    err = jnp.where((row >= lo) & (row < hi), y - tgt, 0.0)
    loss_ref[...] += jnp.full(loss_ref.shape, 0.5 * jnp.sum(jnp.mean(err * err, axis=-1, keepdims=True)), F32)
    dy = err * (1.0 / d)
    dg_ref[...] += jnp.sum(dy * xhat, axis=0, keepdims=True)
    return _rms_bwd(dy, g, r, xhat)


def _ffn_fwd(h, g, wg, wu, wd, name, ride=None, head=None):
    t, d = h.shape
    f = wg.shape[1]
    tm = _tile(t, TOKEN_TILE)
    chunks = _chunks(f, FFN_CHUNK)

    def body(h_ref, g_ref, wg_ref, wu_ref, wd_ref, *rest):
        t_ref, gh_ref = rest[:2] if head else (None, None)
        o_ref, a_ref, b_ref = rest[2:5] if head else rest
        hv = h_ref[...]
        n = _rms_parts(hv, g_ref[...])[2].astype(BF16)
        acc = jnp.zeros((tm, d), F32)
        for s, e in chunks:
            a = _dot(n, wg_ref[:, s:e])
            b = _dot(n, wu_ref[:, s:e])
            a_ref[:, s:e] = a.astype(BF16)
            b_ref[:, s:e] = b.astype(BF16)
            acc = acc + _dot((a * jax.nn.sigmoid(a) * b).astype(BF16), wd_ref[s:e, :])
        out = hv + 0.5 * acc
        o_ref[...] = _loss_tile(out, t_ref[...], gh_ref[...], head[2], head[3], rest[5], rest[6]) if head else out

    tile = pl.BlockSpec((tm, d), lambda i: (i, 0))
    wide = pl.BlockSpec((tm, f), lambda i: (i, 0))
    in_specs, args = [tile, _VM, _VM, _VM, _VM], (h, g, wg, wu, wd)
    out_specs = [tile, wide, wide]
    out_shape = [jax.ShapeDtypeStruct((t, d), F32), jax.ShapeDtypeStruct((t, f), BF16), jax.ShapeDtypeStruct((t, f), BF16)]
    if head:
        in_specs, args = in_specs + [tile, _VM], args + (head[0], head[1])
        out_specs = out_specs + [pl.BlockSpec((SUBLANES, 128), lambda i: (0, 0)), pl.BlockSpec((1, d), lambda i: (0, 0))]
        out_shape = out_shape + [jax.ShapeDtypeStruct((SUBLANES, 128), F32), jax.ShapeDtypeStruct((1, d), F32)]
    return _grid_call(body, name, t // tm, in_specs, out_specs, out_shape, [], args, ride)


def _ffn_bwd(h, dh_out, a16, b16, g, wg, wu, wd, name, ride=None):
    t, d = h.shape
    f = wg.shape[1]
    tm = _tile(t, TOKEN_TILE)
    chunks = _chunks(f, FFN_CHUNK)

    def body(h_ref, dho_ref, a_ref, b_ref, g_ref, wg_ref, wu_ref, wd_ref, dh_ref, dg_ref, n_ref, da_ref, db_ref, s_ref, do_ref):
        @pl.when(pl.program_id(0) == 0)
        def _():
            dg_ref[...] = jnp.zeros_like(dg_ref)

        hv = h_ref[...]
        gv = g_ref[...]
        r, xhat, n32 = _rms_parts(hv, gv)
        dho = dho_ref[...]
        do = (0.5 * dho).astype(BF16)
        dn = jnp.zeros((tm, d), F32)
        for s, e in chunks:
            a = a_ref[:, s:e].astype(F32)
            b = b_ref[:, s:e].astype(F32)
            sig = jax.nn.sigmoid(a)
            sa = a * sig
            ds = _dot_nt(do, wd_ref[s:e, :])
            da = (ds * b * (sig * (1.0 + a * (1.0 - sig)))).astype(BF16)
            db = (ds * sa).astype(BF16)
            s_ref[:, s:e] = (sa * b).astype(BF16)
            da_ref[:, s:e] = da
            db_ref[:, s:e] = db
            dn = dn + _dot_nt(da, wg_ref[:, s:e]) + _dot_nt(db, wu_ref[:, s:e])
        dh_ref[...] = dho + _rms_bwd(dn, gv, r, xhat)
        dg_ref[...] += jnp.sum(dn * xhat, axis=0, keepdims=True)
        n_ref[...] = n32.astype(BF16)
        do_ref[...] = do

    tile = pl.BlockSpec((tm, d), lambda i: (i, 0))
    wide = pl.BlockSpec((tm, f), lambda i: (i, 0))
    one = pl.BlockSpec((1, d), lambda i: (0, 0))
    return _grid_call(
        body, name, t // tm, [tile, tile, wide, wide, _VM, _VM, _VM, _VM], [tile, one, tile, wide, wide, wide, tile],
        [jax.ShapeDtypeStruct((t, d), F32), jax.ShapeDtypeStruct((1, d), F32),
         jax.ShapeDtypeStruct((t, d), BF16), jax.ShapeDtypeStruct((t, f), BF16),
         jax.ShapeDtypeStruct((t, f), BF16), jax.ShapeDtypeStruct((t, f), BF16),
         jax.ShapeDtypeStruct((t, d), BF16)],
        [], (h, dh_out, a16, b16, g, wg, wu, wd), ride)


def _dw(a, b, name, ride=None):
    t, m = a.shape
    n = b.shape[1]
    bn = next(k for k in (512, 256, n) if n % k == 0)

    def body(a_ref, b_ref, o_ref):
        o_ref[...] = _dot_tn(a_ref[...], b_ref[...]).astype(BF16)

    (out,), got = _grid_call(
        body, name, n // bn, [_VM, pl.BlockSpec((t, bn), lambda j: (0, j))], [pl.BlockSpec((m, bn), lambda j: (0, j))],
        [jax.ShapeDtypeStruct((m, n), BF16)], [], (a, b), ride)
    return (out, got) if ride else out


def _to_bf16(arrays, name):
    k = len(arrays)

    def body(*refs):
        for x_ref, o_ref in zip(refs[:k], refs[k:]):
            o_ref[...] = x_ref[...].astype(BF16)

    return pl.pallas_call(
        body, name=name, out_shape=[jax.ShapeDtypeStruct(a.shape, BF16) for a in arrays],
        compiler_params=pltpu.CompilerParams(vmem_limit_bytes=VMEM_LIMIT_BYTES),
    )(*arrays)


def _s5_discretise(a_re, a_im, log_dt, b_re, b_im):
    dt = jnp.exp(log_dt)
    mag = jnp.exp(a_re * dt)
    lam_re = mag * jnp.cos(a_im * dt)
    lam_im = mag * jnp.sin(a_im * dt)
    den = a_re * a_re + a_im * a_im
    q_re = ((lam_re - 1.0) * a_re + lam_im * a_im) / den
    q_im = (lam_im * a_re - (lam_re - 1.0) * a_im) / den
    bb_re = q_re[:, None, :] * b_re - q_im[:, None, :] * b_im
    bb_im = q_re[:, None, :] * b_im + q_im[:, None, :] * b_re
    return lam_re, lam_im, bb_re, bb_im


def _s5_params_fwd(a_re, a_im, log_dt, b_re, b_im):
    g, p = a_re.shape
    c = b_re.shape[1]

    def body(are_ref, aim_ref, ldt_ref, bre_ref, bim_ref, pwr_ref, pwi_ref, bbr_ref, bbi_ref):
        lr, li, bbr, bbi = _s5_discretise(are_ref[...], aim_ref[...], ldt_ref[...], bre_ref[...], bim_ref[...])
        bbr_ref[...] = bbr
        bbi_ref[...] = bbi
        pr, pi = lr, li
        pwr_ref[0] = pr
        pwi_ref[0] = pi
        for k in range(1, SUBLANES):
            pr, pi = pr * lr - pi * li, pr * li + pi * lr
            pwr_ref[k] = pr
            pwi_ref[k] = pi

    return pl.pallas_call(
        body, name="s5_params_fwd",
        out_shape=[jax.ShapeDtypeStruct((SUBLANES, g, p), F32), jax.ShapeDtypeStruct((SUBLANES, g, p), F32),
                   jax.ShapeDtypeStruct((g, c, p), F32), jax.ShapeDtypeStruct((g, c, p), F32)],
    )(a_re, a_im, log_dt, b_re, b_im)


def _s5_params_bwd(a_re, a_im, log_dt, b_re, b_im, dlam, dbb_re, dbb_im):
    g, p = a_re.shape
    c = b_re.shape[1]

    def body(are_ref, aim_ref, ldt_ref, bre_ref, bim_ref, dlam_ref, dbr_ref, dbi_ref,
             dare_ref, daim_ref, dldt_ref, dbre_ref, dbim_ref):
        dlr = jnp.sum(dlam_ref[0], axis=0)
        dli = jnp.sum(dlam_ref[1], axis=0)
        _, vjp = jax.vjp(_s5_discretise, are_ref[...], aim_ref[...], ldt_ref[...], bre_ref[...], bim_ref[...])
        dare, daim, dldt, dbre, dbim = vjp((dlr, dli, dbr_ref[...], dbi_ref[...]))
        dare_ref[...] = dare
        daim_ref[...] = daim
        dldt_ref[...] = dldt
        dbre_ref[...] = dbre
        dbim_ref[...] = dbim

    return pl.pallas_call(
        body, name="s5_params_bwd",
        out_shape=[jax.ShapeDtypeStruct((g, p), F32), jax.ShapeDtypeStruct((g, p), F32),
                   jax.ShapeDtypeStruct((g, 1), F32), jax.ShapeDtypeStruct((g, c, p), F32),
                   jax.ShapeDtypeStruct((g, c, p), F32)],
    )(a_re, a_im, log_dt, b_re, b_im, dlam, dbb_re, dbb_im)


def _scan_chunks(gp):
    hg = gp // 2
    w = min(SCAN_LANES, hg)
    return w, [(half * hg + k * w, half * gp + k * w, half * gp + hg + k * w) for half in range(2) for k in range(hg // w)]


def _cmul_acc(xr, xi, tr, ti, sr, si):
    return xr + tr * sr - ti * si, xi + tr * si + ti * sr


def _scan_fwd(buf_ref, row0, tm, ltab_ref, cin_ref, cout_ref, gp):
    w, chunks = _scan_chunks(gp)
    for lo_t, lo_r, lo_i in chunks:
        def body(r, carry, lo_t=lo_t, lo_r=lo_r, lo_i=lo_i):
            cr, ci = carry
            row = pl.multiple_of(row0 + r * SUBLANES, SUBLANES)
            xr = buf_ref[pl.ds(row, SUBLANES), lo_r:lo_r + w]
            xi = buf_ref[pl.ds(row, SUBLANES), lo_i:lo_i + w]
            for tab, shift in ((0, 1), (2, 2), (4, 4)):
                xr, xi = _cmul_acc(xr, xi, ltab_ref[tab, :, lo_t:lo_t + w], ltab_ref[tab + 1, :, lo_t:lo_t + w],
                                   pltpu.roll(xr, shift, 0), pltpu.roll(xi, shift, 0))
            xr, xi = _cmul_acc(xr, xi, ltab_ref[6, :, lo_t:lo_t + w], ltab_ref[7, :, lo_t:lo_t + w], cr, ci)
            buf_ref[pl.ds(row, SUBLANES), lo_r:lo_r + w] = xr
            buf_ref[pl.ds(row, SUBLANES), lo_i:lo_i + w] = xi
            last = SUBLANES - 1
            return (jnp.broadcast_to(xr[last:last + 1], (SUBLANES, w)), jnp.broadcast_to(xi[last:last + 1], (SUBLANES, w)))

        cr, ci = lax.fori_loop(0, tm // SUBLANES, body,
                               (cin_ref[0:SUBLANES, lo_r:lo_r + w], cin_ref[0:SUBLANES, lo_i:lo_i + w]), unroll=True)
        if cout_ref is not None:
            cout_ref[0:SUBLANES, lo_r:lo_r + w] = cr
            cout_ref[0:SUBLANES, lo_i:lo_i + w] = ci


def _scan_rev(g_ref, hext_ref, tm, ltab_ref, gc_ref, dlam_ref, gp):
    w, chunks = _scan_chunks(gp)
    nb = tm // SUBLANES
    for lo_t, lo_r, lo_i in chunks:
        def body(k, carry, lo_t=lo_t, lo_r=lo_r, lo_i=lo_i):
            cr, ci, ar, ai = carry
            row = pl.multiple_of((nb - 1 - k) * SUBLANES, SUBLANES)
            xr = g_ref[pl.ds(row, SUBLANES), lo_r:lo_r + w]
            xi = g_ref[pl.ds(row, SUBLANES), lo_i:lo_i + w]
            for tab, shift in ((8, 7), (10, 6), (12, 4)):
                xr, xi = _cmul_acc(xr, xi, ltab_ref[tab, :, lo_t:lo_t + w], ltab_ref[tab + 1, :, lo_t:lo_t + w],
                                   pltpu.roll(xr, shift, 0), pltpu.roll(xi, shift, 0))
            xr, xi = _cmul_acc(xr, xi, ltab_ref[14, :, lo_t:lo_t + w], ltab_ref[15, :, lo_t:lo_t + w], cr, ci)
            g_ref[pl.ds(row, SUBLANES), lo_r:lo_r + w] = xr
            g_ref[pl.ds(row, SUBLANES), lo_i:lo_i + w] = xi
            first = lax.broadcasted_iota(jnp.int32, (SUBLANES, w), 0) == 0
            prev = pl.ds(row, SUBLANES)
            here = pl.ds(row + SUBLANES, SUBLANES)
            hpr = jnp.where(first, pltpu.roll(hext_ref[prev, lo_r:lo_r + w], 1, 0), pltpu.roll(hext_ref[here, lo_r:lo_r + w], 1, 0))
            hpi = jnp.where(first, pltpu.roll(hext_ref[prev, lo_i:lo_i + w], 1, 0), pltpu.roll(hext_ref[here, lo_i:lo_i + w], 1, 0))
            ar = ar + xr * hpr + xi * hpi
            ai = ai - xr * hpi + xi * hpr
            return (jnp.broadcast_to(xr[0:1], (SUBLANES, w)), jnp.broadcast_to(xi[0:1], (SUBLANES, w)), ar, ai)

        cr, ci, ar, ai = lax.fori_loop(
            0, nb, body, (gc_ref[:, lo_r:lo_r + w], gc_ref[:, lo_i:lo_i + w], dlam_ref[:, lo_r:lo_r + w], dlam_ref[:, lo_i:lo_i + w]),
            unroll=True)
        gc_ref[:, lo_r:lo_r + w] = cr
        gc_ref[:, lo_i:lo_i + w] = ci
        dlam_ref[:, lo_r:lo_r + w] = ar
        dlam_ref[:, lo_i:lo_i + w] = ai


def _conv_taps(cw, cext_ref, cin, tm):
    return (cw[0:1] * cext_ref[SUBLANES - 2:SUBLANES - 2 + tm, :] + cw[1:2] * cext_ref[SUBLANES - 1:SUBLANES - 1 + tm, :]
            + cw[2:3] * cin)


def _mix_fwd(h, gm, win, bg, bc, cc, dsk, wglu, cw, wco, wo, ltab, dims):
    d, ds, dc, gp = dims
    t = h.shape[0]
    tm = _tile(t, MIX_TILE)
    nt = t // tm
    dsh = ds // 2
    o1, o2, o3 = ds + dc, ds + 2 * dc, ds + 3 * dc
    ncols = o3 + 2 * d

    def body(h_ref, gm_ref, win_ref, bg_ref, bc_ref, cc_ref, dsk_ref, wglu_ref, cw_ref, wco_ref, wo_ref, ltab_ref,
             h2_ref, p_ref, hs_ref, y5_ref, z_ref, yc_ref, hbuf_ref, carry_ref, cext_ref):
        @pl.when(pl.program_id(0) == 0)
        def _():
            carry_ref[...] = jnp.zeros_like(carry_ref)
            cext_ref[0:SUBLANES, :] = jnp.zeros((SUBLANES, dc), F32)

        hv = h_ref[...]
        bg = bg_ref[...]
        u = _rms_parts(hv, gm_ref[...])[2].astype(BF16)
        us = _dot(u, win_ref[:, 0:ds])
        v = _dot(u, win_ref[:, ds:o1])
        gb = _dot(u, win_ref[:, o1:o2])
        gcv = _dot(u, win_ref[:, o2:o3])
        gs = jax.nn.sigmoid(_dot(u, win_ref[:, o3:o3 + d]) + bg[:, 0:d])
        gcg = jax.nn.sigmoid(_dot(u, win_ref[:, o3 + d:o3 + 2 * d]) + bg[:, d:2 * d])
        us16 = us.astype(BF16)
        p_ref[:, 0:ds] = us16
        p_ref[:, ds:o1] = v.astype(BF16)
        p_ref[:, o1:o2] = gb.astype(BF16)
        p_ref[:, o2:o3] = gcv.astype(BF16)
        p_ref[:, o3:o3 + d] = gs.astype(BF16)
        p_ref[:, o3 + d:ncols] = gcg.astype(BF16)
        for half in range(2):
            hbuf_ref[:, half * gp:(half + 1) * gp] = _dot(us16[:, half * dsh:(half + 1) * dsh], bc_ref[half])
        _scan_fwd(hbuf_ref, 0, tm, ltab_ref, carry_ref, carry_ref, gp)
        hs_ref[...] = hbuf_ref[...].astype(BF16)
        y5 = jnp.concatenate([_dot(hs_ref[:, half * gp:(half + 1) * gp], cc_ref[half]) for half in range(2)], axis=1) + dsk_ref[...] * us
        y5_ref[...] = y5.astype(BF16)
        z = _dot(jax.nn.gelu(y5).astype(BF16), wglu_ref[...])
        z_ref[...] = z.astype(BF16)
        ys = z[:, 0:d] * jax.nn.sigmoid(z[:, d:2 * d])
        cin = gcv * v
        cext_ref[SUBLANES:SUBLANES + tm, :] = cin
        yc = _dot((gb * _conv_taps(cw_ref[...], cext_ref, cin, tm)).astype(BF16), wco_ref[...])
        yc_ref[...] = yc.astype(BF16)
        h2_ref[...] = hv + _dot((gs * ys + gcg * yc).astype(BF16), wo_ref[...])
        cext_ref[0:SUBLANES, :] = cext_ref[tm:tm + SUBLANES, :]

    def tile(cols):
        return pl.BlockSpec((tm, cols), lambda i: (i, 0))

    def bf(cols):
        return jax.ShapeDtypeStruct((t, cols), BF16)

    return pl.pallas_call(
        body, name="mix_fwd", grid=(nt,),
        in_specs=[tile(d)] + [_VM] * 11,
        out_specs=[tile(d), tile(ncols), tile(2 * gp), tile(ds), tile(2 * d), tile(d)],
        out_shape=[jax.ShapeDtypeStruct((t, d), F32), bf(ncols), bf(2 * gp), bf(ds), bf(2 * d), bf(d)],
        scratch_shapes=[pltpu.VMEM((tm, 2 * gp), F32), pltpu.VMEM((SUBLANES, 2 * gp), F32), pltpu.VMEM((SUBLANES + tm, dc), F32)],
        compiler_params=_params(),
    )(h, gm, win, bg, bc, cc, dsk, wglu, cw, wco, wo, ltab)


HALO = 16


def _mix_bwd_gates(dh2, p16, y516, z16, yc16, wglu, cw, wco, wo, dims):
    d, ds, dc, gp = dims
    t = dh2.shape[0]
    tm = _tile(t, TOKEN_TILE)
    nt = t // tm
    o1, o2, o3 = ds + dc, ds + 2 * dc, ds + 3 * dc
    ncols = o3 + 2 * d

    def body(dh2_ref, p_ref, halo_ref, y5_ref, z_ref, yc_ref, wglu_ref, cw_ref, wco_ref, wo_ref,
             dp_ref, dy5_ref, ge_ref, dz_ref, cg_ref, dyc_ref, mx_ref, dh216_ref, dbg_ref, dcw_ref, cext_ref, dcvext_ref):
        j = pl.program_id(0)

        @pl.when(j == 0)
        def _():
            dbg_ref[...] = jnp.zeros_like(dbg_ref)
            dcw_ref[...] = jnp.zeros_like(dcw_ref)
            dcvext_ref[tm:tm + SUBLANES, :] = jnp.zeros((SUBLANES, dc), F32)

        before = halo_ref[:, o2:o3].astype(F32) * halo_ref[:, ds:o1].astype(F32)
        cext_ref[0:SUBLANES, :] = jnp.where(j == nt - 1, 0.0, before[HALO - SUBLANES:HALO])
        cw_v = cw_ref[...]
        v = p_ref[:, ds:o1].astype(F32)
        gb = p_ref[:, o1:o2].astype(F32)
        gcv = p_ref[:, o2:o3].astype(F32)
        gs = p_ref[:, o3:o3 + d].astype(F32)
        gcg = p_ref[:, o3 + d:ncols].astype(F32)
        z1 = z_ref[:, 0:d].astype(F32)
        sz = jax.nn.sigmoid(z_ref[:, d:2 * d].astype(F32))
        ys = z1 * sz
        yc = yc_ref[...].astype(F32)
        ge, gelu_vjp = jax.vjp(jax.nn.gelu, y5_ref[...].astype(F32))
        cin = gcv * v
        cext_ref[SUBLANES:SUBLANES + tm, :] = cin
        cv = _conv_taps(cw_v, cext_ref, cin, tm)

        dh216 = dh2_ref[...].astype(BF16)
        dmixed = _dot_nt(dh216, wo_ref[...])
        dys = dmixed * gs
        dyc16 = (dmixed * gcg).astype(BF16)
        dpgs = dmixed * ys * gs * (1.0 - gs)
        dpgc = dmixed * yc * gcg * (1.0 - gcg)
        dz16 = jnp.concatenate([dys * sz, dys * z1 * sz * (1.0 - sz)], axis=1).astype(BF16)
        dy5_ref[...] = gelu_vjp(_dot_nt(dz16, wglu_ref[...]))[0].astype(BF16)
        dcg = _dot_nt(dyc16, wco_ref[...])
        dcv = dcg * gb
        dcvext_ref[0:tm, :] = dcv
        dcin = cw_v[2:3] * dcv + cw_v[1:2] * dcvext_ref[1:1 + tm, :] + cw_v[0:1] * dcvext_ref[2:2 + tm, :]
        dcw_ref[0:1, :] += jnp.sum(dcv * cext_ref[SUBLANES - 2:SUBLANES - 2 + tm, :], axis=0, keepdims=True)
        dcw_ref[1:2, :] += jnp.sum(dcv * cext_ref[SUBLANES - 1:SUBLANES - 1 + tm, :], axis=0, keepdims=True)
        dcw_ref[2:3, :] += jnp.sum(dcv * cin, axis=0, keepdims=True)
        dcvext_ref[tm:tm + SUBLANES, :] = dcvext_ref[0:SUBLANES, :]
        dbg_ref[...] += jnp.concatenate([jnp.sum(dpgs, axis=0, keepdims=True), jnp.sum(dpgc, axis=0, keepdims=True)], axis=1)
        dp_ref[:, 0:ds] = jnp.zeros((tm, ds), BF16)
        dp_ref[:, ds:o1] = (dcin * gcv).astype(BF16)
        dp_ref[:, o1:o2] = (dcg * cv).astype(BF16)
        dp_ref[:, o2:o3] = (dcin * v).astype(BF16)
        dp_ref[:, o3:o3 + d] = dpgs.astype(BF16)
        dp_ref[:, o3 + d:ncols] = dpgc.astype(BF16)
        ge_ref[...] = ge.astype(BF16)
        dz_ref[...] = dz16
        cg_ref[...] = (gb * cv).astype(BF16)
        dyc_ref[...] = dyc16
        mx_ref[...] = (gs * ys + gcg * yc).astype(BF16)
        dh216_ref[...] = dh216

    def rev(cols):
        return pl.BlockSpec((tm, cols), lambda j: (nt - 1 - j, 0))

    def bf(cols):
        return jax.ShapeDtypeStruct((t, cols), BF16)

    halo = pl.BlockSpec((HALO, ncols), lambda j: (jnp.maximum((nt - 1 - j) * (tm // HALO) - 1, 0), 0))
    return pl.pallas_call(
        body, name="mix_bwd_gates", grid=(nt,),
        in_specs=[rev(d), rev(ncols), halo, rev(ds), rev(2 * d), rev(d), _VM, _VM, _VM, _VM],
        out_specs=[rev(ncols), rev(ds), rev(ds), rev(2 * d), rev(dc), rev(d), rev(d), rev(d), _VM, _VM],
        out_shape=[bf(ncols), bf(ds), bf(ds), bf(2 * d), bf(dc), bf(d), bf(d), bf(d),
                   jax.ShapeDtypeStruct((1, 2 * d), F32), jax.ShapeDtypeStruct((SUBLANES, dc), F32)],
        scratch_shapes=[pltpu.VMEM((SUBLANES + tm, dc), F32), pltpu.VMEM((tm + SUBLANES, dc), F32)],
        compiler_params=_params(),
    )(dh2, p16, p16, y516, z16, yc16, wglu, cw, wco, wo)


def _mix_bwd_scan(dy516, hs16, p16, bc, cc, dsk, ltab, dims):
    d, ds, dc, gp = dims
    t = dy516.shape[0]
    tm = _tile(t, TOKEN_TILE)
    nt = t // tm
    dsh = ds // 2

    def body(dy5_ref, hs_ref, halo_ref, us_ref, bc_ref, cc_ref, dsk_ref, ltab_ref,
             dus_ref, ddsk_ref, dlam_ref, dbc_ref, dcc_ref, hext_ref, gbuf_ref, gcarry_ref):
        j = pl.program_id(0)

        @pl.when(j == 0)
        def _():
            for ref in (ddsk_ref, dlam_ref, dbc_ref, dcc_ref, gcarry_ref):
                ref[...] = jnp.zeros_like(ref)

        before = jnp.where(j == nt - 1, 0.0, halo_ref[...].astype(F32)[HALO - 1:HALO])
        hext_ref[0:SUBLANES, :] = jnp.broadcast_to(before, (SUBLANES, 2 * gp))
        hext_ref[SUBLANES:SUBLANES + tm, :] = hs_ref[...].astype(F32)
        dy516v = dy5_ref[...]
        for half in range(2):
            gbuf_ref[:, half * gp:(half + 1) * gp] = _dot_nt(dy516v[:, half * dsh:(half + 1) * dsh], cc_ref[half])
        _scan_rev(gbuf_ref, hext_ref, tm, ltab_ref, gcarry_ref, dlam_ref, gp)
        dus = []
        for half in range(2):
            g16 = gbuf_ref[:, half * gp:(half + 1) * gp].astype(BF16)
            dus.append(_dot_nt(g16, bc_ref[half]))
            dbc_ref[half] += _dot_tn(us_ref[:, half * dsh:(half + 1) * dsh], g16)
            dcc_ref[half] += _dot_tn(hs_ref[:, half * gp:(half + 1) * gp], dy516v[:, half * dsh:(half + 1) * dsh])
        dy5 = dy516v.astype(F32)
        dus_ref[...] = (jnp.concatenate(dus, axis=1) + dsk_ref[...] * dy5).astype(BF16)
        ddsk_ref[...] += jnp.sum(dy5 * us_ref[...].astype(F32), axis=0, keepdims=True)

    def rev(cols):
        return pl.BlockSpec((tm, cols), lambda j: (nt - 1 - j, 0))

    halo = pl.BlockSpec((HALO, 2 * gp), lambda j: (jnp.maximum((nt - 1 - j) * (tm // HALO) - 1, 0), 0))
    return pl.pallas_call(
        body, name="mix_bwd_scan", grid=(nt,),
        in_specs=[rev(ds), rev(2 * gp), halo, rev(ds), _VM, _VM, _VM, _VM],
        out_specs=[rev(ds), _VM, _VM, _VM, _VM],
        out_shape=[jax.ShapeDtypeStruct((t, ds), BF16), jax.ShapeDtypeStruct((1, ds), F32),
                   jax.ShapeDtypeStruct((SUBLANES, 2 * gp), F32),
                   jax.ShapeDtypeStruct((2, dsh, gp), F32), jax.ShapeDtypeStruct((2, gp, dsh), F32)],
        scratch_shapes=[pltpu.VMEM((SUBLANES + tm, 2 * gp), F32), pltpu.VMEM((tm, 2 * gp), F32), pltpu.VMEM((SUBLANES, 2 * gp), F32)],
        compiler_params=_params(),
    )(dy516, hs16, hs16, p16, bc, cc, dsk, ltab)


def _mix_bwd_in(h, dh2, dp16, dus16, gm, win, dims):
    d, ds, dc, gp = dims
    t = h.shape[0]
    tm = _tile(t, TOKEN_TILE)
    ncols = dp16.shape[1]

    def body(h_ref, dh2_ref, dp_ref, dus_ref, gm_ref, win_ref, dh1_ref, u_ref, dpf_ref, dgm_ref):
        @pl.when(pl.program_id(0) == 0)
        def _():
            dgm_ref[...] = jnp.zeros_like(dgm_ref)

        gmv = gm_ref[...]
        r, xhat, n32 = _rms_parts(h_ref[...], gmv)
        du = _dot_nt(dus_ref[...], win_ref[:, 0:ds]) + _dot_nt(dp_ref[:, ds:ncols], win_ref[:, ds:ncols])
        dh1_ref[...] = dh2_ref[...] + _rms_bwd(du, gmv, r, xhat)
        dgm_ref[...] += jnp.sum(du * xhat, axis=0, keepdims=True)
        u_ref[...] = n32.astype(BF16)
        dpf_ref[:, 0:ds] = dus_ref[...]
        dpf_ref[:, ds:ncols] = dp_ref[:, ds:ncols]

    def tile(cols):
        return pl.BlockSpec((tm, cols), lambda i: (i, 0))

    return pl.pallas_call(
        body, name="mix_bwd_in", grid=(t // tm,),
        in_specs=[tile(d), tile(d), tile(ncols), tile(ds), _VM, _VM],
        out_specs=[tile(d), tile(d), tile(ncols), pl.BlockSpec((1, d), lambda i: (0, 0))],
        out_shape=[jax.ShapeDtypeStruct((t, d), F32), jax.ShapeDtypeStruct((t, d), BF16),
                   jax.ShapeDtypeStruct((t, ncols), BF16), jax.ShapeDtypeStruct((1, d), F32)],
        compiler_params=_params(),
    )(h, dh2, dp16, dus16, gm, win)


def _pad_rows(a, rows, axis=0):
    pad = [(0, 0)] * a.ndim
    pad[axis] = (0, rows - a.shape[axis])
    return jnp.pad(a, pad)


def _as_rows(a):
    flat = a.reshape(-1)
    n = -(-flat.shape[0] // SLAB_COLS) * SLAB_COLS
    return jnp.pad(flat, (0, n - flat.shape[0])).reshape(-1, SLAB_COLS)


def _pack(arrs):
    rows = jnp.concatenate([_as_rows(a) for a in arrs], axis=0)
    return _pad_rows(rows, -(-rows.shape[0] // 16) * 16)


def _unpack(slab, shapes):
    out, r = [], 0
    for shp in shapes:
        size = 1
        for s in shp:
            size *= s
        n = -(-size // SLAB_COLS)
        out.append(slab[r:r + n].reshape(-1)[:size].reshape(shp))
        r += n
    return out


def _block_diag(blocks):
    n, a, b = blocks.shape
    eye = jnp.eye(n, dtype=blocks.dtype)
    return (blocks[:, :, None, :] * eye[:, None, :, None]).reshape(n * a, n * b)


def _diag_blocks(mat, n):
    a, b = mat.shape[0] // n, mat.shape[1] // n
    eye = jnp.eye(n, dtype=mat.dtype)
    return jnp.sum(mat.reshape(n, a, n, b) * eye[:, None, :, None], axis=2)


BIG = (("ffn1_w_gate", "col"), ("ffn1_w_up", "col"), ("ffn1_w_down", "row"), ("w_in", "col"), ("ssm_w_glu", "col"),
       ("conv_w_out", "col"), ("w_o", "row"), ("ffn2_w_gate", "col"), ("ffn2_w_up", "col"), ("ffn2_w_down", "row"))
REPLICATED = ("g_ffn1", "g_mix", "b_gate", "ssm_a_re", "ssm_a_im", "ssm_log_dt", "ssm_b_re", "ssm_b_im", "ssm_c_re",
              "ssm_c_im", "ssm_d", "g_ffn2", "g_final")
WEIGHTS = ("meta_tokens", "g_ffn1", "ffn1_w_gate", "ffn1_w_up", "ffn1_w_down", "g_mix", "w_in", "b_gate", "ssm_a_re",
           "ssm_a_im", "ssm_log_dt", "ssm_b_re", "ssm_b_im", "ssm_c_re", "ssm_c_im", "ssm_d", "ssm_w_glu", "conv_w",
           "conv_w_out", "w_o", "g_ffn2", "ffn2_w_gate", "ffn2_w_up", "ffn2_w_down", "g_final")
N_EARLY = 3


def _full_from_blocks(blocks, kind):
    n, r, c = blocks.shape
    if kind == "col":
        return jnp.transpose(blocks, (1, 0, 2)).reshape(r, n * c)
    return blocks.reshape(n * r, c)


def _blocks_from_full(full, kind):
    if kind == "col":
        r, nc = full.shape
        return jnp.transpose(full.reshape(r, NDEV, nc // NDEV), (1, 0, 2))
    nr, c = full.shape
    return full.reshape(NDEV, nr // NDEV, c)


def kernel(x, meta_tokens, g_ffn1, ffn1_w_gate, ffn1_w_up, ffn1_w_down, g_mix, w_in, b_gate, ssm_a_re, ssm_a_im, ssm_log_dt, ssm_b_re, ssm_b_im, ssm_c_re, ssm_c_im, ssm_d, ssm_w_glu, conv_w, conv_w_out, w_o, g_ffn2, ffn2_w_gate, ffn2_w_up, ffn2_w_down, g_final, loss_target, m_meta_tokens, m_g_ffn1, m_ffn1_w_gate, m_ffn1_w_up, m_ffn1_w_down, m_g_mix, m_w_in, m_b_gate, m_ssm_a_re, m_ssm_a_im, m_ssm_log_dt, m_ssm_b_re, m_ssm_b_im, m_ssm_c_re, m_ssm_c_im, m_ssm_d, m_ssm_w_glu, m_conv_w, m_conv_w_out, m_w_o, m_g_ffn2, m_ffn2_w_gate, m_ffn2_w_up, m_ffn2_w_down, m_g_final, v_meta_tokens, v_g_ffn1, v_ffn1_w_gate, v_ffn1_w_up, v_ffn1_w_down, v_g_mix, v_w_in, v_b_gate, v_ssm_a_re, v_ssm_a_im, v_ssm_log_dt, v_ssm_b_re, v_ssm_b_im, v_ssm_c_re, v_ssm_c_im, v_ssm_d, v_ssm_w_glu, v_conv_w, v_conv_w_out, v_w_o, v_g_ffn2, v_ffn2_w_gate, v_ffn2_w_up, v_ffn2_w_down, v_g_final):
    args = dict(locals())
    w = {n: args[n] for n in WEIGHTS}
    mom_m = {n: args["m_" + n] for n in WEIGHTS}
    mom_v = {n: args["v_" + n] for n in WEIGHTS}

    seq, d = x.shape[1], x.shape[2]
    n_meta = meta_tokens.shape[0]
    ds = ssm_d.shape[1]
    n_grp, n_state = ssm_a_re.shape[1], ssm_a_re.shape[2]
    gp = n_grp * n_state
    dc = conv_w.shape[3] * NDEV
    dims = (d, ds, dc, gp)
    t_real = n_meta + seq
    t_pad = -(-t_real // ROW_ALIGN) * ROW_ALIGN
    me_chip = 2 * lax.axis_index("x") + lax.axis_index("y")
    me_core = lax.axis_index("c")
    me = 2 * me_chip + me_core
    mcols, ccols = d // NDEV, dc // NDEV

    cw_shard = _pad_rows(_pad_rows(conv_w.reshape(3, ccols), SUBLANES), 128, axis=1)
    shard16 = dict(zip([name for name, _ in BIG], _to_bf16([w[name][0] for name, _ in BIG], "weights_to_bf16")))
    early, late = BIG[:N_EARLY], BIG[N_EARLY:]
    got = _exchange(_gather_ride([shard16[name] for name, _ in early] + [meta_tokens, cw_shard]), "gather_first")
    full = {name: _full_from_blocks(got[i], kind) for i, (name, kind) in enumerate(early)}
    meta_full = _full_from_blocks(got[-2], "col")
    cw_rows = _pad_rows(_full_from_blocks(got[-1][:, 0:3, 0:ccols], "col"), SUBLANES)

    a_re, a_im, ldt = ssm_a_re[0], ssm_a_im[0], ssm_log_dt[0].reshape(n_grp, 1)
    b_re_t = jnp.transpose(ssm_b_re[0], (0, 2, 1))
    b_im_t = jnp.transpose(ssm_b_im[0], (0, 2, 1))
    pw_r, pw_i, bb_r, bb_i = _s5_params_fwd(a_re, a_im, ldt, b_re_t, b_im_t)
    pw_r = pw_r.reshape(SUBLANES, gp)
    pw_i = pw_i.reshape(SUBLANES, gp)
    sub = jnp.arange(SUBLANES)[:, None]

    def fwd_tab(p, k):
        return jnp.where(sub >= k, p[k - 1][None, :], 0.0)

    def rev_tab(p, k):
        return jnp.where(sub <= SUBLANES - 1 - k, p[k - 1][None, :], 0.0)

    ltab = jnp.stack(
        [fwd_tab(pw_r, 1), fwd_tab(pw_i, 1), fwd_tab(pw_r, 2), fwd_tab(pw_i, 2), fwd_tab(pw_r, 4), fwd_tab(pw_i, 4), pw_r, pw_i,
         rev_tab(pw_r, 1), -rev_tab(pw_i, 1), rev_tab(pw_r, 2), -rev_tab(pw_i, 2), rev_tab(pw_r, 4), -rev_tab(pw_i, 4),
         pw_r[::-1], -pw_i[::-1]], axis=0)
    gh = n_grp // 2
    bc = jnp.stack([jnp.concatenate([_block_diag(bb_r[h * gh:(h + 1) * gh]), _block_diag(bb_i[h * gh:(h + 1) * gh])], axis=1)
                    for h in range(2)]).astype(BF16)
    c_re_t = jnp.transpose(ssm_c_re[0], (0, 2, 1))
    c_im_t = jnp.transpose(ssm_c_im[0], (0, 2, 1))
    cc = jnp.stack([jnp.concatenate([_block_diag(c_re_t[h * gh:(h + 1) * gh]), -_block_diag(c_im_t[h * gh:(h + 1) * gh])], axis=0)
                    for h in range(2)]).astype(BF16)

    zpad = jnp.zeros((t_pad - t_real, d), F32)
    h0 = jnp.concatenate([meta_full, x[0], zpad], axis=0)
    tgt = jnp.concatenate([jnp.zeros((n_meta, d), F32), loss_target[0], zpad], axis=0)
    (h1, a1, b1), got = _ffn_fwd(h0, g_ffn1, full["ffn1_w_gate"], full["ffn1_w_up"], full["ffn1_w_down"], "ffn1_fwd",
                                 ride=_gather_ride([shard16[name] for name, _ in late]))
    full.update({name: _full_from_blocks(got[i], kind) for i, (name, kind) in enumerate(late)})
    h2, *saved = _mix_fwd(h1, g_mix, full["w_in"], b_gate, bc, cc, ssm_d, full["ssm_w_glu"], cw_rows, full["conv_w_out"],
                          full["w_o"], ltab, dims)
    (dh3, a2, b2, loss_blk, dg_final), _ = _ffn_fwd(h2, g_ffn2, full["ffn2_w_gate"], full["ffn2_w_up"], full["ffn2_w_down"], "ffn2_fwd",
                                                 head=(tgt, g_final.reshape(1, d), n_meta, t_real))

    (dh2, dg_ffn2, n2, da2, db2, s2, do2), _ = _ffn_bwd(
        h2, dh3, a2, b2, g_ffn2, full["ffn2_w_gate"], full["ffn2_w_up"], full["ffn2_w_down"], "ffn2_bwd")
    p16, hs16, y516, z16, yc16 = saved
    dp_part, dy516, ge16, dz16, cg16, dyc16, mx16, dh216, dbg, dcw = _mix_bwd_gates(
        dh2, p16, y516, z16, yc16, full["ssm_w_glu"], cw_rows, full["conv_w_out"], full["w_o"], dims)
    dus16, ddsk, dlam, dbc, dcc = _mix_bwd_scan(dy516, hs16, p16, bc, cc, ssm_d, ltab, dims)
    dh1, u16, dp16, dg_mix = _mix_bwd_in(h1, dh2, dp_part, dus16, g_mix, full["w_in"], dims)
    dblocks = {
        "w_in": _blocks_from_full(_dw(u16, dp16, "dw_in"), "col"),
        "ssm_w_glu": _blocks_from_full(_dw(ge16, dz16, "dw_glu"), "col"),
        "conv_w_out": _blocks_from_full(_dw(cg16, dyc16, "dw_conv_out"), "col"),
        "w_o": _blocks_from_full(_dw(mx16, dh216, "dw_o"), "row"),
        "ffn2_w_gate": _blocks_from_full(_dw(n2, da2, "dw_ffn2_gate"), "col"),
        "ffn2_w_up": _blocks_from_full(_dw(n2, db2, "dw_ffn2_up"), "col"),
        "ffn2_w_down": jnp.transpose(_blocks_from_full(_dw(do2, s2, "dw_ffn2_down"), "col"), (0, 2, 1)),
    }

    def pair_sums(names, tag):
        gs = [dblocks[name] for name in names]
        from_sibling = _exchange(_pair_ride(gs), "reduce_pair_" + tag)
        return [_add_pairs(g, me_core, b, "reduce_pair_add_" + name) for g, b, name in zip(gs, from_sibling, names)]

    late_names = [name for name, _ in late]
    pairs = dict(zip(late_names, pair_sums(late_names, "late")))
    (dh0, dg_ffn1, n1, da1, db1, s1, do1), got = _ffn_bwd(
        h0, dh1, a1, b1, g_ffn1, full["ffn1_w_gate"], full["ffn1_w_up"], full["ffn1_w_down"], "ffn1_bwd",
        ride=_chips_ride([pairs[name] for name in late_names]))
    from_chips = dict(zip(late_names, got))
    dlam4 = dlam.reshape(SUBLANES, 2, 2, gh, n_state)
    dlam_in = jnp.transpose(dlam4, (2, 0, 1, 3, 4)).reshape(2, SUBLANES, n_grp, n_state)
    hg = gp // 2
    dbb_r = jnp.concatenate([_diag_blocks(dbc[h][:, :hg], gh) for h in range(2)], axis=0)
    dbb_i = jnp.concatenate([_diag_blocks(dbc[h][:, hg:], gh) for h in range(2)], axis=0)
    da_re, da_im, dldt, dbre_t, dbim_t = _s5_params_bwd(a_re, a_im, ldt, b_re_t, b_im_t, dlam_in, dbb_r, dbb_i)
    dc_re = jnp.concatenate([_diag_blocks(dcc[h][:hg], gh) for h in range(2)], axis=0)
    dc_im = -jnp.concatenate([_diag_blocks(dcc[h][hg:], gh) for h in range(2)], axis=0)

    grads_rep = {
        "g_ffn1": dg_ffn1, "g_mix": dg_mix, "b_gate": dbg, "ssm_a_re": da_re[None], "ssm_a_im": da_im[None],
        "ssm_log_dt": dldt.reshape(1, n_grp), "ssm_b_re": jnp.transpose(dbre_t, (0, 2, 1))[None],
        "ssm_b_im": jnp.transpose(dbim_t, (0, 2, 1))[None], "ssm_c_re": jnp.transpose(dc_re, (0, 2, 1))[None],
        "ssm_c_im": jnp.transpose(dc_im, (0, 2, 1))[None], "ssm_d": ddsk, "g_ffn2": dg_ffn2, "g_final": dg_final.reshape(d),
    }

    rep_shapes = [w[n].shape for n in REPLICATED]
    small_g_shapes = rep_shapes + [(n_meta, d), (3, dc), (1, 1)]
    gsmall = _pack([grads_rep[n] for n in REPLICATED] + [dh0[0:n_meta], dcw[0:3], loss_blk[0:1, 0:1]])
    dw_gate, (gall,) = _dw(n1, da1, "dw_ffn1_gate", ride=_gather_ride([gsmall]))
    dblocks.update({
        "ffn1_w_gate": _blocks_from_full(dw_gate, "col"),
        "ffn1_w_up": _blocks_from_full(_dw(n1, db1, "dw_ffn1_up"), "col"),
        "ffn1_w_down": jnp.transpose(_blocks_from_full(_dw(do1, s1, "dw_ffn1_down"), "col"), (0, 2, 1)),
    })
    early_names = [name for name, _ in early]
    pairs.update(zip(early_names, pair_sums(early_names, "early")))
    from_chips.update(zip(early_names, _exchange(_chips_ride([pairs[name] for name in early_names]), "reduce_chips_early")))

    out_g, out_d, out_m, out_v = {}, {}, {}, {}
    for name, _ in BIG:
        fc = from_chips[name]
        out_g[name], out_d[name], out_m[name], out_v[name] = _adamw(
            w[name], mom_m[name], mom_v[name], [(pairs[name], None), (fc, 0), (fc, 1), (fc, 2)], me_chip, "adamw_" + name)

    zer = [jnp.zeros((n_meta, d), F32), jnp.zeros((3, dc), F32), jnp.zeros((1, 1), F32)]
    gr, dr, mr, vr = [o[0] for o in _adamw(
        _pack([w[n] for n in REPLICATED] + zer)[None], _pack([mom_m[n] for n in REPLICATED] + zer)[None],
        _pack([mom_v[n] for n in REPLICATED] + zer)[None], [(gall, b) for b in range(NDEV)], None, "adamw_replicated")]
    g_list = _unpack(gr, small_g_shapes)
    out_g.update(zip(REPLICATED, g_list[:len(REPLICATED)]))
    out_d.update(zip(REPLICATED, _unpack(dr, rep_shapes)))
    out_m.update(zip(REPLICATED, _unpack(mr, rep_shapes)))
    out_v.update(zip(REPLICATED, _unpack(vr, rep_shapes)))

    loss = g_list[-1][0, 0]
    g_meta = lax.dynamic_slice_in_dim(g_list[-3], me * mcols, mcols, axis=1)
    g_cw = lax.dynamic_slice_in_dim(g_list[-2], me * ccols, ccols, axis=1).reshape(conv_w.shape)
    tiny = ("meta_tokens", "conv_w")
    tiny_shapes = [meta_tokens.shape, conv_w.shape]
    gt, dt_, mt, vt = [o[0] for o in _adamw(
        _pack([w[n] for n in tiny])[None], _pack([mom_m[n] for n in tiny])[None], _pack([mom_v[n] for n in tiny])[None],
        [(_pack([g_meta, g_cw])[None], 0)], None, "adamw_tiny")]
    out_g.update(zip(tiny, _unpack(gt, tiny_shapes)))
    out_d.update(zip(tiny, _unpack(dt_, tiny_shapes)))
    out_m.update(zip(tiny, _unpack(mt, tiny_shapes)))
    out_v.update(zip(tiny, _unpack(vt, tiny_shapes)))

    grad_x = dh0[n_meta:t_real][None]
    return (loss, grad_x, *[out_g[n] for n in WEIGHTS], *[out_d[n] for n in WEIGHTS],
            *[out_m[n] for n in WEIGHTS], *[out_v[n] for n in WEIGHTS])
```

```python
import functools

import jax
import jax.numpy as jnp
from jax import lax
from jax.experimental import pallas as pl
from jax.experimental.pallas import tpu as pltpu

F32 = jnp.float32
BF16 = jnp.bfloat16
MESH = pl.DeviceIdType.MESH
NDEV = 8
SLAB_COLS = 1024
RMS_EPS = 1e-6
TOKEN_TILE = 320
MIX_TILE = 320
ROW_ALIGN = 128
SUBLANES = 8
SCAN_LANES = 512
FFN_CHUNK = 4096
VMEM_LIMIT_BYTES = 56 * 1024 * 1024

ADAM_LR = 0.001
ADAM_B1 = 0.9
ADAM_B2 = 0.999
ADAM_EPS = 1e-08
ADAM_WD = 0.01
ADAM_STEP = 10

_VM = pl.BlockSpec(memory_space=pltpu.VMEM)
_ANY = pl.BlockSpec(memory_space=pl.ANY)


def _params(sem=("arbitrary",)):
    return pltpu.CompilerParams(dimension_semantics=sem, vmem_limit_bytes=VMEM_LIMIT_BYTES)


def _dot(a, b):
    return jnp.dot(a, b, preferred_element_type=F32)


def _dot_nt(a, b):
    return lax.dot_general(a, b, (((1,), (1,)), ((), ())), preferred_element_type=F32)


def _dot_tn(a, b):
    return lax.dot_general(a, b, (((0,), (0,)), ((), ())), preferred_element_type=F32)


def _tile(rows, most):
    return next(k for k in range(most - most % 16, 0, -16) if rows % k == 0)


def _chunks(n, step):
    return [(s, min(s + step, n)) for s in range(0, n, step)]


def _gather_plan(x_refs, out_refs, send_sems, recv_sems, local_sems):
    n = len(x_refs)
    x, y, c = lax.axis_index("x"), lax.axis_index("y"), lax.axis_index("c")
    me, sibling = (x, y, c), (x, y, 1 - c)
    chips = [(1 - x, y), (x, 1 - y), (1 - x, 1 - y)]

    def copy(i, k, block, to, src=None):
        slot = out_refs[i].at[4 * block[0] + 2 * block[1] + block[2]]
        return pltpu.make_async_remote_copy(
            src_ref=slot if src is None else src, dst_ref=slot,
            send_sem=send_sems.at[7 * i + k], recv_sem=recv_sems.at[7 * i + k], device_id=to, device_id_type=MESH)

    def mine():
        return [pltpu.make_async_copy(x_refs[i], out_refs[i].at[4 * x + 2 * y + c], local_sems.at[i]) for i in range(n)]

    def first():
        out = []
        for i in range(n):
            out.append(copy(i, 0, me, sibling, src=x_refs[i]))
            out += [copy(i, 1 + j, me, (*chip, c), src=x_refs[i]) for j, chip in enumerate(chips)]
        return out

    def start():
        for cp in mine() + first():
            cp.start()

    def finish():
        passed = []
        for j, chip in enumerate(chips):
            for i in range(n):
                copy(i, 1 + j, (*chip, c), me).wait_recv()
                cp = copy(i, 4 + j, (*chip, c), sibling)
                cp.start()
                passed.append(cp)
        for i in range(n):
            copy(i, 0, sibling, me).wait_recv()
            for j, chip in enumerate(chips):
                copy(i, 4 + j, (*chip, 1 - c), me).wait_recv()
        for cp in first() + passed:
            cp.wait_send()
        for cp in mine():
            cp.wait()

    return start, finish


def _pair_plan(g_refs, out_refs, send_sems, recv_sems):
    x, y, c = lax.axis_index("x"), lax.axis_index("y"), lax.axis_index("c")

    def copies():
        return [pltpu.make_async_remote_copy(
            src_ref=g_refs[i].at[2 * j + (1 - c)], dst_ref=out_refs[i].at[j],
            send_sem=send_sems.at[4 * i + j], recv_sem=recv_sems.at[4 * i + j],
            device_id=(x, y, 1 - c), device_id_type=MESH) for i in range(len(g_refs)) for j in range(4)]

    def start():
        for cp in copies():
            cp.start()

    def finish():
        for cp in copies():
            cp.wait()

    return start, finish


def _chips_plan(p_refs, out_refs, send_sems, recv_sems):
    x, y, c = lax.axis_index("x"), lax.axis_index("y"), lax.axis_index("c")

    def copies():
        return [pltpu.make_async_remote_copy(
            src_ref=p_refs[i].at[2 * px + py], dst_ref=out_refs[i].at[k],
            send_sem=send_sems.at[3 * i + k], recv_sem=recv_sems.at[3 * i + k],
            device_id=(px, py, c), device_id_type=MESH)
            for i in range(len(p_refs)) for k, (px, py) in enumerate([(1 - x, y), (x, 1 - y), (1 - x, 1 - y)])]

    def start():
        for cp in copies():
            cp.start()

    def finish():
        for cp in copies():
            cp.wait()

    return start, finish


def _gather_ride(shards):
    n = len(shards)
    return dict(plan=_gather_plan, arrays=list(shards),
                out_shape=[jax.ShapeDtypeStruct((NDEV, *s.shape), s.dtype) for s in shards],
                sems=[pltpu.SemaphoreType.DMA((7 * n,)), pltpu.SemaphoreType.DMA((7 * n,)), pltpu.SemaphoreType.DMA((n,))])


def _pair_ride(blocks):
    n = len(blocks)
    return dict(plan=_pair_plan, arrays=list(blocks),
                out_shape=[jax.ShapeDtypeStruct((4, *b.shape[1:]), b.dtype) for b in blocks],
                sems=[pltpu.SemaphoreType.DMA((4 * n,)), pltpu.SemaphoreType.DMA((4 * n,))])


def _chips_ride(partials):
    n = len(partials)
    return dict(plan=_chips_plan, arrays=list(partials),
                out_shape=[jax.ShapeDtypeStruct((3, *p.shape[1:]), p.dtype) for p in partials],
                sems=[pltpu.SemaphoreType.DMA((3 * n,)), pltpu.SemaphoreType.DMA((3 * n,))])


def _exchange(ride, name):
    n = len(ride["arrays"])

    def body(*refs):
        start, finish = ride["plan"](refs[:n], refs[n:2 * n], *refs[2 * n:])
        start()
        finish()

    return pl.pallas_call(
        body, name=name, out_shape=ride["out_shape"], in_specs=[_ANY] * n, out_specs=[_ANY] * n, scratch_shapes=ride["sems"],
    )(*ride["arrays"])


def _grid_call(body, name, steps, in_specs, out_specs, out_shape, scratch_shapes, args, ride=None):
    if ride is None:
        outs = pl.pallas_call(body, name=name, grid=(steps,), in_specs=in_specs, out_specs=out_specs, out_shape=out_shape,
                              scratch_shapes=scratch_shapes, compiler_params=_params())(*args)
        return list(outs), []
    n_in, n_out, n_scr, n_ride, n_sems = len(in_specs), len(out_specs), len(scratch_shapes), len(ride["arrays"]), len(ride["sems"])

    def carrying(*refs):
        ins, r_in = refs[:n_in], refs[n_in:n_in + n_ride]
        o0 = n_in + n_ride
        outs, r_out = refs[o0:o0 + n_out], refs[o0 + n_out:o0 + n_out + n_ride]
        s0 = o0 + n_out + n_ride
        scratch, sems = refs[s0:s0 + n_scr], refs[s0 + n_scr:s0 + n_scr + n_sems]
        start, finish = ride["plan"](r_in, r_out, *sems)
        pl.when(pl.program_id(0) == 0)(start)
        body(*ins, *outs, *scratch)
        pl.when(pl.program_id(0) == steps - 1)(finish)

    outs = pl.pallas_call(
        carrying, name=name, grid=(steps,), in_specs=list(in_specs) + [_ANY] * n_ride, out_specs=list(out_specs) + [_ANY] * n_ride,
        out_shape=list(out_shape) + ride["out_shape"], scratch_shapes=list(scratch_shapes) + ride["sems"],
        compiler_params=_params())(*args, *ride["arrays"])
    return list(outs[:n_out]), list(outs[n_out:])


def _row_block(rows):
    return rows if rows <= 512 else next(k for k in (512, 256, 128, rows) if rows % k == 0)


def _add_pairs(gs, core, b, name):
    k, r, n = b.shape
    tr = _row_block(r)

    def body(core_ref, a_ref, b_ref, o_ref):
        o_ref[0] = (a_ref[0, 0].astype(F32) + b_ref[0].astype(F32)).astype(o_ref.dtype)

    spec = pl.BlockSpec((1, tr, n), lambda j, i, c: (j, i, 0))
    return pl.pallas_call(
        body, name=name,
        grid_spec=pltpu.PrefetchScalarGridSpec(
            num_scalar_prefetch=1, grid=(k, r // tr),
            in_specs=[pl.BlockSpec((1, 1, tr, n), lambda j, i, c: (j, c[0], i, 0)), spec], out_specs=spec),
        out_shape=jax.ShapeDtypeStruct(b.shape, b.dtype), compiler_params=_params(("arbitrary", "arbitrary")),
    )(core.reshape(1), gs.reshape(k, 2, r, n), b)


def _adamw(w, m, v, parts, sel, name):
    _, r, n = w.shape
    tr = _row_block(r)
    nparts = len(parts)
    bc1 = 1.0 - ADAM_B1 ** ADAM_STEP
    bc2 = 1.0 - ADAM_B2 ** ADAM_STEP

    def body(sel_ref, *refs):
        w_ref, m_ref, v_ref = refs[:3]
        p_refs = refs[3:3 + nparts]
        g_ref, d_ref, nm_ref, nv_ref = refs[3 + nparts:]
        g = p_refs[0][...].astype(F32)
        for p in p_refs[1:]:
            g = g + p[...].astype(F32)
        nm = ADAM_B1 * m_ref[...] + (1.0 - ADAM_B1) * g
        nv = ADAM_B2 * v_ref[...] + (1.0 - ADAM_B2) * (g * g)
        m_hat = nm / bc1
        v_hat = nv / bc2
        g_ref[...] = g
        d_ref[...] = -ADAM_LR * (m_hat / (jnp.sqrt(v_hat) + ADAM_EPS) + ADAM_WD * w_ref[...])
        nm_ref[...] = nm
        nv_ref[...] = nv

    def part_spec(idx):
        if idx is None:
            return pl.BlockSpec((1, tr, n), lambda i, s: (s[0], i, 0))
        return pl.BlockSpec((1, tr, n), lambda i, s, idx=idx: (idx, i, 0))

    spec = pl.BlockSpec((1, tr, n), lambda i, s: (0, i, 0))
    out = jax.ShapeDtypeStruct((1, r, n), F32)
    return pl.pallas_call(
        body, name=name,
        grid_spec=pltpu.PrefetchScalarGridSpec(
            num_scalar_prefetch=1, grid=(r // tr,),
            in_specs=[spec] * 3 + [part_spec(idx) for _, idx in parts], out_specs=[spec] * 4),
        out_shape=[out] * 4, compiler_params=_params(),
    )(jnp.zeros((1,), jnp.int32) if sel is None else sel.reshape(1), w, m, v, *[p for p, _ in parts])


def _rms_parts(h, g):
    r = lax.rsqrt(jnp.mean(h * h, axis=-1, keepdims=True) + RMS_EPS)
    xhat = h * r
    return r, xhat, xhat * g


def _rms_bwd(dn, g, r, xhat):
    dxh = dn * g
    return r * (dxh - xhat * jnp.mean(dxh * xhat, axis=-1, keepdims=True))


def _loss_tile(h, tgt, g, lo, hi, loss_ref, dg_ref):
    tm, d = h.shape
    i = pl.program_id(0)

    @pl.when(i == 0)
    def _():
        loss_ref[...] = jnp.zeros_like(loss_ref)
        dg_ref[...] = jnp.zeros_like(dg_ref)

    r, xhat, y = _rms_parts(h, g)
    row = i * tm + lax.broadcasted_iota(jnp.int32, (tm, 1), 0)
    err = jnp.where((row >= lo) & (row < hi), y - tgt, 0.0)
    loss_ref[...] += jnp.full(loss_ref.shape, 0.5 * jnp.sum(jnp.mean(err * err, axis=-1, keepdims=True)), F32)
    dy = err * (1.0 / d)
    dg_ref[...] += jnp.sum(dy * xhat, axis=0, keepdims=True)
    return _rms_bwd(dy, g, r, xhat)


def _ffn_fwd(h, g, wg, wu, wd, name, ride=None, head=None):
    t, d = h.shape
    f = wg.shape[1]
    tm = _tile(t, TOKEN_TILE)
    chunks = _chunks(f, FFN_CHUNK)

    def body(h_ref, g_ref, wg_ref, wu_ref, wd_ref, *rest):
        t_ref, gh_ref = rest[:2] if head else (None, None)
        o_ref, a_ref, b_ref = rest[2:5] if head else rest
        hv = h_ref[...]
        n = _rms_parts(hv, g_ref[...])[2].astype(BF16)
        acc = jnp.zeros((tm, d), F32)
        for s, e in chunks:
            a = _dot(n, wg_ref[:, s:e])
            b = _dot(n, wu_ref[:, s:e])
            a_ref[:, s:e] = a.astype(BF16)
            b_ref[:, s:e] = b.astype(BF16)
            acc = acc + _dot((a * jax.nn.sigmoid(a) * b).astype(BF16), wd_ref[s:e, :])
        out = hv + 0.5 * acc
        o_ref[...] = _loss_tile(out, t_ref[...], gh_ref[...], head[2], head[3], rest[5], rest[6]) if head else out

    tile = pl.BlockSpec((tm, d), lambda i: (i, 0))
    wide = pl.BlockSpec((tm, f), lambda i: (i, 0))
    in_specs, args = [tile, _VM, _VM, _VM, _VM], (h, g, wg, wu, wd)
    out_specs = [tile, wide, wide]
    out_shape = [jax.ShapeDtypeStruct((t, d), F32), jax.ShapeDtypeStruct((t, f), BF16), jax.ShapeDtypeStruct((t, f), BF16)]
    if head:
        in_specs, args = in_specs + [tile, _VM], args + (head[0], head[1])
        out_specs = out_specs + [pl.BlockSpec((SUBLANES, 128), lambda i: (0, 0)), pl.BlockSpec((1, d), lambda i: (0, 0))]
        out_shape = out_shape + [jax.ShapeDtypeStruct((SUBLANES, 128), F32), jax.ShapeDtypeStruct((1, d), F32)]
    return _grid_call(body, name, t // tm, in_specs, out_specs, out_shape, [], args, ride)


def _ffn_bwd(h, dh_out, a16, b16, g, wg, wu, wd, name, ride=None):
    t, d = h.shape
    f = wg.shape[1]
    tm = _tile(t, TOKEN_TILE)
    chunks = _chunks(f, FFN_CHUNK)

    def body(h_ref, dho_ref, a_ref, b_ref, g_ref, wg_ref, wu_ref, wd_ref, dh_ref, dg_ref, n_ref, dab_ref, s_ref, do_ref):
        @pl.when(pl.program_id(0) == 0)
        def _():
            dg_ref[...] = jnp.zeros_like(dg_ref)

        hv = h_ref[...]
        gv = g_ref[...]
        r, xhat, n32 = _rms_parts(hv, gv)
        dho = dho_ref[...]
        do = (0.5 * dho).astype(BF16)
        dn = jnp.zeros((tm, d), F32)
        for s, e in chunks:
            a = a_ref[:, s:e].astype(F32)
            b = b_ref[:, s:e].astype(F32)
            sig = jax.nn.sigmoid(a)
            sa = a * sig
            ds = _dot_nt(do, wd_ref[s:e, :])
            da = (ds * b * (sig * (1.0 + a * (1.0 - sig)))).astype(BF16)
            db = (ds * sa).astype(BF16)
            s_ref[:, s:e] = (sa * b).astype(BF16)
            dab_ref[:, s:e] = da
            dab_ref[:, f + s:f + e] = db
            dn = dn + _dot_nt(da, wg_ref[:, s:e]) + _dot_nt(db, wu_ref[:, s:e])
        dh_ref[...] = dho + _rms_bwd(dn, gv, r, xhat)
        dg_ref[...] += jnp.sum(dn * xhat, axis=0, keepdims=True)
        n_ref[...] = n32.astype(BF16)
        do_ref[...] = do

    tile = pl.BlockSpec((tm, d), lambda i: (i, 0))
    wide = pl.BlockSpec((tm, f), lambda i: (i, 0))
    one = pl.BlockSpec((1, d), lambda i: (0, 0))
    return _grid_call(
        body, name, t // tm, [tile, tile, wide, wide, _VM, _VM, _VM, _VM],
        [tile, one, tile, pl.BlockSpec((tm, 2 * f), lambda i: (i, 0)), wide, tile],
        [jax.ShapeDtypeStruct((t, d), F32), jax.ShapeDtypeStruct((1, d), F32),
         jax.ShapeDtypeStruct((t, d), BF16), jax.ShapeDtypeStruct((t, 2 * f), BF16),
         jax.ShapeDtypeStruct((t, f), BF16), jax.ShapeDtypeStruct((t, d), BF16)],
        [], (h, dh_out, a16, b16, g, wg, wu, wd), ride)


def _dw(a, b, name, ride=None):
    t, m = a.shape
    n = b.shape[1]
    bn = next(k for k in (512, 256, n) if n % k == 0)

    def body(a_ref, b_ref, o_ref):
        o_ref[...] = _dot_tn(a_ref[...], b_ref[...]).astype(BF16)

    (out,), got = _grid_call(
        body, name, n // bn, [_VM, pl.BlockSpec((t, bn), lambda j: (0, j))], [pl.BlockSpec((m, bn), lambda j: (0, j))],
        [jax.ShapeDtypeStruct((m, n), BF16)], [], (a, b), ride)
    return (out, got) if ride else out


def _to_bf16(arrays, name):
    k = len(arrays)

    def body(*refs):
        for x_ref, o_ref in zip(refs[:k], refs[k:]):
            o_ref[...] = x_ref[...].astype(BF16)

    return pl.pallas_call(
        body, name=name, out_shape=[jax.ShapeDtypeStruct(a.shape, BF16) for a in arrays],
        compiler_params=pltpu.CompilerParams(vmem_limit_bytes=VMEM_LIMIT_BYTES),
    )(*arrays)


def _s5_discretise(a_re, a_im, log_dt, b_re, b_im):
    dt = jnp.exp(log_dt)
    mag = jnp.exp(a_re * dt)
    lam_re = mag * jnp.cos(a_im * dt)
    lam_im = mag * jnp.sin(a_im * dt)
    den = a_re * a_re + a_im * a_im
    q_re = ((lam_re - 1.0) * a_re + lam_im * a_im) / den
    q_im = (lam_im * a_re - (lam_re - 1.0) * a_im) / den
    bb_re = q_re[:, None, :] * b_re - q_im[:, None, :] * b_im
    bb_im = q_re[:, None, :] * b_im + q_im[:, None, :] * b_re
    return lam_re, lam_im, bb_re, bb_im


def _s5_params_fwd(a_re, a_im, log_dt, b_re, b_im):
    g, p = a_re.shape
    c = b_re.shape[1]

    def body(are_ref, aim_ref, ldt_ref, bre_ref, bim_ref, pwr_ref, pwi_ref, bbr_ref, bbi_ref):
        lr, li, bbr, bbi = _s5_discretise(are_ref[...], aim_ref[...], ldt_ref[...], bre_ref[...], bim_ref[...])
        bbr_ref[...] = bbr
        bbi_ref[...] = bbi
        pr, pi = lr, li
        pwr_ref[0] = pr
        pwi_ref[0] = pi
        for k in range(1, SUBLANES):
            pr, pi = pr * lr - pi * li, pr * li + pi * lr
            pwr_ref[k] = pr
            pwi_ref[k] = pi

    return pl.pallas_call(
        body, name="s5_params_fwd",
        out_shape=[jax.ShapeDtypeStruct((SUBLANES, g, p), F32), jax.ShapeDtypeStruct((SUBLANES, g, p), F32),
                   jax.ShapeDtypeStruct((g, c, p), F32), jax.ShapeDtypeStruct((g, c, p), F32)],
    )(a_re, a_im, log_dt, b_re, b_im)


def _s5_params_bwd(a_re, a_im, log_dt, b_re, b_im, dlam, dbb_re, dbb_im):
    g, p = a_re.shape
    c = b_re.shape[1]

    def body(are_ref, aim_ref, ldt_ref, bre_ref, bim_ref, dlam_ref, dbr_ref, dbi_ref,
             dare_ref, daim_ref, dldt_ref, dbre_ref, dbim_ref):
        dlr = jnp.sum(dlam_ref[0], axis=0)
        dli = jnp.sum(dlam_ref[1], axis=0)
        _, vjp = jax.vjp(_s5_discretise, are_ref[...], aim_ref[...], ldt_ref[...], bre_ref[...], bim_ref[...])
        dare, daim, dldt, dbre, dbim = vjp((dlr, dli, dbr_ref[...], dbi_ref[...]))
        dare_ref[...] = dare
        daim_ref[...] = daim
        dldt_ref[...] = dldt
        dbre_ref[...] = dbre
        dbim_ref[...] = dbim

    return pl.pallas_call(
        body, name="s5_params_bwd",
        out_shape=[jax.ShapeDtypeStruct((g, p), F32), jax.ShapeDtypeStruct((g, p), F32),
                   jax.ShapeDtypeStruct((g, 1), F32), jax.ShapeDtypeStruct((g, c, p), F32),
                   jax.ShapeDtypeStruct((g, c, p), F32)],
    )(a_re, a_im, log_dt, b_re, b_im, dlam, dbb_re, dbb_im)


def _scan_chunks(gp):
    hg = gp // 2
    w = min(SCAN_LANES, hg)
    return w, [(half * hg + k * w, half * gp + k * w, half * gp + hg + k * w) for half in range(2) for k in range(hg // w)]


def _cmul_acc(xr, xi, tr, ti, sr, si):
    return xr + tr * sr - ti * si, xi + tr * si + ti * sr


def _scan_fwd(buf_ref, row0, tm, ltab_ref, cin_ref, cout_ref, gp):
    w, chunks = _scan_chunks(gp)
    for lo_t, lo_r, lo_i in chunks:
        def body(r, carry, lo_t=lo_t, lo_r=lo_r, lo_i=lo_i):
            cr, ci = carry
            row = pl.multiple_of(row0 + r * SUBLANES, SUBLANES)
            xr = buf_ref[pl.ds(row, SUBLANES), lo_r:lo_r + w]
            xi = buf_ref[pl.ds(row, SUBLANES), lo_i:lo_i + w]
            for tab, shift in ((0, 1), (2, 2), (4, 4)):
                xr, xi = _cmul_acc(xr, xi, ltab_ref[tab, :, lo_t:lo_t + w], ltab_ref[tab + 1, :, lo_t:lo_t + w],
                                   pltpu.roll(xr, shift, 0), pltpu.roll(xi, shift, 0))
            xr, xi = _cmul_acc(xr, xi, ltab_ref[6, :, lo_t:lo_t + w], ltab_ref[7, :, lo_t:lo_t + w], cr, ci)
            buf_ref[pl.ds(row, SUBLANES), lo_r:lo_r + w] = xr
            buf_ref[pl.ds(row, SUBLANES), lo_i:lo_i + w] = xi
            last = SUBLANES - 1
            return (jnp.broadcast_to(xr[last:last + 1], (SUBLANES, w)), jnp.broadcast_to(xi[last:last + 1], (SUBLANES, w)))

        cr, ci = lax.fori_loop(0, tm // SUBLANES, body,
                               (cin_ref[0:SUBLANES, lo_r:lo_r + w], cin_ref[0:SUBLANES, lo_i:lo_i + w]), unroll=True)
        if cout_ref is not None:
            cout_ref[0:SUBLANES, lo_r:lo_r + w] = cr
            cout_ref[0:SUBLANES, lo_i:lo_i + w] = ci


def _scan_rev(g_ref, hext_ref, tm, ltab_ref, gc_ref, dlam_ref, gp):
    w, chunks = _scan_chunks(gp)
    nb = tm // SUBLANES
    for lo_t, lo_r, lo_i in chunks:
        def body(k, carry, lo_t=lo_t, lo_r=lo_r, lo_i=lo_i):
            cr, ci, ar, ai = carry
            row = pl.multiple_of((nb - 1 - k) * SUBLANES, SUBLANES)
            xr = g_ref[pl.ds(row, SUBLANES), lo_r:lo_r + w]
            xi = g_ref[pl.ds(row, SUBLANES), lo_i:lo_i + w]
            for tab, shift in ((8, 7), (10, 6), (12, 4)):
                xr, xi = _cmul_acc(xr, xi, ltab_ref[tab, :, lo_t:lo_t + w], ltab_ref[tab + 1, :, lo_t:lo_t + w],
                                   pltpu.roll(xr, shift, 0), pltpu.roll(xi, shift, 0))
            xr, xi = _cmul_acc(xr, xi, ltab_ref[14, :, lo_t:lo_t + w], ltab_ref[15, :, lo_t:lo_t + w], cr, ci)
            g_ref[pl.ds(row, SUBLANES), lo_r:lo_r + w] = xr
            g_ref[pl.ds(row, SUBLANES), lo_i:lo_i + w] = xi
            first = lax.broadcasted_iota(jnp.int32, (SUBLANES, w), 0) == 0
            prev = pl.ds(row, SUBLANES)
            here = pl.ds(row + SUBLANES, SUBLANES)
            hpr = jnp.where(first, pltpu.roll(hext_ref[prev, lo_r:lo_r + w], 1, 0), pltpu.roll(hext_ref[here, lo_r:lo_r + w], 1, 0))
            hpi = jnp.where(first, pltpu.roll(hext_ref[prev, lo_i:lo_i + w], 1, 0), pltpu.roll(hext_ref[here, lo_i:lo_i + w], 1, 0))
            ar = ar + xr * hpr + xi * hpi
            ai = ai - xr * hpi + xi * hpr
            return (jnp.broadcast_to(xr[0:1], (SUBLANES, w)), jnp.broadcast_to(xi[0:1], (SUBLANES, w)), ar, ai)

        cr, ci, ar, ai = lax.fori_loop(
            0, nb, body, (gc_ref[:, lo_r:lo_r + w], gc_ref[:, lo_i:lo_i + w], dlam_ref[:, lo_r:lo_r + w], dlam_ref[:, lo_i:lo_i + w]),
            unroll=True)
        gc_ref[:, lo_r:lo_r + w] = cr
        gc_ref[:, lo_i:lo_i + w] = ci
        dlam_ref[:, lo_r:lo_r + w] = ar
        dlam_ref[:, lo_i:lo_i + w] = ai


def _conv_taps(cw, cext_ref, cin, tm):
    return (cw[0:1] * cext_ref[SUBLANES - 2:SUBLANES - 2 + tm, :] + cw[1:2] * cext_ref[SUBLANES - 1:SUBLANES - 1 + tm, :]
            + cw[2:3] * cin)


def _mix_fwd(h, gm, win, bg, bc, cc, dsk, wglu, cw, wco, wo, ltab, dims):
    d, ds, dc, gp = dims
    t = h.shape[0]
    tm = _tile(t, MIX_TILE)
    nt = t // tm
    dsh = ds // 2
    o1, o2, o3 = ds + dc, ds + 2 * dc, ds + 3 * dc
    ncols = o3 + 2 * d

    def body(h_ref, gm_ref, win_ref, bg_ref, bc_ref, cc_ref, dsk_ref, wglu_ref, cw_ref, wco_ref, wo_ref, ltab_ref,
             h2_ref, p_ref, hs_ref, y5_ref, z_ref, yc_ref, hbuf_ref, carry_ref, cext_ref):
        @pl.when(pl.program_id(0) == 0)
        def _():
            carry_ref[...] = jnp.zeros_like(carry_ref)
            cext_ref[0:SUBLANES, :] = jnp.zeros((SUBLANES, dc), F32)

        hv = h_ref[...]
        bg = bg_ref[...]
        u = _rms_parts(hv, gm_ref[...])[2].astype(BF16)
        us = _dot(u, win_ref[:, 0:ds])
        v = _dot(u, win_ref[:, ds:o1])
        gb = _dot(u, win_ref[:, o1:o2])
        gcv = _dot(u, win_ref[:, o2:o3])
        gs = jax.nn.sigmoid(_dot(u, win_ref[:, o3:o3 + d]) + bg[:, 0:d])
        gcg = jax.nn.sigmoid(_dot(u, win_ref[:, o3 + d:o3 + 2 * d]) + bg[:, d:2 * d])
        us16 = us.astype(BF16)
        p_ref[:, 0:ds] = us16
        p_ref[:, ds:o1] = v.astype(BF16)
        p_ref[:, o1:o2] = gb.astype(BF16)
        p_ref[:, o2:o3] = gcv.astype(BF16)
        p_ref[:, o3:o3 + d] = gs.astype(BF16)
        p_ref[:, o3 + d:ncols] = gcg.astype(BF16)
        for half in range(2):
            hbuf_ref[:, half * gp:(half + 1) * gp] = _dot(us16[:, half * dsh:(half + 1) * dsh], bc_ref[half])
        _scan_fwd(hbuf_ref, 0, tm, ltab_ref, carry_ref, carry_ref, gp)
        hs_ref[...] = hbuf_ref[...].astype(BF16)
        y5 = jnp.concatenate([_dot(hs_ref[:, half * gp:(half + 1) * gp], cc_ref[half]) for half in range(2)], axis=1) + dsk_ref[...] * us
        y5_ref[...] = y5.astype(BF16)
        z = _dot(jax.nn.gelu(y5).astype(BF16), wglu_ref[...])
        z_ref[...] = z.astype(BF16)
        ys = z[:, 0:d] * jax.nn.sigmoid(z[:, d:2 * d])
        cin = gcv * v
        cext_ref[SUBLANES:SUBLANES + tm, :] = cin
        yc = _dot((gb * _conv_taps(cw_ref[...], cext_ref, cin, tm)).astype(BF16), wco_ref[...])
        yc_ref[...] = yc.astype(BF16)
        h2_ref[...] = hv + _dot((gs * ys + gcg * yc).astype(BF16), wo_ref[...])
        cext_ref[0:SUBLANES, :] = cext_ref[tm:tm + SUBLANES, :]

    def tile(cols):
        return pl.BlockSpec((tm, cols), lambda i: (i, 0))

    def bf(cols):
        return jax.ShapeDtypeStruct((t, cols), BF16)

    return pl.pallas_call(
        body, name="mix_fwd", grid=(nt,),
        in_specs=[tile(d)] + [_VM] * 11,
        out_specs=[tile(d), tile(ncols), tile(2 * gp), tile(ds), tile(2 * d), tile(d)],
        out_shape=[jax.ShapeDtypeStruct((t, d), F32), bf(ncols), bf(2 * gp), bf(ds), bf(2 * d), bf(d)],
        scratch_shapes=[pltpu.VMEM((tm, 2 * gp), F32), pltpu.VMEM((SUBLANES, 2 * gp), F32), pltpu.VMEM((SUBLANES + tm, dc), F32)],
        compiler_params=_params(),
    )(h, gm, win, bg, bc, cc, dsk, wglu, cw, wco, wo, ltab)


HALO = 16


def _mix_bwd_gates(dh2, p16, y516, z16, yc16, wglu, cw, wco, wo, dims):
    d, ds, dc, gp = dims
    t = dh2.shape[0]
    tm = _tile(t, TOKEN_TILE)
    nt = t // tm
    o1, o2, o3 = ds + dc, ds + 2 * dc, ds + 3 * dc
    ncols = o3 + 2 * d

    def body(dh2_ref, p_ref, halo_ref, y5_ref, z_ref, yc_ref, wglu_ref, cw_ref, wco_ref, wo_ref,
             dp_ref, dy5_ref, ge_ref, dz_ref, cg_ref, dyc_ref, mx_ref, dh216_ref, dbg_ref, dcw_ref, cext_ref, dcvext_ref):
        j = pl.program_id(0)

        @pl.when(j == 0)
        def _():
            dbg_ref[...] = jnp.zeros_like(dbg_ref)
            dcw_ref[...] = jnp.zeros_like(dcw_ref)
            dcvext_ref[tm:tm + SUBLANES, :] = jnp.zeros((SUBLANES, dc), F32)

        before = halo_ref[:, o2:o3].astype(F32) * halo_ref[:, ds:o1].astype(F32)
        cext_ref[0:SUBLANES, :] = jnp.where(j == nt - 1, 0.0, before[HALO - SUBLANES:HALO])
        cw_v = cw_ref[...]
        v = p_ref[:, ds:o1].astype(F32)
        gb = p_ref[:, o1:o2].astype(F32)
        gcv = p_ref[:, o2:o3].astype(F32)
        gs = p_ref[:, o3:o3 + d].astype(F32)
        gcg = p_ref[:, o3 + d:ncols].astype(F32)
        z1 = z_ref[:, 0:d].astype(F32)
        sz = jax.nn.sigmoid(z_ref[:, d:2 * d].astype(F32))
        ys = z1 * sz
        yc = yc_ref[...].astype(F32)
        ge, gelu_vjp = jax.vjp(jax.nn.gelu, y5_ref[...].astype(F32))
        cin = gcv * v
        cext_ref[SUBLANES:SUBLANES + tm, :] = cin
        cv = _conv_taps(cw_v, cext_ref, cin, tm)

        dh216 = dh2_ref[...].astype(BF16)
        dmixed = _dot_nt(dh216, wo_ref[...])
        dys = dmixed * gs
        dyc16 = (dmixed * gcg).astype(BF16)
        dpgs = dmixed * ys * gs * (1.0 - gs)
        dpgc = dmixed * yc * gcg * (1.0 - gcg)
        dz16 = jnp.concatenate([dys * sz, dys * z1 * sz * (1.0 - sz)], axis=1).astype(BF16)
        dy5_ref[...] = gelu_vjp(_dot_nt(dz16, wglu_ref[...]))[0].astype(BF16)
        dcg = _dot_nt(dyc16, wco_ref[...])
        dcv = dcg * gb
        dcvext_ref[0:tm, :] = dcv
        dcin = cw_v[2:3] * dcv + cw_v[1:2] * dcvext_ref[1:1 + tm, :] + cw_v[0:1] * dcvext_ref[2:2 + tm, :]
        dcw_ref[0:1, :] += jnp.sum(dcv * cext_ref[SUBLANES - 2:SUBLANES - 2 + tm, :], axis=0, keepdims=True)
        dcw_ref[1:2, :] += jnp.sum(dcv * cext_ref[SUBLANES - 1:SUBLANES - 1 + tm, :], axis=0, keepdims=True)
        dcw_ref[2:3, :] += jnp.sum(dcv * cin, axis=0, keepdims=True)
        dcvext_ref[tm:tm + SUBLANES, :] = dcvext_ref[0:SUBLANES, :]
        dbg_ref[...] += jnp.concatenate([jnp.sum(dpgs, axis=0, keepdims=True), jnp.sum(dpgc, axis=0, keepdims=True)], axis=1)
        dp_ref[:, 0:ds] = jnp.zeros((tm, ds), BF16)
        dp_ref[:, ds:o1] = (dcin * gcv).astype(BF16)
        dp_ref[:, o1:o2] = (dcg * cv).astype(BF16)
        dp_ref[:, o2:o3] = (dcin * v).astype(BF16)
        dp_ref[:, o3:o3 + d] = dpgs.astype(BF16)
        dp_ref[:, o3 + d:ncols] = dpgc.astype(BF16)
        ge_ref[...] = ge.astype(BF16)
        dz_ref[...] = dz16
        cg_ref[...] = (gb * cv).astype(BF16)
        dyc_ref[...] = dyc16
        mx_ref[...] = (gs * ys + gcg * yc).astype(BF16)
        dh216_ref[...] = dh216

    def rev(cols):
        return pl.BlockSpec((tm, cols), lambda j: (nt - 1 - j, 0))

    def bf(cols):
        return jax.ShapeDtypeStruct((t, cols), BF16)

    halo = pl.BlockSpec((HALO, ncols), lambda j: (jnp.maximum((nt - 1 - j) * (tm // HALO) - 1, 0), 0))
    return pl.pallas_call(
        body, name="mix_bwd_gates", grid=(nt,),
        in_specs=[rev(d), rev(ncols), halo, rev(ds), rev(2 * d), rev(d), _VM, _VM, _VM, _VM],
        out_specs=[rev(ncols), rev(ds), rev(ds), rev(2 * d), rev(dc), rev(d), rev(d), rev(d), _VM, _VM],
        out_shape=[bf(ncols), bf(ds), bf(ds), bf(2 * d), bf(dc), bf(d), bf(d), bf(d),
                   jax.ShapeDtypeStruct((1, 2 * d), F32), jax.ShapeDtypeStruct((SUBLANES, dc), F32)],
        scratch_shapes=[pltpu.VMEM((SUBLANES + tm, dc), F32), pltpu.VMEM((tm + SUBLANES, dc), F32)],
        compiler_params=_params(),
    )(dh2, p16, p16, y516, z16, yc16, wglu, cw, wco, wo)


def _mix_bwd_scan(dy516, hs16, p16, bc, cc, dsk, ltab, dims):
    d, ds, dc, gp = dims
    t = dy516.shape[0]
    tm = _tile(t, TOKEN_TILE)
    nt = t // tm
    dsh = ds // 2

    def body(dy5_ref, hs_ref, halo_ref, us_ref, bc_ref, cc_ref, dsk_ref, ltab_ref,
             dus_ref, ddsk_ref, dlam_ref, dbc_ref, dcc_ref, hext_ref, gbuf_ref, gcarry_ref):
        j = pl.program_id(0)

        @pl.when(j == 0)
        def _():
            for ref in (ddsk_ref, dlam_ref, dbc_ref, dcc_ref, gcarry_ref):
                ref[...] = jnp.zeros_like(ref)

        before = jnp.where(j == nt - 1, 0.0, halo_ref[...].astype(F32)[HALO - 1:HALO])
        hext_ref[0:SUBLANES, :] = jnp.broadcast_to(before, (SUBLANES, 2 * gp))
        hext_ref[SUBLANES:SUBLANES + tm, :] = hs_ref[...].astype(F32)
        dy516v = dy5_ref[...]
        for half in range(2):
            gbuf_ref[:, half * gp:(half + 1) * gp] = _dot_nt(dy516v[:, half * dsh:(half + 1) * dsh], cc_ref[half])
        _scan_rev(gbuf_ref, hext_ref, tm, ltab_ref, gcarry_ref, dlam_ref, gp)
        dus = []
        for half in range(2):
            g16 = gbuf_ref[:, half * gp:(half + 1) * gp].astype(BF16)
            dus.append(_dot_nt(g16, bc_ref[half]))
            dbc_ref[half] += _dot_tn(us_ref[:, half * dsh:(half + 1) * dsh], g16)
            dcc_ref[half] += _dot_tn(hs_ref[:, half * gp:(half + 1) * gp], dy516v[:, half * dsh:(half + 1) * dsh])
        dy5 = dy516v.astype(F32)
        dus_ref[...] = (jnp.concatenate(dus, axis=1) + dsk_ref[...] * dy5).astype(BF16)
        ddsk_ref[...] += jnp.sum(dy5 * us_ref[...].astype(F32), axis=0, keepdims=True)

    def rev(cols):
        return pl.BlockSpec((tm, cols), lambda j: (nt - 1 - j, 0))

    halo = pl.BlockSpec((HALO, 2 * gp), lambda j: (jnp.maximum((nt - 1 - j) * (tm // HALO) - 1, 0), 0))
    return pl.pallas_call(
        body, name="mix_bwd_scan", grid=(nt,),
        in_specs=[rev(ds), rev(2 * gp), halo, rev(ds), _VM, _VM, _VM, _VM],
        out_specs=[rev(ds), _VM, _VM, _VM, _VM],
        out_shape=[jax.ShapeDtypeStruct((t, ds), BF16), jax.ShapeDtypeStruct((1, ds), F32),
                   jax.ShapeDtypeStruct((SUBLANES, 2 * gp), F32),
                   jax.ShapeDtypeStruct((2, dsh, gp), F32), jax.ShapeDtypeStruct((2, gp, dsh), F32)],
        scratch_shapes=[pltpu.VMEM((SUBLANES + tm, 2 * gp), F32), pltpu.VMEM((tm, 2 * gp), F32), pltpu.VMEM((SUBLANES, 2 * gp), F32)],
        compiler_params=_params(),
    )(dy516, hs16, hs16, p16, bc, cc, dsk, ltab)


def _mix_bwd_in(h, dh2, dp16, dus16, gm, win, dims):
    d, ds, dc, gp = dims
    t = h.shape[0]
    tm = _tile(t, TOKEN_TILE)
    ncols = dp16.shape[1]

    def body(h_ref, dh2_ref, dp_ref, dus_ref, gm_ref, win_ref, dh1_ref, u_ref, dpf_ref, dgm_ref):
        @pl.when(pl.program_id(0) == 0)
        def _():
            dgm_ref[...] = jnp.zeros_like(dgm_ref)

        gmv = gm_ref[...]
        r, xhat, n32 = _rms_parts(h_ref[...], gmv)
        du = _dot_nt(dus_ref[...], win_ref[:, 0:ds]) + _dot_nt(dp_ref[:, ds:ncols], win_ref[:, ds:ncols])
        dh1_ref[...] = dh2_ref[...] + _rms_bwd(du, gmv, r, xhat)
        dgm_ref[...] += jnp.sum(du * xhat, axis=0, keepdims=True)
        u_ref[...] = n32.astype(BF16)
        dpf_ref[:, 0:ds] = dus_ref[...]
        dpf_ref[:, ds:ncols] = dp_ref[:, ds:ncols]

    def tile(cols):
        return pl.BlockSpec((tm, cols), lambda i: (i, 0))

    return pl.pallas_call(
        body, name="mix_bwd_in", grid=(t // tm,),
        in_specs=[tile(d), tile(d), tile(ncols), tile(ds), _VM, _VM],
        out_specs=[tile(d), tile(d), tile(ncols), pl.BlockSpec((1, d), lambda i: (0, 0))],
        out_shape=[jax.ShapeDtypeStruct((t, d), F32), jax.ShapeDtypeStruct((t, d), BF16),
                   jax.ShapeDtypeStruct((t, ncols), BF16), jax.ShapeDtypeStruct((1, d), F32)],
        compiler_params=_params(),
    )(h, dh2, dp16, dus16, gm, win)


def _pad_rows(a, rows, axis=0):
    pad = [(0, 0)] * a.ndim
    pad[axis] = (0, rows - a.shape[axis])
    return jnp.pad(a, pad)


def _as_rows(a):
    flat = a.reshape(-1)
    n = -(-flat.shape[0] // SLAB_COLS) * SLAB_COLS
    return jnp.pad(flat, (0, n - flat.shape[0])).reshape(-1, SLAB_COLS)


def _pack(arrs):
    rows = jnp.concatenate([_as_rows(a) for a in arrs], axis=0)
    return _pad_rows(rows, -(-rows.shape[0] // 16) * 16)


def _unpack(slab, shapes):
    out, r = [], 0
    for shp in shapes:
        size = 1
        for s in shp:
            size *= s
        n = -(-size // SLAB_COLS)
        out.append(slab[r:r + n].reshape(-1)[:size].reshape(shp))
        r += n
    return out


def _block_diag(blocks):
    n, a, b = blocks.shape
    eye = jnp.eye(n, dtype=blocks.dtype)
    return (blocks[:, :, None, :] * eye[:, None, :, None]).reshape(n * a, n * b)


def _diag_blocks(mat, n):
    a, b = mat.shape[0] // n, mat.shape[1] // n
    eye = jnp.eye(n, dtype=mat.dtype)
    return jnp.sum(mat.reshape(n, a, n, b) * eye[:, None, :, None], axis=2)


BIG = (("ffn1_w_gate", "col"), ("ffn1_w_up", "col"), ("ffn1_w_down", "row"), ("w_in", "col"), ("ssm_w_glu", "col"),
       ("conv_w_out", "col"), ("w_o", "row"), ("ffn2_w_gate", "col"), ("ffn2_w_up", "col"), ("ffn2_w_down", "row"))
REPLICATED = ("g_ffn1", "g_mix", "b_gate", "ssm_a_re", "ssm_a_im", "ssm_log_dt", "ssm_b_re", "ssm_b_im", "ssm_c_re",
              "ssm_c_im", "ssm_d", "g_ffn2", "g_final")
WEIGHTS = ("meta_tokens", "g_ffn1", "ffn1_w_gate", "ffn1_w_up", "ffn1_w_down", "g_mix", "w_in", "b_gate", "ssm_a_re",
           "ssm_a_im", "ssm_log_dt", "ssm_b_re", "ssm_b_im", "ssm_c_re", "ssm_c_im", "ssm_d", "ssm_w_glu", "conv_w",
           "conv_w_out", "w_o", "g_ffn2", "ffn2_w_gate", "ffn2_w_up", "ffn2_w_down", "g_final")
N_EARLY = 3


def _full_from_blocks(blocks, kind):
    n, r, c = blocks.shape
    if kind == "col":
        return jnp.transpose(blocks, (1, 0, 2)).reshape(r, n * c)
    return blocks.reshape(n * r, c)


def _blocks_from_full(full, kind):
    if kind == "col":
        r, nc = full.shape
        return jnp.transpose(full.reshape(r, NDEV, nc // NDEV), (1, 0, 2))
    nr, c = full.shape
    return full.reshape(NDEV, nr // NDEV, c)


def kernel(x, meta_tokens, g_ffn1, ffn1_w_gate, ffn1_w_up, ffn1_w_down, g_mix, w_in, b_gate, ssm_a_re, ssm_a_im, ssm_log_dt, ssm_b_re, ssm_b_im, ssm_c_re, ssm_c_im, ssm_d, ssm_w_glu, conv_w, conv_w_out, w_o, g_ffn2, ffn2_w_gate, ffn2_w_up, ffn2_w_down, g_final, loss_target, m_meta_tokens, m_g_ffn1, m_ffn1_w_gate, m_ffn1_w_up, m_ffn1_w_down, m_g_mix, m_w_in, m_b_gate, m_ssm_a_re, m_ssm_a_im, m_ssm_log_dt, m_ssm_b_re, m_ssm_b_im, m_ssm_c_re, m_ssm_c_im, m_ssm_d, m_ssm_w_glu, m_conv_w, m_conv_w_out, m_w_o, m_g_ffn2, m_ffn2_w_gate, m_ffn2_w_up, m_ffn2_w_down, m_g_final, v_meta_tokens, v_g_ffn1, v_ffn1_w_gate, v_ffn1_w_up, v_ffn1_w_down, v_g_mix, v_w_in, v_b_gate, v_ssm_a_re, v_ssm_a_im, v_ssm_log_dt, v_ssm_b_re, v_ssm_b_im, v_ssm_c_re, v_ssm_c_im, v_ssm_d, v_ssm_w_glu, v_conv_w, v_conv_w_out, v_w_o, v_g_ffn2, v_ffn2_w_gate, v_ffn2_w_up, v_ffn2_w_down, v_g_final):
    args = dict(locals())
    w = {n: args[n] for n in WEIGHTS}
    mom_m = {n: args["m_" + n] for n in WEIGHTS}
    mom_v = {n: args["v_" + n] for n in WEIGHTS}

    seq, d = x.shape[1], x.shape[2]
    n_meta = meta_tokens.shape[0]
    ds = ssm_d.shape[1]
    n_grp, n_state = ssm_a_re.shape[1], ssm_a_re.shape[2]
    gp = n_grp * n_state
    dc = conv_w.shape[3] * NDEV
    dims = (d, ds, dc, gp)
    t_real = n_meta + seq
    t_pad = -(-t_real // ROW_ALIGN) * ROW_ALIGN
    me_chip = 2 * lax.axis_index("x") + lax.axis_index("y")
    me_core = lax.axis_index("c")
    me = 2 * me_chip + me_core
    mcols, ccols = d // NDEV, dc // NDEV

    cw_shard = _pad_rows(_pad_rows(conv_w.reshape(3, ccols), SUBLANES), 128, axis=1)
    shard16 = dict(zip([name for name, _ in BIG], _to_bf16([w[name][0] for name, _ in BIG], "weights_to_bf16")))
    early, late = BIG[:N_EARLY], BIG[N_EARLY:]
    got = _exchange(_gather_ride([shard16[name] for name, _ in early] + [meta_tokens, cw_shard]), "gather_first")
    full = {name: _full_from_blocks(got[i], kind) for i, (name, kind) in enumerate(early)}
    meta_full = _full_from_blocks(got[-2], "col")
    cw_rows = _pad_rows(_full_from_blocks(got[-1][:, 0:3, 0:ccols], "col"), SUBLANES)

    a_re, a_im, ldt = ssm_a_re[0], ssm_a_im[0], ssm_log_dt[0].reshape(n_grp, 1)
    b_re_t = jnp.transpose(ssm_b_re[0], (0, 2, 1))
    b_im_t = jnp.transpose(ssm_b_im[0], (0, 2, 1))
    pw_r, pw_i, bb_r, bb_i = _s5_params_fwd(a_re, a_im, ldt, b_re_t, b_im_t)
    pw_r = pw_r.reshape(SUBLANES, gp)
    pw_i = pw_i.reshape(SUBLANES, gp)
    sub = jnp.arange(SUBLANES)[:, None]

    def fwd_tab(p, k):
        return jnp.where(sub >= k, p[k - 1][None, :], 0.0)

    def rev_tab(p, k):
        return jnp.where(sub <= SUBLANES - 1 - k, p[k - 1][None, :], 0.0)

    ltab = jnp.stack(
        [fwd_tab(pw_r, 1), fwd_tab(pw_i, 1), fwd_tab(pw_r, 2), fwd_tab(pw_i, 2), fwd_tab(pw_r, 4), fwd_tab(pw_i, 4), pw_r, pw_i,
         rev_tab(pw_r, 1), -rev_tab(pw_i, 1), rev_tab(pw_r, 2), -rev_tab(pw_i, 2), rev_tab(pw_r, 4), -rev_tab(pw_i, 4),
         pw_r[::-1], -pw_i[::-1]], axis=0)
    gh = n_grp // 2
    bc = jnp.stack([jnp.concatenate([_block_diag(bb_r[h * gh:(h + 1) * gh]), _block_diag(bb_i[h * gh:(h + 1) * gh])], axis=1)
                    for h in range(2)]).astype(BF16)
    c_re_t = jnp.transpose(ssm_c_re[0], (0, 2, 1))
    c_im_t = jnp.transpose(ssm_c_im[0], (0, 2, 1))
    cc = jnp.stack([jnp.concatenate([_block_diag(c_re_t[h * gh:(h + 1) * gh]), -_block_diag(c_im_t[h * gh:(h + 1) * gh])], axis=0)
                    for h in range(2)]).astype(BF16)

    zpad = jnp.zeros((t_pad - t_real, d), F32)
    h0 = jnp.concatenate([meta_full, x[0], zpad], axis=0)
    tgt = jnp.concatenate([jnp.zeros((n_meta, d), F32), loss_target[0], zpad], axis=0)
    (h1, a1, b1), got = _ffn_fwd(h0, g_ffn1, full["ffn1_w_gate"], full["ffn1_w_up"], full["ffn1_w_down"], "ffn1_fwd",
                                 ride=_gather_ride([shard16[name] for name, _ in late]))
    full.update({name: _full_from_blocks(got[i], kind) for i, (name, kind) in enumerate(late)})
    h2, *saved = _mix_fwd(h1, g_mix, full["w_in"], b_gate, bc, cc, ssm_d, full["ssm_w_glu"], cw_rows, full["conv_w_out"],
                          full["w_o"], ltab, dims)
    (dh3, a2, b2, loss_blk, dg_final), _ = _ffn_fwd(h2, g_ffn2, full["ffn2_w_gate"], full["ffn2_w_up"], full["ffn2_w_down"], "ffn2_fwd",
                                                 head=(tgt, g_final.reshape(1, d), n_meta, t_real))

    (dh2, dg_ffn2, n2, dab2, s2, do2), _ = _ffn_bwd(
        h2, dh3, a2, b2, g_ffn2, full["ffn2_w_gate"], full["ffn2_w_up"], full["ffn2_w_down"], "ffn2_bwd")
    f_ff = a2.shape[1]
    dw_gu2 = _dw(n2, dab2, "dw_ffn2_gate_up")
    p16, hs16, y516, z16, yc16 = saved
    dp_part, dy516, ge16, dz16, cg16, dyc16, mx16, dh216, dbg, dcw = _mix_bwd_gates(
        dh2, p16, y516, z16, yc16, full["ssm_w_glu"], cw_rows, full["conv_w_out"], full["w_o"], dims)
    dus16, ddsk, dlam, dbc, dcc = _mix_bwd_scan(dy516, hs16, p16, bc, cc, ssm_d, ltab, dims)
    dh1, u16, dp16, dg_mix = _mix_bwd_in(h1, dh2, dp_part, dus16, g_mix, full["w_in"], dims)
    dblocks = {
        "w_in": _blocks_from_full(_dw(u16, dp16, "dw_in"), "col"),
        "ssm_w_glu": _blocks_from_full(_dw(ge16, dz16, "dw_glu"), "col"),
        "conv_w_out": _blocks_from_full(_dw(cg16, dyc16, "dw_conv_out"), "col"),
        "w_o": _blocks_from_full(_dw(mx16, dh216, "dw_o"), "row"),
        "ffn2_w_gate": _blocks_from_full(dw_gu2[:, :f_ff], "col"),
        "ffn2_w_up": _blocks_from_full(dw_gu2[:, f_ff:], "col"),
        "ffn2_w_down": jnp.transpose(_blocks_from_full(_dw(do2, s2, "dw_ffn2_down"), "col"), (0, 2, 1)),
    }

    def pair_sums(names, tag):
        gs = [dblocks[name] for name in names]
        from_sibling = _exchange(_pair_ride(gs), "reduce_pair_" + tag)
        return [_add_pairs(g, me_core, b, "reduce_pair_add_" + name) for g, b, name in zip(gs, from_sibling, names)]

    late_names = [name for name, _ in late]
    pairs = dict(zip(late_names, pair_sums(late_names, "late")))
    (dh0, dg_ffn1, n1, dab1, s1, do1), got = _ffn_bwd(
        h0, dh1, a1, b1, g_ffn1, full["ffn1_w_gate"], full["ffn1_w_up"], full["ffn1_w_down"], "ffn1_bwd",
        ride=_chips_ride([pairs[name] for name in late_names]))
    from_chips = dict(zip(late_names, got))
    dlam4 = dlam.reshape(SUBLANES, 2, 2, gh, n_state)
    dlam_in = jnp.transpose(dlam4, (2, 0, 1, 3, 4)).reshape(2, SUBLANES, n_grp, n_state)
    hg = gp // 2
    dbb_r = jnp.concatenate([_diag_blocks(dbc[h][:, :hg], gh) for h in range(2)], axis=0)
    dbb_i = jnp.concatenate([_diag_blocks(dbc[h][:, hg:], gh) for h in range(2)], axis=0)
    da_re, da_im, dldt, dbre_t, dbim_t = _s5_params_bwd(a_re, a_im, ldt, b_re_t, b_im_t, dlam_in, dbb_r, dbb_i)
    dc_re = jnp.concatenate([_diag_blocks(dcc[h][:hg], gh) for h in range(2)], axis=0)
    dc_im = -jnp.concatenate([_diag_blocks(dcc[h][hg:], gh) for h in range(2)], axis=0)

    grads_rep = {
        "g_ffn1": dg_ffn1, "g_mix": dg_mix, "b_gate": dbg, "ssm_a_re": da_re[None], "ssm_a_im": da_im[None],
        "ssm_log_dt": dldt.reshape(1, n_grp), "ssm_b_re": jnp.transpose(dbre_t, (0, 2, 1))[None],
        "ssm_b_im": jnp.transpose(dbim_t, (0, 2, 1))[None], "ssm_c_re": jnp.transpose(dc_re, (0, 2, 1))[None],
        "ssm_c_im": jnp.transpose(dc_im, (0, 2, 1))[None], "ssm_d": ddsk, "g_ffn2": dg_ffn2, "g_final": dg_final.reshape(d),
    }

    rep_shapes = [w[n].shape for n in REPLICATED]
    small_g_shapes = rep_shapes + [(n_meta, d), (3, dc), (1, 1)]
    gsmall = _pack([grads_rep[n] for n in REPLICATED] + [dh0[0:n_meta], dcw[0:3], loss_blk[0:1, 0:1]])
    dw_gu1, (gall,) = _dw(n1, dab1, "dw_ffn1_gate_up", ride=_gather_ride([gsmall]))
    dblocks.update({
        "ffn1_w_gate": _blocks_from_full(dw_gu1[:, :f_ff], "col"),
        "ffn1_w_up": _blocks_from_full(dw_gu1[:, f_ff:], "col"),
        "ffn1_w_down": jnp.transpose(_blocks_from_full(_dw(do1, s1, "dw_ffn1_down"), "col"), (0, 2, 1)),
    })
    early_names = [name for name, _ in early]
    pairs.update(zip(early_names, pair_sums(early_names, "early")))
    from_chips.update(zip(early_names, _exchange(_chips_ride([pairs[name] for name in early_names]), "reduce_chips_early")))

    out_g, out_d, out_m, out_v = {}, {}, {}, {}
    for name, _ in BIG:
        fc = from_chips[name]
        out_g[name], out_d[name], out_m[name], out_v[name] = _adamw(
            w[name], mom_m[name], mom_v[name], [(pairs[name], None), (fc, 0), (fc, 1), (fc, 2)], me_chip, "adamw_" + name)

    zer = [jnp.zeros((n_meta, d), F32), jnp.zeros((3, dc), F32), jnp.zeros((1, 1), F32)]
    gr, dr, mr, vr = [o[0] for o in _adamw(
        _pack([w[n] for n in REPLICATED] + zer)[None], _pack([mom_m[n] for n in REPLICATED] + zer)[None],
        _pack([mom_v[n] for n in REPLICATED] + zer)[None], [(gall, b) for b in range(NDEV)], None, "adamw_replicated")]
    g_list = _unpack(gr, small_g_shapes)
    out_g.update(zip(REPLICATED, g_list[:len(REPLICATED)]))
    out_d.update(zip(REPLICATED, _unpack(dr, rep_shapes)))
    out_m.update(zip(REPLICATED, _unpack(mr, rep_shapes)))
    out_v.update(zip(REPLICATED, _unpack(vr, rep_shapes)))

    loss = g_list[-1][0, 0]
    g_meta = lax.dynamic_slice_in_dim(g_list[-3], me * mcols, mcols, axis=1)
    g_cw = lax.dynamic_slice_in_dim(g_list[-2], me * ccols, ccols, axis=1).reshape(conv_w.shape)
    tiny = ("meta_tokens", "conv_w")
    tiny_shapes = [meta_tokens.shape, conv_w.shape]
    gt, dt_, mt, vt = [o[0] for o in _adamw(
        _pack([w[n] for n in tiny])[None], _pack([mom_m[n] for n in tiny])[None], _pack([mom_v[n] for n in tiny])[None],
        [(_pack([g_meta, g_cw])[None], 0)], None, "adamw_tiny")]
    out_g.update(zip(tiny, _unpack(gt, tiny_shapes)))
    out_d.update(zip(tiny, _unpack(dt_, tiny_shapes)))
    out_m.update(zip(tiny, _unpack(mt, tiny_shapes)))
    out_v.update(zip(tiny, _unpack(vt, tiny_shapes)))

    grad_x = dh0[n_meta:t_real][None]
    return (loss, grad_x, *[out_g[n] for n in WEIGHTS], *[out_d[n] for n in WEIGHTS],
            *[out_m[n] for n in WEIGHTS], *[out_v[n] for n in WEIGHTS])
```

```python
import functools

import jax
import jax.numpy as jnp
from jax import lax
from jax.experimental import pallas as pl
from jax.experimental.pallas import tpu as pltpu

F32 = jnp.float32
BF16 = jnp.bfloat16
MESH = pl.DeviceIdType.MESH
NDEV = 8
SLAB_COLS = 1024
RMS_EPS = 1e-6
TOKEN_TILE = 320
LIGHT_TILE = 416
MIX_TILE = 320
ROW_ALIGN = 128
SUBLANES = 8
SCAN_LANES = 512
FFN_CHUNK = 4096
VMEM_LIMIT_BYTES = 56 * 1024 * 1024

ADAM_LR = 0.001
ADAM_B1 = 0.9
ADAM_B2 = 0.999
ADAM_EPS = 1e-08
ADAM_WD = 0.01
ADAM_STEP = 10

_VM = pl.BlockSpec(memory_space=pltpu.VMEM)
_ANY = pl.BlockSpec(memory_space=pl.ANY)


def _params(sem=("arbitrary",)):
    return pltpu.CompilerParams(dimension_semantics=sem, vmem_limit_bytes=VMEM_LIMIT_BYTES)


def _dot(a, b):
    return jnp.dot(a, b, preferred_element_type=F32)


def _dot_nt(a, b):
    return lax.dot_general(a, b, (((1,), (1,)), ((), ())), preferred_element_type=F32)


def _dot_tn(a, b):
    return lax.dot_general(a, b, (((0,), (0,)), ((), ())), preferred_element_type=F32)


def _tile(rows, most):
    return next(k for k in range(most - most % 16, 0, -16) if rows % k == 0)


def _chunks(n, step):
    return [(s, min(s + step, n)) for s in range(0, n, step)]


def _gather_plan(x_refs, out_refs, send_sems, recv_sems, local_sems):
    n = len(x_refs)
    x, y, c = lax.axis_index("x"), lax.axis_index("y"), lax.axis_index("c")
    me, sibling = (x, y, c), (x, y, 1 - c)
    chips = [(1 - x, y), (x, 1 - y), (1 - x, 1 - y)]

    def copy(i, k, block, to, src=None):
        slot = out_refs[i].at[4 * block[0] + 2 * block[1] + block[2]]
        return pltpu.make_async_remote_copy(
            src_ref=slot if src is None else src, dst_ref=slot,
            send_sem=send_sems.at[7 * i + k], recv_sem=recv_sems.at[7 * i + k], device_id=to, device_id_type=MESH)

    def mine():
        return [pltpu.make_async_copy(x_refs[i], out_refs[i].at[4 * x + 2 * y + c], local_sems.at[i]) for i in range(n)]

    def first():
        out = []
        for i in range(n):
            out.append(copy(i, 0, me, sibling, src=x_refs[i]))
            out += [copy(i, 1 + j, me, (*chip, c), src=x_refs[i]) for j, chip in enumerate(chips)]
        return out

    def start():
        for cp in mine() + first():
            cp.start()

    def finish():
        passed = []
        for j, chip in enumerate(chips):
            for i in range(n):
                copy(i, 1 + j, (*chip, c), me).wait_recv()
                cp = copy(i, 4 + j, (*chip, c), sibling)
                cp.start()
                passed.append(cp)
        for i in range(n):
            copy(i, 0, sibling, me).wait_recv()
            for j, chip in enumerate(chips):
                copy(i, 4 + j, (*chip, 1 - c), me).wait_recv()
        for cp in first() + passed:
            cp.wait_send()
        for cp in mine():
            cp.wait()

    return start, finish


def _pair_plan(g_refs, out_refs, send_sems, recv_sems):
    x, y, c = lax.axis_index("x"), lax.axis_index("y"), lax.axis_index("c")

    def copies():
        return [pltpu.make_async_remote_copy(
            src_ref=g_refs[i].at[2 * j + (1 - c)], dst_ref=out_refs[i].at[j],
            send_sem=send_sems.at[4 * i + j], recv_sem=recv_sems.at[4 * i + j],
            device_id=(x, y, 1 - c), device_id_type=MESH) for i in range(len(g_refs)) for j in range(4)]

    def start():
        for cp in copies():
            cp.start()

    def finish():
        for cp in copies():
            cp.wait()

    return start, finish


def _chips_plan(p_refs, out_refs, send_sems, recv_sems):
    x, y, c = lax.axis_index("x"), lax.axis_index("y"), lax.axis_index("c")

    def copies():
        return [pltpu.make_async_remote_copy(
            src_ref=p_refs[i].at[2 * px + py], dst_ref=out_refs[i].at[k],
            send_sem=send_sems.at[3 * i + k], recv_sem=recv_sems.at[3 * i + k],
            device_id=(px, py, c), device_id_type=MESH)
            for i in range(len(p_refs)) for k, (px, py) in enumerate([(1 - x, y), (x, 1 - y), (1 - x, 1 - y)])]

    def start():
        for cp in copies():
            cp.start()

    def finish():
        for cp in copies():
            cp.wait()

    return start, finish


def _gather_ride(shards):
    n = len(shards)
    return dict(plan=_gather_plan, arrays=list(shards),
                out_shape=[jax.ShapeDtypeStruct((NDEV, *s.shape), s.dtype) for s in shards],
                sems=[pltpu.SemaphoreType.DMA((7 * n,)), pltpu.SemaphoreType.DMA((7 * n,)), pltpu.SemaphoreType.DMA((n,))])


def _pair_ride(blocks):
    n = len(blocks)
    return dict(plan=_pair_plan, arrays=list(blocks),
                out_shape=[jax.ShapeDtypeStruct((4, *b.shape[1:]), b.dtype) for b in blocks],
                sems=[pltpu.SemaphoreType.DMA((4 * n,)), pltpu.SemaphoreType.DMA((4 * n,))])


def _chips_ride(partials):
    n = len(partials)
    return dict(plan=_chips_plan, arrays=list(partials),
                out_shape=[jax.ShapeDtypeStruct((3, *p.shape[1:]), p.dtype) for p in partials],
                sems=[pltpu.SemaphoreType.DMA((3 * n,)), pltpu.SemaphoreType.DMA((3 * n,))])


def _exchange(ride, name):
    n = len(ride["arrays"])

    def body(*refs):
        start, finish = ride["plan"](refs[:n], refs[n:2 * n], *refs[2 * n:])
        start()
        finish()

    return pl.pallas_call(
        body, name=name, out_shape=ride["out_shape"], in_specs=[_ANY] * n, out_specs=[_ANY] * n, scratch_shapes=ride["sems"],
    )(*ride["arrays"])


def _grid_call(body, name, steps, in_specs, out_specs, out_shape, scratch_shapes, args, ride=None):
    if ride is None:
        outs = pl.pallas_call(body, name=name, grid=(steps,), in_specs=in_specs, out_specs=out_specs, out_shape=out_shape,
                              scratch_shapes=scratch_shapes, compiler_params=_params())(*args)
        return list(outs), []
    n_in, n_out, n_scr, n_ride, n_sems = len(in_specs), len(out_specs), len(scratch_shapes), len(ride["arrays"]), len(ride["sems"])

    def carrying(*refs):
        ins, r_in = refs[:n_in], refs[n_in:n_in + n_ride]
        o0 = n_in + n_ride
        outs, r_out = refs[o0:o0 + n_out], refs[o0 + n_out:o0 + n_out + n_ride]
        s0 = o0 + n_out + n_ride
        scratch, sems = refs[s0:s0 + n_scr], refs[s0 + n_scr:s0 + n_scr + n_sems]
        start, finish = ride["plan"](r_in, r_out, *sems)
        pl.when(pl.program_id(0) == 0)(start)
        body(*ins, *outs, *scratch)
        pl.when(pl.program_id(0) == steps - 1)(finish)

    outs = pl.pallas_call(
        carrying, name=name, grid=(steps,), in_specs=list(in_specs) + [_ANY] * n_ride, out_specs=list(out_specs) + [_ANY] * n_ride,
        out_shape=list(out_shape) + ride["out_shape"], scratch_shapes=list(scratch_shapes) + ride["sems"],
        compiler_params=_params())(*args, *ride["arrays"])
    return list(outs[:n_out]), list(outs[n_out:])


def _row_block(rows):
    return rows if rows <= 512 else next(k for k in (512, 256, 128, rows) if rows % k == 0)


def _add_pairs(gs, core, b, name):
    k, r, n = b.shape
    tr = _row_block(r)

    def body(core_ref, a_ref, b_ref, o_ref):
        o_ref[0] = (a_ref[0, 0].astype(F32) + b_ref[0].astype(F32)).astype(o_ref.dtype)

    spec = pl.BlockSpec((1, tr, n), lambda j, i, c: (j, i, 0))
    return pl.pallas_call(
        body, name=name,
        grid_spec=pltpu.PrefetchScalarGridSpec(
            num_scalar_prefetch=1, grid=(k, r // tr),
            in_specs=[pl.BlockSpec((1, 1, tr, n), lambda j, i, c: (j, c[0], i, 0)), spec], out_specs=spec),
        out_shape=jax.ShapeDtypeStruct(b.shape, b.dtype), compiler_params=_params(("arbitrary", "arbitrary")),
    )(core.reshape(1), gs.reshape(k, 2, r, n), b)


def _adamw(w, m, v, parts, sel, name):
    _, r, n = w.shape
    tr = _row_block(r)
    nparts = len(parts)
    bc1 = 1.0 - ADAM_B1 ** ADAM_STEP
    bc2 = 1.0 - ADAM_B2 ** ADAM_STEP

    def body(sel_ref, *refs):
        w_ref, m_ref, v_ref = refs[:3]
        p_refs = refs[3:3 + nparts]
        g_ref, d_ref, nm_ref, nv_ref = refs[3 + nparts:]
        g = p_refs[0][...].astype(F32)
        for p in p_refs[1:]:
            g = g + p[...].astype(F32)
        nm = ADAM_B1 * m_ref[...] + (1.0 - ADAM_B1) * g
        nv = ADAM_B2 * v_ref[...] + (1.0 - ADAM_B2) * (g * g)
        m_hat = nm / bc1
        v_hat = nv / bc2
        g_ref[...] = g
        d_ref[...] = -ADAM_LR * (m_hat / (jnp.sqrt(v_hat) + ADAM_EPS) + ADAM_WD * w_ref[...])
        nm_ref[...] = nm
        nv_ref[...] = nv

    def part_spec(idx):
        if idx is None:
            return pl.BlockSpec((1, tr, n), lambda i, s: (s[0], i, 0))
        return pl.BlockSpec((1, tr, n), lambda i, s, idx=idx: (idx, i, 0))

    spec = pl.BlockSpec((1, tr, n), lambda i, s: (0, i, 0))
    out = jax.ShapeDtypeStruct((1, r, n), F32)
    return pl.pallas_call(
        body, name=name,
        grid_spec=pltpu.PrefetchScalarGridSpec(
            num_scalar_prefetch=1, grid=(r // tr,),
            in_specs=[spec] * 3 + [part_spec(idx) for _, idx in parts], out_specs=[spec] * 4),
        out_shape=[out] * 4, compiler_params=_params(),
    )(jnp.zeros((1,), jnp.int32) if sel is None else sel.reshape(1), w, m, v, *[p for p, _ in parts])


def _rms_parts(h, g):
    r = lax.rsqrt(jnp.mean(h * h, axis=-1, keepdims=True) + RMS_EPS)
    xhat = h * r
    return r, xhat, xhat * g


def _rms_bwd(dn, g, r, xhat):
    dxh = dn * g
    return r * (dxh - xhat * jnp.mean(dxh * xhat, axis=-1, keepdims=True))


def _loss_tile(h, tgt, g, lo, hi, loss_ref, dg_ref):
    tm, d = h.shape
    i = pl.program_id(0)

    @pl.when(i == 0)
    def _():
        loss_ref[...] = jnp.zeros_like(loss_ref)
        dg_ref[...] = jnp.zeros_like(dg_ref)

    r, xhat, y = _rms_parts(h, g)
    row = i * tm + lax.broadcasted_iota(jnp.int32, (tm, 1), 0)
    err = jnp.where((row >= lo) & (row < hi), y - tgt, 0.0)
    loss_ref[...] += jnp.full(loss_ref.shape, 0.5 * jnp.sum(jnp.mean(err * err, axis=-1, keepdims=True)), F32)
    dy = err * (1.0 / d)
    dg_ref[...] += jnp.sum(dy * xhat, axis=0, keepdims=True)
    return _rms_bwd(dy, g, r, xhat)


def _ffn_fwd(h, g, wg, wu, wd, name, ride=None, head=None):
    t, d = h.shape
    f = wg.shape[1]
    tm = _tile(t, TOKEN_TILE)
    chunks = _chunks(f, FFN_CHUNK)

    def body(h_ref, g_ref, wg_ref, wu_ref, wd_ref, *rest):
        t_ref, gh_ref = rest[:2] if head else (None, None)
        o_ref, a_ref, b_ref = rest[2:5] if head else rest
        hv = h_ref[...]
        n = _rms_parts(hv, g_ref[...])[2].astype(BF16)
        acc = jnp.zeros((tm, d), F32)
        for s, e in chunks:
            a = _dot(n, wg_ref[:, s:e])
            b = _dot(n, wu_ref[:, s:e])
            a_ref[:, s:e] = a.astype(BF16)
            b_ref[:, s:e] = b.astype(BF16)
            acc = acc + _dot((a * jax.nn.sigmoid(a) * b).astype(BF16), wd_ref[s:e, :])
        out = hv + 0.5 * acc
        o_ref[...] = _loss_tile(out, t_ref[...], gh_ref[...], head[2], head[3], rest[5], rest[6]) if head else out

    tile = pl.BlockSpec((tm, d), lambda i: (i, 0))
    wide = pl.BlockSpec((tm, f), lambda i: (i, 0))
    in_specs, args = [tile, _VM, _VM, _VM, _VM], (h, g, wg, wu, wd)
    out_specs = [tile, wide, wide]
    out_shape = [jax.ShapeDtypeStruct((t, d), F32), jax.ShapeDtypeStruct((t, f), BF16), jax.ShapeDtypeStruct((t, f), BF16)]
    if head:
        in_specs, args = in_specs + [tile, _VM], args + (head[0], head[1])
        out_specs = out_specs + [pl.BlockSpec((SUBLANES, 128), lambda i: (0, 0)), pl.BlockSpec((1, d), lambda i: (0, 0))]
        out_shape = out_shape + [jax.ShapeDtypeStruct((SUBLANES, 128), F32), jax.ShapeDtypeStruct((1, d), F32)]
    return _grid_call(body, name, t // tm, in_specs, out_specs, out_shape, [], args, ride)


def _ffn_bwd(h, dh_out, a16, b16, g, wg, wu, wd, name, ride=None):
    t, d = h.shape
    f = wg.shape[1]
    tm = _tile(t, TOKEN_TILE)
    chunks = _chunks(f, FFN_CHUNK)

    def body(h_ref, dho_ref, a_ref, b_ref, g_ref, wg_ref, wu_ref, wd_ref, dh_ref, dg_ref, n_ref, dab_ref, s_ref, do_ref):
        @pl.when(pl.program_id(0) == 0)
        def _():
            dg_ref[...] = jnp.zeros_like(dg_ref)

        hv = h_ref[...]
        gv = g_ref[...]
        r, xhat, n32 = _rms_parts(hv, gv)
        dho = dho_ref[...]
        do = (0.5 * dho).astype(BF16)
        dn = jnp.zeros((tm, d), F32)
        for s, e in chunks:
            a = a_ref[:, s:e].astype(F32)
            b = b_ref[:, s:e].astype(F32)
            sig = jax.nn.sigmoid(a)
            sa = a * sig
            ds = _dot_nt(do, wd_ref[s:e, :])
            da = (ds * b * (sig * (1.0 + a * (1.0 - sig)))).astype(BF16)
            db = (ds * sa).astype(BF16)
            s_ref[:, s:e] = (sa * b).astype(BF16)
            dab_ref[:, s:e] = da
            dab_ref[:, f + s:f + e] = db
            dn = dn + _dot_nt(da, wg_ref[:, s:e]) + _dot_nt(db, wu_ref[:, s:e])
        dh_ref[...] = dho + _rms_bwd(dn, gv, r, xhat)
        dg_ref[...] += jnp.sum(dn * xhat, axis=0, keepdims=True)
        n_ref[...] = n32.astype(BF16)
        do_ref[...] = do

    tile = pl.BlockSpec((tm, d), lambda i: (i, 0))
    wide = pl.BlockSpec((tm, f), lambda i: (i, 0))
    one = pl.BlockSpec((1, d), lambda i: (0, 0))
    return _grid_call(
        body, name, t // tm, [tile, tile, wide, wide, _VM, _VM, _VM, _VM],
        [tile, one, tile, pl.BlockSpec((tm, 2 * f), lambda i: (i, 0)), wide, tile],
        [jax.ShapeDtypeStruct((t, d), F32), jax.ShapeDtypeStruct((1, d), F32),
         jax.ShapeDtypeStruct((t, d), BF16), jax.ShapeDtypeStruct((t, 2 * f), BF16),
         jax.ShapeDtypeStruct((t, f), BF16), jax.ShapeDtypeStruct((t, d), BF16)],
        [], (h, dh_out, a16, b16, g, wg, wu, wd), ride)


def _dw(a, b, name, ride=None):
    t, m = a.shape
    n = b.shape[1]
    bn = next(k for k in (512, 256, n) if n % k == 0)

    def body(a_ref, b_ref, o_ref):
        o_ref[...] = _dot_tn(a_ref[...], b_ref[...]).astype(BF16)

    (out,), got = _grid_call(
        body, name, n // bn, [_VM, pl.BlockSpec((t, bn), lambda j: (0, j))], [pl.BlockSpec((m, bn), lambda j: (0, j))],
        [jax.ShapeDtypeStruct((m, n), BF16)], [], (a, b), ride)
    return (out, got) if ride else out


def _to_bf16(arrays, name):
    k = len(arrays)

    def body(*refs):
        for x_ref, o_ref in zip(refs[:k], refs[k:]):
            o_ref[...] = x_ref[...].astype(BF16)

    return pl.pallas_call(
        body, name=name, out_shape=[jax.ShapeDtypeStruct(a.shape, BF16) for a in arrays],
        compiler_params=pltpu.CompilerParams(vmem_limit_bytes=VMEM_LIMIT_BYTES),
    )(*arrays)


def _s5_discretise(a_re, a_im, log_dt, b_re, b_im):
    dt = jnp.exp(log_dt)
    mag = jnp.exp(a_re * dt)
    lam_re = mag * jnp.cos(a_im * dt)
    lam_im = mag * jnp.sin(a_im * dt)
    den = a_re * a_re + a_im * a_im
    q_re = ((lam_re - 1.0) * a_re + lam_im * a_im) / den
    q_im = (lam_im * a_re - (lam_re - 1.0) * a_im) / den
    bb_re = q_re[:, None, :] * b_re - q_im[:, None, :] * b_im
    bb_im = q_re[:, None, :] * b_im + q_im[:, None, :] * b_re
    return lam_re, lam_im, bb_re, bb_im


def _s5_params_fwd(a_re, a_im, log_dt, b_re, b_im):
    g, p = a_re.shape
    c = b_re.shape[1]

    def body(are_ref, aim_ref, ldt_ref, bre_ref, bim_ref, pwr_ref, pwi_ref, bbr_ref, bbi_ref):
        lr, li, bbr, bbi = _s5_discretise(are_ref[...], aim_ref[...], ldt_ref[...], bre_ref[...], bim_ref[...])
        bbr_ref[...] = bbr
        bbi_ref[...] = bbi
        pr, pi = lr, li
        pwr_ref[0] = pr
        pwi_ref[0] = pi
        for k in range(1, SUBLANES):
            pr, pi = pr * lr - pi * li, pr * li + pi * lr
            pwr_ref[k] = pr
            pwi_ref[k] = pi

    return pl.pallas_call(
        body, name="s5_params_fwd",
        out_shape=[jax.ShapeDtypeStruct((SUBLANES, g, p), F32), jax.ShapeDtypeStruct((SUBLANES, g, p), F32),
                   jax.ShapeDtypeStruct((g, c, p), F32), jax.ShapeDtypeStruct((g, c, p), F32)],
    )(a_re, a_im, log_dt, b_re, b_im)


def _s5_params_bwd(a_re, a_im, log_dt, b_re, b_im, dlam, dbb_re, dbb_im):
    g, p = a_re.shape
    c = b_re.shape[1]

    def body(are_ref, aim_ref, ldt_ref, bre_ref, bim_ref, dlam_ref, dbr_ref, dbi_ref,
             dare_ref, daim_ref, dldt_ref, dbre_ref, dbim_ref):
        dlr = jnp.sum(dlam_ref[0], axis=0)
        dli = jnp.sum(dlam_ref[1], axis=0)
        _, vjp = jax.vjp(_s5_discretise, are_ref[...], aim_ref[...], ldt_ref[...], bre_ref[...], bim_ref[...])
        dare, daim, dldt, dbre, dbim = vjp((dlr, dli, dbr_ref[...], dbi_ref[...]))
        dare_ref[...] = dare
        daim_ref[...] = daim
        dldt_ref[...] = dldt
        dbre_ref[...] = dbre
        dbim_ref[...] = dbim

    return pl.pallas_call(
        body, name="s5_params_bwd",
        out_shape=[jax.ShapeDtypeStruct((g, p), F32), jax.ShapeDtypeStruct((g, p), F32),
                   jax.ShapeDtypeStruct((g, 1), F32), jax.ShapeDtypeStruct((g, c, p), F32),
                   jax.ShapeDtypeStruct((g, c, p), F32)],
    )(a_re, a_im, log_dt, b_re, b_im, dlam, dbb_re, dbb_im)


def _scan_chunks(gp):
    hg = gp // 2
    w = min(SCAN_LANES, hg)
    return w, [(half * hg + k * w, half * gp + k * w, half * gp + hg + k * w) for half in range(2) for k in range(hg // w)]


def _cmul_acc(xr, xi, tr, ti, sr, si):
    return xr + tr * sr - ti * si, xi + tr * si + ti * sr


def _scan_fwd(buf_ref, row0, tm, ltab_ref, cin_ref, cout_ref, gp):
    w, chunks = _scan_chunks(gp)
    for lo_t, lo_r, lo_i in chunks:
        def body(r, carry, lo_t=lo_t, lo_r=lo_r, lo_i=lo_i):
            cr, ci = carry
            row = pl.multiple_of(row0 + r * SUBLANES, SUBLANES)
            xr = buf_ref[pl.ds(row, SUBLANES), lo_r:lo_r + w]
            xi = buf_ref[pl.ds(row, SUBLANES), lo_i:lo_i + w]
            for tab, shift in ((0, 1), (2, 2), (4, 4)):
                xr, xi = _cmul_acc(xr, xi, ltab_ref[tab, :, lo_t:lo_t + w], ltab_ref[tab + 1, :, lo_t:lo_t + w],
                                   pltpu.roll(xr, shift, 0), pltpu.roll(xi, shift, 0))
            xr, xi = _cmul_acc(xr, xi, ltab_ref[6, :, lo_t:lo_t + w], ltab_ref[7, :, lo_t:lo_t + w], cr, ci)
            buf_ref[pl.ds(row, SUBLANES), lo_r:lo_r + w] = xr
            buf_ref[pl.ds(row, SUBLANES), lo_i:lo_i + w] = xi
            last = SUBLANES - 1
            return (jnp.broadcast_to(xr[last:last + 1], (SUBLANES, w)), jnp.broadcast_to(xi[last:last + 1], (SUBLANES, w)))

        cr, ci = lax.fori_loop(0, tm // SUBLANES, body,
                               (cin_ref[0:SUBLANES, lo_r:lo_r + w], cin_ref[0:SUBLANES, lo_i:lo_i + w]), unroll=True)
        if cout_ref is not None:
            cout_ref[0:SUBLANES, lo_r:lo_r + w] = cr
            cout_ref[0:SUBLANES, lo_i:lo_i + w] = ci


def _scan_rev(g_ref, hext_ref, tm, ltab_ref, gc_ref, dlam_ref, gp):
    w, chunks = _scan_chunks(gp)
    nb = tm // SUBLANES
    for lo_t, lo_r, lo_i in chunks:
        def body(k, carry, lo_t=lo_t, lo_r=lo_r, lo_i=lo_i):
            cr, ci, ar, ai = carry
            row = pl.multiple_of((nb - 1 - k) * SUBLANES, SUBLANES)
            xr = g_ref[pl.ds(row, SUBLANES), lo_r:lo_r + w]
            xi = g_ref[pl.ds(row, SUBLANES), lo_i:lo_i + w]
            for tab, shift in ((8, 7), (10, 6), (12, 4)):
                xr, xi = _cmul_acc(xr, xi, ltab_ref[tab, :, lo_t:lo_t + w], ltab_ref[tab + 1, :, lo_t:lo_t + w],
                                   pltpu.roll(xr, shift, 0), pltpu.roll(xi, shift, 0))
            xr, xi = _cmul_acc(xr, xi, ltab_ref[14, :, lo_t:lo_t + w], ltab_ref[15, :, lo_t:lo_t + w], cr, ci)
            g_ref[pl.ds(row, SUBLANES), lo_r:lo_r + w] = xr
            g_ref[pl.ds(row, SUBLANES), lo_i:lo_i + w] = xi
            first = lax.broadcasted_iota(jnp.int32, (SUBLANES, w), 0) == 0
            prev = pl.ds(row, SUBLANES)
            here = pl.ds(row + SUBLANES, SUBLANES)
            hpr = jnp.where(first, pltpu.roll(hext_ref[prev, lo_r:lo_r + w], 1, 0), pltpu.roll(hext_ref[here, lo_r:lo_r + w], 1, 0))
            hpi = jnp.where(first, pltpu.roll(hext_ref[prev, lo_i:lo_i + w], 1, 0), pltpu.roll(hext_ref[here, lo_i:lo_i + w], 1, 0))
            ar = ar + xr * hpr + xi * hpi
            ai = ai - xr * hpi + xi * hpr
            return (jnp.broadcast_to(xr[0:1], (SUBLANES, w)), jnp.broadcast_to(xi[0:1], (SUBLANES, w)), ar, ai)

        cr, ci, ar, ai = lax.fori_loop(
            0, nb, body, (gc_ref[:, lo_r:lo_r + w], gc_ref[:, lo_i:lo_i + w], dlam_ref[:, lo_r:lo_r + w], dlam_ref[:, lo_i:lo_i + w]),
            unroll=True)
        gc_ref[:, lo_r:lo_r + w] = cr
        gc_ref[:, lo_i:lo_i + w] = ci
        dlam_ref[:, lo_r:lo_r + w] = ar
        dlam_ref[:, lo_i:lo_i + w] = ai


def _conv_taps(cw, cext_ref, cin, tm):
    return (cw[0:1] * cext_ref[SUBLANES - 2:SUBLANES - 2 + tm, :] + cw[1:2] * cext_ref[SUBLANES - 1:SUBLANES - 1 + tm, :]
            + cw[2:3] * cin)


def _mix_fwd(h, gm, win, bg, bc, cc, dsk, wglu, cw, wco, wo, ltab, dims):
    d, ds, dc, gp = dims
    t = h.shape[0]
    tm = _tile(t, MIX_TILE)
    nt = t // tm
    dsh = ds // 2
    o1, o2, o3 = ds + dc, ds + 2 * dc, ds + 3 * dc
    ncols = o3 + 2 * d

    def body(h_ref, gm_ref, win_ref, bg_ref, bc_ref, cc_ref, dsk_ref, wglu_ref, cw_ref, wco_ref, wo_ref, ltab_ref,
             h2_ref, p_ref, hs_ref, y5_ref, z_ref, yc_ref, hbuf_ref, carry_ref, cext_ref):
        @pl.when(pl.program_id(0) == 0)
        def _():
            carry_ref[...] = jnp.zeros_like(carry_ref)
            cext_ref[0:SUBLANES, :] = jnp.zeros((SUBLANES, dc), F32)

        hv = h_ref[...]
        bg = bg_ref[...]
        u = _rms_parts(hv, gm_ref[...])[2].astype(BF16)
        us = _dot(u, win_ref[:, 0:ds])
        v = _dot(u, win_ref[:, ds:o1])
        gb = _dot(u, win_ref[:, o1:o2])
        gcv = _dot(u, win_ref[:, o2:o3])
        gs = jax.nn.sigmoid(_dot(u, win_ref[:, o3:o3 + d]) + bg[:, 0:d])
        gcg = jax.nn.sigmoid(_dot(u, win_ref[:, o3 + d:o3 + 2 * d]) + bg[:, d:2 * d])
        us16 = us.astype(BF16)
        p_ref[:, 0:ds] = us16
        p_ref[:, ds:o1] = v.astype(BF16)
        p_ref[:, o1:o2] = gb.astype(BF16)
        p_ref[:, o2:o3] = gcv.astype(BF16)
        p_ref[:, o3:o3 + d] = gs.astype(BF16)
        p_ref[:, o3 + d:ncols] = gcg.astype(BF16)
        for half in range(2):
            hbuf_ref[:, half * gp:(half + 1) * gp] = _dot(us16[:, half * dsh:(half + 1) * dsh], bc_ref[half])
        _scan_fwd(hbuf_ref, 0, tm, ltab_ref, carry_ref, carry_ref, gp)
        hs_ref[...] = hbuf_ref[...].astype(BF16)
        y5 = jnp.concatenate([_dot(hs_ref[:, half * gp:(half + 1) * gp], cc_ref[half]) for half in range(2)], axis=1) + dsk_ref[...] * us
        y5_ref[...] = y5.astype(BF16)
        z = _dot(jax.nn.gelu(y5).astype(BF16), wglu_ref[...])
        z_ref[...] = z.astype(BF16)
        ys = z[:, 0:d] * jax.nn.sigmoid(z[:, d:2 * d])
        cin = gcv * v
        cext_ref[SUBLANES:SUBLANES + tm, :] = cin
        yc = _dot((gb * _conv_taps(cw_ref[...], cext_ref, cin, tm)).astype(BF16), wco_ref[...])
        yc_ref[...] = yc.astype(BF16)
        h2_ref[...] = hv + _dot((gs * ys + gcg * yc).astype(BF16), wo_ref[...])
        cext_ref[0:SUBLANES, :] = cext_ref[tm:tm + SUBLANES, :]

    def tile(cols):
        return pl.BlockSpec((tm, cols), lambda i: (i, 0))

    def bf(cols):
        return jax.ShapeDtypeStruct((t, cols), BF16)

    return pl.pallas_call(
        body, name="mix_fwd", grid=(nt,),
        in_specs=[tile(d)] + [_VM] * 11,
        out_specs=[tile(d), tile(ncols), tile(2 * gp), tile(ds), tile(2 * d), tile(d)],
        out_shape=[jax.ShapeDtypeStruct((t, d), F32), bf(ncols), bf(2 * gp), bf(ds), bf(2 * d), bf(d)],
        scratch_shapes=[pltpu.VMEM((tm, 2 * gp), F32), pltpu.VMEM((SUBLANES, 2 * gp), F32), pltpu.VMEM((SUBLANES + tm, dc), F32)],
        compiler_params=_params(),
    )(h, gm, win, bg, bc, cc, dsk, wglu, cw, wco, wo, ltab)


HALO = 16


def _mix_bwd_gates(dh2, p16, y516, z16, yc16, wglu, cw, wco, wo, dims):
    d, ds, dc, gp = dims
    t = dh2.shape[0]
    tm = _tile(t, LIGHT_TILE)
    nt = t // tm
    o1, o2, o3 = ds + dc, ds + 2 * dc, ds + 3 * dc
    ncols = o3 + 2 * d

    def body(dh2_ref, p_ref, halo_ref, y5_ref, z_ref, yc_ref, wglu_ref, cw_ref, wco_ref, wo_ref,
             dp_ref, dy5_ref, ge_ref, dz_ref, cg_ref, dyc_ref, mx_ref, dh216_ref, dbg_ref, dcw_ref, cext_ref, dcvext_ref):
        j = pl.program_id(0)

        @pl.when(j == 0)
        def _():
            dbg_ref[...] = jnp.zeros_like(dbg_ref)
            dcw_ref[...] = jnp.zeros_like(dcw_ref)
            dcvext_ref[tm:tm + SUBLANES, :] = jnp.zeros((SUBLANES, dc), F32)

        before = halo_ref[:, o2:o3].astype(F32) * halo_ref[:, ds:o1].astype(F32)
        cext_ref[0:SUBLANES, :] = jnp.where(j == nt - 1, 0.0, before[HALO - SUBLANES:HALO])
        cw_v = cw_ref[...]
        v = p_ref[:, ds:o1].astype(F32)
        gb = p_ref[:, o1:o2].astype(F32)
        gcv = p_ref[:, o2:o3].astype(F32)
        gs = p_ref[:, o3:o3 + d].astype(F32)
        gcg = p_ref[:, o3 + d:ncols].astype(F32)
        z1 = z_ref[:, 0:d].astype(F32)
        sz = jax.nn.sigmoid(z_ref[:, d:2 * d].astype(F32))
        ys = z1 * sz
        yc = yc_ref[...].astype(F32)
        ge, gelu_vjp = jax.vjp(jax.nn.gelu, y5_ref[...].astype(F32))
        cin = gcv * v
        cext_ref[SUBLANES:SUBLANES + tm, :] = cin
        cv = _conv_taps(cw_v, cext_ref, cin, tm)

        dh216 = dh2_ref[...].astype(BF16)
        dmixed = _dot_nt(dh216, wo_ref[...])
        dys = dmixed * gs
        dyc16 = (dmixed * gcg).astype(BF16)
        dpgs = dmixed * ys * gs * (1.0 - gs)
        dpgc = dmixed * yc * gcg * (1.0 - gcg)
        dz16 = jnp.concatenate([dys * sz, dys * z1 * sz * (1.0 - sz)], axis=1).astype(BF16)
        dy5_ref[...] = gelu_vjp(_dot_nt(dz16, wglu_ref[...]))[0].astype(BF16)
        dcg = _dot_nt(dyc16, wco_ref[...])
        dcv = dcg * gb
        dcvext_ref[0:tm, :] = dcv
        dcin = cw_v[2:3] * dcv + cw_v[1:2] * dcvext_ref[1:1 + tm, :] + cw_v[0:1] * dcvext_ref[2:2 + tm, :]
        dcw_ref[0:1, :] += jnp.sum(dcv * cext_ref[SUBLANES - 2:SUBLANES - 2 + tm, :], axis=0, keepdims=True)
        dcw_ref[1:2, :] += jnp.sum(dcv * cext_ref[SUBLANES - 1:SUBLANES - 1 + tm, :], axis=0, keepdims=True)
        dcw_ref[2:3, :] += jnp.sum(dcv * cin, axis=0, keepdims=True)
        dcvext_ref[tm:tm + SUBLANES, :] = dcvext_ref[0:SUBLANES, :]
        dbg_ref[...] += jnp.concatenate([jnp.sum(dpgs, axis=0, keepdims=True), jnp.sum(dpgc, axis=0, keepdims=True)], axis=1)
        dp_ref[:, 0:ds] = jnp.zeros((tm, ds), BF16)
        dp_ref[:, ds:o1] = (dcin * gcv).astype(BF16)
        dp_ref[:, o1:o2] = (dcg * cv).astype(BF16)
        dp_ref[:, o2:o3] = (dcin * v).astype(BF16)
        dp_ref[:, o3:o3 + d] = dpgs.astype(BF16)
        dp_ref[:, o3 + d:ncols] = dpgc.astype(BF16)
        ge_ref[...] = ge.astype(BF16)
        dz_ref[...] = dz16
        cg_ref[...] = (gb * cv).astype(BF16)
        dyc_ref[...] = dyc16
        mx_ref[...] = (gs * ys + gcg * yc).astype(BF16)
        dh216_ref[...] = dh216

    def rev(cols):
        return pl.BlockSpec((tm, cols), lambda j: (nt - 1 - j, 0))

    def bf(cols):
        return jax.ShapeDtypeStruct((t, cols), BF16)

    halo = pl.BlockSpec((HALO, ncols), lambda j: (jnp.maximum((nt - 1 - j) * (tm // HALO) - 1, 0), 0))
    return pl.pallas_call(
        body, name="mix_bwd_gates", grid=(nt,),
        in_specs=[rev(d), rev(ncols), halo, rev(ds), rev(2 * d), rev(d), _VM, _VM, _VM, _VM],
        out_specs=[rev(ncols), rev(ds), rev(ds), rev(2 * d), rev(dc), rev(d), rev(d), rev(d), _VM, _VM],
        out_shape=[bf(ncols), bf(ds), bf(ds), bf(2 * d), bf(dc), bf(d), bf(d), bf(d),
                   jax.ShapeDtypeStruct((1, 2 * d), F32), jax.ShapeDtypeStruct((SUBLANES, dc), F32)],
        scratch_shapes=[pltpu.VMEM((SUBLANES + tm, dc), F32), pltpu.VMEM((tm + SUBLANES, dc), F32)],
        compiler_params=_params(),
    )(dh2, p16, p16, y516, z16, yc16, wglu, cw, wco, wo)


def _mix_bwd_scan(dy516, hs16, p16, bc, cc, dsk, ltab, dims):
    d, ds, dc, gp = dims
    t = dy516.shape[0]
    tm = _tile(t, TOKEN_TILE)
    nt = t // tm
    dsh = ds // 2

    def body(dy5_ref, hs_ref, halo_ref, us_ref, bc_ref, cc_ref, dsk_ref, ltab_ref,
             dus_ref, ddsk_ref, dlam_ref, dbc_ref, dcc_ref, hext_ref, gbuf_ref, gcarry_ref):
        j = pl.program_id(0)

        @pl.when(j == 0)
        def _():
            for ref in (ddsk_ref, dlam_ref, dbc_ref, dcc_ref, gcarry_ref):
                ref[...] = jnp.zeros_like(ref)

        before = jnp.where(j == nt - 1, 0.0, halo_ref[...].astype(F32)[HALO - 1:HALO])
        hext_ref[0:SUBLANES, :] = jnp.broadcast_to(before, (SUBLANES, 2 * gp))
        hext_ref[SUBLANES:SUBLANES + tm, :] = hs_ref[...].astype(F32)
        dy516v = dy5_ref[...]
        for half in range(2):
            gbuf_ref[:, half * gp:(half + 1) * gp] = _dot_nt(dy516v[:, half * dsh:(half + 1) * dsh], cc_ref[half])
        _scan_rev(gbuf_ref, hext_ref, tm, ltab_ref, gcarry_ref, dlam_ref, gp)
        dus = []
        for half in range(2):
            g16 = gbuf_ref[:, half * gp:(half + 1) * gp].astype(BF16)
            dus.append(_dot_nt(g16, bc_ref[half]))
            dbc_ref[half] += _dot_tn(us_ref[:, half * dsh:(half + 1) * dsh], g16)
            dcc_ref[half] += _dot_tn(hs_ref[:, half * gp:(half + 1) * gp], dy516v[:, half * dsh:(half + 1) * dsh])
        dy5 = dy516v.astype(F32)
        dus_ref[...] = (jnp.concatenate(dus, axis=1) + dsk_ref[...] * dy5).astype(BF16)
        ddsk_ref[...] += jnp.sum(dy5 * us_ref[...].astype(F32), axis=0, keepdims=True)

    def rev(cols):
        return pl.BlockSpec((tm, cols), lambda j: (nt - 1 - j, 0))

    halo = pl.BlockSpec((HALO, 2 * gp), lambda j: (jnp.maximum((nt - 1 - j) * (tm // HALO) - 1, 0), 0))
    return pl.pallas_call(
        body, name="mix_bwd_scan", grid=(nt,),
        in_specs=[rev(ds), rev(2 * gp), halo, rev(ds), _VM, _VM, _VM, _VM],
        out_specs=[rev(ds), _VM, _VM, _VM, _VM],
        out_shape=[jax.ShapeDtypeStruct((t, ds), BF16), jax.ShapeDtypeStruct((1, ds), F32),
                   jax.ShapeDtypeStruct((SUBLANES, 2 * gp), F32),
                   jax.ShapeDtypeStruct((2, dsh, gp), F32), jax.ShapeDtypeStruct((2, gp, dsh), F32)],
        scratch_shapes=[pltpu.VMEM((SUBLANES + tm, 2 * gp), F32), pltpu.VMEM((tm, 2 * gp), F32), pltpu.VMEM((SUBLANES, 2 * gp), F32)],
        compiler_params=_params(),
    )(dy516, hs16, hs16, p16, bc, cc, dsk, ltab)


def _mix_bwd_in(h, dh2, dp16, dus16, gm, win, dims):
    d, ds, dc, gp = dims
    t = h.shape[0]
    tm = _tile(t, LIGHT_TILE)
    ncols = dp16.shape[1]

    def body(h_ref, dh2_ref, dp_ref, dus_ref, gm_ref, win_ref, dh1_ref, u_ref, dpf_ref, dgm_ref):
        @pl.when(pl.program_id(0) == 0)
        def _():
            dgm_ref[...] = jnp.zeros_like(dgm_ref)

        gmv = gm_ref[...]
        r, xhat, n32 = _rms_parts(h_ref[...], gmv)
        du = _dot_nt(dus_ref[...], win_ref[:, 0:ds]) + _dot_nt(dp_ref[:, ds:ncols], win_ref[:, ds:ncols])
        dh1_ref[...] = dh2_ref[...] + _rms_bwd(du, gmv, r, xhat)
        dgm_ref[...] += jnp.sum(du * xhat, axis=0, keepdims=True)
        u_ref[...] = n32.astype(BF16)
        dpf_ref[:, 0:ds] = dus_ref[...]
        dpf_ref[:, ds:ncols] = dp_ref[:, ds:ncols]

    def tile(cols):
        return pl.BlockSpec((tm, cols), lambda i: (i, 0))

    return pl.pallas_call(
        body, name="mix_bwd_in", grid=(t // tm,),
        in_specs=[tile(d), tile(d), tile(ncols), tile(ds), _VM, _VM],
        out_specs=[tile(d), tile(d), tile(ncols), pl.BlockSpec((1, d), lambda i: (0, 0))],
        out_shape=[jax.ShapeDtypeStruct((t, d), F32), jax.ShapeDtypeStruct((t, d), BF16),
                   jax.ShapeDtypeStruct((t, ncols), BF16), jax.ShapeDtypeStruct((1, d), F32)],
        compiler_params=_params(),
    )(h, dh2, dp16, dus16, gm, win)


def _pad_rows(a, rows, axis=0):
    pad = [(0, 0)] * a.ndim
    pad[axis] = (0, rows - a.shape[axis])
    return jnp.pad(a, pad)


def _as_rows(a):
    flat = a.reshape(-1)
    n = -(-flat.shape[0] // SLAB_COLS) * SLAB_COLS
    return jnp.pad(flat, (0, n - flat.shape[0])).reshape(-1, SLAB_COLS)


def _pack(arrs):
    rows = jnp.concatenate([_as_rows(a) for a in arrs], axis=0)
    return _pad_rows(rows, -(-rows.shape[0] // 16) * 16)


def _unpack(slab, shapes):
    out, r = [], 0
    for shp in shapes:
        size = 1
        for s in shp:
            size *= s
        n = -(-size // SLAB_COLS)
        out.append(slab[r:r + n].reshape(-1)[:size].reshape(shp))
        r += n
    return out


def _block_diag(blocks):
    n, a, b = blocks.shape
    eye = jnp.eye(n, dtype=blocks.dtype)
    return (blocks[:, :, None, :] * eye[:, None, :, None]).reshape(n * a, n * b)


def _diag_blocks(mat, n):
    a, b = mat.shape[0] // n, mat.shape[1] // n
    eye = jnp.eye(n, dtype=mat.dtype)
    return jnp.sum(mat.reshape(n, a, n, b) * eye[:, None, :, None], axis=2)


BIG = (("ffn1_w_gate", "col"), ("ffn1_w_up", "col"), ("ffn1_w_down", "row"), ("w_in", "col"), ("ssm_w_glu", "col"),
       ("conv_w_out", "col"), ("w_o", "row"), ("ffn2_w_gate", "col"), ("ffn2_w_up", "col"), ("ffn2_w_down", "row"))
REPLICATED = ("g_ffn1", "g_mix", "b_gate", "ssm_a_re", "ssm_a_im", "ssm_log_dt", "ssm_b_re", "ssm_b_im", "ssm_c_re",
              "ssm_c_im", "ssm_d", "g_ffn2", "g_final")
WEIGHTS = ("meta_tokens", "g_ffn1", "ffn1_w_gate", "ffn1_w_up", "ffn1_w_down", "g_mix", "w_in", "b_gate", "ssm_a_re",
           "ssm_a_im", "ssm_log_dt", "ssm_b_re", "ssm_b_im", "ssm_c_re", "ssm_c_im", "ssm_d", "ssm_w_glu", "conv_w",
           "conv_w_out", "w_o", "g_ffn2", "ffn2_w_gate", "ffn2_w_up", "ffn2_w_down", "g_final")
N_EARLY = 3


def _full_from_blocks(blocks, kind):
    n, r, c = blocks.shape
    if kind == "col":
        return jnp.transpose(blocks, (1, 0, 2)).reshape(r, n * c)
    return blocks.reshape(n * r, c)


def _blocks_from_full(full, kind):
    if kind == "col":
        r, nc = full.shape
        return jnp.transpose(full.reshape(r, NDEV, nc // NDEV), (1, 0, 2))
    nr, c = full.shape
    return full.reshape(NDEV, nr // NDEV, c)


def kernel(x, meta_tokens, g_ffn1, ffn1_w_gate, ffn1_w_up, ffn1_w_down, g_mix, w_in, b_gate, ssm_a_re, ssm_a_im, ssm_log_dt, ssm_b_re, ssm_b_im, ssm_c_re, ssm_c_im, ssm_d, ssm_w_glu, conv_w, conv_w_out, w_o, g_ffn2, ffn2_w_gate, ffn2_w_up, ffn2_w_down, g_final, loss_target, m_meta_tokens, m_g_ffn1, m_ffn1_w_gate, m_ffn1_w_up, m_ffn1_w_down, m_g_mix, m_w_in, m_b_gate, m_ssm_a_re, m_ssm_a_im, m_ssm_log_dt, m_ssm_b_re, m_ssm_b_im, m_ssm_c_re, m_ssm_c_im, m_ssm_d, m_ssm_w_glu, m_conv_w, m_conv_w_out, m_w_o, m_g_ffn2, m_ffn2_w_gate, m_ffn2_w_up, m_ffn2_w_down, m_g_final, v_meta_tokens, v_g_ffn1, v_ffn1_w_gate, v_ffn1_w_up, v_ffn1_w_down, v_g_mix, v_w_in, v_b_gate, v_ssm_a_re, v_ssm_a_im, v_ssm_log_dt, v_ssm_b_re, v_ssm_b_im, v_ssm_c_re, v_ssm_c_im, v_ssm_d, v_ssm_w_glu, v_conv_w, v_conv_w_out, v_w_o, v_g_ffn2, v_ffn2_w_gate, v_ffn2_w_up, v_ffn2_w_down, v_g_final):
    args = dict(locals())
    w = {n: args[n] for n in WEIGHTS}
    mom_m = {n: args["m_" + n] for n in WEIGHTS}
    mom_v = {n: args["v_" + n] for n in WEIGHTS}

    seq, d = x.shape[1], x.shape[2]
    n_meta = meta_tokens.shape[0]
    ds = ssm_d.shape[1]
    n_grp, n_state = ssm_a_re.shape[1], ssm_a_re.shape[2]
    gp = n_grp * n_state
    dc = conv_w.shape[3] * NDEV
    dims = (d, ds, dc, gp)
    t_real = n_meta + seq
    t_pad = -(-t_real // ROW_ALIGN) * ROW_ALIGN
    me_chip = 2 * lax.axis_index("x") + lax.axis_index("y")
    me_core = lax.axis_index("c")
    me = 2 * me_chip + me_core
    mcols, ccols = d // NDEV, dc // NDEV

    cw_shard = _pad_rows(_pad_rows(conv_w.reshape(3, ccols), SUBLANES), 128, axis=1)
    shard16 = dict(zip([name for name, _ in BIG], _to_bf16([w[name][0] for name, _ in BIG], "weights_to_bf16")))
    early, late = BIG[:N_EARLY], BIG[N_EARLY:]
    got = _exchange(_gather_ride([shard16[name] for name, _ in early] + [meta_tokens, cw_shard]), "gather_first")
    full = {name: _full_from_blocks(got[i], kind) for i, (name, kind) in enumerate(early)}
    meta_full = _full_from_blocks(got[-2], "col")
    cw_rows = _pad_rows(_full_from_blocks(got[-1][:, 0:3, 0:ccols], "col"), SUBLANES)

    a_re, a_im, ldt = ssm_a_re[0], ssm_a_im[0], ssm_log_dt[0].reshape(n_grp, 1)
    b_re_t = jnp.transpose(ssm_b_re[0], (0, 2, 1))
    b_im_t = jnp.transpose(ssm_b_im[0], (0, 2, 1))
    pw_r, pw_i, bb_r, bb_i = _s5_params_fwd(a_re, a_im, ldt, b_re_t, b_im_t)
    pw_r = pw_r.reshape(SUBLANES, gp)
    pw_i = pw_i.reshape(SUBLANES, gp)
    sub = jnp.arange(SUBLANES)[:, None]

    def fwd_tab(p, k):
        return jnp.where(sub >= k, p[k - 1][None, :], 0.0)

    def rev_tab(p, k):
        return jnp.where(sub <= SUBLANES - 1 - k, p[k - 1][None, :], 0.0)

    ltab = jnp.stack(
        [fwd_tab(pw_r, 1), fwd_tab(pw_i, 1), fwd_tab(pw_r, 2), fwd_tab(pw_i, 2), fwd_tab(pw_r, 4), fwd_tab(pw_i, 4), pw_r, pw_i,
         rev_tab(pw_r, 1), -rev_tab(pw_i, 1), rev_tab(pw_r, 2), -rev_tab(pw_i, 2), rev_tab(pw_r, 4), -rev_tab(pw_i, 4),
         pw_r[::-1], -pw_i[::-1]], axis=0)
    gh = n_grp // 2
    bc = jnp.stack([jnp.concatenate([_block_diag(bb_r[h * gh:(h + 1) * gh]), _block_diag(bb_i[h * gh:(h + 1) * gh])], axis=1)
                    for h in range(2)]).astype(BF16)
    c_re_t = jnp.transpose(ssm_c_re[0], (0, 2, 1))
    c_im_t = jnp.transpose(ssm_c_im[0], (0, 2, 1))
    cc = jnp.stack([jnp.concatenate([_block_diag(c_re_t[h * gh:(h + 1) * gh]), -_block_diag(c_im_t[h * gh:(h + 1) * gh])], axis=0)
                    for h in range(2)]).astype(BF16)

    zpad = jnp.zeros((t_pad - t_real, d), F32)
    h0 = jnp.concatenate([meta_full, x[0], zpad], axis=0)
    tgt = jnp.concatenate([jnp.zeros((n_meta, d), F32), loss_target[0], zpad], axis=0)
    (h1, a1, b1), got = _ffn_fwd(h0, g_ffn1, full["ffn1_w_gate"], full["ffn1_w_up"], full["ffn1_w_down"], "ffn1_fwd",
                                 ride=_gather_ride([shard16[name] for name, _ in late]))
    full.update({name: _full_from_blocks(got[i], kind) for i, (name, kind) in enumerate(late)})
    h2, *saved = _mix_fwd(h1, g_mix, full["w_in"], b_gate, bc, cc, ssm_d, full["ssm_w_glu"], cw_rows, full["conv_w_out"],
                          full["w_o"], ltab, dims)
    (dh3, a2, b2, loss_blk, dg_final), _ = _ffn_fwd(h2, g_ffn2, full["ffn2_w_gate"], full["ffn2_w_up"], full["ffn2_w_down"], "ffn2_fwd",
                                                 head=(tgt, g_final.reshape(1, d), n_meta, t_real))

    (dh2, dg_ffn2, n2, dab2, s2, do2), _ = _ffn_bwd(
        h2, dh3, a2, b2, g_ffn2, full["ffn2_w_gate"], full["ffn2_w_up"], full["ffn2_w_down"], "ffn2_bwd")
    f_ff = a2.shape[1]
    dw_gu2 = _dw(n2, dab2, "dw_ffn2_gate_up")
    p16, hs16, y516, z16, yc16 = saved
    dp_part, dy516, ge16, dz16, cg16, dyc16, mx16, dh216, dbg, dcw = _mix_bwd_gates(
        dh2, p16, y516, z16, yc16, full["ssm_w_glu"], cw_rows, full["conv_w_out"], full["w_o"], dims)
    dus16, ddsk, dlam, dbc, dcc = _mix_bwd_scan(dy516, hs16, p16, bc, cc, ssm_d, ltab, dims)
    dh1, u16, dp16, dg_mix = _mix_bwd_in(h1, dh2, dp_part, dus16, g_mix, full["w_in"], dims)
    dblocks = {
        "w_in": _blocks_from_full(_dw(u16, dp16, "dw_in"), "col"),
        "ssm_w_glu": _blocks_from_full(_dw(ge16, dz16, "dw_glu"), "col"),
        "conv_w_out": _blocks_from_full(_dw(cg16, dyc16, "dw_conv_out"), "col"),
        "w_o": _blocks_from_full(_dw(mx16, dh216, "dw_o"), "row"),
        "ffn2_w_gate": _blocks_from_full(dw_gu2[:, :f_ff], "col"),
        "ffn2_w_up": _blocks_from_full(dw_gu2[:, f_ff:], "col"),
        "ffn2_w_down": jnp.transpose(_blocks_from_full(_dw(do2, s2, "dw_ffn2_down"), "col"), (0, 2, 1)),
    }

    def pair_sums(names, tag):
        gs = [dblocks[name] for name in names]
        from_sibling = _exchange(_pair_ride(gs), "reduce_pair_" + tag)
        return [_add_pairs(g, me_core, b, "reduce_pair_add_" + name) for g, b, name in zip(gs, from_sibling, names)]

    late_names = [name for name, _ in late]
    pairs = dict(zip(late_names, pair_sums(late_names, "late")))
    (dh0, dg_ffn1, n1, dab1, s1, do1), got = _ffn_bwd(
        h0, dh1, a1, b1, g_ffn1, full["ffn1_w_gate"], full["ffn1_w_up"], full["ffn1_w_down"], "ffn1_bwd",
        ride=_chips_ride([pairs[name] for name in late_names]))
    from_chips = dict(zip(late_names, got))
    dlam4 = dlam.reshape(SUBLANES, 2, 2, gh, n_state)
    dlam_in = jnp.transpose(dlam4, (2, 0, 1, 3, 4)).reshape(2, SUBLANES, n_grp, n_state)
    hg = gp // 2
    dbb_r = jnp.concatenate([_diag_blocks(dbc[h][:, :hg], gh) for h in range(2)], axis=0)
    dbb_i = jnp.concatenate([_diag_blocks(dbc[h][:, hg:], gh) for h in range(2)], axis=0)
    da_re, da_im, dldt, dbre_t, dbim_t = _s5_params_bwd(a_re, a_im, ldt, b_re_t, b_im_t, dlam_in, dbb_r, dbb_i)
    dc_re = jnp.concatenate([_diag_blocks(dcc[h][:hg], gh) for h in range(2)], axis=0)
    dc_im = -jnp.concatenate([_diag_blocks(dcc[h][hg:], gh) for h in range(2)], axis=0)

    grads_rep = {
        "g_ffn1": dg_ffn1, "g_mix": dg_mix, "b_gate": dbg, "ssm_a_re": da_re[None], "ssm_a_im": da_im[None],
        "ssm_log_dt": dldt.reshape(1, n_grp), "ssm_b_re": jnp.transpose(dbre_t, (0, 2, 1))[None],
        "ssm_b_im": jnp.transpose(dbim_t, (0, 2, 1))[None], "ssm_c_re": jnp.transpose(dc_re, (0, 2, 1))[None],
        "ssm_c_im": jnp.transpose(dc_im, (0, 2, 1))[None], "ssm_d": ddsk, "g_ffn2": dg_ffn2, "g_final": dg_final.reshape(d),
    }

    rep_shapes = [w[n].shape for n in REPLICATED]
    small_g_shapes = rep_shapes + [(n_meta, d), (3, dc), (1, 1)]
    gsmall = _pack([grads_rep[n] for n in REPLICATED] + [dh0[0:n_meta], dcw[0:3], loss_blk[0:1, 0:1]])
    dw_gu1, (gall,) = _dw(n1, dab1, "dw_ffn1_gate_up", ride=_gather_ride([gsmall]))
    dblocks.update({
        "ffn1_w_gate": _blocks_from_full(dw_gu1[:, :f_ff], "col"),
        "ffn1_w_up": _blocks_from_full(dw_gu1[:, f_ff:], "col"),
        "ffn1_w_down": jnp.transpose(_blocks_from_full(_dw(do1, s1, "dw_ffn1_down"), "col"), (0, 2, 1)),
    })
    early_names = [name for name, _ in early]
    pairs.update(zip(early_names, pair_sums(early_names, "early")))
    from_chips.update(zip(early_names, _exchange(_chips_ride([pairs[name] for name in early_names]), "reduce_chips_early")))

    out_g, out_d, out_m, out_v = {}, {}, {}, {}
    for name, _ in BIG:
        fc = from_chips[name]
        out_g[name], out_d[name], out_m[name], out_v[name] = _adamw(
            w[name], mom_m[name], mom_v[name], [(pairs[name], None), (fc, 0), (fc, 1), (fc, 2)], me_chip, "adamw_" + name)

    zer = [jnp.zeros((n_meta, d), F32), jnp.zeros((3, dc), F32), jnp.zeros((1, 1), F32)]
    gr, dr, mr, vr = [o[0] for o in _adamw(
        _pack([w[n] for n in REPLICATED] + zer)[None], _pack([mom_m[n] for n in REPLICATED] + zer)[None],
        _pack([mom_v[n] for n in REPLICATED] + zer)[None], [(gall, b) for b in range(NDEV)], None, "adamw_replicated")]
    g_list = _unpack(gr, small_g_shapes)
    out_g.update(zip(REPLICATED, g_list[:len(REPLICATED)]))
    out_d.update(zip(REPLICATED, _unpack(dr, rep_shapes)))
    out_m.update(zip(REPLICATED, _unpack(mr, rep_shapes)))
    out_v.update(zip(REPLICATED, _unpack(vr, rep_shapes)))

    loss = g_list[-1][0, 0]
    g_meta = lax.dynamic_slice_in_dim(g_list[-3], me * mcols, mcols, axis=1)
    g_cw = lax.dynamic_slice_in_dim(g_list[-2], me * ccols, ccols, axis=1).reshape(conv_w.shape)
    tiny = ("meta_tokens", "conv_w")
    tiny_shapes = [meta_tokens.shape, conv_w.shape]
    gt, dt_, mt, vt = [o[0] for o in _adamw(
        _pack([w[n] for n in tiny])[None], _pack([mom_m[n] for n in tiny])[None], _pack([mom_v[n] for n in tiny])[None],
        [(_pack([g_meta, g_cw])[None], 0)], None, "adamw_tiny")]
    out_g.update(zip(tiny, _unpack(gt, tiny_shapes)))
    out_d.update(zip(tiny, _unpack(dt_, tiny_shapes)))
    out_m.update(zip(tiny, _unpack(mt, tiny_shapes)))
    out_v.update(zip(tiny, _unpack(vt, tiny_shapes)))

    grad_x = dh0[n_meta:t_real][None]
    return (loss, grad_x, *[out_g[n] for n in WEIGHTS], *[out_d[n] for n in WEIGHTS],
            *[out_m[n] for n in WEIGHTS], *[out_v[n] for n in WEIGHTS])
```

```python
import functools

import jax
import jax.numpy as jnp
from jax import lax
from jax.experimental import pallas as pl
from jax.experimental.pallas import tpu as pltpu

F32 = jnp.float32
BF16 = jnp.bfloat16
MESH = pl.DeviceIdType.MESH
NDEV = 8
SLAB_COLS = 1024
RMS_EPS = 1e-6
TOKEN_TILE = 320
LIGHT_TILE = 416
RELAY_LEAD = 3
MIX_TILE = 320
ROW_ALIGN = 128
SUBLANES = 8
SCAN_LANES = 512
FFN_CHUNK = 4096
VMEM_LIMIT_BYTES = 56 * 1024 * 1024

ADAM_LR = 0.001
ADAM_B1 = 0.9
ADAM_B2 = 0.999
ADAM_EPS = 1e-08
ADAM_WD = 0.01
ADAM_STEP = 10

_VM = pl.BlockSpec(memory_space=pltpu.VMEM)
_ANY = pl.BlockSpec(memory_space=pl.ANY)


def _params(sem=("arbitrary",)):
    return pltpu.CompilerParams(dimension_semantics=sem, vmem_limit_bytes=VMEM_LIMIT_BYTES)


def _dot(a, b):
    return jnp.dot(a, b, preferred_element_type=F32)


def _dot_nt(a, b):
    return lax.dot_general(a, b, (((1,), (1,)), ((), ())), preferred_element_type=F32)


def _dot_tn(a, b):
    return lax.dot_general(a, b, (((0,), (0,)), ((), ())), preferred_element_type=F32)


def _tile(rows, most):
    return next(k for k in range(most - most % 16, 0, -16) if rows % k == 0)


def _chunks(n, step):
    return [(s, min(s + step, n)) for s in range(0, n, step)]


def _gather_plan(x_refs, out_refs, send_sems, recv_sems, local_sems):
    n = len(x_refs)
    x, y, c = lax.axis_index("x"), lax.axis_index("y"), lax.axis_index("c")
    me, sibling = (x, y, c), (x, y, 1 - c)
    chips = [(1 - x, y), (x, 1 - y), (1 - x, 1 - y)]

    def copy(i, k, block, to, src=None):
        slot = out_refs[i].at[4 * block[0] + 2 * block[1] + block[2]]
        return pltpu.make_async_remote_copy(
            src_ref=slot if src is None else src, dst_ref=slot,
            send_sem=send_sems.at[7 * i + k], recv_sem=recv_sems.at[7 * i + k], device_id=to, device_id_type=MESH)

    def mine():
        return [pltpu.make_async_copy(x_refs[i], out_refs[i].at[4 * x + 2 * y + c], local_sems.at[i]) for i in range(n)]

    def first():
        out = []
        for i in range(n):
            out.append(copy(i, 0, me, sibling, src=x_refs[i]))
            out += [copy(i, 1 + j, me, (*chip, c), src=x_refs[i]) for j, chip in enumerate(chips)]
        return out

    def start():
        for cp in mine() + first():
            cp.start()

    def passed():
        return [copy(i, 4 + j, (*chip, c), sibling) for j, chip in enumerate(chips) for i in range(n)]

    def relay():
        for j, chip in enumerate(chips):
            for i in range(n):
                copy(i, 1 + j, (*chip, c), me).wait_recv()
                copy(i, 4 + j, (*chip, c), sibling).start()

    def finish():
        for i in range(n):
            copy(i, 0, sibling, me).wait_recv()
            for j, chip in enumerate(chips):
                copy(i, 4 + j, (*chip, 1 - c), me).wait_recv()
        for cp in first() + passed():
            cp.wait_send()
        for cp in mine():
            cp.wait()

    return start, relay, finish


def _pair_plan(g_refs, out_refs, send_sems, recv_sems):
    x, y, c = lax.axis_index("x"), lax.axis_index("y"), lax.axis_index("c")

    def copies():
        return [pltpu.make_async_remote_copy(
            src_ref=g_refs[i].at[2 * j + (1 - c)], dst_ref=out_refs[i].at[j],
            send_sem=send_sems.at[4 * i + j], recv_sem=recv_sems.at[4 * i + j],
            device_id=(x, y, 1 - c), device_id_type=MESH) for i in range(len(g_refs)) for j in range(4)]

    def start():
        for cp in copies():
            cp.start()

    def finish():
        for cp in copies():
            cp.wait()

    return start, None, finish


def _chips_plan(p_refs, out_refs, send_sems, recv_sems):
    x, y, c = lax.axis_index("x"), lax.axis_index("y"), lax.axis_index("c")

    def copies():
        return [pltpu.make_async_remote_copy(
            src_ref=p_refs[i].at[2 * px + py], dst_ref=out_refs[i].at[k],
            send_sem=send_sems.at[3 * i + k], recv_sem=recv_sems.at[3 * i + k],
            device_id=(px, py, c), device_id_type=MESH)
            for i in range(len(p_refs)) for k, (px, py) in enumerate([(1 - x, y), (x, 1 - y), (1 - x, 1 - y)])]

    def start():
        for cp in copies():
            cp.start()

    def finish():
        for cp in copies():
            cp.wait()

    return start, None, finish


def _gather_ride(shards):
    n = len(shards)
    return dict(plan=_gather_plan, arrays=list(shards),
                out_shape=[jax.ShapeDtypeStruct((NDEV, *s.shape), s.dtype) for s in shards],
                sems=[pltpu.SemaphoreType.DMA((7 * n,)), pltpu.SemaphoreType.DMA((7 * n,)), pltpu.SemaphoreType.DMA((n,))])


def _pair_ride(blocks):
    n = len(blocks)
    return dict(plan=_pair_plan, arrays=list(blocks),
                out_shape=[jax.ShapeDtypeStruct((4, *b.shape[1:]), b.dtype) for b in blocks],
                sems=[pltpu.SemaphoreType.DMA((4 * n,)), pltpu.SemaphoreType.DMA((4 * n,))])


def _chips_ride(partials):
    n = len(partials)
    return dict(plan=_chips_plan, arrays=list(partials),
                out_shape=[jax.ShapeDtypeStruct((3, *p.shape[1:]), p.dtype) for p in partials],
                sems=[pltpu.SemaphoreType.DMA((3 * n,)), pltpu.SemaphoreType.DMA((3 * n,))])


def _exchange(ride, name):
    n = len(ride["arrays"])

    def body(*refs):
        start, relay, finish = ride["plan"](refs[:n], refs[n:2 * n], *refs[2 * n:])
        start()
        if relay:
            relay()
        finish()

    return pl.pallas_call(
        body, name=name, out_shape=ride["out_shape"], in_specs=[_ANY] * n, out_specs=[_ANY] * n, scratch_shapes=ride["sems"],
    )(*ride["arrays"])


def _grid_call(body, name, steps, in_specs, out_specs, out_shape, scratch_shapes, args, ride=None):
    if ride is None:
        outs = pl.pallas_call(body, name=name, grid=(steps,), in_specs=in_specs, out_specs=out_specs, out_shape=out_shape,
                              scratch_shapes=scratch_shapes, compiler_params=_params())(*args)
        return list(outs), []
    n_in, n_out, n_scr, n_ride, n_sems = len(in_specs), len(out_specs), len(scratch_shapes), len(ride["arrays"]), len(ride["sems"])

    def carrying(*refs):
        ins, r_in = refs[:n_in], refs[n_in:n_in + n_ride]
        o0 = n_in + n_ride
        outs, r_out = refs[o0:o0 + n_out], refs[o0 + n_out:o0 + n_out + n_ride]
        s0 = o0 + n_out + n_ride
        scratch, sems = refs[s0:s0 + n_scr], refs[s0 + n_scr:s0 + n_scr + n_sems]
        start, relay, finish = ride["plan"](r_in, r_out, *sems)
        pl.when(pl.program_id(0) == 0)(start)
        if relay:
            pl.when(pl.program_id(0) == max(steps - 1 - RELAY_LEAD, 0))(relay)
        body(*ins, *outs, *scratch)
        pl.when(pl.program_id(0) == steps - 1)(finish)

    outs = pl.pallas_call(
        carrying, name=name, grid=(steps,), in_specs=list(in_specs) + [_ANY] * n_ride, out_specs=list(out_specs) + [_ANY] * n_ride,
        out_shape=list(out_shape) + ride["out_shape"], scratch_shapes=list(scratch_shapes) + ride["sems"],
        compiler_params=_params())(*args, *ride["arrays"])
    return list(outs[:n_out]), list(outs[n_out:])


def _row_block(rows):
    return rows if rows <= 512 else next(k for k in (512, 256, 128, rows) if rows % k == 0)


def _add_pairs(gs, core, b, name):
    k, r, n = b.shape
    tr = _row_block(r)

    def body(core_ref, a_ref, b_ref, o_ref):
        o_ref[0] = (a_ref[0, 0].astype(F32) + b_ref[0].astype(F32)).astype(o_ref.dtype)

    spec = pl.BlockSpec((1, tr, n), lambda j, i, c: (j, i, 0))
    return pl.pallas_call(
        body, name=name,
        grid_spec=pltpu.PrefetchScalarGridSpec(
            num_scalar_prefetch=1, grid=(k, r // tr),
            in_specs=[pl.BlockSpec((1, 1, tr, n), lambda j, i, c: (j, c[0], i, 0)), spec], out_specs=spec),
        out_shape=jax.ShapeDtypeStruct(b.shape, b.dtype), compiler_params=_params(("arbitrary", "arbitrary")),
    )(core.reshape(1), gs.reshape(k, 2, r, n), b)


def _adamw(w, m, v, parts, sel, name):
    _, r, n = w.shape
    tr = _row_block(r)
    nparts = len(parts)
    bc1 = 1.0 - ADAM_B1 ** ADAM_STEP
    bc2 = 1.0 - ADAM_B2 ** ADAM_STEP

    def body(sel_ref, *refs):
        w_ref, m_ref, v_ref = refs[:3]
        p_refs = refs[3:3 + nparts]
        g_ref, d_ref, nm_ref, nv_ref = refs[3 + nparts:]
        g = p_refs[0][...].astype(F32)
        for p in p_refs[1:]:
            g = g + p[...].astype(F32)
        nm = ADAM_B1 * m_ref[...] + (1.0 - ADAM_B1) * g
        nv = ADAM_B2 * v_ref[...] + (1.0 - ADAM_B2) * (g * g)
        m_hat = nm / bc1
        v_hat = nv / bc2
        g_ref[...] = g
        d_ref[...] = -ADAM_LR * (m_hat / (jnp.sqrt(v_hat) + ADAM_EPS) + ADAM_WD * w_ref[...])
        nm_ref[...] = nm
        nv_ref[...] = nv

    def part_spec(idx):
        if idx is None:
            return pl.BlockSpec((1, tr, n), lambda i, s: (s[0], i, 0))
        return pl.BlockSpec((1, tr, n), lambda i, s, idx=idx: (idx, i, 0))

    spec = pl.BlockSpec((1, tr, n), lambda i, s: (0, i, 0))
    out = jax.ShapeDtypeStruct((1, r, n), F32)
    return pl.pallas_call(
        body, name=name,
        grid_spec=pltpu.PrefetchScalarGridSpec(
            num_scalar_prefetch=1, grid=(r // tr,),
            in_specs=[spec] * 3 + [part_spec(idx) for _, idx in parts], out_specs=[spec] * 4),
        out_shape=[out] * 4, compiler_params=_params(),
    )(jnp.zeros((1,), jnp.int32) if sel is None else sel.reshape(1), w, m, v, *[p for p, _ in parts])


def _rms_parts(h, g):
    r = lax.rsqrt(jnp.mean(h * h, axis=-1, keepdims=True) + RMS_EPS)
    xhat = h * r
    return r, xhat, xhat * g


def _rms_bwd(dn, g, r, xhat):
    dxh = dn * g
    return r * (dxh - xhat * jnp.mean(dxh * xhat, axis=-1, keepdims=True))


def _loss_tile(h, tgt, g, lo, hi, loss_ref, dg_ref):
    tm, d = h.shape
    i = pl.program_id(0)

    @pl.when(i == 0)
    def _():
        loss_ref[...] = jnp.zeros_like(loss_ref)
        dg_ref[...] = jnp.zeros_like(dg_ref)

    r, xhat, y = _rms_parts(h, g)
    row = i * tm + lax.broadcasted_iota(jnp.int32, (tm, 1), 0)
    err = jnp.where((row >= lo) & (row < hi), y - tgt, 0.0)
    loss_ref[...] += jnp.full(loss_ref.shape, 0.5 * jnp.sum(jnp.mean(err * err, axis=-1, keepdims=True)), F32)
    dy = err * (1.0 / d)
    dg_ref[...] += jnp.sum(dy * xhat, axis=0, keepdims=True)
    return _rms_bwd(dy, g, r, xhat)


def _ffn_fwd(h, g, wg, wu, wd, name, ride=None, head=None):
    t, d = h.shape
    f = wg.shape[1]
    tm = _tile(t, TOKEN_TILE)
    chunks = _chunks(f, FFN_CHUNK)

    def body(h_ref, g_ref, wg_ref, wu_ref, wd_ref, *rest):
        t_ref, gh_ref = rest[:2] if head else (None, None)
        o_ref, a_ref, b_ref = rest[2:5] if head else rest
        hv = h_ref[...]
        n = _rms_parts(hv, g_ref[...])[2].astype(BF16)
        acc = jnp.zeros((tm, d), F32)
        for s, e in chunks:
            a = _dot(n, wg_ref[:, s:e])
            b = _dot(n, wu_ref[:, s:e])
            a_ref[:, s:e] = a.astype(BF16)
            b_ref[:, s:e] = b.astype(BF16)
            acc = acc + _dot((a * jax.nn.sigmoid(a) * b).astype(BF16), wd_ref[s:e, :])
        out = hv + 0.5 * acc
        o_ref[...] = _loss_tile(out, t_ref[...], gh_ref[...], head[2], head[3], rest[5], rest[6]) if head else out

    tile = pl.BlockSpec((tm, d), lambda i: (i, 0))
    wide = pl.BlockSpec((tm, f), lambda i: (i, 0))
    in_specs, args = [tile, _VM, _VM, _VM, _VM], (h, g, wg, wu, wd)
    out_specs = [tile, wide, wide]
    out_shape = [jax.ShapeDtypeStruct((t, d), F32), jax.ShapeDtypeStruct((t, f), BF16), jax.ShapeDtypeStruct((t, f), BF16)]
    if head:
        in_specs, args = in_specs + [tile, _VM], args + (head[0], head[1])
        out_specs = out_specs + [pl.BlockSpec((SUBLANES, 128), lambda i: (0, 0)), pl.BlockSpec((1, d), lambda i: (0, 0))]
        out_shape = out_shape + [jax.ShapeDtypeStruct((SUBLANES, 128), F32), jax.ShapeDtypeStruct((1, d), F32)]
    return _grid_call(body, name, t // tm, in_specs, out_specs, out_shape, [], args, ride)


def _ffn_bwd(h, dh_out, a16, b16, g, wg, wu, wd, name, ride=None):
    t, d = h.shape
    f = wg.shape[1]
    tm = _tile(t, TOKEN_TILE)
    chunks = _chunks(f, FFN_CHUNK)

    def body(h_ref, dho_ref, a_ref, b_ref, g_ref, wg_ref, wu_ref, wd_ref, dh_ref, dg_ref, n_ref, dab_ref, s_ref, do_ref):
        @pl.when(pl.program_id(0) == 0)
        def _():
            dg_ref[...] = jnp.zeros_like(dg_ref)

        hv = h_ref[...]
        gv = g_ref[...]
        r, xhat, n32 = _rms_parts(hv, gv)
        dho = dho_ref[...]
        do = (0.5 * dho).astype(BF16)
        dn = jnp.zeros((tm, d), F32)
        for s, e in chunks:
            a = a_ref[:, s:e].astype(F32)
            b = b_ref[:, s:e].astype(F32)
            sig = jax.nn.sigmoid(a)
            sa = a * sig
            ds = _dot_nt(do, wd_ref[s:e, :])
            da = (ds * b * (sig * (1.0 + a * (1.0 - sig)))).astype(BF16)
            db = (ds * sa).astype(BF16)
            s_ref[:, s:e] = (sa * b).astype(BF16)
            dab_ref[:, s:e] = da
            dab_ref[:, f + s:f + e] = db
            dn = dn + _dot_nt(da, wg_ref[:, s:e]) + _dot_nt(db, wu_ref[:, s:e])
        dh_ref[...] = dho + _rms_bwd(dn, gv, r, xhat)
        dg_ref[...] += jnp.sum(dn * xhat, axis=0, keepdims=True)
        n_ref[...] = n32.astype(BF16)
        do_ref[...] = do

    tile = pl.BlockSpec((tm, d), lambda i: (i, 0))
    wide = pl.BlockSpec((tm, f), lambda i: (i, 0))
    one = pl.BlockSpec((1, d), lambda i: (0, 0))
    return _grid_call(
        body, name, t // tm, [tile, tile, wide, wide, _VM, _VM, _VM, _VM],
        [tile, one, tile, pl.BlockSpec((tm, 2 * f), lambda i: (i, 0)), wide, tile],
        [jax.ShapeDtypeStruct((t, d), F32), jax.ShapeDtypeStruct((1, d), F32),
         jax.ShapeDtypeStruct((t, d), BF16), jax.ShapeDtypeStruct((t, 2 * f), BF16),
         jax.ShapeDtypeStruct((t, f), BF16), jax.ShapeDtypeStruct((t, d), BF16)],
        [], (h, dh_out, a16, b16, g, wg, wu, wd), ride)


def _dw(a, b, name, ride=None):
    t, m = a.shape
    n = b.shape[1]
    bn = next(k for k in (512, 256, n) if n % k == 0)

    def body(a_ref, b_ref, o_ref):
        o_ref[...] = _dot_tn(a_ref[...], b_ref[...]).astype(BF16)

    (out,), got = _grid_call(
        body, name, n // bn, [_VM, pl.BlockSpec((t, bn), lambda j: (0, j))], [pl.BlockSpec((m, bn), lambda j: (0, j))],
        [jax.ShapeDtypeStruct((m, n), BF16)], [], (a, b), ride)
    return (out, got) if ride else out


def _to_bf16(arrays, name):
    k = len(arrays)

    def body(*refs):
        for x_ref, o_ref in zip(refs[:k], refs[k:]):
            o_ref[...] = x_ref[...].astype(BF16)

    return pl.pallas_call(
        body, name=name, out_shape=[jax.ShapeDtypeStruct(a.shape, BF16) for a in arrays],
        compiler_params=pltpu.CompilerParams(vmem_limit_bytes=VMEM_LIMIT_BYTES),
    )(*arrays)


def _s5_discretise(a_re, a_im, log_dt, b_re, b_im):
    dt = jnp.exp(log_dt)
    mag = jnp.exp(a_re * dt)
    lam_re = mag * jnp.cos(a_im * dt)
    lam_im = mag * jnp.sin(a_im * dt)
    den = a_re * a_re + a_im * a_im
    q_re = ((lam_re - 1.0) * a_re + lam_im * a_im) / den
    q_im = (lam_im * a_re - (lam_re - 1.0) * a_im) / den
    bb_re = q_re[:, None, :] * b_re - q_im[:, None, :] * b_im
    bb_im = q_re[:, None, :] * b_im + q_im[:, None, :] * b_re
    return lam_re, lam_im, bb_re, bb_im


def _s5_params_fwd(a_re, a_im, log_dt, b_re, b_im):
    g, p = a_re.shape
    c = b_re.shape[1]

    def body(are_ref, aim_ref, ldt_ref, bre_ref, bim_ref, pwr_ref, pwi_ref, bbr_ref, bbi_ref):
        lr, li, bbr, bbi = _s5_discretise(are_ref[...], aim_ref[...], ldt_ref[...], bre_ref[...], bim_ref[...])
        bbr_ref[...] = bbr
        bbi_ref[...] = bbi
        pr, pi = lr, li
        pwr_ref[0] = pr
        pwi_ref[0] = pi
        for k in range(1, SUBLANES):
            pr, pi = pr * lr - pi * li, pr * li + pi * lr
            pwr_ref[k] = pr
            pwi_ref[k] = pi

    return pl.pallas_call(
        body, name="s5_params_fwd",
        out_shape=[jax.ShapeDtypeStruct((SUBLANES, g, p), F32), jax.ShapeDtypeStruct((SUBLANES, g, p), F32),
                   jax.ShapeDtypeStruct((g, c, p), F32), jax.ShapeDtypeStruct((g, c, p), F32)],
    )(a_re, a_im, log_dt, b_re, b_im)


def _s5_params_bwd(a_re, a_im, log_dt, b_re, b_im, dlam, dbb_re, dbb_im):
    g, p = a_re.shape
    c = b_re.shape[1]

    def body(are_ref, aim_ref, ldt_ref, bre_ref, bim_ref, dlam_ref, dbr_ref, dbi_ref,
             dare_ref, daim_ref, dldt_ref, dbre_ref, dbim_ref):
        dlr = jnp.sum(dlam_ref[0], axis=0)
        dli = jnp.sum(dlam_ref[1], axis=0)
        _, vjp = jax.vjp(_s5_discretise, are_ref[...], aim_ref[...], ldt_ref[...], bre_ref[...], bim_ref[...])
        dare, daim, dldt, dbre, dbim = vjp((dlr, dli, dbr_ref[...], dbi_ref[...]))
        dare_ref[...] = dare
        daim_ref[...] = daim
        dldt_ref[...] = dldt
        dbre_ref[...] = dbre
        dbim_ref[...] = dbim

    return pl.pallas_call(
        body, name="s5_params_bwd",
        out_shape=[jax.ShapeDtypeStruct((g, p), F32), jax.ShapeDtypeStruct((g, p), F32),
                   jax.ShapeDtypeStruct((g, 1), F32), jax.ShapeDtypeStruct((g, c, p), F32),
                   jax.ShapeDtypeStruct((g, c, p), F32)],
    )(a_re, a_im, log_dt, b_re, b_im, dlam, dbb_re, dbb_im)


def _scan_chunks(gp):
    hg = gp // 2
    w = min(SCAN_LANES, hg)
    return w, [(half * hg + k * w, half * gp + k * w, half * gp + hg + k * w) for half in range(2) for k in range(hg // w)]


def _cmul_acc(xr, xi, tr, ti, sr, si):
    return xr + tr * sr - ti * si, xi + tr * si + ti * sr


def _scan_fwd(buf_ref, row0, tm, ltab_ref, cin_ref, cout_ref, gp):
    w, chunks = _scan_chunks(gp)
    for lo_t, lo_r, lo_i in chunks:
        def body(r, carry, lo_t=lo_t, lo_r=lo_r, lo_i=lo_i):
            cr, ci = carry
            row = pl.multiple_of(row0 + r * SUBLANES, SUBLANES)
            xr = buf_ref[pl.ds(row, SUBLANES), lo_r:lo_r + w]
            xi = buf_ref[pl.ds(row, SUBLANES), lo_i:lo_i + w]
            for tab, shift in ((0, 1), (2, 2), (4, 4)):
                xr, xi = _cmul_acc(xr, xi, ltab_ref[tab, :, lo_t:lo_t + w], ltab_ref[tab + 1, :, lo_t:lo_t + w],
                                   pltpu.roll(xr, shift, 0), pltpu.roll(xi, shift, 0))
            xr, xi = _cmul_acc(xr, xi, ltab_ref[6, :, lo_t:lo_t + w], ltab_ref[7, :, lo_t:lo_t + w], cr, ci)
            buf_ref[pl.ds(row, SUBLANES), lo_r:lo_r + w] = xr
            buf_ref[pl.ds(row, SUBLANES), lo_i:lo_i + w] = xi
            last = SUBLANES - 1
            return (jnp.broadcast_to(xr[last:last + 1], (SUBLANES, w)), jnp.broadcast_to(xi[last:last + 1], (SUBLANES, w)))

        cr, ci = lax.fori_loop(0, tm // SUBLANES, body,
                               (cin_ref[0:SUBLANES, lo_r:lo_r + w], cin_ref[0:SUBLANES, lo_i:lo_i + w]), unroll=True)
        if cout_ref is not None:
            cout_ref[0:SUBLANES, lo_r:lo_r + w] = cr
            cout_ref[0:SUBLANES, lo_i:lo_i + w] = ci


def _scan_rev(g_ref, hext_ref, tm, ltab_ref, gc_ref, dlam_ref, gp):
    w, chunks = _scan_chunks(gp)
    nb = tm // SUBLANES
    for lo_t, lo_r, lo_i in chunks:
        def body(k, carry, lo_t=lo_t, lo_r=lo_r, lo_i=lo_i):
            cr, ci, ar, ai = carry
            row = pl.multiple_of((nb - 1 - k) * SUBLANES, SUBLANES)
            xr = g_ref[pl.ds(row, SUBLANES), lo_r:lo_r + w]
            xi = g_ref[pl.ds(row, SUBLANES), lo_i:lo_i + w]
            for tab, shift in ((8, 7), (10, 6), (12, 4)):
                xr, xi = _cmul_acc(xr, xi, ltab_ref[tab, :, lo_t:lo_t + w], ltab_ref[tab + 1, :, lo_t:lo_t + w],
                                   pltpu.roll(xr, shift, 0), pltpu.roll(xi, shift, 0))
            xr, xi = _cmul_acc(xr, xi, ltab_ref[14, :, lo_t:lo_t + w], ltab_ref[15, :, lo_t:lo_t + w], cr, ci)
            g_ref[pl.ds(row, SUBLANES), lo_r:lo_r + w] = xr
            g_ref[pl.ds(row, SUBLANES), lo_i:lo_i + w] = xi
            first = lax.broadcasted_iota(jnp.int32, (SUBLANES, w), 0) == 0
            prev = pl.ds(row, SUBLANES)
            here = pl.ds(row + SUBLANES, SUBLANES)
            hpr = jnp.where(first, pltpu.roll(hext_ref[prev, lo_r:lo_r + w], 1, 0), pltpu.roll(hext_ref[here, lo_r:lo_r + w], 1, 0))
            hpi = jnp.where(first, pltpu.roll(hext_ref[prev, lo_i:lo_i + w], 1, 0), pltpu.roll(hext_ref[here, lo_i:lo_i + w], 1, 0))
            ar = ar + xr * hpr + xi * hpi
            ai = ai - xr * hpi + xi * hpr
            return (jnp.broadcast_to(xr[0:1], (SUBLANES, w)), jnp.broadcast_to(xi[0:1], (SUBLANES, w)), ar, ai)

        cr, ci, ar, ai = lax.fori_loop(
            0, nb, body, (gc_ref[:, lo_r:lo_r + w], gc_ref[:, lo_i:lo_i + w], dlam_ref[:, lo_r:lo_r + w], dlam_ref[:, lo_i:lo_i + w]),
            unroll=True)
        gc_ref[:, lo_r:lo_r + w] = cr
        gc_ref[:, lo_i:lo_i + w] = ci
        dlam_ref[:, lo_r:lo_r + w] = ar
        dlam_ref[:, lo_i:lo_i + w] = ai


def _conv_taps(cw, cext_ref, cin, tm):
    return (cw[0:1] * cext_ref[SUBLANES - 2:SUBLANES - 2 + tm, :] + cw[1:2] * cext_ref[SUBLANES - 1:SUBLANES - 1 + tm, :]
            + cw[2:3] * cin)


def _mix_fwd(h, gm, win, bg, bc, cc, dsk, wglu, cw, wco, wo, ltab, dims):
    d, ds, dc, gp = dims
    t = h.shape[0]
    tm = _tile(t, MIX_TILE)
    nt = t // tm
    dsh = ds // 2
    o1, o2, o3 = ds + dc, ds + 2 * dc, ds + 3 * dc
    ncols = o3 + 2 * d

    def body(h_ref, gm_ref, win_ref, bg_ref, bc_ref, cc_ref, dsk_ref, wglu_ref, cw_ref, wco_ref, wo_ref, ltab_ref,
             h2_ref, p_ref, hs_ref, y5_ref, z_ref, yc_ref, hbuf_ref, carry_ref, cext_ref):
        @pl.when(pl.program_id(0) == 0)
        def _():
            carry_ref[...] = jnp.zeros_like(carry_ref)
            cext_ref[0:SUBLANES, :] = jnp.zeros((SUBLANES, dc), F32)

        hv = h_ref[...]
        bg = bg_ref[...]
        u = _rms_parts(hv, gm_ref[...])[2].astype(BF16)
        us = _dot(u, win_ref[:, 0:ds])
        v = _dot(u, win_ref[:, ds:o1])
        gb = _dot(u, win_ref[:, o1:o2])
        gcv = _dot(u, win_ref[:, o2:o3])
        gs = jax.nn.sigmoid(_dot(u, win_ref[:, o3:o3 + d]) + bg[:, 0:d])
        gcg = jax.nn.sigmoid(_dot(u, win_ref[:, o3 + d:o3 + 2 * d]) + bg[:, d:2 * d])
        us16 = us.astype(BF16)
        p_ref[:, 0:ds] = us16
        p_ref[:, ds:o1] = v.astype(BF16)
        p_ref[:, o1:o2] = gb.astype(BF16)
        p_ref[:, o2:o3] = gcv.astype(BF16)
        p_ref[:, o3:o3 + d] = gs.astype(BF16)
        p_ref[:, o3 + d:ncols] = gcg.astype(BF16)
        for half in range(2):
            hbuf_ref[:, half * gp:(half + 1) * gp] = _dot(us16[:, half * dsh:(half + 1) * dsh], bc_ref[half])
        _scan_fwd(hbuf_ref, 0, tm, ltab_ref, carry_ref, carry_ref, gp)
        hs_ref[...] = hbuf_ref[...].astype(BF16)
        y5 = jnp.concatenate([_dot(hs_ref[:, half * gp:(half + 1) * gp], cc_ref[half]) for half in range(2)], axis=1) + dsk_ref[...] * us
        y5_ref[...] = y5.astype(BF16)
        z = _dot(jax.nn.gelu(y5).astype(BF16), wglu_ref[...])
        z_ref[...] = z.astype(BF16)
        ys = z[:, 0:d] * jax.nn.sigmoid(z[:, d:2 * d])
        cin = gcv * v
        cext_ref[SUBLANES:SUBLANES + tm, :] = cin
        yc = _dot((gb * _conv_taps(cw_ref[...], cext_ref, cin, tm)).astype(BF16), wco_ref[...])
        yc_ref[...] = yc.astype(BF16)
        h2_ref[...] = hv + _dot((gs * ys + gcg * yc).astype(BF16), wo_ref[...])
        cext_ref[0:SUBLANES, :] = cext_ref[tm:tm + SUBLANES, :]

    def tile(cols):
        return pl.BlockSpec((tm, cols), lambda i: (i, 0))

    def bf(cols):
        return jax.ShapeDtypeStruct((t, cols), BF16)

    return pl.pallas_call(
        body, name="mix_fwd", grid=(nt,),
        in_specs=[tile(d)] + [_VM] * 11,
        out_specs=[tile(d), tile(ncols), tile(2 * gp), tile(ds), tile(2 * d), tile(d)],
        out_shape=[jax.ShapeDtypeStruct((t, d), F32), bf(ncols), bf(2 * gp), bf(ds), bf(2 * d), bf(d)],
        scratch_shapes=[pltpu.VMEM((tm, 2 * gp), F32), pltpu.VMEM((SUBLANES, 2 * gp), F32), pltpu.VMEM((SUBLANES + tm, dc), F32)],
        compiler_params=_params(),
    )(h, gm, win, bg, bc, cc, dsk, wglu, cw, wco, wo, ltab)


HALO = 16


def _mix_bwd_gates(dh2, p16, y516, z16, yc16, wglu, cw, wco, wo, dims):
    d, ds, dc, gp = dims
    t = dh2.shape[0]
    tm = _tile(t, LIGHT_TILE)
    nt = t // tm
    o1, o2, o3 = ds + dc, ds + 2 * dc, ds + 3 * dc
    ncols = o3 + 2 * d

    def body(dh2_ref, p_ref, halo_ref, y5_ref, z_ref, yc_ref, wglu_ref, cw_ref, wco_ref, wo_ref,
             dp_ref, dy5_ref, ge_ref, dz_ref, cg_ref, dyc_ref, mx_ref, dh216_ref, dbg_ref, dcw_ref, cext_ref, dcvext_ref):
        j = pl.program_id(0)

        @pl.when(j == 0)
        def _():
            dbg_ref[...] = jnp.zeros_like(dbg_ref)
            dcw_ref[...] = jnp.zeros_like(dcw_ref)
            dcvext_ref[tm:tm + SUBLANES, :] = jnp.zeros((SUBLANES, dc), F32)

        before = halo_ref[:, o2:o3].astype(F32) * halo_ref[:, ds:o1].astype(F32)
        cext_ref[0:SUBLANES, :] = jnp.where(j == nt - 1, 0.0, before[HALO - SUBLANES:HALO])
        cw_v = cw_ref[...]
        v = p_ref[:, ds:o1].astype(F32)
        gb = p_ref[:, o1:o2].astype(F32)
        gcv = p_ref[:, o2:o3].astype(F32)
        gs = p_ref[:, o3:o3 + d].astype(F32)
        gcg = p_ref[:, o3 + d:ncols].astype(F32)
        z1 = z_ref[:, 0:d].astype(F32)
        sz = jax.nn.sigmoid(z_ref[:, d:2 * d].astype(F32))
        ys = z1 * sz
        yc = yc_ref[...].astype(F32)
        ge, gelu_vjp = jax.vjp(jax.nn.gelu, y5_ref[...].astype(F32))
        cin = gcv * v
        cext_ref[SUBLANES:SUBLANES + tm, :] = cin
        cv = _conv_taps(cw_v, cext_ref, cin, tm)

        dh216 = dh2_ref[...].astype(BF16)
        dmixed = _dot_nt(dh216, wo_ref[...])
        dys = dmixed * gs
        dyc16 = (dmixed * gcg).astype(BF16)
        dpgs = dmixed * ys * gs * (1.0 - gs)
        dpgc = dmixed * yc * gcg * (1.0 - gcg)
        dz16 = jnp.concatenate([dys * sz, dys * z1 * sz * (1.0 - sz)], axis=1).astype(BF16)
        dy5_ref[...] = gelu_vjp(_dot_nt(dz16, wglu_ref[...]))[0].astype(BF16)
        dcg = _dot_nt(dyc16, wco_ref[...])
        dcv = dcg * gb
        dcvext_ref[0:tm, :] = dcv
        dcin = cw_v[2:3] * dcv + cw_v[1:2] * dcvext_ref[1:1 + tm, :] + cw_v[0:1] * dcvext_ref[2:2 + tm, :]
        dcw_ref[0:1, :] += jnp.sum(dcv * cext_ref[SUBLANES - 2:SUBLANES - 2 + tm, :], axis=0, keepdims=True)
        dcw_ref[1:2, :] += jnp.sum(dcv * cext_ref[SUBLANES - 1:SUBLANES - 1 + tm, :], axis=0, keepdims=True)
        dcw_ref[2:3, :] += jnp.sum(dcv * cin, axis=0, keepdims=True)
        dcvext_ref[tm:tm + SUBLANES, :] = dcvext_ref[0:SUBLANES, :]
        dbg_ref[...] += jnp.concatenate([jnp.sum(dpgs, axis=0, keepdims=True), jnp.sum(dpgc, axis=0, keepdims=True)], axis=1)
        dp_ref[:, 0:ds] = jnp.zeros((tm, ds), BF16)
        dp_ref[:, ds:o1] = (dcin * gcv).astype(BF16)
        dp_ref[:, o1:o2] = (dcg * cv).astype(BF16)
        dp_ref[:, o2:o3] = (dcin * v).astype(BF16)
        dp_ref[:, o3:o3 + d] = dpgs.astype(BF16)
        dp_ref[:, o3 + d:ncols] = dpgc.astype(BF16)
        ge_ref[...] = ge.astype(BF16)
        dz_ref[...] = dz16
        cg_ref[...] = (gb * cv).astype(BF16)
        dyc_ref[...] = dyc16
        mx_ref[...] = (gs * ys + gcg * yc).astype(BF16)
        dh216_ref[...] = dh216

    def rev(cols):
        return pl.BlockSpec((tm, cols), lambda j: (nt - 1 - j, 0))

    def bf(cols):
        return jax.ShapeDtypeStruct((t, cols), BF16)

    halo = pl.BlockSpec((HALO, ncols), lambda j: (jnp.maximum((nt - 1 - j) * (tm // HALO) - 1, 0), 0))
    return pl.pallas_call(
        body, name="mix_bwd_gates", grid=(nt,),
        in_specs=[rev(d), rev(ncols), halo, rev(ds), rev(2 * d), rev(d), _VM, _VM, _VM, _VM],
        out_specs=[rev(ncols), rev(ds), rev(ds), rev(2 * d), rev(dc), rev(d), rev(d), rev(d), _VM, _VM],
        out_shape=[bf(ncols), bf(ds), bf(ds), bf(2 * d), bf(dc), bf(d), bf(d), bf(d),
                   jax.ShapeDtypeStruct((1, 2 * d), F32), jax.ShapeDtypeStruct((SUBLANES, dc), F32)],
        scratch_shapes=[pltpu.VMEM((SUBLANES + tm, dc), F32), pltpu.VMEM((tm + SUBLANES, dc), F32)],
        compiler_params=_params(),
    )(dh2, p16, p16, y516, z16, yc16, wglu, cw, wco, wo)


def _mix_bwd_scan(dy516, hs16, p16, bc, cc, dsk, ltab, dims):
    d, ds, dc, gp = dims
    t = dy516.shape[0]
    tm = _tile(t, TOKEN_TILE)
    nt = t // tm
    dsh = ds // 2

    def body(dy5_ref, hs_ref, halo_ref, us_ref, bc_ref, cc_ref, dsk_ref, ltab_ref,
             dus_ref, ddsk_ref, dlam_ref, dbc_ref, dcc_ref, hext_ref, gbuf_ref, gcarry_ref):
        j = pl.program_id(0)

        @pl.when(j == 0)
        def _():
            for ref in (ddsk_ref, dlam_ref, dbc_ref, dcc_ref, gcarry_ref):
                ref[...] = jnp.zeros_like(ref)

        before = jnp.where(j == nt - 1, 0.0, halo_ref[...].astype(F32)[HALO - 1:HALO])
        hext_ref[0:SUBLANES, :] = jnp.broadcast_to(before, (SUBLANES, 2 * gp))
        hext_ref[SUBLANES:SUBLANES + tm, :] = hs_ref[...].astype(F32)
        dy516v = dy5_ref[...]
        for half in range(2):
            gbuf_ref[:, half * gp:(half + 1) * gp] = _dot_nt(dy516v[:, half * dsh:(half + 1) * dsh], cc_ref[half])
        _scan_rev(gbuf_ref, hext_ref, tm, ltab_ref, gcarry_ref, dlam_ref, gp)
        dus = []
        for half in range(2):
            g16 = gbuf_ref[:, half * gp:(half + 1) * gp].astype(BF16)
            dus.append(_dot_nt(g16, bc_ref[half]))
            dbc_ref[half] += _dot_tn(us_ref[:, half * dsh:(half + 1) * dsh], g16)
            dcc_ref[half] += _dot_tn(hs_ref[:, half * gp:(half + 1) * gp], dy516v[:, half * dsh:(half + 1) * dsh])
        dy5 = dy516v.astype(F32)
        dus_ref[...] = (jnp.concatenate(dus, axis=1) + dsk_ref[...] * dy5).astype(BF16)
        ddsk_ref[...] += jnp.sum(dy5 * us_ref[...].astype(F32), axis=0, keepdims=True)

    def rev(cols):
        return pl.BlockSpec((tm, cols), lambda j: (nt - 1 - j, 0))

    halo = pl.BlockSpec((HALO, 2 * gp), lambda j: (jnp.maximum((nt - 1 - j) * (tm // HALO) - 1, 0), 0))
    return pl.pallas_call(
        body, name="mix_bwd_scan", grid=(nt,),
        in_specs=[rev(ds), rev(2 * gp), halo, rev(ds), _VM, _VM, _VM, _VM],
        out_specs=[rev(ds), _VM, _VM, _VM, _VM],
        out_shape=[jax.ShapeDtypeStruct((t, ds), BF16), jax.ShapeDtypeStruct((1, ds), F32),
                   jax.ShapeDtypeStruct((SUBLANES, 2 * gp), F32),
                   jax.ShapeDtypeStruct((2, dsh, gp), F32), jax.ShapeDtypeStruct((2, gp, dsh), F32)],
        scratch_shapes=[pltpu.VMEM((SUBLANES + tm, 2 * gp), F32), pltpu.VMEM((tm, 2 * gp), F32), pltpu.VMEM((SUBLANES, 2 * gp), F32)],
        compiler_params=_params(),
    )(dy516, hs16, hs16, p16, bc, cc, dsk, ltab)


def _mix_bwd_in(h, dh2, dp16, dus16, gm, win, dims):
    d, ds, dc, gp = dims
    t = h.shape[0]
    tm = _tile(t, LIGHT_TILE)
    ncols = dp16.shape[1]

    def body(h_ref, dh2_ref, dp_ref, dus_ref, gm_ref, win_ref, dh1_ref, u_ref, dpf_ref, dgm_ref):
        @pl.when(pl.program_id(0) == 0)
        def _():
            dgm_ref[...] = jnp.zeros_like(dgm_ref)

        gmv = gm_ref[...]
        r, xhat, n32 = _rms_parts(h_ref[...], gmv)
        du = _dot_nt(dus_ref[...], win_ref[:, 0:ds]) + _dot_nt(dp_ref[:, ds:ncols], win_ref[:, ds:ncols])
        dh1_ref[...] = dh2_ref[...] + _rms_bwd(du, gmv, r, xhat)
        dgm_ref[...] += jnp.sum(du * xhat, axis=0, keepdims=True)
        u_ref[...] = n32.astype(BF16)
        dpf_ref[:, 0:ds] = dus_ref[...]
        dpf_ref[:, ds:ncols] = dp_ref[:, ds:ncols]

    def tile(cols):
        return pl.BlockSpec((tm, cols), lambda i: (i, 0))

    return pl.pallas_call(
        body, name="mix_bwd_in", grid=(t // tm,),
        in_specs=[tile(d), tile(d), tile(ncols), tile(ds), _VM, _VM],
        out_specs=[tile(d), tile(d), tile(ncols), pl.BlockSpec((1, d), lambda i: (0, 0))],
        out_shape=[jax.ShapeDtypeStruct((t, d), F32), jax.ShapeDtypeStruct((t, d), BF16),
                   jax.ShapeDtypeStruct((t, ncols), BF16), jax.ShapeDtypeStruct((1, d), F32)],
        compiler_params=_params(),
    )(h, dh2, dp16, dus16, gm, win)


def _pad_rows(a, rows, axis=0):
    pad = [(0, 0)] * a.ndim
    pad[axis] = (0, rows - a.shape[axis])
    return jnp.pad(a, pad)


def _as_rows(a):
    flat = a.reshape(-1)
    n = -(-flat.shape[0] // SLAB_COLS) * SLAB_COLS
    return jnp.pad(flat, (0, n - flat.shape[0])).reshape(-1, SLAB_COLS)


def _pack(arrs):
    rows = jnp.concatenate([_as_rows(a) for a in arrs], axis=0)
    return _pad_rows(rows, -(-rows.shape[0] // 16) * 16)


def _unpack(slab, shapes):
    out, r = [], 0
    for shp in shapes:
        size = 1
        for s in shp:
            size *= s
        n = -(-size // SLAB_COLS)
        out.append(slab[r:r + n].reshape(-1)[:size].reshape(shp))
        r += n
    return out


def _block_diag(blocks):
    n, a, b = blocks.shape
    eye = jnp.eye(n, dtype=blocks.dtype)
    return (blocks[:, :, None, :] * eye[:, None, :, None]).reshape(n * a, n * b)


def _diag_blocks(mat, n):
    a, b = mat.shape[0] // n, mat.shape[1] // n
    eye = jnp.eye(n, dtype=mat.dtype)
    return jnp.sum(mat.reshape(n, a, n, b) * eye[:, None, :, None], axis=2)


BIG = (("ffn1_w_gate", "col"), ("ffn1_w_up", "col"), ("ffn1_w_down", "row"), ("w_in", "col"), ("ssm_w_glu", "col"),
       ("conv_w_out", "col"), ("w_o", "row"), ("ffn2_w_gate", "col"), ("ffn2_w_up", "col"), ("ffn2_w_down", "row"))
REPLICATED = ("g_ffn1", "g_mix", "b_gate", "ssm_a_re", "ssm_a_im", "ssm_log_dt", "ssm_b_re", "ssm_b_im", "ssm_c_re",
              "ssm_c_im", "ssm_d", "g_ffn2", "g_final")
WEIGHTS = ("meta_tokens", "g_ffn1", "ffn1_w_gate", "ffn1_w_up", "ffn1_w_down", "g_mix", "w_in", "b_gate", "ssm_a_re",
           "ssm_a_im", "ssm_log_dt", "ssm_b_re", "ssm_b_im", "ssm_c_re", "ssm_c_im", "ssm_d", "ssm_w_glu", "conv_w",
           "conv_w_out", "w_o", "g_ffn2", "ffn2_w_gate", "ffn2_w_up", "ffn2_w_down", "g_final")
N_EARLY = 3


def _full_from_blocks(blocks, kind):
    n, r, c = blocks.shape
    if kind == "col":
        return jnp.transpose(blocks, (1, 0, 2)).reshape(r, n * c)
    return blocks.reshape(n * r, c)


def _blocks_from_full(full, kind):
    if kind == "col":
        r, nc = full.shape
        return jnp.transpose(full.reshape(r, NDEV, nc // NDEV), (1, 0, 2))
    nr, c = full.shape
    return full.reshape(NDEV, nr // NDEV, c)


def kernel(x, meta_tokens, g_ffn1, ffn1_w_gate, ffn1_w_up, ffn1_w_down, g_mix, w_in, b_gate, ssm_a_re, ssm_a_im, ssm_log_dt, ssm_b_re, ssm_b_im, ssm_c_re, ssm_c_im, ssm_d, ssm_w_glu, conv_w, conv_w_out, w_o, g_ffn2, ffn2_w_gate, ffn2_w_up, ffn2_w_down, g_final, loss_target, m_meta_tokens, m_g_ffn1, m_ffn1_w_gate, m_ffn1_w_up, m_ffn1_w_down, m_g_mix, m_w_in, m_b_gate, m_ssm_a_re, m_ssm_a_im, m_ssm_log_dt, m_ssm_b_re, m_ssm_b_im, m_ssm_c_re, m_ssm_c_im, m_ssm_d, m_ssm_w_glu, m_conv_w, m_conv_w_out, m_w_o, m_g_ffn2, m_ffn2_w_gate, m_ffn2_w_up, m_ffn2_w_down, m_g_final, v_meta_tokens, v_g_ffn1, v_ffn1_w_gate, v_ffn1_w_up, v_ffn1_w_down, v_g_mix, v_w_in, v_b_gate, v_ssm_a_re, v_ssm_a_im, v_ssm_log_dt, v_ssm_b_re, v_ssm_b_im, v_ssm_c_re, v_ssm_c_im, v_ssm_d, v_ssm_w_glu, v_conv_w, v_conv_w_out, v_w_o, v_g_ffn2, v_ffn2_w_gate, v_ffn2_w_up, v_ffn2_w_down, v_g_final):
    args = dict(locals())
    w = {n: args[n] for n in WEIGHTS}
    mom_m = {n: args["m_" + n] for n in WEIGHTS}
    mom_v = {n: args["v_" + n] for n in WEIGHTS}

    seq, d = x.shape[1], x.shape[2]
    n_meta = meta_tokens.shape[0]
    ds = ssm_d.shape[1]
    n_grp, n_state = ssm_a_re.shape[1], ssm_a_re.shape[2]
    gp = n_grp * n_state
    dc = conv_w.shape[3] * NDEV
    dims = (d, ds, dc, gp)
    t_real = n_meta + seq
    t_pad = -(-t_real // ROW_ALIGN) * ROW_ALIGN
    me_chip = 2 * lax.axis_index("x") + lax.axis_index("y")
    me_core = lax.axis_index("c")
    me = 2 * me_chip + me_core
    mcols, ccols = d // NDEV, dc // NDEV

    cw_shard = _pad_rows(_pad_rows(conv_w.reshape(3, ccols), SUBLANES), 128, axis=1)
    shard16 = dict(zip([name for name, _ in BIG], _to_bf16([w[name][0] for name, _ in BIG], "weights_to_bf16")))
    early, late = BIG[:N_EARLY], BIG[N_EARLY:]
    got = _exchange(_gather_ride([shard16[name] for name, _ in early] + [meta_tokens, cw_shard]), "gather_first")
    full = {name: _full_from_blocks(got[i], kind) for i, (name, kind) in enumerate(early)}
    meta_full = _full_from_blocks(got[-2], "col")
    cw_rows = _pad_rows(_full_from_blocks(got[-1][:, 0:3, 0:ccols], "col"), SUBLANES)

    a_re, a_im, ldt = ssm_a_re[0], ssm_a_im[0], ssm_log_dt[0].reshape(n_grp, 1)
    b_re_t = jnp.transpose(ssm_b_re[0], (0, 2, 1))
    b_im_t = jnp.transpose(ssm_b_im[0], (0, 2, 1))
    pw_r, pw_i, bb_r, bb_i = _s5_params_fwd(a_re, a_im, ldt, b_re_t, b_im_t)
    pw_r = pw_r.reshape(SUBLANES, gp)
    pw_i = pw_i.reshape(SUBLANES, gp)
    sub = jnp.arange(SUBLANES)[:, None]

    def fwd_tab(p, k):
        return jnp.where(sub >= k, p[k - 1][None, :], 0.0)

    def rev_tab(p, k):
        return jnp.where(sub <= SUBLANES - 1 - k, p[k - 1][None, :], 0.0)

    ltab = jnp.stack(
        [fwd_tab(pw_r, 1), fwd_tab(pw_i, 1), fwd_tab(pw_r, 2), fwd_tab(pw_i, 2), fwd_tab(pw_r, 4), fwd_tab(pw_i, 4), pw_r, pw_i,
         rev_tab(pw_r, 1), -rev_tab(pw_i, 1), rev_tab(pw_r, 2), -rev_tab(pw_i, 2), rev_tab(pw_r, 4), -rev_tab(pw_i, 4),
         pw_r[::-1], -pw_i[::-1]], axis=0)
    gh = n_grp // 2
    bc = jnp.stack([jnp.concatenate([_block_diag(bb_r[h * gh:(h + 1) * gh]), _block_diag(bb_i[h * gh:(h + 1) * gh])], axis=1)
                    for h in range(2)]).astype(BF16)
    c_re_t = jnp.transpose(ssm_c_re[0], (0, 2, 1))
    c_im_t = jnp.transpose(ssm_c_im[0], (0, 2, 1))
    cc = jnp.stack([jnp.concatenate([_block_diag(c_re_t[h * gh:(h + 1) * gh]), -_block_diag(c_im_t[h * gh:(h + 1) * gh])], axis=0)
                    for h in range(2)]).astype(BF16)

    zpad = jnp.zeros((t_pad - t_real, d), F32)
    h0 = jnp.concatenate([meta_full, x[0], zpad], axis=0)
    tgt = jnp.concatenate([jnp.zeros((n_meta, d), F32), loss_target[0], zpad], axis=0)
    (h1, a1, b1), got = _ffn_fwd(h0, g_ffn1, full["ffn1_w_gate"], full["ffn1_w_up"], full["ffn1_w_down"], "ffn1_fwd",
                                 ride=_gather_ride([shard16[name] for name, _ in late]))
    full.update({name: _full_from_blocks(got[i], kind) for i, (name, kind) in enumerate(late)})
    h2, *saved = _mix_fwd(h1, g_mix, full["w_in"], b_gate, bc, cc, ssm_d, full["ssm_w_glu"], cw_rows, full["conv_w_out"],
                          full["w_o"], ltab, dims)
    (dh3, a2, b2, loss_blk, dg_final), _ = _ffn_fwd(h2, g_ffn2, full["ffn2_w_gate"], full["ffn2_w_up"], full["ffn2_w_down"], "ffn2_fwd",
                                                 head=(tgt, g_final.reshape(1, d), n_meta, t_real))

    (dh2, dg_ffn2, n2, dab2, s2, do2), _ = _ffn_bwd(
        h2, dh3, a2, b2, g_ffn2, full["ffn2_w_gate"], full["ffn2_w_up"], full["ffn2_w_down"], "ffn2_bwd")
    f_ff = a2.shape[1]
    dw_gu2 = _dw(n2, dab2, "dw_ffn2_gate_up")
    p16, hs16, y516, z16, yc16 = saved
    dp_part, dy516, ge16, dz16, cg16, dyc16, mx16, dh216, dbg, dcw = _mix_bwd_gates(
        dh2, p16, y516, z16, yc16, full["ssm_w_glu"], cw_rows, full["conv_w_out"], full["w_o"], dims)
    dus16, ddsk, dlam, dbc, dcc = _mix_bwd_scan(dy516, hs16, p16, bc, cc, ssm_d, ltab, dims)
    dh1, u16, dp16, dg_mix = _mix_bwd_in(h1, dh2, dp_part, dus16, g_mix, full["w_in"], dims)
    dblocks = {
        "w_in": _blocks_from_full(_dw(u16, dp16, "dw_in"), "col"),
        "ssm_w_glu": _blocks_from_full(_dw(ge16, dz16, "dw_glu"), "col"),
        "conv_w_out": _blocks_from_full(_dw(cg16, dyc16, "dw_conv_out"), "col"),
        "w_o": _blocks_from_full(_dw(mx16, dh216, "dw_o"), "row"),
        "ffn2_w_gate": _blocks_from_full(dw_gu2[:, :f_ff], "col"),
        "ffn2_w_up": _blocks_from_full(dw_gu2[:, f_ff:], "col"),
        "ffn2_w_down": jnp.transpose(_blocks_from_full(_dw(do2, s2, "dw_ffn2_down"), "col"), (0, 2, 1)),
    }

    def pair_sums(names, tag):
        gs = [dblocks[name] for name in names]
        from_sibling = _exchange(_pair_ride(gs), "reduce_pair_" + tag)
        return [_add_pairs(g, me_core, b, "reduce_pair_add_" + name) for g, b, name in zip(gs, from_sibling, names)]

    late_names = [name for name, _ in late]
    pairs = dict(zip(late_names, pair_sums(late_names, "late")))
    (dh0, dg_ffn1, n1, dab1, s1, do1), got = _ffn_bwd(
        h0, dh1, a1, b1, g_ffn1, full["ffn1_w_gate"], full["ffn1_w_up"], full["ffn1_w_down"], "ffn1_bwd",
        ride=_chips_ride([pairs[name] for name in late_names]))
    from_chips = dict(zip(late_names, got))
    dlam4 = dlam.reshape(SUBLANES, 2, 2, gh, n_state)
    dlam_in = jnp.transpose(dlam4, (2, 0, 1, 3, 4)).reshape(2, SUBLANES, n_grp, n_state)
    hg = gp // 2
    dbb_r = jnp.concatenate([_diag_blocks(dbc[h][:, :hg], gh) for h in range(2)], axis=0)
    dbb_i = jnp.concatenate([_diag_blocks(dbc[h][:, hg:], gh) for h in range(2)], axis=0)
    da_re, da_im, dldt, dbre_t, dbim_t = _s5_params_bwd(a_re, a_im, ldt, b_re_t, b_im_t, dlam_in, dbb_r, dbb_i)
    dc_re = jnp.concatenate([_diag_blocks(dcc[h][:hg], gh) for h in range(2)], axis=0)
    dc_im = -jnp.concatenate([_diag_blocks(dcc[h][hg:], gh) for h in range(2)], axis=0)

    grads_rep = {
        "g_ffn1": dg_ffn1, "g_mix": dg_mix, "b_gate": dbg, "ssm_a_re": da_re[None], "ssm_a_im": da_im[None],
        "ssm_log_dt": dldt.reshape(1, n_grp), "ssm_b_re": jnp.transpose(dbre_t, (0, 2, 1))[None],
        "ssm_b_im": jnp.transpose(dbim_t, (0, 2, 1))[None], "ssm_c_re": jnp.transpose(dc_re, (0, 2, 1))[None],
        "ssm_c_im": jnp.transpose(dc_im, (0, 2, 1))[None], "ssm_d": ddsk, "g_ffn2": dg_ffn2, "g_final": dg_final.reshape(d),
    }

    rep_shapes = [w[n].shape for n in REPLICATED]
    small_g_shapes = rep_shapes + [(n_meta, d), (3, dc), (1, 1)]
    gsmall = _pack([grads_rep[n] for n in REPLICATED] + [dh0[0:n_meta], dcw[0:3], loss_blk[0:1, 0:1]])
    dw_gu1, (gall,) = _dw(n1, dab1, "dw_ffn1_gate_up", ride=_gather_ride([gsmall]))
    dblocks.update({
        "ffn1_w_gate": _blocks_from_full(dw_gu1[:, :f_ff], "col"),
        "ffn1_w_up": _blocks_from_full(dw_gu1[:, f_ff:], "col"),
        "ffn1_w_down": jnp.transpose(_blocks_from_full(_dw(do1, s1, "dw_ffn1_down"), "col"), (0, 2, 1)),
    })
    early_names = [name for name, _ in early]
    pairs.update(zip(early_names, pair_sums(early_names, "early")))
    from_chips.update(zip(early_names, _exchange(_chips_ride([pairs[name] for name in early_names]), "reduce_chips_early")))

    out_g, out_d, out_m, out_v = {}, {}, {}, {}
    for name, _ in BIG:
        fc = from_chips[name]
        out_g[name], out_d[name], out_m[name], out_v[name] = _adamw(
            w[name], mom_m[name], mom_v[name], [(pairs[name], None), (fc, 0), (fc, 1), (fc, 2)], me_chip, "adamw_" + name)

    zer = [jnp.zeros((n_meta, d), F32), jnp.zeros((3, dc), F32), jnp.zeros((1, 1), F32)]
    gr, dr, mr, vr = [o[0] for o in _adamw(
        _pack([w[n] for n in REPLICATED] + zer)[None], _pack([mom_m[n] for n in REPLICATED] + zer)[None],
        _pack([mom_v[n] for n in REPLICATED] + zer)[None], [(gall, b) for b in range(NDEV)], None, "adamw_replicated")]
    g_list = _unpack(gr, small_g_shapes)
    out_g.update(zip(REPLICATED, g_list[:len(REPLICATED)]))
    out_d.update(zip(REPLICATED, _unpack(dr, rep_shapes)))
    out_m.update(zip(REPLICATED, _unpack(mr, rep_shapes)))
    out_v.update(zip(REPLICATED, _unpack(vr, rep_shapes)))

    loss = g_list[-1][0, 0]
    g_meta = lax.dynamic_slice_in_dim(g_list[-3], me * mcols, mcols, axis=1)
    g_cw = lax.dynamic_slice_in_dim(g_list[-2], me * ccols, ccols, axis=1).reshape(conv_w.shape)
    tiny = ("meta_tokens", "conv_w")
    tiny_shapes = [meta_tokens.shape, conv_w.shape]
    gt, dt_, mt, vt = [o[0] for o in _adamw(
        _pack([w[n] for n in tiny])[None], _pack([mom_m[n] for n in tiny])[None], _pack([mom_v[n] for n in tiny])[None],
        [(_pack([g_meta, g_cw])[None], 0)], None, "adamw_tiny")]
    out_g.update(zip(tiny, _unpack(gt, tiny_shapes)))
    out_d.update(zip(tiny, _unpack(dt_, tiny_shapes)))
    out_m.update(zip(tiny, _unpack(mt, tiny_shapes)))
    out_v.update(zip(tiny, _unpack(vt, tiny_shapes)))

    grad_x = dh0[n_meta:t_real][None]
    return (loss, grad_x, *[out_g[n] for n in WEIGHTS], *[out_d[n] for n in WEIGHTS],
            *[out_m[n] for n in WEIGHTS], *[out_v[n] for n in WEIGHTS])
```

```python
import functools

import jax
import jax.numpy as jnp
from jax import lax
from jax.experimental import pallas as pl
from jax.experimental.pallas import tpu as pltpu

F32 = jnp.float32
BF16 = jnp.bfloat16
MESH = pl.DeviceIdType.MESH
NDEV = 8
SLAB_COLS = 1024
RMS_EPS = 1e-6
TOKEN_TILE = 320
LIGHT_TILE = 416
MIX_TILE = 320
ROW_ALIGN = 128
SUBLANES = 8
SCAN_LANES = 512
FFN_CHUNK = 4096
VMEM_LIMIT_BYTES = 56 * 1024 * 1024

ADAM_LR = 0.001
ADAM_B1 = 0.9
ADAM_B2 = 0.999
ADAM_EPS = 1e-08
ADAM_WD = 0.01
ADAM_STEP = 10

_VM = pl.BlockSpec(memory_space=pltpu.VMEM)
_ANY = pl.BlockSpec(memory_space=pl.ANY)


def _params(sem=("arbitrary",)):
    return pltpu.CompilerParams(dimension_semantics=sem, vmem_limit_bytes=VMEM_LIMIT_BYTES)


def _dot(a, b):
    return jnp.dot(a, b, preferred_element_type=F32)


def _dot_nt(a, b):
    return lax.dot_general(a, b, (((1,), (1,)), ((), ())), preferred_element_type=F32)


def _dot_tn(a, b):
    return lax.dot_general(a, b, (((0,), (0,)), ((), ())), preferred_element_type=F32)


def _tile(rows, most):
    return next(k for k in range(most - most % 16, 0, -16) if rows % k == 0)


def _chunks(n, step):
    return [(s, min(s + step, n)) for s in range(0, n, step)]


def _gather_plan(x_refs, out_refs, send_sems, recv_sems, local_sems):
    n = len(x_refs)
    x, y, c = lax.axis_index("x"), lax.axis_index("y"), lax.axis_index("c")
    me, sibling = (x, y, c), (x, y, 1 - c)
    chips = [(1 - x, y), (x, 1 - y), (1 - x, 1 - y)]

    def copy(i, k, block, to, src=None):
        slot = out_refs[i].at[4 * block[0] + 2 * block[1] + block[2]]
        return pltpu.make_async_remote_copy(
            src_ref=slot if src is None else src, dst_ref=slot,
            send_sem=send_sems.at[7 * i + k], recv_sem=recv_sems.at[7 * i + k], device_id=to, device_id_type=MESH)

    def mine():
        return [pltpu.make_async_copy(x_refs[i], out_refs[i].at[4 * x + 2 * y + c], local_sems.at[i]) for i in range(n)]

    def first():
        out = []
        for i in range(n):
            out.append(copy(i, 0, me, sibling, src=x_refs[i]))
            out += [copy(i, 1 + j, me, (*chip, c), src=x_refs[i]) for j, chip in enumerate(chips)]
        return out

    def start():
        for cp in mine() + first():
            cp.start()

    def finish():
        passed = []
        for j, chip in enumerate(chips):
            for i in range(n):
                copy(i, 1 + j, (*chip, c), me).wait_recv()
                cp = copy(i, 4 + j, (*chip, c), sibling)
                cp.start()
                passed.append(cp)
        for i in range(n):
            copy(i, 0, sibling, me).wait_recv()
            for j, chip in enumerate(chips):
                copy(i, 4 + j, (*chip, 1 - c), me).wait_recv()
        for cp in first() + passed:
            cp.wait_send()
        for cp in mine():
            cp.wait()

    return start, finish


def _pair_plan(g_refs, out_refs, send_sems, recv_sems):
    x, y, c = lax.axis_index("x"), lax.axis_index("y"), lax.axis_index("c")

    def copies():
        return [pltpu.make_async_remote_copy(
            src_ref=g_refs[i].at[2 * j + (1 - c)], dst_ref=out_refs[i].at[j],
            send_sem=send_sems.at[4 * i + j], recv_sem=recv_sems.at[4 * i + j],
            device_id=(x, y, 1 - c), device_id_type=MESH) for i in range(len(g_refs)) for j in range(4)]

    def start():
        for cp in copies():
            cp.start()

    def finish():
        for cp in copies():
            cp.wait()

    return start, finish


def _chips_plan(p_refs, out_refs, send_sems, recv_sems):
    x, y, c = lax.axis_index("x"), lax.axis_index("y"), lax.axis_index("c")

    def copies():
        return [pltpu.make_async_remote_copy(
            src_ref=p_refs[i].at[2 * px + py], dst_ref=out_refs[i].at[k],
            send_sem=send_sems.at[3 * i + k], recv_sem=recv_sems.at[3 * i + k],
            device_id=(px, py, c), device_id_type=MESH)
            for i in range(len(p_refs)) for k, (px, py) in enumerate([(1 - x, y), (x, 1 - y), (1 - x, 1 - y)])]

    def start():
        for cp in copies():
            cp.start()

    def finish():
        for cp in copies():
            cp.wait()

    return start, finish


def _gather_ride(shards):
    n = len(shards)
    return dict(plan=_gather_plan, arrays=list(shards),
                out_shape=[jax.ShapeDtypeStruct((NDEV, *s.shape), s.dtype) for s in shards],
                sems=[pltpu.SemaphoreType.DMA((7 * n,)), pltpu.SemaphoreType.DMA((7 * n,)), pltpu.SemaphoreType.DMA((n,))])


def _pair_ride(blocks):
    n = len(blocks)
    return dict(plan=_pair_plan, arrays=list(blocks),
                out_shape=[jax.ShapeDtypeStruct((4, *b.shape[1:]), b.dtype) for b in blocks],
                sems=[pltpu.SemaphoreType.DMA((4 * n,)), pltpu.SemaphoreType.DMA((4 * n,))])


def _chips_ride(partials):
    n = len(partials)
    return dict(plan=_chips_plan, arrays=list(partials),
                out_shape=[jax.ShapeDtypeStruct((3, *p.shape[1:]), p.dtype) for p in partials],
                sems=[pltpu.SemaphoreType.DMA((3 * n,)), pltpu.SemaphoreType.DMA((3 * n,))])


def _exchange(ride, name):
    n = len(ride["arrays"])

    def body(*refs):
        start, finish = ride["plan"](refs[:n], refs[n:2 * n], *refs[2 * n:])
        start()
        finish()

    return pl.pallas_call(
        body, name=name, out_shape=ride["out_shape"], in_specs=[_ANY] * n, out_specs=[_ANY] * n, scratch_shapes=ride["sems"],
    )(*ride["arrays"])


def _grid_call(body, name, steps, in_specs, out_specs, out_shape, scratch_shapes, args, ride=None):
    if ride is None:
        outs = pl.pallas_call(body, name=name, grid=(steps,), in_specs=in_specs, out_specs=out_specs, out_shape=out_shape,
                              scratch_shapes=scratch_shapes, compiler_params=_params())(*args)
        return list(outs), []
    n_in, n_out, n_scr, n_ride, n_sems = len(in_specs), len(out_specs), len(scratch_shapes), len(ride["arrays"]), len(ride["sems"])

    def carrying(*refs):
        ins, r_in = refs[:n_in], refs[n_in:n_in + n_ride]
        o0 = n_in + n_ride
        outs, r_out = refs[o0:o0 + n_out], refs[o0 + n_out:o0 + n_out + n_ride]
        s0 = o0 + n_out + n_ride
        scratch, sems = refs[s0:s0 + n_scr], refs[s0 + n_scr:s0 + n_scr + n_sems]
        start, finish = ride["plan"](r_in, r_out, *sems)
        pl.when(pl.program_id(0) == 0)(start)
        body(*ins, *outs, *scratch)
        pl.when(pl.program_id(0) == steps - 1)(finish)

    outs = pl.pallas_call(
        carrying, name=name, grid=(steps,), in_specs=list(in_specs) + [_ANY] * n_ride, out_specs=list(out_specs) + [_ANY] * n_ride,
        out_shape=list(out_shape) + ride["out_shape"], scratch_shapes=list(scratch_shapes) + ride["sems"],
        compiler_params=_params())(*args, *ride["arrays"])
    return list(outs[:n_out]), list(outs[n_out:])


def _row_block(rows):
    return rows if rows <= 512 else next(k for k in (512, 256, 128, rows) if rows % k == 0)


def _add_pairs(gs, core, b, name):
    k, r, n = b.shape
    tr = _row_block(r)

    def body(core_ref, a_ref, b_ref, o_ref):
        o_ref[0] = (a_ref[0, 0].astype(F32) + b_ref[0].astype(F32)).astype(o_ref.dtype)

    spec = pl.BlockSpec((1, tr, n), lambda j, i, c: (j, i, 0))
    return pl.pallas_call(
        body, name=name,
        grid_spec=pltpu.PrefetchScalarGridSpec(
            num_scalar_prefetch=1, grid=(k, r // tr),
            in_specs=[pl.BlockSpec((1, 1, tr, n), lambda j, i, c: (j, c[0], i, 0)), spec], out_specs=spec),
        out_shape=jax.ShapeDtypeStruct(b.shape, b.dtype), compiler_params=_params(("arbitrary", "arbitrary")),
    )(core.reshape(1), gs.reshape(k, 2, r, n), b)


def _adamw(w, m, v, parts, sel, name):
    _, r, n = w.shape
    tr = _row_block(r)
    nparts = len(parts)
    bc1 = 1.0 - ADAM_B1 ** ADAM_STEP
    bc2 = 1.0 - ADAM_B2 ** ADAM_STEP

    def body(sel_ref, *refs):
        w_ref, m_ref, v_ref = refs[:3]
        p_refs = refs[3:3 + nparts]
        g_ref, d_ref, nm_ref, nv_ref = refs[3 + nparts:]
        g = p_refs[0][...].astype(F32)
        for p in p_refs[1:]:
            g = g + p[...].astype(F32)
        nm = ADAM_B1 * m_ref[...] + (1.0 - ADAM_B1) * g
        nv = ADAM_B2 * v_ref[...] + (1.0 - ADAM_B2) * (g * g)
        m_hat = nm / bc1
        v_hat = nv / bc2
        g_ref[...] = g
        d_ref[...] = -ADAM_LR * (m_hat / (jnp.sqrt(v_hat) + ADAM_EPS) + ADAM_WD * w_ref[...])
        nm_ref[...] = nm
        nv_ref[...] = nv

    def part_spec(idx):
        if idx is None:
            return pl.BlockSpec((1, tr, n), lambda i, s: (s[0], i, 0))
        return pl.BlockSpec((1, tr, n), lambda i, s, idx=idx: (idx, i, 0))

    spec = pl.BlockSpec((1, tr, n), lambda i, s: (0, i, 0))
    out = jax.ShapeDtypeStruct((1, r, n), F32)
    return pl.pallas_call(
        body, name=name,
        grid_spec=pltpu.PrefetchScalarGridSpec(
            num_scalar_prefetch=1, grid=(r // tr,),
            in_specs=[spec] * 3 + [part_spec(idx) for _, idx in parts], out_specs=[spec] * 4),
        out_shape=[out] * 4, compiler_params=_params(),
    )(jnp.zeros((1,), jnp.int32) if sel is None else sel.reshape(1), w, m, v, *[p for p, _ in parts])


def _rms_parts(h, g):
    r = lax.rsqrt(jnp.mean(h * h, axis=-1, keepdims=True) + RMS_EPS)
    xhat = h * r
    return r, xhat, xhat * g


def _rms_bwd(dn, g, r, xhat):
    dxh = dn * g
    return r * (dxh - xhat * jnp.mean(dxh * xhat, axis=-1, keepdims=True))


def _loss_tile(h, tgt, g, lo, hi, loss_ref, dg_ref):
    tm, d = h.shape
    i = pl.program_id(0)

    @pl.when(i == 0)
    def _():
        loss_ref[...] = jnp.zeros_like(loss_ref)
        dg_ref[...] = jnp.zeros_like(dg_ref)

    r, xhat, y = _rms_parts(h, g)
    row = i * tm + lax.broadcasted_iota(jnp.int32, (tm, 1), 0)
    err = jnp.where((row >= lo) & (row < hi), y - tgt, 0.0)
    loss_ref[...] += jnp.full(loss_ref.shape, 0.5 * jnp.sum(jnp.mean(err * err, axis=-1, keepdims=True)), F32)
    dy = err * (1.0 / d)
    dg_ref[...] += jnp.sum(dy * xhat, axis=0, keepdims=True)
    return _rms_bwd(dy, g, r, xhat)


def _ffn_fwd(h, g, wg, wu, wd, name, ride=None, head=None):
    t, d = h.shape
    f = wg.shape[1]
    tm = _tile(t, TOKEN_TILE)
    chunks = _chunks(f, FFN_CHUNK)

    def body(h_ref, g_ref, wg_ref, wu_ref, wd_ref, *rest):
        t_ref, gh_ref = rest[:2] if head else (None, None)
        o_ref, a_ref, b_ref = rest[2:5] if head else rest
        hv = h_ref[...]
        n = _rms_parts(hv, g_ref[...])[2].astype(BF16)
        acc = jnp.zeros((tm, d), F32)
        for s, e in chunks:
            a = _dot(n, wg_ref[:, s:e])
            b = _dot(n, wu_ref[:, s:e])
            a_ref[:, s:e] = a.astype(BF16)
            b_ref[:, s:e] = b.astype(BF16)
            acc = acc + _dot((a * jax.nn.sigmoid(a) * b).astype(BF16), wd_ref[s:e, :])
        out = hv + 0.5 * acc
        o_ref[...] = _loss_tile(out, t_ref[...], gh_ref[...], head[2], head[3], rest[5], rest[6]) if head else out

    tile = pl.BlockSpec((tm, d), lambda i: (i, 0))
    wide = pl.BlockSpec((tm, f), lambda i: (i, 0))
    in_specs, args = [tile, _VM, _VM, _VM, _VM], (h, g, wg, wu, wd)
    out_specs = [tile, wide, wide]
    out_shape = [jax.ShapeDtypeStruct((t, d), F32), jax.ShapeDtypeStruct((t, f), BF16), jax.ShapeDtypeStruct((t, f), BF16)]
    if head:
        in_specs, args = in_specs + [tile, _VM], args + (head[0], head[1])
        out_specs = out_specs + [pl.BlockSpec((SUBLANES, 128), lambda i: (0, 0)), pl.BlockSpec((1, d), lambda i: (0, 0))]
        out_shape = out_shape + [jax.ShapeDtypeStruct((SUBLANES, 128), F32), jax.ShapeDtypeStruct((1, d), F32)]
    return _grid_call(body, name, t // tm, in_specs, out_specs, out_shape, [], args, ride)


def _ffn_bwd(h, dh_out, a16, b16, g, wg, wu, wd, name, ride=None):
    t, d = h.shape
    f = wg.shape[1]
    tm = _tile(t, TOKEN_TILE)
    chunks = _chunks(f, FFN_CHUNK)

    def body(h_ref, dho_ref, a_ref, b_ref, g_ref, wg_ref, wu_ref, wd_ref, dh_ref, dg_ref, n_ref, dab_ref, s_ref, do_ref):
        @pl.when(pl.program_id(0) == 0)
        def _():
            dg_ref[...] = jnp.zeros_like(dg_ref)

        hv = h_ref[...]
        gv = g_ref[...]
        r, xhat, n32 = _rms_parts(hv, gv)
        dho = dho_ref[...]
        do = (0.5 * dho).astype(BF16)
        dn = jnp.zeros((tm, d), F32)
        for s, e in chunks:
            a = a_ref[:, s:e].astype(F32)
            b = b_ref[:, s:e].astype(F32)
            sig = jax.nn.sigmoid(a)
            sa = a * sig
            ds = _dot_nt(do, wd_ref[s:e, :])
            da = (ds * b * (sig * (1.0 + a * (1.0 - sig)))).astype(BF16)
            db = (ds * sa).astype(BF16)
            s_ref[:, s:e] = (sa * b).astype(BF16)
            dab_ref[:, s:e] = da
            dab_ref[:, f + s:f + e] = db
            dn = dn + _dot_nt(da, wg_ref[:, s:e]) + _dot_nt(db, wu_ref[:, s:e])
        dh_ref[...] = dho + _rms_bwd(dn, gv, r, xhat)
        dg_ref[...] += jnp.sum(dn * xhat, axis=0, keepdims=True)
        n_ref[...] = n32.astype(BF16)
        do_ref[...] = do

    tile = pl.BlockSpec((tm, d), lambda i: (i, 0))
    wide = pl.BlockSpec((tm, f), lambda i: (i, 0))
    one = pl.BlockSpec((1, d), lambda i: (0, 0))
    return _grid_call(
        body, name, t // tm, [tile, tile, wide, wide, _VM, _VM, _VM, _VM],
        [tile, one, tile, pl.BlockSpec((tm, 2 * f), lambda i: (i, 0)), wide, tile],
        [jax.ShapeDtypeStruct((t, d), F32), jax.ShapeDtypeStruct((1, d), F32),
         jax.ShapeDtypeStruct((t, d), BF16), jax.ShapeDtypeStruct((t, 2 * f), BF16),
         jax.ShapeDtypeStruct((t, f), BF16), jax.ShapeDtypeStruct((t, d), BF16)],
        [], (h, dh_out, a16, b16, g, wg, wu, wd), ride)


def _dw(a, b, name, ride=None):
    t, m = a.shape
    n = b.shape[1]
    bn = next(k for k in (512, 256, n) if n % k == 0)

    def body(a_ref, b_ref, o_ref):
        o_ref[...] = _dot_tn(a_ref[...], b_ref[...]).astype(BF16)

    (out,), got = _grid_call(
        body, name, n // bn, [_VM, pl.BlockSpec((t, bn), lambda j: (0, j))], [pl.BlockSpec((m, bn), lambda j: (0, j))],
        [jax.ShapeDtypeStruct((m, n), BF16)], [], (a, b), ride)
    return (out, got) if ride else out


def _to_bf16(arrays, name):
    k = len(arrays)

    def body(*refs):
        for x_ref, o_ref in zip(refs[:k], refs[k:]):
            o_ref[...] = x_ref[...].astype(BF16)

    return pl.pallas_call(
        body, name=name, out_shape=[jax.ShapeDtypeStruct(a.shape, BF16) for a in arrays],
        compiler_params=pltpu.CompilerParams(vmem_limit_bytes=VMEM_LIMIT_BYTES),
    )(*arrays)


def _s5_discretise(a_re, a_im, log_dt, b_re, b_im):
    dt = jnp.exp(log_dt)
    mag = jnp.exp(a_re * dt)
    lam_re = mag * jnp.cos(a_im * dt)
    lam_im = mag * jnp.sin(a_im * dt)
    den = a_re * a_re + a_im * a_im
    q_re = ((lam_re - 1.0) * a_re + lam_im * a_im) / den
    q_im = (lam_im * a_re - (lam_re - 1.0) * a_im) / den
    bb_re = q_re[:, None, :] * b_re - q_im[:, None, :] * b_im
    bb_im = q_re[:, None, :] * b_im + q_im[:, None, :] * b_re
    return lam_re, lam_im, bb_re, bb_im


def _s5_params_fwd(a_re, a_im, log_dt, b_re, b_im):
    g, p = a_re.shape
    c = b_re.shape[1]

    def body(are_ref, aim_ref, ldt_ref, bre_ref, bim_ref, pwr_ref, pwi_ref, bbr_ref, bbi_ref):
        lr, li, bbr, bbi = _s5_discretise(are_ref[...], aim_ref[...], ldt_ref[...], bre_ref[...], bim_ref[...])
        bbr_ref[...] = bbr
        bbi_ref[...] = bbi
        pr, pi = lr, li
        pwr_ref[0] = pr
        pwi_ref[0] = pi
        for k in range(1, SUBLANES):
            pr, pi = pr * lr - pi * li, pr * li + pi * lr
            pwr_ref[k] = pr
            pwi_ref[k] = pi

    return pl.pallas_call(
        body, name="s5_params_fwd",
        out_shape=[jax.ShapeDtypeStruct((SUBLANES, g, p), F32), jax.ShapeDtypeStruct((SUBLANES, g, p), F32),
                   jax.ShapeDtypeStruct((g, c, p), F32), jax.ShapeDtypeStruct((g, c, p), F32)],
    )(a_re, a_im, log_dt, b_re, b_im)


def _s5_params_bwd(a_re, a_im, log_dt, b_re, b_im, dlam, dbb_re, dbb_im):
    g, p = a_re.shape
    c = b_re.shape[1]

    def body(are_ref, aim_ref, ldt_ref, bre_ref, bim_ref, dlam_ref, dbr_ref, dbi_ref,
             dare_ref, daim_ref, dldt_ref, dbre_ref, dbim_ref):
        dlr = jnp.sum(dlam_ref[0], axis=0)
        dli = jnp.sum(dlam_ref[1], axis=0)
        _, vjp = jax.vjp(_s5_discretise, are_ref[...], aim_ref[...], ldt_ref[...], bre_ref[...], bim_ref[...])
        dare, daim, dldt, dbre, dbim = vjp((dlr, dli, dbr_ref[...], dbi_ref[...]))
        dare_ref[...] = dare
        daim_ref[...] = daim
        dldt_ref[...] = dldt
        dbre_ref[...] = dbre
        dbim_ref[...] = dbim

    return pl.pallas_call(
        body, name="s5_params_bwd",
        out_shape=[jax.ShapeDtypeStruct((g, p), F32), jax.ShapeDtypeStruct((g, p), F32),
                   jax.ShapeDtypeStruct((g, 1), F32), jax.ShapeDtypeStruct((g, c, p), F32),
                   jax.ShapeDtypeStruct((g, c, p), F32)],
    )(a_re, a_im, log_dt, b_re, b_im, dlam, dbb_re, dbb_im)


def _scan_chunks(gp):
    hg = gp // 2
    w = min(SCAN_LANES, hg)
    return w, [(half * hg + k * w, half * gp + k * w, half * gp + hg + k * w) for half in range(2) for k in range(hg // w)]


def _cmul_acc(xr, xi, tr, ti, sr, si):
    return xr + tr * sr - ti * si, xi + tr * si + ti * sr


def _scan_fwd(buf_ref, row0, tm, ltab_ref, cin_ref, cout_ref, gp):
    w, chunks = _scan_chunks(gp)
    for lo_t, lo_r, lo_i in chunks:
        def body(r, carry, lo_t=lo_t, lo_r=lo_r, lo_i=lo_i):
            cr, ci = carry
            row = pl.multiple_of(row0 + r * SUBLANES, SUBLANES)
            xr = buf_ref[pl.ds(row, SUBLANES), lo_r:lo_r + w]
            xi = buf_ref[pl.ds(row, SUBLANES), lo_i:lo_i + w]
            for tab, shift in ((0, 1), (2, 2), (4, 4)):
                xr, xi = _cmul_acc(xr, xi, ltab_ref[tab, :, lo_t:lo_t + w], ltab_ref[tab + 1, :, lo_t:lo_t + w],
                                   pltpu.roll(xr, shift, 0), pltpu.roll(xi, shift, 0))
            xr, xi = _cmul_acc(xr, xi, ltab_ref[6, :, lo_t:lo_t + w], ltab_ref[7, :, lo_t:lo_t + w], cr, ci)
            buf_ref[pl.ds(row, SUBLANES), lo_r:lo_r + w] = xr
            buf_ref[pl.ds(row, SUBLANES), lo_i:lo_i + w] = xi
            last = SUBLANES - 1
            return (jnp.broadcast_to(xr[last:last + 1], (SUBLANES, w)), jnp.broadcast_to(xi[last:last + 1], (SUBLANES, w)))

        cr, ci = lax.fori_loop(0, tm // SUBLANES, body,
                               (cin_ref[0:SUBLANES, lo_r:lo_r + w], cin_ref[0:SUBLANES, lo_i:lo_i + w]), unroll=True)
        if cout_ref is not None:
            cout_ref[0:SUBLANES, lo_r:lo_r + w] = cr
            cout_ref[0:SUBLANES, lo_i:lo_i + w] = ci


def _scan_rev(g_ref, hext_ref, tm, ltab_ref, gc_ref, dlam_ref, gp):
    w, chunks = _scan_chunks(gp)
    nb = tm // SUBLANES
    for lo_t, lo_r, lo_i in chunks:
        def body(k, carry, lo_t=lo_t, lo_r=lo_r, lo_i=lo_i):
            cr, ci, ar, ai = carry
            row = pl.multiple_of((nb - 1 - k) * SUBLANES, SUBLANES)
            xr = g_ref[pl.ds(row, SUBLANES), lo_r:lo_r + w]
            xi = g_ref[pl.ds(row, SUBLANES), lo_i:lo_i + w]
            for tab, shift in ((8, 7), (10, 6), (12, 4)):
                xr, xi = _cmul_acc(xr, xi, ltab_ref[tab, :, lo_t:lo_t + w], ltab_ref[tab + 1, :, lo_t:lo_t + w],
                                   pltpu.roll(xr, shift, 0), pltpu.roll(xi, shift, 0))
            xr, xi = _cmul_acc(xr, xi, ltab_ref[14, :, lo_t:lo_t + w], ltab_ref[15, :, lo_t:lo_t + w], cr, ci)
            g_ref[pl.ds(row, SUBLANES), lo_r:lo_r + w] = xr
            g_ref[pl.ds(row, SUBLANES), lo_i:lo_i + w] = xi
            first = lax.broadcasted_iota(jnp.int32, (SUBLANES, w), 0) == 0
            prev = pl.ds(row, SUBLANES)
            here = pl.ds(row + SUBLANES, SUBLANES)
            hpr = jnp.where(first, pltpu.roll(hext_ref[prev, lo_r:lo_r + w], 1, 0), pltpu.roll(hext_ref[here, lo_r:lo_r + w], 1, 0))
            hpi = jnp.where(first, pltpu.roll(hext_ref[prev, lo_i:lo_i + w], 1, 0), pltpu.roll(hext_ref[here, lo_i:lo_i + w], 1, 0))
            ar = ar + xr * hpr + xi * hpi
            ai = ai - xr * hpi + xi * hpr
            return (jnp.broadcast_to(xr[0:1], (SUBLANES, w)), jnp.broadcast_to(xi[0:1], (SUBLANES, w)), ar, ai)

        cr, ci, ar, ai = lax.fori_loop(
            0, nb, body, (gc_ref[:, lo_r:lo_r + w], gc_ref[:, lo_i:lo_i + w], dlam_ref[:, lo_r:lo_r + w], dlam_ref[:, lo_i:lo_i + w]),
            unroll=True)
        gc_ref[:, lo_r:lo_r + w] = cr
        gc_ref[:, lo_i:lo_i + w] = ci
        dlam_ref[:, lo_r:lo_r + w] = ar
        dlam_ref[:, lo_i:lo_i + w] = ai


def _conv_taps(cw, cext_ref, cin, tm):
    return (cw[0:1] * cext_ref[SUBLANES - 2:SUBLANES - 2 + tm, :] + cw[1:2] * cext_ref[SUBLANES - 1:SUBLANES - 1 + tm, :]
            + cw[2:3] * cin)


def _mix_fwd(h, gm, win, bg, bc, cc, dsk, wglu, cw, wco, wo, ltab, dims):
    d, ds, dc, gp = dims
    t = h.shape[0]
    tm = _tile(t, MIX_TILE)
    nt = t // tm
    dsh = ds // 2
    o1, o2, o3 = ds + dc, ds + 2 * dc, ds + 3 * dc
    ncols = o3 + 2 * d

    def body(h_ref, gm_ref, win_ref, bg_ref, bc_ref, cc_ref, dsk_ref, wglu_ref, cw_ref, wco_ref, wo_ref, ltab_ref,
             h2_ref, p_ref, hs_ref, y5_ref, z_ref, yc_ref, hbuf_ref, carry_ref, cext_ref):
        @pl.when(pl.program_id(0) == 0)
        def _():
            carry_ref[...] = jnp.zeros_like(carry_ref)
            cext_ref[0:SUBLANES, :] = jnp.zeros((SUBLANES, dc), F32)

        hv = h_ref[...]
        bg = bg_ref[...]
        u = _rms_parts(hv, gm_ref[...])[2].astype(BF16)
        us = _dot(u, win_ref[:, 0:ds])
        v = _dot(u, win_ref[:, ds:o1])
        gb = _dot(u, win_ref[:, o1:o2])
        gcv = _dot(u, win_ref[:, o2:o3])
        gs = jax.nn.sigmoid(_dot(u, win_ref[:, o3:o3 + d]) + bg[:, 0:d])
        gcg = jax.nn.sigmoid(_dot(u, win_ref[:, o3 + d:o3 + 2 * d]) + bg[:, d:2 * d])
        us16 = us.astype(BF16)
        p_ref[:, 0:ds] = us16
        p_ref[:, ds:o1] = v.astype(BF16)
        p_ref[:, o1:o2] = gb.astype(BF16)
        p_ref[:, o2:o3] = gcv.astype(BF16)
        p_ref[:, o3:o3 + d] = gs.astype(BF16)
        p_ref[:, o3 + d:ncols] = gcg.astype(BF16)
        for half in range(2):
            hbuf_ref[:, half * gp:(half + 1) * gp] = _dot(us16[:, half * dsh:(half + 1) * dsh], bc_ref[half])
        _scan_fwd(hbuf_ref, 0, tm, ltab_ref, carry_ref, carry_ref, gp)
        hs_ref[...] = hbuf_ref[...].astype(BF16)
        y5 = jnp.concatenate([_dot(hs_ref[:, half * gp:(half + 1) * gp], cc_ref[half]) for half in range(2)], axis=1) + dsk_ref[...] * us
        y5_ref[...] = y5.astype(BF16)
        z = _dot(jax.nn.gelu(y5).astype(BF16), wglu_ref[...])
        z_ref[...] = z.astype(BF16)
        ys = z[:, 0:d] * jax.nn.sigmoid(z[:, d:2 * d])
        cin = gcv * v
        cext_ref[SUBLANES:SUBLANES + tm, :] = cin
        yc = _dot((gb * _conv_taps(cw_ref[...], cext_ref, cin, tm)).astype(BF16), wco_ref[...])
        yc_ref[...] = yc.astype(BF16)
        h2_ref[...] = hv + _dot((gs * ys + gcg * yc).astype(BF16), wo_ref[...])
        cext_ref[0:SUBLANES, :] = cext_ref[tm:tm + SUBLANES, :]

    def tile(cols):
        return pl.BlockSpec((tm, cols), lambda i: (i, 0))

    def bf(cols):
        return jax.ShapeDtypeStruct((t, cols), BF16)

    return pl.pallas_call(
        body, name="mix_fwd", grid=(nt,),
        in_specs=[tile(d)] + [_VM] * 11,
        out_specs=[tile(d), tile(ncols), tile(2 * gp), tile(ds), tile(2 * d), tile(d)],
        out_shape=[jax.ShapeDtypeStruct((t, d), F32), bf(ncols), bf(2 * gp), bf(ds), bf(2 * d), bf(d)],
        scratch_shapes=[pltpu.VMEM((tm, 2 * gp), F32), pltpu.VMEM((SUBLANES, 2 * gp), F32), pltpu.VMEM((SUBLANES + tm, dc), F32)],
        compiler_params=_params(),
    )(h, gm, win, bg, bc, cc, dsk, wglu, cw, wco, wo, ltab)


HALO = 16


def _mix_bwd_gates(dh2, p16, y516, z16, yc16, wglu, cw, wco, wo, dims):
    d, ds, dc, gp = dims
    t = dh2.shape[0]
    tm = _tile(t, LIGHT_TILE)
    nt = t // tm
    o1, o2, o3 = ds + dc, ds + 2 * dc, ds + 3 * dc
    ncols = o3 + 2 * d

    def body(dh2_ref, p_ref, halo_ref, y5_ref, z_ref, yc_ref, wglu_ref, cw_ref, wco_ref, wo_ref,
             dp_ref, dy5_ref, ge_ref, dz_ref, cg_ref, dyc_ref, mx_ref, dh216_ref, dbg_ref, dcw_ref, cext_ref, dcvext_ref):
        j = pl.program_id(0)

        @pl.when(j == 0)
        def _():
            dbg_ref[...] = jnp.zeros_like(dbg_ref)
            dcw_ref[...] = jnp.zeros_like(dcw_ref)
            dcvext_ref[tm:tm + SUBLANES, :] = jnp.zeros((SUBLANES, dc), F32)

        before = halo_ref[:, o2:o3].astype(F32) * halo_ref[:, ds:o1].astype(F32)
        cext_ref[0:SUBLANES, :] = jnp.where(j == nt - 1, 0.0, before[HALO - SUBLANES:HALO])
        cw_v = cw_ref[...]
        v = p_ref[:, ds:o1].astype(F32)
        gb = p_ref[:, o1:o2].astype(F32)
        gcv = p_ref[:, o2:o3].astype(F32)
        gs = p_ref[:, o3:o3 + d].astype(F32)
        gcg = p_ref[:, o3 + d:ncols].astype(F32)
        z1 = z_ref[:, 0:d].astype(F32)
        sz = jax.nn.sigmoid(z_ref[:, d:2 * d].astype(F32))
        ys = z1 * sz
        yc = yc_ref[...].astype(F32)
        ge, gelu_vjp = jax.vjp(jax.nn.gelu, y5_ref[...].astype(F32))
        cin = gcv * v
        cext_ref[SUBLANES:SUBLANES + tm, :] = cin
        cv = _conv_taps(cw_v, cext_ref, cin, tm)

        dh216 = dh2_ref[...].astype(BF16)
        dmixed = _dot_nt(dh216, wo_ref[...])
        dys = dmixed * gs
        dyc16 = (dmixed * gcg).astype(BF16)
        dpgs = dmixed * ys * gs * (1.0 - gs)
        dpgc = dmixed * yc * gcg * (1.0 - gcg)
        dz16 = jnp.concatenate([dys * sz, dys * z1 * sz * (1.0 - sz)], axis=1).astype(BF16)
        dy5_ref[...] = gelu_vjp(_dot_nt(dz16, wglu_ref[...]))[0].astype(BF16)
        dcg = _dot_nt(dyc16, wco_ref[...])
        dcv = dcg * gb
        dcvext_ref[0:tm, :] = dcv
        dcin = cw_v[2:3] * dcv + cw_v[1:2] * dcvext_ref[1:1 + tm, :] + cw_v[0:1] * dcvext_ref[2:2 + tm, :]
        dcw_ref[0:1, :] += jnp.sum(dcv * cext_ref[SUBLANES - 2:SUBLANES - 2 + tm, :], axis=0, keepdims=True)
        dcw_ref[1:2, :] += jnp.sum(dcv * cext_ref[SUBLANES - 1:SUBLANES - 1 + tm, :], axis=0, keepdims=True)
        dcw_ref[2:3, :] += jnp.sum(dcv * cin, axis=0, keepdims=True)
        dcvext_ref[tm:tm + SUBLANES, :] = dcvext_ref[0:SUBLANES, :]
        dbg_ref[...] += jnp.concatenate([jnp.sum(dpgs, axis=0, keepdims=True), jnp.sum(dpgc, axis=0, keepdims=True)], axis=1)
        dp_ref[:, 0:ds] = jnp.zeros((tm, ds), BF16)
        dp_ref[:, ds:o1] = (dcin * gcv).astype(BF16)
        dp_ref[:, o1:o2] = (dcg * cv).astype(BF16)
        dp_ref[:, o2:o3] = (dcin * v).astype(BF16)
        dp_ref[:, o3:o3 + d] = dpgs.astype(BF16)
        dp_ref[:, o3 + d:ncols] = dpgc.astype(BF16)
        ge_ref[...] = ge.astype(BF16)
        dz_ref[...] = dz16
        cg_ref[...] = (gb * cv).astype(BF16)
        dyc_ref[...] = dyc16
        mx_ref[...] = (gs * ys + gcg * yc).astype(BF16)
        dh216_ref[...] = dh216

    def rev(cols):
        return pl.BlockSpec((tm, cols), lambda j: (nt - 1 - j, 0))

    def bf(cols):
        return jax.ShapeDtypeStruct((t, cols), BF16)

    halo = pl.BlockSpec((HALO, ncols), lambda j: (jnp.maximum((nt - 1 - j) * (tm // HALO) - 1, 0), 0))
    return pl.pallas_call(
        body, name="mix_bwd_gates", grid=(nt,),
        in_specs=[rev(d), rev(ncols), halo, rev(ds), rev(2 * d), rev(d), _VM, _VM, _VM, _VM],
        out_specs=[rev(ncols), rev(ds), rev(ds), rev(2 * d), rev(dc), rev(d), rev(d), rev(d), _VM, _VM],
        out_shape=[bf(ncols), bf(ds), bf(ds), bf(2 * d), bf(dc), bf(d), bf(d), bf(d),
                   jax.ShapeDtypeStruct((1, 2 * d), F32), jax.ShapeDtypeStruct((SUBLANES, dc), F32)],
        scratch_shapes=[pltpu.VMEM((SUBLANES + tm, dc), F32), pltpu.VMEM((tm + SUBLANES, dc), F32)],
        compiler_params=_params(),
    )(dh2, p16, p16, y516, z16, yc16, wglu, cw, wco, wo)


def _mix_bwd_scan(dy516, hs16, p16, bc, cc, dsk, ltab, dims):
    d, ds, dc, gp = dims
    t = dy516.shape[0]
    tm = _tile(t, TOKEN_TILE)
    nt = t // tm
    dsh = ds // 2

    def body(dy5_ref, hs_ref, halo_ref, us_ref, bc_ref, cc_ref, dsk_ref, ltab_ref,
             dus_ref, ddsk_ref, dlam_ref, dbc_ref, dcc_ref, hext_ref, gbuf_ref, gcarry_ref):
        j = pl.program_id(0)

        @pl.when(j == 0)
        def _():
            for ref in (ddsk_ref, dlam_ref, dbc_ref, dcc_ref, gcarry_ref):
                ref[...] = jnp.zeros_like(ref)

        before = jnp.where(j == nt - 1, 0.0, halo_ref[...].astype(F32)[HALO - 1:HALO])
        hext_ref[0:SUBLANES, :] = jnp.broadcast_to(before, (SUBLANES, 2 * gp))
        hext_ref[SUBLANES:SUBLANES + tm, :] = hs_ref[...].astype(F32)
        dy516v = dy5_ref[...]
        for half in range(2):
            gbuf_ref[:, half * gp:(half + 1) * gp] = _dot_nt(dy516v[:, half * dsh:(half + 1) * dsh], cc_ref[half])
        _scan_rev(gbuf_ref, hext_ref, tm, ltab_ref, gcarry_ref, dlam_ref, gp)
        dus = []
        for half in range(2):
            g16 = gbuf_ref[:, half * gp:(half + 1) * gp].astype(BF16)
            dus.append(_dot_nt(g16, bc_ref[half]))
            dbc_ref[half] += _dot_tn(us_ref[:, half * dsh:(half + 1) * dsh], g16)
            dcc_ref[half] += _dot_tn(hs_ref[:, half * gp:(half + 1) * gp], dy516v[:, half * dsh:(half + 1) * dsh])
        dy5 = dy516v.astype(F32)
        dus_ref[...] = (jnp.concatenate(dus, axis=1) + dsk_ref[...] * dy5).astype(BF16)
        ddsk_ref[...] += jnp.sum(dy5 * us_ref[...].astype(F32), axis=0, keepdims=True)

    def rev(cols):
        return pl.BlockSpec((tm, cols), lambda j: (nt - 1 - j, 0))

    halo = pl.BlockSpec((HALO, 2 * gp), lambda j: (jnp.maximum((nt - 1 - j) * (tm // HALO) - 1, 0), 0))
    return pl.pallas_call(
        body, name="mix_bwd_scan", grid=(nt,),
        in_specs=[rev(ds), rev(2 * gp), halo, rev(ds), _VM, _VM, _VM, _VM],
        out_specs=[rev(ds), _VM, _VM, _VM, _VM],
        out_shape=[jax.ShapeDtypeStruct((t, ds), BF16), jax.ShapeDtypeStruct((1, ds), F32),
                   jax.ShapeDtypeStruct((SUBLANES, 2 * gp), F32),
                   jax.ShapeDtypeStruct((2, dsh, gp), F32), jax.ShapeDtypeStruct((2, gp, dsh), F32)],
        scratch_shapes=[pltpu.VMEM((SUBLANES + tm, 2 * gp), F32), pltpu.VMEM((tm, 2 * gp), F32), pltpu.VMEM((SUBLANES, 2 * gp), F32)],
        compiler_params=_params(),
    )(dy516, hs16, hs16, p16, bc, cc, dsk, ltab)


def _mix_bwd_in(h, dh2, dp16, dus16, gm, win, dims):
    d, ds, dc, gp = dims
    t = h.shape[0]
    tm = _tile(t, LIGHT_TILE)
    ncols = dp16.shape[1]

    def body(h_ref, dh2_ref, dp_ref, dus_ref, gm_ref, win_ref, dh1_ref, u_ref, dpf_ref, dgm_ref):
        @pl.when(pl.program_id(0) == 0)
        def _():
            dgm_ref[...] = jnp.zeros_like(dgm_ref)

        gmv = gm_ref[...]
        r, xhat, n32 = _rms_parts(h_ref[...], gmv)
        du = _dot_nt(dus_ref[...], win_ref[:, 0:ds]) + _dot_nt(dp_ref[:, ds:ncols], win_ref[:, ds:ncols])
        dh1_ref[...] = dh2_ref[...] + _rms_bwd(du, gmv, r, xhat)
        dgm_ref[...] += jnp.sum(du * xhat, axis=0, keepdims=True)
        u_ref[...] = n32.astype(BF16)
        dpf_ref[:, 0:ds] = dus_ref[...]
        dpf_ref[:, ds:ncols] = dp_ref[:, ds:ncols]

    def tile(cols):
        return pl.BlockSpec((tm, cols), lambda i: (i, 0))

    return pl.pallas_call(
        body, name="mix_bwd_in", grid=(t // tm,),
        in_specs=[tile(d), tile(d), tile(ncols), tile(ds), _VM, _VM],
        out_specs=[tile(d), tile(d), tile(ncols), pl.BlockSpec((1, d), lambda i: (0, 0))],
        out_shape=[jax.ShapeDtypeStruct((t, d), F32), jax.ShapeDtypeStruct((t, d), BF16),
                   jax.ShapeDtypeStruct((t, ncols), BF16), jax.ShapeDtypeStruct((1, d), F32)],
        compiler_params=_params(),
    )(h, dh2, dp16, dus16, gm, win)


def _pad_rows(a, rows, axis=0):
    pad = [(0, 0)] * a.ndim
    pad[axis] = (0, rows - a.shape[axis])
    return jnp.pad(a, pad)


def _as_rows(a):
    flat = a.reshape(-1)
    n = -(-flat.shape[0] // SLAB_COLS) * SLAB_COLS
    return jnp.pad(flat, (0, n - flat.shape[0])).reshape(-1, SLAB_COLS)


def _pack(arrs):
    rows = jnp.concatenate([_as_rows(a) for a in arrs], axis=0)
    return _pad_rows(rows, -(-rows.shape[0] // 16) * 16)


def _unpack(slab, shapes):
    out, r = [], 0
    for shp in shapes:
        size = 1
        for s in shp:
            size *= s
        n = -(-size // SLAB_COLS)
        out.append(slab[r:r + n].reshape(-1)[:size].reshape(shp))
        r += n
    return out


def _block_diag(blocks):
    n, a, b = blocks.shape
    eye = jnp.eye(n, dtype=blocks.dtype)
    return (blocks[:, :, None, :] * eye[:, None, :, None]).reshape(n * a, n * b)


def _diag_blocks(mat, n):
    a, b = mat.shape[0] // n, mat.shape[1] // n
    eye = jnp.eye(n, dtype=mat.dtype)
    return jnp.sum(mat.reshape(n, a, n, b) * eye[:, None, :, None], axis=2)


BIG = (("ffn1_w_gate", "col"), ("ffn1_w_up", "col"), ("ffn1_w_down", "row"), ("w_in", "col"), ("ssm_w_glu", "col"),
       ("conv_w_out", "col"), ("w_o", "row"), ("ffn2_w_gate", "col"), ("ffn2_w_up", "col"), ("ffn2_w_down", "row"))
REPLICATED = ("g_ffn1", "g_mix", "b_gate", "ssm_a_re", "ssm_a_im", "ssm_log_dt", "ssm_b_re", "ssm_b_im", "ssm_c_re",
              "ssm_c_im", "ssm_d", "g_ffn2", "g_final")
WEIGHTS = ("meta_tokens", "g_ffn1", "ffn1_w_gate", "ffn1_w_up", "ffn1_w_down", "g_mix", "w_in", "b_gate", "ssm_a_re",
           "ssm_a_im", "ssm_log_dt", "ssm_b_re", "ssm_b_im", "ssm_c_re", "ssm_c_im", "ssm_d", "ssm_w_glu", "conv_w",
           "conv_w_out", "w_o", "g_ffn2", "ffn2_w_gate", "ffn2_w_up", "ffn2_w_down", "g_final")
N_EARLY = 3


def _full_from_blocks(blocks, kind):
    n, r, c = blocks.shape
    if kind == "col":
        return jnp.transpose(blocks, (1, 0, 2)).reshape(r, n * c)
    return blocks.reshape(n * r, c)


def _blocks_from_full(full, kind):
    if kind == "col":
        r, nc = full.shape
        return jnp.transpose(full.reshape(r, NDEV, nc // NDEV), (1, 0, 2))
    nr, c = full.shape
    return full.reshape(NDEV, nr // NDEV, c)


def kernel(x, meta_tokens, g_ffn1, ffn1_w_gate, ffn1_w_up, ffn1_w_down, g_mix, w_in, b_gate, ssm_a_re, ssm_a_im, ssm_log_dt, ssm_b_re, ssm_b_im, ssm_c_re, ssm_c_im, ssm_d, ssm_w_glu, conv_w, conv_w_out, w_o, g_ffn2, ffn2_w_gate, ffn2_w_up, ffn2_w_down, g_final, loss_target, m_meta_tokens, m_g_ffn1, m_ffn1_w_gate, m_ffn1_w_up, m_ffn1_w_down, m_g_mix, m_w_in, m_b_gate, m_ssm_a_re, m_ssm_a_im, m_ssm_log_dt, m_ssm_b_re, m_ssm_b_im, m_ssm_c_re, m_ssm_c_im, m_ssm_d, m_ssm_w_glu, m_conv_w, m_conv_w_out, m_w_o, m_g_ffn2, m_ffn2_w_gate, m_ffn2_w_up, m_ffn2_w_down, m_g_final, v_meta_tokens, v_g_ffn1, v_ffn1_w_gate, v_ffn1_w_up, v_ffn1_w_down, v_g_mix, v_w_in, v_b_gate, v_ssm_a_re, v_ssm_a_im, v_ssm_log_dt, v_ssm_b_re, v_ssm_b_im, v_ssm_c_re, v_ssm_c_im, v_ssm_d, v_ssm_w_glu, v_conv_w, v_conv_w_out, v_w_o, v_g_ffn2, v_ffn2_w_gate, v_ffn2_w_up, v_ffn2_w_down, v_g_final):
    args = dict(locals())
    w = {n: args[n] for n in WEIGHTS}
    mom_m = {n: args["m_" + n] for n in WEIGHTS}
    mom_v = {n: args["v_" + n] for n in WEIGHTS}

    seq, d = x.shape[1], x.shape[2]
    n_meta = meta_tokens.shape[0]
    ds = ssm_d.shape[1]
    n_grp, n_state = ssm_a_re.shape[1], ssm_a_re.shape[2]
    gp = n_grp * n_state
    dc = conv_w.shape[3] * NDEV
    dims = (d, ds, dc, gp)
    t_real = n_meta + seq
    t_pad = -(-t_real // ROW_ALIGN) * ROW_ALIGN
    me_chip = 2 * lax.axis_index("x") + lax.axis_index("y")
    me_core = lax.axis_index("c")
    me = 2 * me_chip + me_core
    mcols, ccols = d // NDEV, dc // NDEV

    cw_shard = _pad_rows(_pad_rows(conv_w.reshape(3, ccols), SUBLANES), 128, axis=1)
    shard16 = dict(zip([name for name, _ in BIG], _to_bf16([w[name][0] for name, _ in BIG], "weights_to_bf16")))
    early, late = BIG[:N_EARLY], BIG[N_EARLY:]
    got = _exchange(_gather_ride([shard16[name] for name, _ in early] + [meta_tokens, cw_shard]), "gather_first")
    full = {name: _full_from_blocks(got[i], kind) for i, (name, kind) in enumerate(early)}
    meta_full = _full_from_blocks(got[-2], "col")
    cw_rows = _pad_rows(_full_from_blocks(got[-1][:, 0:3, 0:ccols], "col"), SUBLANES)

    a_re, a_im, ldt = ssm_a_re[0], ssm_a_im[0], ssm_log_dt[0].reshape(n_grp, 1)
    b_re_t = jnp.transpose(ssm_b_re[0], (0, 2, 1))
    b_im_t = jnp.transpose(ssm_b_im[0], (0, 2, 1))
    pw_r, pw_i, bb_r, bb_i = _s5_params_fwd(a_re, a_im, ldt, b_re_t, b_im_t)
    pw_r = pw_r.reshape(SUBLANES, gp)
    pw_i = pw_i.reshape(SUBLANES, gp)
    sub = jnp.arange(SUBLANES)[:, None]

    def fwd_tab(p, k):
        return jnp.where(sub >= k, p[k - 1][None, :], 0.0)

    def rev_tab(p, k):
        return jnp.where(sub <= SUBLANES - 1 - k, p[k - 1][None, :], 0.0)

    ltab = jnp.stack(
        [fwd_tab(pw_r, 1), fwd_tab(pw_i, 1), fwd_tab(pw_r, 2), fwd_tab(pw_i, 2), fwd_tab(pw_r, 4), fwd_tab(pw_i, 4), pw_r, pw_i,
         rev_tab(pw_r, 1), -rev_tab(pw_i, 1), rev_tab(pw_r, 2), -rev_tab(pw_i, 2), rev_tab(pw_r, 4), -rev_tab(pw_i, 4),
         pw_r[::-1], -pw_i[::-1]], axis=0)
    gh = n_grp // 2
    bc = jnp.stack([jnp.concatenate([_block_diag(bb_r[h * gh:(h + 1) * gh]), _block_diag(bb_i[h * gh:(h + 1) * gh])], axis=1)
                    for h in range(2)]).astype(BF16)
    c_re_t = jnp.transpose(ssm_c_re[0], (0, 2, 1))
    c_im_t = jnp.transpose(ssm_c_im[0], (0, 2, 1))
    cc = jnp.stack([jnp.concatenate([_block_diag(c_re_t[h * gh:(h + 1) * gh]), -_block_diag(c_im_t[h * gh:(h + 1) * gh])], axis=0)
                    for h in range(2)]).astype(BF16)

    zpad = jnp.zeros((t_pad - t_real, d), F32)
    h0 = jnp.concatenate([meta_full, x[0], zpad], axis=0)
    tgt = jnp.concatenate([jnp.zeros((n_meta, d), F32), loss_target[0], zpad], axis=0)
    (h1, a1, b1), got = _ffn_fwd(h0, g_ffn1, full["ffn1_w_gate"], full["ffn1_w_up"], full["ffn1_w_down"], "ffn1_fwd",
                                 ride=_gather_ride([shard16[name] for name, _ in late]))
    full.update({name: _full_from_blocks(got[i], kind) for i, (name, kind) in enumerate(late)})
    h2, *saved = _mix_fwd(h1, g_mix, full["w_in"], b_gate, bc, cc, ssm_d, full["ssm_w_glu"], cw_rows, full["conv_w_out"],
                          full["w_o"], ltab, dims)
    (dh3, a2, b2, loss_blk, dg_final), _ = _ffn_fwd(h2, g_ffn2, full["ffn2_w_gate"], full["ffn2_w_up"], full["ffn2_w_down"], "ffn2_fwd",
                                                 head=(tgt, g_final.reshape(1, d), n_meta, t_real))

    (dh2, dg_ffn2, n2, dab2, s2, do2), _ = _ffn_bwd(
        h2, dh3, a2, b2, g_ffn2, full["ffn2_w_gate"], full["ffn2_w_up"], full["ffn2_w_down"], "ffn2_bwd")
    f_ff = a2.shape[1]
    dw_gu2 = _dw(n2, dab2, "dw_ffn2_gate_up")
    p16, hs16, y516, z16, yc16 = saved
    dp_part, dy516, ge16, dz16, cg16, dyc16, mx16, dh216, dbg, dcw = _mix_bwd_gates(
        dh2, p16, y516, z16, yc16, full["ssm_w_glu"], cw_rows, full["conv_w_out"], full["w_o"], dims)
    dus16, ddsk, dlam, dbc, dcc = _mix_bwd_scan(dy516, hs16, p16, bc, cc, ssm_d, ltab, dims)
    dh1, u16, dp16, dg_mix = _mix_bwd_in(h1, dh2, dp_part, dus16, g_mix, full["w_in"], dims)
    dblocks = {
        "w_in": _blocks_from_full(_dw(u16, dp16, "dw_in"), "col"),
        "ssm_w_glu": _blocks_from_full(_dw(ge16, dz16, "dw_glu"), "col"),
        "conv_w_out": _blocks_from_full(_dw(cg16, dyc16, "dw_conv_out"), "col"),
        "w_o": _blocks_from_full(_dw(mx16, dh216, "dw_o"), "row"),
        "ffn2_w_gate": _blocks_from_full(dw_gu2[:, :f_ff], "col"),
        "ffn2_w_up": _blocks_from_full(dw_gu2[:, f_ff:], "col"),
        "ffn2_w_down": jnp.transpose(_blocks_from_full(_dw(do2, s2, "dw_ffn2_down"), "col"), (0, 2, 1)),
    }

    def pair_sums(names, tag):
        gs = [dblocks[name] for name in names]
        from_sibling = _exchange(_pair_ride(gs), "reduce_pair_" + tag)
        return [_add_pairs(g, me_core, b, "reduce_pair_add_" + name) for g, b, name in zip(gs, from_sibling, names)]

    late_names = [name for name, _ in late]
    pairs = dict(zip(late_names, pair_sums(late_names, "late")))
    (dh0, dg_ffn1, n1, dab1, s1, do1), got = _ffn_bwd(
        h0, dh1, a1, b1, g_ffn1, full["ffn1_w_gate"], full["ffn1_w_up"], full["ffn1_w_down"], "ffn1_bwd",
        ride=_chips_ride([pairs[name] for name in late_names]))
    from_chips = dict(zip(late_names, got))
    dlam4 = dlam.reshape(SUBLANES, 2, 2, gh, n_state)
    dlam_in = jnp.transpose(dlam4, (2, 0, 1, 3, 4)).reshape(2, SUBLANES, n_grp, n_state)
    hg = gp // 2
    dbb_r = jnp.concatenate([_diag_blocks(dbc[h][:, :hg], gh) for h in range(2)], axis=0)
    dbb_i = jnp.concatenate([_diag_blocks(dbc[h][:, hg:], gh) for h in range(2)], axis=0)
    da_re, da_im, dldt, dbre_t, dbim_t = _s5_params_bwd(a_re, a_im, ldt, b_re_t, b_im_t, dlam_in, dbb_r, dbb_i)
    dc_re = jnp.concatenate([_diag_blocks(dcc[h][:hg], gh) for h in range(2)], axis=0)
    dc_im = -jnp.concatenate([_diag_blocks(dcc[h][hg:], gh) for h in range(2)], axis=0)

    grads_rep = {
        "g_ffn1": dg_ffn1, "g_mix": dg_mix, "b_gate": dbg, "ssm_a_re": da_re[None], "ssm_a_im": da_im[None],
        "ssm_log_dt": dldt.reshape(1, n_grp), "ssm_b_re": jnp.transpose(dbre_t, (0, 2, 1))[None],
        "ssm_b_im": jnp.transpose(dbim_t, (0, 2, 1))[None], "ssm_c_re": jnp.transpose(dc_re, (0, 2, 1))[None],
        "ssm_c_im": jnp.transpose(dc_im, (0, 2, 1))[None], "ssm_d": ddsk, "g_ffn2": dg_ffn2, "g_final": dg_final.reshape(d),
    }

    rep_shapes = [w[n].shape for n in REPLICATED]
    small_g_shapes = rep_shapes + [(n_meta, d), (3, dc), (1, 1)]
    gsmall = _pack([grads_rep[n] for n in REPLICATED] + [dh0[0:n_meta], dcw[0:3], loss_blk[0:1, 0:1]])
    dw_gu1, (gall,) = _dw(n1, dab1, "dw_ffn1_gate_up", ride=_gather_ride([gsmall]))
    dblocks.update({
        "ffn1_w_gate": _blocks_from_full(dw_gu1[:, :f_ff], "col"),
        "ffn1_w_up": _blocks_from_full(dw_gu1[:, f_ff:], "col"),
    })
    first_two, last_one = [name for name, _ in early[:2]], [name for name, _ in early[2:]]
    pairs.update(zip(first_two, pair_sums(first_two, "early_gate_up")))
    dw_down1, got = _dw(do1, s1, "dw_ffn1_down", ride=_chips_ride([pairs[name] for name in first_two]))
    from_chips.update(zip(first_two, got))
    dblocks["ffn1_w_down"] = jnp.transpose(_blocks_from_full(dw_down1, "col"), (0, 2, 1))
    pairs.update(zip(last_one, pair_sums(last_one, "early_down")))
    from_chips.update(zip(last_one, _exchange(_chips_ride([pairs[name] for name in last_one]), "reduce_chips_early")))

    out_g, out_d, out_m, out_v = {}, {}, {}, {}
    for name, _ in BIG:
        fc = from_chips[name]
        out_g[name], out_d[name], out_m[name], out_v[name] = _adamw(
            w[name], mom_m[name], mom_v[name], [(pairs[name], None), (fc, 0), (fc, 1), (fc, 2)], me_chip, "adamw_" + name)

    zer = [jnp.zeros((n_meta, d), F32), jnp.zeros((3, dc), F32), jnp.zeros((1, 1), F32)]
    gr, dr, mr, vr = [o[0] for o in _adamw(
        _pack([w[n] for n in REPLICATED] + zer)[None], _pack([mom_m[n] for n in REPLICATED] + zer)[None],
        _pack([mom_v[n] for n in REPLICATED] + zer)[None], [(gall, b) for b in range(NDEV)], None, "adamw_replicated")]
    g_list = _unpack(gr, small_g_shapes)
    out_g.update(zip(REPLICATED, g_list[:len(REPLICATED)]))
    out_d.update(zip(REPLICATED, _unpack(dr, rep_shapes)))
    out_m.update(zip(REPLICATED, _unpack(mr, rep_shapes)))
    out_v.update(zip(REPLICATED, _unpack(vr, rep_shapes)))

    loss = g_list[-1][0, 0]
    g_meta = lax.dynamic_slice_in_dim(g_list[-3], me * mcols, mcols, axis=1)
    g_cw = lax.dynamic_slice_in_dim(g_list[-2], me * ccols, ccols, axis=1).reshape(conv_w.shape)
    tiny = ("meta_tokens", "conv_w")
    tiny_shapes = [meta_tokens.shape, conv_w.shape]
    gt, dt_, mt, vt = [o[0] for o in _adamw(
        _pack([w[n] for n in tiny])[None], _pack([mom_m[n] for n in tiny])[None], _pack([mom_v[n] for n in tiny])[None],
        [(_pack([g_meta, g_cw])[None], 0)], None, "adamw_tiny")]
    out_g.update(zip(tiny, _unpack(gt, tiny_shapes)))
    out_d.update(zip(tiny, _unpack(dt_, tiny_shapes)))
    out_m.update(zip(tiny, _unpack(mt, tiny_shapes)))
    out_v.update(zip(tiny, _unpack(vt, tiny_shapes)))

    grad_x = dh0[n_meta:t_real][None]
    return (loss, grad_x, *[out_g[n] for n in WEIGHTS], *[out_d[n] for n in WEIGHTS],
            *[out_m[n] for n in WEIGHTS], *[out_v[n] for n in WEIGHTS])
```
